```python
import jax
import jax.numpy as jnp
from jax import lax
import numpy as np

D_MODEL = 2048
BATCH = 4
SEQ = 2048
DEPTH = 1
DEC_BATCH = 128
DEC_SEQ = 1
PAST_LEN = 16384
PAGE_SIZE = 128

MIX_WIDTH = D_MODEL
GLA_WIDTH = MIX_WIDTH // 2
RWKV_WIDTH = MIX_WIDTH - GLA_WIDTH
GLA_HEADS = 4
GLA_DV = GLA_WIDTH // GLA_HEADS
GLA_DK = GLA_DV // 2
GLA_KEY_WIDTH = GLA_HEADS * GLA_DK
GLA_GATE_RANK = 16
GLA_GATE_NORMALIZER = 16.0
GLA_CHUNK = 64
RWKV_HEAD = 64
RWKV_HEADS = RWKV_WIDTH // RWKV_HEAD
RWKV_W_RANK = 64
RWKV_A_RANK = 64
RWKV_G_RANK = 128
GLA_SPLITS = (GLA_KEY_WIDTH, GLA_KEY_WIDTH, GLA_WIDTH, GLA_GATE_RANK, GLA_WIDTH)
RWKV_SPLITS = (RWKV_WIDTH, RWKV_W_RANK, RWKV_WIDTH, RWKV_WIDTH, RWKV_A_RANK, RWKV_G_RANK)
GLA_COLS = sum(GLA_SPLITS)
RWKV_COLS = sum(RWKV_SPLITS)
IN_COLS = GLA_COLS + RWKV_COLS
N_GROUPS = 4
EXPERTS_PER_GROUP = 8
TOP_K_IN_GROUP = 2
EXPERT_FF = 512
NORM_EPS = 1e-6
GLA_NORM_EPS = 1e-5
RWKV_GN_EPS = 64e-5

kernel_name = 'hymba_gla_rwkv7_hmoe_adaln'


def split_cols(a, sizes):
    idx = [int(i) for i in np.cumsum(sizes)[:-1]]
    return jnp.split(a, idx, axis=-1)


def rmsnorm(x, g, eps=NORM_EPS):
    xf = x.astype(jnp.float32)
    return xf * lax.rsqrt(jnp.mean(xf * xf, axis=-1, keepdims=True) + eps) * g


def adaln(c, w, b, n):
    mod = jax.nn.silu(c.astype(jnp.float32)) @ w + b
    return jnp.split(mod[:, None, :], n, axis=-1)


def gla_chunked(q, k, v, logd, s0):
    B, T, H, _ = q.shape
    C = GLA_CHUNK
    n = T // C

    def to_chunks(a):
        return a.reshape(B, n, C, H, a.shape[-1]).transpose(1, 0, 3, 2, 4)

    mask = jnp.tril(jnp.ones((C, C), dtype=bool))[:, :, None]

    def step(S, inp):
        qi, ki, vi, gi = inp
        b = jnp.cumsum(gi, axis=2)
        o_inter = jnp.einsum('bhcd,bhde->bhce', qi * jnp.exp(b), S)
        rel = jnp.where(mask, b[:, :, :, None, :] - b[:, :, None, :, :], -jnp.inf)
        att = jnp.einsum('bhid,bhjd,bhijd->bhij', qi, ki, jnp.exp(rel))
        o = o_inter + jnp.einsum('bhij,bhje->bhie', att, vi)
        b_last = b[:, :, -1:, :]
        S = jnp.exp(b_last[:, :, 0, :, None]) * S + jnp.einsum('bhcd,bhce->bhde', ki * jnp.exp(b_last - b), vi)
        return S, o

    S, o = lax.scan(step, s0, (to_chunks(q), to_chunks(k), to_chunks(v), to_chunks(logd)))
    o = o.transpose(1, 0, 3, 2, 4).reshape(B, T, H, -1)
    return o, S


def gla_recurrent(q, k, v, logd, s0):
    def step(S, inp):
        qt, kt, vt, gt = inp
        S = jnp.exp(gt)[..., None] * S + kt[..., :, None] * vt[..., None, :]
        return S, jnp.einsum('bhd,bhde->bhe', qt, S)

    xs = tuple(a.transpose(1, 0, 2, 3) for a in (q, k, v, logd))
    S, o = lax.scan(step, s0, xs)
    return o.transpose(1, 0, 2, 3), S


def rwkv7_scan(r, decay, k, v, kk, a, s0):
    def step(S, inp):
        rt, wt, kt, vt, kkt, at = inp
        sa = jnp.einsum('bhij,bhj->bhi', S, -kkt)
        S = S * wt[:, :, None, :] + sa[..., None] * (kkt * at)[:, :, None, :] + vt[..., None] * kt[:, :, None, :]
        return S, jnp.einsum('bhij,bhj->bhi', S, rt)

    xs = tuple(t.transpose(1, 0, 2, 3) for t in (r, decay, k, v, kk, a))
    S, y = lax.scan(step, s0, xs)
    return y.transpose(1, 0, 2, 3), S


def token_mixers(h, s_gla, s_rwkv, shift_prev, w_in, gla_gate_w2, gla_gate_b, gla_norm,
                 rwkv_mu, rwkv_w0, rwkv_w2, rwkv_a0, rwkv_a2, rwkv_g2, rwkv_k_k, rwkv_k_a,
                 rwkv_r_k, rwkv_gn_w, rwkv_gn_b, w_out, chunked):
    B, T, _ = h.shape
    f32 = jnp.float32
    p = jnp.einsum('btd,dc->btc', h, w_in).astype(f32)
    gla_p, rwkv_p = p[..., :GLA_COLS], p[..., GLA_COLS:]

    q, k, v, gl, og = split_cols(gla_p, GLA_SPLITS)
    logd = jax.nn.log_sigmoid(gl @ gla_gate_w2 + gla_gate_b) / GLA_GATE_NORMALIZER
    q = q.reshape(B, T, GLA_HEADS, GLA_DK) * (GLA_DK ** -0.5)
    k = k.reshape(B, T, GLA_HEADS, GLA_DK)
    v = v.reshape(B, T, GLA_HEADS, GLA_DV)
    logd = logd.reshape(B, T, GLA_HEADS, GLA_DK)
    s0 = s_gla.astype(f32)
    if chunked:
        o, s_gla_new = gla_chunked(q, k, v, logd, s0)
    else:
        o, s_gla_new = gla_recurrent(q, k, v, logd, s0)
    o = rmsnorm(o, gla_norm, GLA_NORM_EPS) * jax.nn.silu(og).reshape(B, T, GLA_HEADS, GLA_DV)
    o_gla = o.reshape(B, T, GLA_WIDTH)

    prev = jnp.concatenate([shift_prev[:, None, :].astype(f32), rwkv_p[:, :-1]], axis=1)
    xs = rwkv_p + (prev - rwkv_p) * rwkv_mu
    new_shift = rwkv_p[:, -1]
    r, wl, k7, v7, al, gl7 = split_cols(xs, RWKV_SPLITS)
    w = -jax.nn.softplus(-(rwkv_w0 + jnp.tanh(wl) @ rwkv_w2)) - 0.5
    decay = jnp.exp(-jnp.exp(w))
    a = jax.nn.sigmoid(rwkv_a0 + al @ rwkv_a2)
    g = jax.nn.sigmoid(gl7) @ rwkv_g2
    heads = lambda t: t.reshape(B, T, RWKV_HEADS, RWKV_HEAD)
    kk = heads(k7 * rwkv_k_k)
    kk = kk / jnp.maximum(jnp.sqrt(jnp.sum(kk * kk, axis=-1, keepdims=True)), 1e-12)
    k7 = k7 * (1.0 + (a - 1.0) * rwkv_k_a)
    r, decay, k7, v7, a = heads(r), heads(decay), heads(k7), heads(v7), heads(a)
    y, s_rwkv_new = rwkv7_scan(r, decay, k7, v7, kk, a, s_rwkv.astype(f32))
    mu = jnp.mean(y, axis=-1, keepdims=True)
    var = jnp.mean(jnp.square(y - mu), axis=-1, keepdims=True)
    y = (y - mu) * lax.rsqrt(var + RWKV_GN_EPS) * rwkv_gn_w.reshape(RWKV_HEADS, RWKV_HEAD) \
        + rwkv_gn_b.reshape(RWKV_HEADS, RWKV_HEAD)
    bonus = jnp.sum(r * k7 * rwkv_r_k, axis=-1, keepdims=True) * v7
    o_rwkv = (y + bonus).reshape(B, T, RWKV_WIDTH) * g

    out = jnp.concatenate([o_gla, o_rwkv], axis=-1) @ w_out
    return out, s_gla_new, s_rwkv_new, new_shift


def hier_moe(h, w_router_group, w_router_expert, w_exp_gate, w_exp_up, w_exp_down):
    B, T, D = h.shape
    hf = h.reshape(B * T, D)
    p_group = jax.nn.softmax((hf @ w_router_group).astype(jnp.float32), axis=-1)
    g_sel = jnp.argmax(p_group, axis=-1)
    p_sel = jnp.take_along_axis(p_group, g_sel[:, None], axis=-1)
    logits_e = (hf @ w_router_expert).astype(jnp.float32).reshape(-1, N_GROUPS, EXPERTS_PER_GROUP)
    logits_e = jnp.take_along_axis(logits_e, g_sel[:, None, None], axis=1)[:, 0]
    top_v, top_i = lax.top_k(logits_e, TOP_K_IN_GROUP)
    w_top = jax.nn.softmax(top_v, axis=-1) * p_sel
    comb = jnp.einsum('nk,nke->ne', w_top, jax.nn.one_hot(top_i, EXPERTS_PER_GROUP, dtype=jnp.float32))
    out = jnp.zeros((B * T, D), jnp.float32)
    for grp in range(N_GROUPS):
        wgt = comb * (g_sel == grp)[:, None]
        hg = jnp.einsum('nd,edf->enf', hf, w_exp_gate[grp])
        hu = jnp.einsum('nd,edf->enf', hf, w_exp_up[grp])
        hid = jax.nn.silu(hg) * hu * wgt.T[:, :, None]
        out = out + jnp.einsum('enf,efd->nd', hid, w_exp_down[grp])
    return out.reshape(B, T, D)


def setup_inputs(seed: int = 0) -> dict:
    key = jax.random.key(seed)
    ks = iter(jax.random.split(key, 48))
    nrm = lambda shape, scale: jax.random.normal(next(ks), shape, jnp.float32) * scale
    unif = lambda shape, lo, hi: jax.random.uniform(next(ks), shape, jnp.float32, lo, hi)
    D = D_MODEL
    G, E, F = N_GROUPS, EXPERTS_PER_GROUP, EXPERT_FF
    d = {}
    d['x_prompt'] = nrm((BATCH, SEQ, D), 1.0)
    d['x_sample'] = nrm((DEC_BATCH, DEC_SEQ, D), 1.0)
    d['state_gla'] = nrm((DEPTH, DEC_BATCH, GLA_HEADS, GLA_DK, GLA_DV), 0.5)
    d['state_rwkv'] = nrm((DEPTH, DEC_BATCH, RWKV_HEADS, RWKV_HEAD, RWKV_HEAD), 0.3)
    d['state_shift'] = nrm((DEPTH, DEC_BATCH, RWKV_COLS), 1.0)
    d['c_prompt'] = nrm((BATCH, D), 1.0)
    d['c_sample'] = nrm((DEC_BATCH, D), 1.0)
    d['w_ada'] = nrm((DEPTH, D, 6 * D), 0.5 * D ** -0.5)
    d['b_ada'] = nrm((DEPTH, 6 * D), 0.01)
    d['norm_mix'] = 1.0 + nrm((DEPTH, D), 0.01)
    d['norm_ffn'] = 1.0 + nrm((DEPTH, D), 0.01)
    d['w_in'] = nrm((DEPTH, D, IN_COLS), D ** -0.5)
    d['gla_gate_w2'] = nrm((DEPTH, GLA_GATE_RANK, GLA_KEY_WIDTH), GLA_GATE_RANK ** -0.5)
    d['gla_gate_b'] = nrm((DEPTH, GLA_KEY_WIDTH), 0.1)
    d['gla_norm'] = 1.0 + nrm((DEPTH, GLA_DV), 0.01)
    d['rwkv_mu'] = unif((DEPTH, RWKV_COLS), 0.0, 1.0)
    d['rwkv_w0'] = unif((DEPTH, RWKV_WIDTH), -5.0, -1.0)
    d['rwkv_w2'] = nrm((DEPTH, RWKV_W_RANK, RWKV_WIDTH), 0.1 * RWKV_W_RANK ** -0.5)
    d['rwkv_a0'] = nrm((DEPTH, RWKV_WIDTH), 0.1)
    d['rwkv_a2'] = nrm((DEPTH, RWKV_A_RANK, RWKV_WIDTH), RWKV_A_RANK ** -0.5)
    d['rwkv_g2'] = nrm((DEPTH, RWKV_G_RANK, RWKV_WIDTH), RWKV_G_RANK ** -0.5)
    d['rwkv_k_k'] = 0.85 + nrm((DEPTH, RWKV_WIDTH), 0.02)
    d['rwkv_k_a'] = 1.0 + nrm((DEPTH, RWKV_WIDTH), 0.02)
    d['rwkv_r_k'] = nrm((DEPTH, RWKV_HEADS, RWKV_HEAD), 0.1)
    d['rwkv_gn_w'] = 1.0 + nrm((DEPTH, RWKV_WIDTH), 0.01)
    d['rwkv_gn_b'] = nrm((DEPTH, RWKV_WIDTH), 0.01)
    d['w_out'] = nrm((DEPTH, MIX_WIDTH, D), MIX_WIDTH ** -0.5)
    d['w_router_group'] = nrm((DEPTH, D, G), D ** -0.5)
    d['w_router_expert'] = nrm((DEPTH, D, G * E), D ** -0.5)
    d['w_exp_gate'] = nrm((DEPTH, G, E, D, F), D ** -0.5)
    d['w_exp_up'] = nrm((DEPTH, G, E, D, F), D ** -0.5)
    d['w_exp_down'] = nrm((DEPTH, G, E, F, D), F ** -0.5)
    d['norm_final'] = 1.0 + nrm((D,), 0.01)
    d['w_ada_final'] = nrm((D, 2 * D), 0.5 * D ** -0.5)
    d['b_ada_final'] = nrm((2 * D,), 0.01)
    return d


def reference(x_prompt, x_sample, state_gla, state_rwkv, state_shift, c_prompt, c_sample,
              w_ada, b_ada, norm_mix, norm_ffn, w_in, gla_gate_w2, gla_gate_b, gla_norm,
              rwkv_mu, rwkv_w0, rwkv_w2, rwkv_a0, rwkv_a2, rwkv_g2, rwkv_k_k, rwkv_k_a, rwkv_r_k,
              rwkv_gn_w, rwkv_gn_b, w_out, w_router_group, w_router_expert,
              w_exp_gate, w_exp_up, w_exp_down, norm_final, w_ada_final, b_ada_final):
    def trunk(x, c, gla0, rwkv0, shift0, chunked):
        new_gla, new_rwkv, new_shift = [], [], []
        for l in range(DEPTH):
            sh1, sc1, gt1, sh2, sc2, gt2 = adaln(c, w_ada[l], b_ada[l], 6)
            h = rmsnorm(x, norm_mix[l]) * (1.0 + sc1) + sh1
            mix, s_g, s_r, s_s = token_mixers(
                h, gla0[l], rwkv0[l], shift0[l], w_in[l], gla_gate_w2[l], gla_gate_b[l], gla_norm[l],
                rwkv_mu[l], rwkv_w0[l], rwkv_w2[l], rwkv_a0[l], rwkv_a2[l], rwkv_g2[l], rwkv_k_k[l],
                rwkv_k_a[l], rwkv_r_k[l], rwkv_gn_w[l], rwkv_gn_b[l], w_out[l], chunked)
            x = x + gt1 * mix
            h = rmsnorm(x, norm_ffn[l]) * (1.0 + sc2) + sh2
            x = x + gt2 * hier_moe(h, w_router_group[l], w_router_expert[l],
                                   w_exp_gate[l], w_exp_up[l], w_exp_down[l])
            new_gla.append(s_g)
            new_rwkv.append(s_r)
            new_shift.append(s_s)
        sh_f, sc_f = adaln(c, w_ada_final, b_ada_final, 2)
        y = rmsnorm(x, norm_final) * (1.0 + sc_f) + sh_f
        return y, jnp.stack(new_gla), jnp.stack(new_rwkv), jnp.stack(new_shift)

    bp = x_prompt.shape[0]
    gla0_p = jnp.zeros((DEPTH, bp, GLA_HEADS, GLA_DK, GLA_DV), jnp.float32)
    rwkv0_p = jnp.zeros((DEPTH, bp, RWKV_HEADS, RWKV_HEAD, RWKV_HEAD), jnp.float32)
    shift0_p = jnp.zeros((DEPTH, bp, RWKV_COLS), jnp.float32)
    y_prompt, sg_p, sr_p, ss_p = trunk(x_prompt, c_prompt, gla0_p, rwkv0_p, shift0_p, True)
    y_sample, sg_s, sr_s, ss_s = trunk(x_sample, c_sample, state_gla, state_rwkv, state_shift, False)
    return (y_prompt, y_sample, sg_p, sr_p, ss_p, sg_s, sr_s, ss_s)
```

```python
import functools

import jax
import jax.numpy as jnp
from jax import lax
from jax.experimental import pallas as pl
from jax.experimental.pallas import tpu as pltpu

F32 = jnp.float32
BF16 = jnp.bfloat16

D_MODEL = 2048
GLA_HEADS = 4
GLA_DK = 128
GLA_DV = 256
GLA_KEY_WIDTH = GLA_HEADS * GLA_DK
GLA_WIDTH = GLA_HEADS * GLA_DV
GLA_GATE_RANK = 16
GLA_GATE_NORMALIZER = 16.0
RWKV_HEAD = 64
RWKV_HEADS = 16
RWKV_WIDTH = RWKV_HEAD * RWKV_HEADS
RWKV_W_RANK = 64
RWKV_A_RANK = 64
RWKV_G_RANK = 128
N_GROUPS = 4
EXPERTS_PER_GROUP = 8
N_EXPERTS = N_GROUPS * EXPERTS_PER_GROUP
EXPERT_FF = 512
NORM_EPS = 1e-6
GLA_NORM_EPS = 1e-5
RWKV_GN_EPS = 64e-5

LANES = 128
SUBLANES = 8
VMEM_LIMIT_BYTES = 56 * 1024 * 1024

CHUNK = 64
SUB = 16
RWKV_GROUP = 4
MOE_TILE = 256

_SEGS = (("q", 512), ("k", 512), ("v", 1024), ("gl", 16), ("og", 1024),
         ("r", 1024), ("wl", 64), ("k7", 1024), ("v7", 1024), ("al", 64), ("gl7", 128))


def _round_up(n, m):
    return (n + m - 1) // m * m


def _seg_layout():
    src, dst, out = 0, 0, {}
    for name, w in _SEGS:
        wa = _round_up(w, LANES)
        out[name] = (src, w, dst, wa)
        src += w
        dst += wa
    return out, src, dst


SEG, IN_COLS, IN_COLS_ALIGNED = _seg_layout()
GLA_COLS = SEG["r"][0]
RWKV_COLS = IN_COLS - GLA_COLS

NN = ((1,), (0,))
NT = ((1,), (1,))


def _dot(a, b, dims=NN):
    return lax.dot_general(a, b, (dims, ((), ())), preferred_element_type=F32)


def _split2(x):
    hi = x.astype(BF16)
    lo = (x - hi.astype(F32)).astype(BF16)
    return hi, lo


def _dotp(a, b, dims=NN, passes=1):
    if passes == 1:
        return _dot(a.astype(BF16), b.astype(BF16), dims)
    ah, al = _split2(a)
    bh, bl = _split2(b)
    return _dot(ah, bh, dims) + (_dot(ah, bl, dims) + _dot(al, bh, dims))


def _cumsum_rows(x):
    n = x.shape[0]
    row = lax.broadcasted_iota(jnp.int32, (n, n), 0)
    col = lax.broadcasted_iota(jnp.int32, (n, n), 1)
    tri = (row >= col).astype(BF16)
    x1 = x.astype(BF16)
    r1 = x - x1.astype(F32)
    x2 = r1.astype(BF16)
    x3 = (r1 - x2.astype(F32)).astype(BF16)
    return _dot(tri, x1) + (_dot(tri, x2) + _dot(tri, x3))


def _largest_tile(n, cap, mult=SUBLANES):
    if n <= cap:
        return n
    best = None
    for t in range(mult, cap + 1, mult):
        if n % t == 0:
            best = t
    assert best is not None, (n, cap)
    return best


def _params(*sem):
    return pltpu.CompilerParams(dimension_semantics=sem, vmem_limit_bytes=VMEM_LIMIT_BYTES)


def _mm_kernel(a_ref, w_ref, o_ref, *, passes):
    o_ref[...] = _dotp(a_ref[...], w_ref[...], NN, passes).astype(o_ref.dtype)


def matmul(a, w, *, a_col_block=0, k=None, tm_cap=1664, tn_cap=512, passes=1, out_dtype=F32):
    m = a.shape[0]
    k = w.shape[0] if k is None else k
    assert w.shape[0] == k
    n = w.shape[1]
    tm = _largest_tile(m, tm_cap)
    tn = _largest_tile(n, tn_cap, LANES) if n % LANES == 0 else n
    return pl.pallas_call(
        functools.partial(_mm_kernel, passes=passes),
        grid=(m // tm, n // tn),
        in_specs=[pl.BlockSpec((tm, k), lambda i, j: (i, a_col_block)),
                  pl.BlockSpec((k, tn), lambda i, j: (0, j))],
        out_specs=pl.BlockSpec((tm, tn), lambda i, j: (i, j)),
        out_shape=jax.ShapeDtypeStruct((m, n), out_dtype),
        compiler_params=_params("parallel", "parallel"),
    )(a, w)


def _rwkv_chunk_kernel(r_ref, lw_ref, k_ref, v_ref, kk_ref, a_ref, y_ref, s_ref, state, *, passes):
    c_idx = pl.program_id(1)
    n_tok, width = lw_ref.shape[1], lw_ref.shape[2]
    gw = RWKV_GROUP * RWKV_HEAD

    @pl.when(c_idx == 0)
    def _():
        state[...] = jnp.zeros_like(state)

    lw = lw_ref[0]
    cum = _cumsum_rows(lw)
    cum_end = cum[n_tok - 1:n_tok, :]
    g_in = jnp.exp(cum)
    g_prev = jnp.exp(cum - lw)
    g_inv = jnp.exp(-cum)
    g_end = jnp.exp(cum_end - cum)
    g_tot = jnp.exp(cum_end)
    kk, a, k, v, r = kk_ref[0], a_ref[0], k_ref[0], v_ref[0], r_ref[0]
    beta = kk * a
    a_hat = -kk * g_prev
    r_hat = r * g_in
    b_hat = beta * g_inv
    k_hat = k * g_inv
    b_end = beta * g_end
    k_end = k * g_end

    lane = lax.broadcasted_iota(jnp.int32, (n_tok, gw), 1)
    tok = lax.broadcasted_iota(jnp.int32, (n_tok, gw), 0)
    lane_head = lane // RWKV_HEAD
    src_tok = lane % RWKV_HEAD
    strict = tok > src_tok
    incl = tok >= src_tok

    def bd(y):
        return jnp.concatenate([jnp.where(lane_head == h, y, 0.0) for h in range(RWKV_GROUP)], axis=0)

    for g in range(width // gw):
        sl = slice(g * gw, (g + 1) * gw)
        s0 = state[:, sl]
        lhs2 = jnp.concatenate([a_hat[:, sl], r_hat[:, sl]], axis=0)
        abrb = _dotp(lhs2, bd(b_hat[:, sl]), NT, passes)
        akrk = _dotp(lhs2, bd(k_hat[:, sl]), NT, passes)
        asrs = _dotp(lhs2, bd(s0), NT, passes)
        ab = jnp.where(strict, abrb[:n_tok], 0.0)
        ak = jnp.where(strict, akrk[:n_tok], 0.0)
        rb = jnp.where(incl, abrb[n_tok:], 0.0)
        rk = jnp.where(incl, akrk[n_tok:], 0.0)
        vg = v[:, sl]
        bd_v = bd(vg)
        x = asrs[:n_tok] + _dotp(ak, bd_v, NN, passes)
        p = ab
        n_sq = n_tok.bit_length() - 1
        for it in range(n_sq):
            if it < n_sq - 1:
                both = _dotp(p, jnp.concatenate([bd(p), bd(x)], axis=1), NN, passes)
                x = x + both[:, gw:]
                p = both[:, :gw]
            else:
                x = x + _dotp(p, bd(x), NN, passes)
        y = asrs[n_tok:] + _dotp(jnp.concatenate([rb, rk], axis=1),
                                 jnp.concatenate([bd(x), bd_v], axis=0), NN, passes)
        y_ref[0, :, sl] = y
        ev_t = jnp.concatenate([x, vg], axis=0).T
        full = _dotp(ev_t, jnp.concatenate([b_end[:, sl], k_end[:, sl]], axis=0), NN, passes)
        upd = s0 * g_tot[:, sl]
        for h in range(RWKV_GROUP):
            upd = upd + jnp.where(lane_head == h, full[h * RWKV_HEAD:(h + 1) * RWKV_HEAD, :], 0.0)
        state[:, sl] = upd

    @pl.when(c_idx == pl.num_programs(1) - 1)
    def _():
        s_ref[0] = state[...]


def rwkv7_chunked(r, lw, k, v, kk, a, *, passes=1):
    b, t, w = r.shape
    assert CHUNK == RWKV_HEAD and t % CHUNK == 0 and w % (RWKV_GROUP * RWKV_HEAD) == 0
    tok_spec = pl.BlockSpec((1, CHUNK, w), lambda i, c: (i, c, 0))
    return pl.pallas_call(
        functools.partial(_rwkv_chunk_kernel, passes=passes),
        grid=(b, t // CHUNK),
        in_specs=[tok_spec] * 6,
        out_specs=[tok_spec, pl.BlockSpec((1, RWKV_HEAD, w), lambda i, c: (i, 0, 0))],
        out_shape=[jax.ShapeDtypeStruct((b, t, w), F32), jax.ShapeDtypeStruct((b, RWKV_HEAD, w), F32)],
        scratch_shapes=[pltpu.VMEM((RWKV_HEAD, w), F32)],
        compiler_params=_params("parallel", "arbitrary"),
    )(r, lw, k, v, kk, a)


def _gla_chunk_kernel(q_ref, k_ref, g_ref, v_ref, o_ref, s_ref, state):
    c_idx = pl.program_id(1)
    n_tok = q_ref.shape[1]

    @pl.when(c_idx == 0)
    def _():
        state[...] = jnp.zeros_like(state)

    b_all = _cumsum_rows(g_ref[0])
    row_k = lax.broadcasted_iota(jnp.int32, (n_tok, GLA_DK), 0)
    row_s = lax.broadcasted_iota(jnp.int32, (SUB, GLA_DK), 0)

    for h in range(GLA_HEADS):
        ks = slice(h * GLA_DK, (h + 1) * GLA_DK)
        vs = slice(h * GLA_DV, (h + 1) * GLA_DV)
        q, k, b, v = q_ref[0, :, ks], k_ref[0, :, ks], b_all[:, ks], v_ref[0, :, vs]
        b_last = b[n_tok - 1:n_tok, :]
        st = state[h]
        o_inter = _dotp(q * jnp.exp(b), st, NT)
        blocks = []
        for i in range(n_tok // SUB):
            lo = i * SUB
            qs, kb, bs, vb = q[lo:lo + SUB], k[lo:lo + SUB], b[lo:lo + SUB], v[lo:lo + SUB]
            acc = o_inter[lo:lo + SUB]
            if i > 0:
                b_ref = b[lo - 1:lo, :]
                q_hat = qs * jnp.exp(bs - b_ref)
                earlier = row_k < lo
                k_hat = jnp.where(earlier, k * jnp.exp(jnp.where(earlier, b_ref - b, 0.0)), 0.0)
                att = _dotp(q_hat, k_hat, NT)
                acc = acc + _dotp(att, v, NN)
            for j in range(SUB):
                valid = row_s >= j
                w_j = jnp.where(valid, kb[j:j + 1] * jnp.exp(jnp.where(valid, bs - bs[j:j + 1], 0.0)), 0.0)
                col = jnp.sum(qs * w_j, axis=1, keepdims=True)
                acc = acc + col * vb[j:j + 1]
            blocks.append(acc)
        o_ref[0, :, vs] = jnp.concatenate(blocks, axis=0)
        k_dec = k * jnp.exp(b_last - b)
        state[h] = st * jnp.exp(b_last) + _dotp(v.T, k_dec, NN)

    @pl.when(c_idx == pl.num_programs(1) - 1)
    def _():
        s_ref[0] = state[...]


def gla_chunked(q, k, logd, v):
    b, t, _ = q.shape
    assert t % CHUNK == 0
    kspec = pl.BlockSpec((1, CHUNK, GLA_KEY_WIDTH), lambda i, c: (i, c, 0))
    vspec = pl.BlockSpec((1, CHUNK, GLA_WIDTH), lambda i, c: (i, c, 0))
    return pl.pallas_call(
        _gla_chunk_kernel,
        grid=(b, t // CHUNK),
        in_specs=[kspec, kspec, kspec, vspec],
        out_specs=[vspec, pl.BlockSpec((1, GLA_HEADS, GLA_DV, GLA_DK), lambda i, c: (i, 0, 0, 0))],
        out_shape=[jax.ShapeDtypeStruct((b, t, GLA_WIDTH), F32),
                   jax.ShapeDtypeStruct((b, GLA_HEADS, GLA_DV, GLA_DK), F32)],
        scratch_shapes=[pltpu.VMEM((GLA_HEADS, GLA_DV, GLA_DK), F32)],
        compiler_params=_params("parallel", "arbitrary"),
    )(q, k, logd, v)


def _moe_kernel(tile_expert_ref, tile_valid_ref, x_ref, rw_ref, wg_ref, wu_ref, wd_ref, o_ref):
    del tile_expert_ref
    i = pl.program_id(0)

    @pl.when(tile_valid_ref[i] != 0)
    def _():
        x = x_ref[...]
        gate = _dot(x, wg_ref[0].astype(BF16))
        up = _dot(x, wu_ref[0].astype(BF16))
        hid = gate * jax.nn.sigmoid(gate) * up * rw_ref[...]
        o_ref[...] = _dot(hid.astype(BF16), wd_ref[0].astype(BF16))

    @pl.when(tile_valid_ref[i] == 0)
    def _():
        o_ref[...] = jnp.zeros_like(o_ref)


def moe_experts(x_sorted, row_w, tile_expert, tile_valid, w_gate, w_up, w_down):
    p, d = x_sorted.shape
    n_e, _, ff = w_gate.shape
    n_tiles = p // MOE_TILE
    grid_spec = pltpu.PrefetchScalarGridSpec(
        num_scalar_prefetch=2,
        grid=(n_tiles,),
        in_specs=[pl.BlockSpec((MOE_TILE, d), lambda i, te, tv: (i, 0)),
                  pl.BlockSpec((MOE_TILE, 1), lambda i, te, tv: (i, 0)),
                  pl.BlockSpec((1, d, ff), lambda i, te, tv: (te[i], 0, 0)),
                  pl.BlockSpec((1, d, ff), lambda i, te, tv: (te[i], 0, 0)),
                  pl.BlockSpec((1, ff, d), lambda i, te, tv: (te[i], 0, 0))],
        out_specs=pl.BlockSpec((MOE_TILE, d), lambda i, te, tv: (i, 0)),
    )
    return pl.pallas_call(
        _moe_kernel,
        grid_spec=grid_spec,
        out_shape=jax.ShapeDtypeStruct((p, d), F32),
        compiler_params=_params("arbitrary"),
    )(tile_expert, tile_valid, x_sorted, row_w, w_gate, w_up, w_down)


def _route(logits):
    p_group = jax.nn.softmax(logits[:, :N_GROUPS], axis=-1)
    g_sel = jnp.argmax(p_group, axis=-1)
    p_sel = jnp.take_along_axis(p_group, g_sel[:, None], axis=-1)
    logits_e = logits[:, N_GROUPS:].reshape(-1, N_GROUPS, EXPERTS_PER_GROUP)
    logits_e = jnp.take_along_axis(logits_e, g_sel[:, None, None], axis=1)[:, 0]
    top_v, top_i = lax.top_k(logits_e, 2)
    w_top = jax.nn.softmax(top_v, axis=-1) * p_sel
    expert = g_sel[:, None].astype(jnp.int32) * EXPERTS_PER_GROUP + top_i.astype(jnp.int32)
    return expert, w_top


def _moe_plan(expert, w_top):
    n = expert.shape[0]
    e_flat = expert.reshape(-1)
    one_hot = (e_flat[:, None] == jnp.arange(N_EXPERTS, dtype=jnp.int32)[None, :]).astype(jnp.int32)
    counts = jnp.sum(one_hot, axis=0)
    padded = (counts + MOE_TILE - 1) // MOE_TILE * MOE_TILE
    ends = jnp.cumsum(padded)
    starts = ends - padded
    rank = jnp.sum(jnp.cumsum(one_hot, axis=0) * one_hot, axis=1) - 1
    pos = starts[e_flat] + rank
    p_max = _round_up(2 * n, MOE_TILE) + N_EXPERTS * MOE_TILE
    n_tiles = p_max // MOE_TILE
    token = jnp.arange(2 * n, dtype=jnp.int32) // 2
    src_token = jnp.zeros((p_max,), jnp.int32).at[pos].set(token)
    row_w = jnp.zeros((p_max,), F32).at[pos].set(w_top.reshape(-1))
    tile_start = jnp.arange(n_tiles, dtype=jnp.int32) * MOE_TILE
    tile_valid = (tile_start < ends[-1]).astype(jnp.int32)
    tile_expert = jnp.searchsorted(ends, tile_start, side="right").astype(jnp.int32)
    last_valid = jnp.maximum(jnp.sum(tile_valid) - 1, 0)
    tile_expert = jnp.where(tile_valid != 0, tile_expert, tile_expert[last_valid])
    tile_expert = jnp.minimum(tile_expert, N_EXPERTS - 1)
    return pos.reshape(n, 2), src_token, row_w[:, None], tile_expert, tile_valid


def _align_cols(a, axis=-1, names=None):
    names = [s[0] for s in _SEGS] if names is None else names
    base = SEG[names[0]][0]
    parts = []
    for nm in names:
        src, w, _, wa = SEG[nm]
        part = lax.slice_in_dim(a, src - base, src - base + w, axis=axis % a.ndim)
        if wa != w:
            pad = [(0, 0)] * a.ndim
            pad[axis % a.ndim] = (0, wa - w)
            part = jnp.pad(part, pad)
        parts.append(part)
    return jnp.concatenate(parts, axis=axis)


_RWKV_NAMES = ["r", "wl", "k7", "v7", "al", "gl7"]
RWKV_BASE = SEG["r"][2]
RWKV_COLS_ALIGNED = IN_COLS_ALIGNED - RWKV_BASE


def _unalign_rwkv(a):
    parts = []
    for nm in _RWKV_NAMES:
        _, w, dst, _ = SEG[nm]
        parts.append(a[..., dst - RWKV_BASE:dst - RWKV_BASE + w])
    return jnp.concatenate(parts, axis=-1)


def _seg(a, name, base=0):
    _, _, dst, wa = SEG[name]
    return a[..., dst - base:dst - base + wa]


def _pad_rows(w, rows):
    return jnp.pad(w, ((0, rows - w.shape[0]), (0, 0)))


def _rmsnorm(x, g, eps):
    return x * lax.rsqrt(jnp.mean(x * x, axis=-1, keepdims=True) + eps) * g


def kernel(x_prompt, x_sample, state_gla, state_rwkv, state_shift, c_prompt, c_sample, w_ada, b_ada, norm_mix, norm_ffn, w_in, gla_gate_w2, gla_gate_b, gla_norm, rwkv_mu, rwkv_w0, rwkv_w2, rwkv_a0, rwkv_a2, rwkv_g2, rwkv_k_k, rwkv_k_a, rwkv_r_k, rwkv_gn_w, rwkv_gn_b, w_out, w_router_group, w_router_expert, w_exp_gate, w_exp_up, w_exp_down, norm_final, w_ada_final, b_ada_final):
    assert w_ada.shape[0] == 1, "single-layer step"
    bp, t, d = x_prompt.shape
    bs = x_sample.shape[0]
    assert x_sample.shape[1] == 1
    n_p = bp * t
    hi = lax.Precision.HIGHEST

    c_act = jax.nn.silu(jnp.concatenate([c_prompt, c_sample], axis=0))
    mod = matmul(c_act, w_ada[0]) + b_ada[0]
    mod_f = matmul(c_act, w_ada_final) + b_ada_final
    mods_p = [m[:, None, :] for m in jnp.split(mod[:bp], 6, axis=-1)]
    mods_s = jnp.split(mod[bp:], 6, axis=-1)
    shf_p, scf_p = [m[:, None, :] for m in jnp.split(mod_f[:bp], 2, axis=-1)]
    shf_s, scf_s = jnp.split(mod_f[bp:], 2, axis=-1)
    xs_rows = x_sample[:, 0, :]

    def rows(p_part, s_part):
        return jnp.concatenate([p_part.reshape(n_p, -1), s_part], axis=0)

    h1 = rows(_rmsnorm(x_prompt, norm_mix[0], NORM_EPS) * (1.0 + mods_p[1]) + mods_p[0],
              _rmsnorm(xs_rows, norm_mix[0], NORM_EPS) * (1.0 + mods_s[1]) + mods_s[0]).astype(BF16)
    proj = matmul(h1, _align_cols(w_in[0]).astype(BF16))

    gate_w2 = _pad_rows(gla_gate_w2[0], LANES)
    logd = jax.nn.log_sigmoid(matmul(proj, gate_w2, a_col_block=SEG["gl"][2] // LANES, k=LANES)
                              + gla_gate_b[0]) / GLA_GATE_NORMALIZER
    q = _seg(proj, "q") * (GLA_DK ** -0.5)
    k = _seg(proj, "k")
    v = _seg(proj, "v")
    og = _seg(proj, "og")
    split_p = lambda a: a[:n_p].reshape(bp, t, -1)
    o_p, gla_t_p = gla_chunked(split_p(q), split_p(k), split_p(logd), split_p(v))
    new_gla_p = jnp.swapaxes(gla_t_p, -1, -2)
    heads = lambda a, n: a.reshape(a.shape[0], n, -1)
    q_s, k_s, v_s, g_s = heads(q[n_p:], GLA_HEADS), heads(k[n_p:], GLA_HEADS), heads(v[n_p:], GLA_HEADS), \
        heads(logd[n_p:], GLA_HEADS)
    new_gla_s = jnp.exp(g_s)[..., None] * state_gla[0] + k_s[..., :, None] * v_s[..., None, :]
    o_s = jnp.einsum("bhd,bhde->bhe", q_s, new_gla_s, precision=hi)
    o = rows(o_p, o_s.reshape(bs, -1)).reshape(-1, GLA_HEADS, GLA_DV)
    o = _rmsnorm(o, gla_norm[0], GLA_NORM_EPS) * jax.nn.silu(og).reshape(-1, GLA_HEADS, GLA_DV)
    o_gla = o.reshape(-1, GLA_WIDTH)

    rp = proj[:, RWKV_BASE:]
    rp_p = rp[:n_p].reshape(bp, t, -1)
    rp_s = rp[n_p:]
    mu = _align_cols(rwkv_mu[0], names=_RWKV_NAMES)
    prev_p = jnp.concatenate([jnp.zeros_like(rp_p[:, :1]), rp_p[:, :-1]], axis=1)
    prev_s = _align_cols(state_shift[0], names=_RWKV_NAMES)
    xs = rows(rp_p + (prev_p - rp_p) * mu, rp_s + (prev_s - rp_s) * mu)
    new_shift_p = _unalign_rwkv(rp_p[:, -1])
    new_shift_s = _unalign_rwkv(rp_s)
    sg = lambda name: _seg(xs, name, RWKV_BASE)
    r = sg("r")
    w_pre = rwkv_w0[0] + matmul(jnp.tanh(sg("wl")), _pad_rows(rwkv_w2[0], LANES))
    log_decay = -jnp.exp(-jax.nn.softplus(-w_pre) - 0.5)
    a = jax.nn.sigmoid(rwkv_a0[0] + matmul(sg("al"), _pad_rows(rwkv_a2[0], LANES)))
    g = matmul(jax.nn.sigmoid(sg("gl7")), rwkv_g2[0])
    k7 = sg("k7")
    v7 = sg("v7")
    kk = (k7 * rwkv_k_k[0]).reshape(-1, RWKV_HEADS, RWKV_HEAD)
    kk = (kk / jnp.maximum(jnp.sqrt(jnp.sum(kk * kk, axis=-1, keepdims=True)), 1e-12)).reshape(-1, RWKV_WIDTH)
    k7 = k7 * (1.0 + (a - 1.0) * rwkv_k_a[0])
    y_p, rwkv_t_p = rwkv7_chunked(*(split_p(z) for z in (r, log_decay, k7, v7, kk, a)))
    new_rwkv_p = rwkv_t_p.reshape(bp, RWKV_HEAD, RWKV_HEADS, RWKV_HEAD).transpose(0, 2, 1, 3)
    hs = lambda z: z[n_p:].reshape(bs, RWKV_HEADS, RWKV_HEAD)
    r_s, w_s, k_s7, v_s7, kk_s, a_s = hs(r), jnp.exp(hs(log_decay)), hs(k7), hs(v7), hs(kk), hs(a)
    s_prev = state_rwkv[0]
    sa = jnp.einsum("bhij,bhj->bhi", s_prev, -kk_s, precision=hi)
    new_rwkv_s = s_prev * w_s[:, :, None, :] + sa[..., None] * (kk_s * a_s)[:, :, None, :] \
        + v_s7[..., None] * k_s7[:, :, None, :]
    y_s = jnp.einsum("bhij,bhj->bhi", new_rwkv_s, r_s, precision=hi)
    y = rows(y_p, y_s.reshape(bs, -1)).reshape(-1, RWKV_HEADS, RWKV_HEAD)
    y_mu = jnp.mean(y, axis=-1, keepdims=True)
    y_var = jnp.mean(jnp.square(y - y_mu), axis=-1, keepdims=True)
    y = (y - y_mu) * lax.rsqrt(y_var + RWKV_GN_EPS) * rwkv_gn_w[0].reshape(RWKV_HEADS, RWKV_HEAD) \
        + rwkv_gn_b[0].reshape(RWKV_HEADS, RWKV_HEAD)
    r_h, k_h, v_h = (z.reshape(-1, RWKV_HEADS, RWKV_HEAD) for z in (r, k7, v7))
    bonus = jnp.sum(r_h * k_h * rwkv_r_k[0], axis=-1, keepdims=True) * v_h
    o_rwkv = (y + bonus).reshape(-1, RWKV_WIDTH) * g

    mix = matmul(jnp.concatenate([o_gla, o_rwkv], axis=-1).astype(BF16), w_out[0])
    x1_p = x_prompt + mods_p[2] * mix[:n_p].reshape(bp, t, d)
    x1_s = xs_rows + mods_s[2] * mix[n_p:]

    h2 = rows(_rmsnorm(x1_p, norm_ffn[0], NORM_EPS) * (1.0 + mods_p[4]) + mods_p[3],
              _rmsnorm(x1_s, norm_ffn[0], NORM_EPS) * (1.0 + mods_s[4]) + mods_s[3])
    w_router = jnp.concatenate([w_router_group[0], w_router_expert[0]], axis=-1)
    logits = matmul(h2, w_router, tm_cap=832, passes=3)
    expert, w_top = _route(logits)
    pos, src_token, row_w, tile_expert, tile_valid = _moe_plan(expert, w_top)
    x_sorted = jnp.take(h2.astype(BF16), src_token, axis=0)
    y_sorted = moe_experts(x_sorted, row_w, tile_expert, tile_valid,
                           w_exp_gate[0].reshape(N_EXPERTS, d, EXPERT_FF),
                           w_exp_up[0].reshape(N_EXPERTS, d, EXPERT_FF),
                           w_exp_down[0].reshape(N_EXPERTS, EXPERT_FF, d))
    moe = jnp.take(y_sorted, pos[:, 0], axis=0) + jnp.take(y_sorted, pos[:, 1], axis=0)
    x2_p = x1_p + mods_p[5] * moe[:n_p].reshape(bp, t, d)
    x2_s = x1_s + mods_s[5] * moe[n_p:]

    y_prompt = _rmsnorm(x2_p, norm_final, NORM_EPS) * (1.0 + scf_p) + shf_p
    y_sample = (_rmsnorm(x2_s, norm_final, NORM_EPS) * (1.0 + scf_s) + shf_s)[:, None, :]
    return (y_prompt, y_sample, new_gla_p[None], new_rwkv_p[None], new_shift_p[None],
            new_gla_s[None], new_rwkv_s[None], new_shift_s[None])
```

```python
import functools

import jax
import jax.numpy as jnp
from jax import lax
from jax.experimental import pallas as pl
from jax.experimental.pallas import tpu as pltpu

F32 = jnp.float32
BF16 = jnp.bfloat16

D_MODEL = 2048
GLA_HEADS = 4
GLA_DK = 128
GLA_DV = 256
GLA_KEY_WIDTH = GLA_HEADS * GLA_DK
GLA_WIDTH = GLA_HEADS * GLA_DV
GLA_GATE_NORMALIZER = 16.0
RWKV_HEAD = 64
RWKV_HEADS = 16
RWKV_WIDTH = RWKV_HEAD * RWKV_HEADS
N_GROUPS = 4
EXPERTS_PER_GROUP = 8
N_EXPERTS = N_GROUPS * EXPERTS_PER_GROUP
EXPERT_FF = 512
NORM_EPS = 1e-6
GLA_NORM_EPS = 1e-5
RWKV_GN_EPS = 64e-5

LANES = 128
SUBLANES = 8
VMEM_LIMIT_BYTES = 56 * 1024 * 1024

CHUNK = 64
SUB = 16
RWKV_GROUP = 4
GROUP_W = RWKV_GROUP * RWKV_HEAD
MOE_TILE = 256
MAX_ROW_TILE = 512
NORM_SLAB = 128

_ORIG = (("q", 512), ("k", 512), ("v", 1024), ("gl", 16), ("og", 1024),
         ("r", 1024), ("wl", 64), ("k7", 1024), ("v7", 1024), ("al", 64), ("gl7", 128))
_ALIGNED = ("q", "k", "v", "og", "r", "k7", "v7", "gl", "wl", "al", "gl7")
_RWKV_ORIG = ("r", "wl", "k7", "v7", "al", "gl7")


def _round_up(n, m):
    return (n + m - 1) // m * m


def _layouts():
    src, off = {}, 0
    for name, w in _ORIG:
        src[name] = (off, w)
        off += w
    dst, pos = {}, 0
    for name in _ALIGNED:
        wa = _round_up(src[name][1], LANES)
        dst[name] = (pos, wa)
        pos += wa
    return src, off, dst, pos


SRC, IN_COLS, DST, IN_COLS_ALIGNED = _layouts()
RWKV_SRC_BASE = SRC["r"][0]
BIG_W = 3 * RWKV_WIDTH
SMALL_W = 3 * LANES
assert DST["r"][0] == BIG_W and DST["k7"][0] == BIG_W + RWKV_WIDTH and DST["v7"][0] == BIG_W + 2 * RWKV_WIDTH
assert DST["v"][0] == RWKV_WIDTH and DST["og"][0] == 2 * RWKV_WIDTH and DST["k"][0] == GLA_KEY_WIDTH

NN = ((1,), (0,))
NT = ((1,), (1,))


def _dot(a, b, dims=NN):
    return lax.dot_general(a, b, (dims, ((), ())), preferred_element_type=F32)


def _split2(x):
    hi = x.astype(BF16)
    lo = (x - hi.astype(F32)).astype(BF16)
    return hi, lo


def _dotp(a, b, dims=NN, passes=1):
    if passes == 1:
        return _dot(a.astype(BF16), b.astype(BF16), dims)
    ah, al = _split2(a)
    bh, bl = _split2(b)
    return _dot(ah, bh, dims) + (_dot(ah, bl, dims) + _dot(al, bh, dims))


def _cumsum_rows(x):
    n = x.shape[0]
    row = lax.broadcasted_iota(jnp.int32, (n, n), 0)
    col = lax.broadcasted_iota(jnp.int32, (n, n), 1)
    tri = (row >= col).astype(BF16)
    x1 = x.astype(BF16)
    r1 = x - x1.astype(F32)
    x2 = r1.astype(BF16)
    x3 = (r1 - x2.astype(F32)).astype(BF16)
    return _dot(tri, x1) + (_dot(tri, x2) + _dot(tri, x3))


def _softplus(z):
    return jnp.maximum(z, 0.0) + jnp.log1p(jnp.exp(-jnp.abs(z)))


def _rmsnorm(x, g, eps):
    return x * lax.rsqrt(jnp.mean(x * x, axis=-1, keepdims=True) + eps) * g


def _largest_tile(n, cap, mult=SUBLANES):
    if n <= cap:
        return n
    best = None
    for t in range(mult, cap + 1, mult):
        if n % t == 0:
            best = t
    assert best is not None, (n, cap)
    return best


def _params(*sem):
    return pltpu.CompilerParams(dimension_semantics=sem, vmem_limit_bytes=VMEM_LIMIT_BYTES)


def _mm_kernel(a_ref, w_ref, o_ref):
    o_ref[...] = _dotp(a_ref[...], w_ref[...]).astype(o_ref.dtype)


def matmul(a, w, *, tn_cap=512):
    m, k = a.shape
    n = w.shape[1]
    tn = _largest_tile(n, tn_cap, LANES)
    return pl.pallas_call(
        _mm_kernel,
        grid=(n // tn,),
        in_specs=[pl.BlockSpec((m, k), lambda j: (0, 0)), pl.BlockSpec((k, tn), lambda j: (0, j))],
        out_specs=pl.BlockSpec((m, tn), lambda j: (0, j)),
        out_shape=jax.ShapeDtypeStruct((m, n), F32),
        compiler_params=_params("parallel"),
    )(a, w)


class _Rows:
    def __init__(self, bp, t, bs, tile=None):
        self.pad = MAX_ROW_TILE if t % MAX_ROW_TILE == 0 else t
        self.tile = self.pad if tile is None else min(tile, self.pad)
        assert bs <= self.pad and t % self.tile == 0 and self.pad % self.tile == 0
        self.bp, self.t, self.bs = bp, t, bs
        self.n_p = bp * t
        self.tiles_per_batch = t // self.tile
        self.n_prompt_tiles = self.n_p // self.tile
        self.n_tiles = self.n_prompt_tiles + self.pad // self.tile
        self.rows = self.n_p + self.pad

    def retile(self, tile):
        return _Rows(self.bp, self.t, self.bs, tile)

    def prompt_rows(self, width):
        last = self.n_prompt_tiles - 1
        return pl.BlockSpec((self.tile, width), lambda i, *_: (jnp.minimum(i, last), 0))

    def sample_rows(self, width, col=0, is_input=True):
        first = self.n_prompt_tiles
        index = lambda i, *_: (jnp.maximum(i - first, 0), col)
        if is_input and self.tile == self.pad:
            return pl.BlockSpec((self.tile, width), index, pipeline_mode=pl.Buffered(1))
        return pl.BlockSpec((self.tile, width), index)

    def prompt_mod(self, width, col):
        last, per = self.bp - 1, self.tiles_per_batch
        return pl.BlockSpec((1, 1, width), lambda i, *_: (jnp.minimum(i // per, last), 0, col))

    def all_rows(self, width, col=0):
        return pl.BlockSpec((self.tile, width), lambda i, *_: (i, col))


def _mod_args(rows, mod_p, mod_s, cols, width):
    args, specs = [], []
    for c in cols:
        args += [mod_p, mod_s]
        specs += [rows.prompt_mod(width, c), rows.sample_rows(width, c)]
    return args, specs


def _inproj_kernel(xp_ref, xs_ref, scp_ref, scs_ref, shp_ref, shs_ref, nw_ref, w_ref, o_ref, h_scr, *,
                   n_prompt_tiles):
    i, j = pl.program_id(0), pl.program_id(1)

    def norm_mod(x, sc, sh):
        return (_rmsnorm(x, nw_ref[...], NORM_EPS) * (1.0 + sc) + sh).astype(BF16)

    slabs = [slice(r, r + NORM_SLAB) for r in range(0, h_scr.shape[0], NORM_SLAB)]

    @pl.when((j == 0) & (i < n_prompt_tiles))
    def _():
        for sl in slabs:
            h_scr[sl, :] = norm_mod(xp_ref[sl, :], scp_ref[0], shp_ref[0])

    @pl.when((j == 0) & (i >= n_prompt_tiles))
    def _():
        for sl in slabs:
            h_scr[sl, :] = norm_mod(xs_ref[sl, :], scs_ref[sl, :], shs_ref[sl, :])

    o_ref[...] = _dot(h_scr[...], w_ref[...])


def in_projection(rows, xp, xs, mod_p, mod_s, norm_w, w_aligned):
    d = xp.shape[1]
    n = w_aligned.shape[1]
    tn = _largest_tile(n, 1664, LANES)
    margs, mspecs = _mod_args(rows, mod_p, mod_s, (1, 0), d)
    return pl.pallas_call(
        functools.partial(_inproj_kernel, n_prompt_tiles=rows.n_prompt_tiles),
        grid=(rows.n_tiles, n // tn),
        in_specs=[rows.prompt_rows(d), rows.sample_rows(d)] + mspecs
        + [pl.BlockSpec((1, d), lambda i, j: (0, 0)), pl.BlockSpec((d, tn), lambda i, j: (0, j))],
        out_specs=pl.BlockSpec((rows.tile, tn), lambda i, j: (i, j)),
        out_shape=jax.ShapeDtypeStruct((rows.rows, n), F32),
        scratch_shapes=[pltpu.VMEM((rows.tile, d), BF16)],
        compiler_params=_params("parallel", "arbitrary"),
    )(xp, xs, *margs, norm_w, w_aligned)


def _rwkv_chunk_kernel(big_ref, wl_ref, al_ref, gl_ref, mu_big_ref, mu_small_ref, vec_ref, w2_ref, a2_ref,
                       g2_ref, o_ref, s_ref, state, carry_big, carry_small, *, passes):
    c_idx = pl.program_id(1)
    n_tok = big_ref.shape[0]
    n_grp = RWKV_WIDTH // GROUP_W
    grp = [slice(g * GROUP_W, (g + 1) * GROUP_W) for g in range(n_grp)]

    @pl.when(c_idx == 0)
    def _():
        state[...] = jnp.zeros_like(state)
        carry_big[...] = jnp.zeros_like(carry_big)
        carry_small[...] = jnp.zeros_like(carry_small)

    first_row = lax.broadcasted_iota(jnp.int32, (n_tok, 1), 0) == 0

    def token_shift(cur, carry, mu):
        prev = jnp.where(first_row, carry[0:1, :], pltpu.roll(cur, 1, 0))
        carry[0:1, :] = cur[n_tok - 1:n_tok, :]
        return cur + (prev - cur) * mu

    xs_big = token_shift(big_ref[...], carry_big, mu_big_ref[...])
    small = jnp.concatenate([wl_ref[...], al_ref[...], gl_ref[...]], axis=1)
    xs_small = token_shift(small, carry_small, mu_small_ref[...])
    r = xs_big[:, :RWKV_WIDTH]
    k7 = xs_big[:, RWKV_WIDTH:2 * RWKV_WIDTH]
    v = xs_big[:, 2 * RWKV_WIDTH:]
    w0, a0, k_k, k_a, r_k, gn_w, gn_b = (vec_ref[i:i + 1, :] for i in range(7))
    w_pre = w0 + _dotp(jnp.tanh(xs_small[:, :LANES]), w2_ref[...])
    lw = -jnp.exp(-_softplus(-w_pre) - 0.5)
    a = jax.nn.sigmoid(a0 + _dotp(xs_small[:, LANES:2 * LANES], a2_ref[...]))
    gate = _dotp(jax.nn.sigmoid(xs_small[:, 2 * LANES:]), g2_ref[...])
    kk_raw = k7 * k_k
    k = k7 * (1.0 + (a - 1.0) * k_a)

    gi = lax.broadcasted_iota(jnp.int32, (GROUP_W, GROUP_W), 0) // RWKV_HEAD
    gj = lax.broadcasted_iota(jnp.int32, (GROUP_W, GROUP_W), 1) // RWKV_HEAD
    head_ones = (gi == gj).astype(BF16)

    def head_sums(parts):
        hi, lo = _split2(jnp.concatenate(parts, axis=0))
        out = _dot(hi, head_ones) + _dot(lo, head_ones)
        return [out[i * n_tok:(i + 1) * n_tok] for i in range(len(parts))]

    pre = [head_sums([kk_raw[:, sl] * kk_raw[:, sl], r[:, sl] * k[:, sl] * r_k[:, sl]]) for sl in grp]
    kk = jnp.concatenate([kk_raw[:, sl] / jnp.maximum(jnp.sqrt(pre[g][0]), 1e-12)
                          for g, sl in enumerate(grp)], axis=1)

    cum = _cumsum_rows(lw)
    cum_end = cum[n_tok - 1:n_tok, :]
    beta = kk * a
    g_inv = jnp.exp(-cum)
    g_end = jnp.exp(cum_end - cum)
    g_tot = jnp.exp(cum_end)
    a_hat = -kk * jnp.exp(cum - lw)
    r_hat = r * jnp.exp(cum)
    b_hat = beta * g_inv
    k_hat = k * g_inv
    b_end = beta * g_end
    k_end = k * g_end

    lane = lax.broadcasted_iota(jnp.int32, (n_tok, GROUP_W), 1)
    tok = lax.broadcasted_iota(jnp.int32, (n_tok, GROUP_W), 0)
    lane_head = lane // RWKV_HEAD
    src_tok = lane % RWKV_HEAD
    strict = tok > src_tok
    incl = tok >= src_tok

    def bd(y):
        return jnp.concatenate([jnp.where(lane_head == h, y, 0.0) for h in range(RWKV_GROUP)], axis=0)

    gs = range(n_grp)
    s0 = [state[:, sl] for sl in grp]
    lhs2 = [jnp.concatenate([a_hat[:, sl], r_hat[:, sl]], axis=0) for sl in grp]
    abrb = [_dotp(lhs2[g], bd(b_hat[:, grp[g]]), NT, passes) for g in gs]
    akrk = [_dotp(lhs2[g], bd(k_hat[:, grp[g]]), NT, passes) for g in gs]
    asrs = [_dotp(lhs2[g], bd(s0[g]), NT, passes) for g in gs]
    p = [jnp.where(strict, abrb[g][:n_tok], 0.0) for g in gs]
    ak = [jnp.where(strict, akrk[g][:n_tok], 0.0) for g in gs]
    rb = [jnp.where(incl, abrb[g][n_tok:], 0.0) for g in gs]
    rk = [jnp.where(incl, akrk[g][n_tok:], 0.0) for g in gs]
    bd_v = [bd(v[:, sl]) for sl in grp]
    x = [asrs[g][:n_tok] + _dotp(ak[g], bd_v[g], NN, passes) for g in gs]
    n_sq = n_tok.bit_length() - 1
    for it in range(n_sq):
        if it < n_sq - 1:
            both = [_dotp(p[g], jnp.concatenate([bd(p[g]), bd(x[g])], axis=1), NN, passes) for g in gs]
            x = [x[g] + both[g][:, GROUP_W:] for g in gs]
            p = [both[g][:, :GROUP_W] for g in gs]
        else:
            x = [x[g] + _dotp(p[g], bd(x[g]), NN, passes) for g in gs]
    y = [asrs[g][n_tok:] + _dotp(jnp.concatenate([rb[g], rk[g]], axis=1),
                                 jnp.concatenate([bd(x[g]), bd_v[g]], axis=0), NN, passes) for g in gs]
    full = [_dotp(jnp.concatenate([x[g], v[:, grp[g]]], axis=0).T,
                  jnp.concatenate([b_end[:, grp[g]], k_end[:, grp[g]]], axis=0), NN, passes) for g in gs]
    for g in gs:
        upd = s0[g] * g_tot[:, grp[g]]
        for h in range(RWKV_GROUP):
            upd = upd + jnp.where(lane_head == h, full[g][h * RWKV_HEAD:(h + 1) * RWKV_HEAD, :], 0.0)
        state[:, grp[g]] = upd

    inv_n = 1.0 / RWKV_HEAD
    for g, sl in enumerate(grp):
        dev = y[g] - head_sums([y[g]])[0] * inv_n
        var = head_sums([dev * dev])[0] * inv_n
        yn = dev * lax.rsqrt(var + RWKV_GN_EPS) * gn_w[:, sl] + gn_b[:, sl]
        o_ref[:, sl] = ((yn + pre[g][1] * v[:, sl]) * gate[:, sl]).astype(o_ref.dtype)

    @pl.when(c_idx == pl.num_programs(1) - 1)
    def _():
        s_ref[0] = state[...]


def rwkv7_prompt(rows, proj, mu_big, mu_small, vecs, w2, a2, g2, *, passes=1):
    assert CHUNK == RWKV_HEAD and rows.t % CHUNK == 0
    nc = rows.t // CHUNK
    row_blk = lambda b, c: b * nc + c
    small_col = lambda name: DST[name][0] // LANES
    const = lambda shape: pl.BlockSpec(shape, lambda b, c: (0, 0))
    return pl.pallas_call(
        functools.partial(_rwkv_chunk_kernel, passes=passes),
        grid=(rows.bp, nc),
        in_specs=[pl.BlockSpec((CHUNK, BIG_W), lambda b, c: (row_blk(b, c), 1)),
                  pl.BlockSpec((CHUNK, LANES), lambda b, c: (row_blk(b, c), small_col("wl"))),
                  pl.BlockSpec((CHUNK, LANES), lambda b, c: (row_blk(b, c), small_col("al"))),
                  pl.BlockSpec((CHUNK, LANES), lambda b, c: (row_blk(b, c), small_col("gl7"))),
                  const((1, BIG_W)), const((1, SMALL_W)), const((SUBLANES, RWKV_WIDTH)),
                  const((LANES, RWKV_WIDTH)), const((LANES, RWKV_WIDTH)), const((LANES, RWKV_WIDTH))],
        out_specs=[pl.BlockSpec((CHUNK, RWKV_WIDTH), lambda b, c: (row_blk(b, c), 0)),
                   pl.BlockSpec((1, RWKV_HEAD, RWKV_WIDTH), lambda b, c: (b, 0, 0))],
        out_shape=[jax.ShapeDtypeStruct((rows.rows, RWKV_WIDTH), BF16),
                   jax.ShapeDtypeStruct((rows.bp, RWKV_HEAD, RWKV_WIDTH), F32)],
        scratch_shapes=[pltpu.VMEM((RWKV_HEAD, RWKV_WIDTH), F32), pltpu.VMEM((SUBLANES, BIG_W), F32),
                        pltpu.VMEM((SUBLANES, SMALL_W), F32)],
        compiler_params=_params("parallel", "arbitrary"),
    )(proj, proj, proj, proj, mu_big, mu_small, vecs, w2, a2, g2)


def _gla_chunk_kernel(qk_ref, v_ref, og_ref, gl_ref, w2_ref, gb_ref, nw_ref, o_ref, s_ref, state):
    c_idx = pl.program_id(1)
    n_tok = qk_ref.shape[0]

    @pl.when(c_idx == 0)
    def _():
        state[...] = jnp.zeros_like(state)

    logd = -_softplus(-(_dotp(gl_ref[...], w2_ref[...]) + gb_ref[...])) * (1.0 / GLA_GATE_NORMALIZER)
    b_all = _cumsum_rows(logd)
    row_k = lax.broadcasted_iota(jnp.int32, (n_tok, GLA_DK), 0)
    row_s = lax.broadcasted_iota(jnp.int32, (SUB, GLA_DK), 0)

    for h in range(GLA_HEADS):
        ks = slice(h * GLA_DK, (h + 1) * GLA_DK)
        vs = slice(h * GLA_DV, (h + 1) * GLA_DV)
        q = qk_ref[:, ks] * (GLA_DK ** -0.5)
        k = qk_ref[:, GLA_KEY_WIDTH + h * GLA_DK:GLA_KEY_WIDTH + (h + 1) * GLA_DK]
        b, v = b_all[:, ks], v_ref[:, vs]
        b_last = b[n_tok - 1:n_tok, :]
        st = state[h]
        o_inter = _dotp(q * jnp.exp(b), st, NT)
        blocks = []
        for i in range(n_tok // SUB):
            lo = i * SUB
            qs, kb, bs, vb = q[lo:lo + SUB], k[lo:lo + SUB], b[lo:lo + SUB], v[lo:lo + SUB]
            acc = o_inter[lo:lo + SUB]
            if i > 0:
                b_ref = b[lo - 1:lo, :]
                q_hat = qs * jnp.exp(bs - b_ref)
                earlier = row_k < lo
                k_hat = jnp.where(earlier, k * jnp.exp(jnp.where(earlier, b_ref - b, 0.0)), 0.0)
                att = _dotp(q_hat, k_hat, NT)
                acc = acc + _dotp(att, v, NN)
            for j in range(SUB):
                valid = row_s >= j
                w_j = jnp.where(valid, kb[j:j + 1] * jnp.exp(jnp.where(valid, bs - bs[j:j + 1], 0.0)), 0.0)
                col = jnp.sum(qs * w_j, axis=1, keepdims=True)
                acc = acc + col * vb[j:j + 1]
            blocks.append(acc)
        o = jnp.concatenate(blocks, axis=0)
        og = og_ref[:, vs]
        o_ref[:, vs] = (_rmsnorm(o, nw_ref[...], GLA_NORM_EPS) * (og * jax.nn.sigmoid(og))).astype(o_ref.dtype)
        k_dec = k * jnp.exp(b_last - b)
        state[h] = st * jnp.exp(b_last) + _dotp(v.T, k_dec, NN)

    @pl.when(c_idx == pl.num_programs(1) - 1)
    def _():
        s_ref[0] = state[...]


def gla_prompt(rows, proj, gate_w2, gate_b, norm_w):
    nc = rows.t // CHUNK
    row_blk = lambda b, c: b * nc + c
    wide = lambda col: pl.BlockSpec((CHUNK, GLA_WIDTH), lambda b, c: (row_blk(b, c), col))
    const = lambda shape: pl.BlockSpec(shape, lambda b, c: (0, 0))
    return pl.pallas_call(
        _gla_chunk_kernel,
        grid=(rows.bp, nc),
        in_specs=[wide(0), wide(1), wide(2),
                  pl.BlockSpec((CHUNK, LANES), lambda b, c: (row_blk(b, c), DST["gl"][0] // LANES)),
                  const((LANES, GLA_KEY_WIDTH)), const((1, GLA_KEY_WIDTH)), const((1, GLA_DV))],
        out_specs=[wide(0), pl.BlockSpec((1, GLA_HEADS, GLA_DV, GLA_DK), lambda b, c: (b, 0, 0, 0))],
        out_shape=[jax.ShapeDtypeStruct((rows.rows, GLA_WIDTH), BF16),
                   jax.ShapeDtypeStruct((rows.bp, GLA_HEADS, GLA_DV, GLA_DK), F32)],
        scratch_shapes=[pltpu.VMEM((GLA_HEADS, GLA_DV, GLA_DK), F32)],
        compiler_params=_params("parallel", "arbitrary"),
    )(proj, proj, proj, proj, gate_w2, gate_b, norm_w)


def _outproj_kernel(og_ref, orw_ref, xp_ref, xs_ref, gtp_ref, gts_ref, scp_ref, scs_ref, shp_ref, shs_ref,
                    nw_ref, wo_ref, wr_ref, x1_ref, h2_ref, lg_ref, *, n_prompt_tiles):
    i = pl.program_id(0)
    half = og_ref.shape[1]
    mix = _dot(og_ref[...], wo_ref[:half, :]) + _dot(orw_ref[...], wo_ref[half:, :])

    def finish(x, gt, sc, sh):
        x1 = x + gt * mix
        h2 = _rmsnorm(x1, nw_ref[...], NORM_EPS) * (1.0 + sc) + sh
        x1_ref[...] = x1
        h2_ref[...] = h2
        lg_ref[...] = _dotp(h2, wr_ref[...], NN, 3)

    @pl.when(i < n_prompt_tiles)
    def _():
        finish(xp_ref[...], gtp_ref[0], scp_ref[0], shp_ref[0])

    @pl.when(i >= n_prompt_tiles)
    def _():
        finish(xs_ref[...], gts_ref[...], scs_ref[...], shs_ref[...])


def out_projection(rows, o_gla, o_rwkv, xp, xs, mod_p, mod_s, norm_w, w_out, w_router):
    d = xp.shape[1]
    margs, mspecs = _mod_args(rows, mod_p, mod_s, (2, 4, 3), d)
    const = lambda a: pl.BlockSpec(a.shape, lambda i: (0, 0), pipeline_mode=pl.Buffered(1))
    return pl.pallas_call(
        functools.partial(_outproj_kernel, n_prompt_tiles=rows.n_prompt_tiles),
        grid=(rows.n_tiles,),
        in_specs=[rows.all_rows(o_gla.shape[1]), rows.all_rows(o_rwkv.shape[1]),
                  rows.prompt_rows(d), rows.sample_rows(d)] + mspecs
        + [const(norm_w), const(w_out), const(w_router)],
        out_specs=[rows.all_rows(d), rows.all_rows(d), rows.all_rows(LANES)],
        out_shape=[jax.ShapeDtypeStruct((rows.rows, d), F32), jax.ShapeDtypeStruct((rows.rows, d), F32),
                   jax.ShapeDtypeStruct((rows.rows, LANES), F32)],
        compiler_params=_params("parallel"),
    )(o_gla, o_rwkv, xp, xs, *margs, norm_w, w_out, w_router)


def _moe_kernel(tile_expert_ref, tile_valid_ref, x_ref, rw_ref, wg_ref, wu_ref, wd_ref, o_ref):
    del tile_expert_ref
    i = pl.program_id(0)

    @pl.when(tile_valid_ref[i] != 0)
    def _():
        x = x_ref[...]
        gate = _dot(x, wg_ref[0].astype(BF16))
        up = _dot(x, wu_ref[0].astype(BF16))
        hid = gate * jax.nn.sigmoid(gate) * up * rw_ref[...]
        o_ref[...] = _dot(hid.astype(BF16), wd_ref[0].astype(BF16))

    @pl.when(tile_valid_ref[i] == 0)
    def _():
        o_ref[...] = jnp.zeros_like(o_ref)


def moe_experts(x_sorted, row_w, tile_expert, tile_valid, w_gate, w_up, w_down):
    p, d = x_sorted.shape
    _, _, ff = w_gate.shape
    n_tiles = p // MOE_TILE
    grid_spec = pltpu.PrefetchScalarGridSpec(
        num_scalar_prefetch=2,
        grid=(n_tiles,),
        in_specs=[pl.BlockSpec((MOE_TILE, d), lambda i, te, tv: (i, 0)),
                  pl.BlockSpec((MOE_TILE, 1), lambda i, te, tv: (i, 0)),
                  pl.BlockSpec((1, d, ff), lambda i, te, tv: (te[i], 0, 0)),
                  pl.BlockSpec((1, d, ff), lambda i, te, tv: (te[i], 0, 0)),
                  pl.BlockSpec((1, ff, d), lambda i, te, tv: (te[i], 0, 0))],
        out_specs=pl.BlockSpec((MOE_TILE, d), lambda i, te, tv: (i, 0)),
    )
    return pl.pallas_call(
        _moe_kernel,
        grid_spec=grid_spec,
        out_shape=jax.ShapeDtypeStruct((p, d), F32),
        compiler_params=_params("arbitrary"),
    )(tile_expert, tile_valid, x_sorted, row_w, w_gate, w_up, w_down)


def _route(logits):
    p_group = jax.nn.softmax(logits[:, :N_GROUPS], axis=-1)
    g_sel = jnp.argmax(p_group, axis=-1)
    p_sel = jnp.take_along_axis(p_group, g_sel[:, None], axis=-1)
    logits_e = logits[:, N_GROUPS:N_GROUPS + N_EXPERTS].reshape(-1, N_GROUPS, EXPERTS_PER_GROUP)
    logits_e = jnp.take_along_axis(logits_e, g_sel[:, None, None], axis=1)[:, 0]
    i1 = jnp.argmax(logits_e, axis=-1)
    v1 = jnp.take_along_axis(logits_e, i1[:, None], axis=-1)
    masked = jnp.where(jnp.arange(EXPERTS_PER_GROUP)[None, :] == i1[:, None], -jnp.inf, logits_e)
    i2 = jnp.argmax(masked, axis=-1)
    v2 = jnp.take_along_axis(logits_e, i2[:, None], axis=-1)
    w_top = jax.nn.softmax(jnp.concatenate([v1, v2], axis=-1), axis=-1) * p_sel
    top_i = jnp.stack([i1, i2], axis=-1)
    expert = g_sel[:, None].astype(jnp.int32) * EXPERTS_PER_GROUP + top_i.astype(jnp.int32)
    return expert, w_top


def _moe_plan(expert, w_top):
    n = expert.shape[0]
    e_flat = expert.reshape(-1)
    ids = jnp.arange(N_EXPERTS, dtype=jnp.int32)
    one_hot = (e_flat[:, None] == ids[None, :]).astype(jnp.int32)
    counts = jnp.sum(one_hot, axis=0)
    padded = (counts + MOE_TILE - 1) // MOE_TILE * MOE_TILE
    ends = jnp.cumsum(padded)
    starts = ends - padded
    rank = jnp.sum(jnp.cumsum(one_hot, axis=0) * one_hot, axis=1) - 1
    pos = jnp.sum(one_hot * starts[None, :], axis=1) + rank
    p_max = _round_up(2 * n, MOE_TILE) + N_EXPERTS * MOE_TILE
    n_tiles = p_max // MOE_TILE
    token = jnp.arange(2 * n, dtype=jnp.int32) // 2
    src_token = jnp.zeros((p_max,), jnp.int32).at[pos].set(token)
    row_w = jnp.zeros((p_max,), F32).at[pos].set(w_top.reshape(-1))
    tile_start = jnp.arange(n_tiles, dtype=jnp.int32) * MOE_TILE
    tile_valid = (tile_start < ends[-1]).astype(jnp.int32)
    tile_expert = jnp.sum((tile_start[:, None] >= ends[None, :]).astype(jnp.int32), axis=1)
    last_valid = jnp.maximum(jnp.sum(tile_valid) - 1, 0)
    tile_expert = jnp.where(tile_valid != 0, tile_expert, tile_expert[last_valid])
    tile_expert = jnp.minimum(tile_expert, N_EXPERTS - 1)
    return pos.reshape(n, 2), src_token, row_w[:, None], tile_expert, tile_valid


def _final_kernel(x1_ref, moe_ref, gtp_ref, gts_ref, scp_ref, scs_ref, shp_ref, shs_ref, nw_ref,
                  yp_ref, ys_ref, *, n_prompt_tiles):
    i = pl.program_id(0)

    def finish(gt, sc, sh):
        x2 = x1_ref[...] + gt * moe_ref[...]
        return _rmsnorm(x2, nw_ref[...], NORM_EPS) * (1.0 + sc) + sh

    @pl.when(i < n_prompt_tiles)
    def _():
        yp_ref[...] = finish(gtp_ref[0], scp_ref[0], shp_ref[0])

    @pl.when(i >= n_prompt_tiles)
    def _():
        ys_ref[...] = finish(gts_ref[...], scs_ref[...], shs_ref[...])


def final_norm(rows, x1, moe, mod_p, mod_s, modf_p, modf_s, norm_w):
    d = x1.shape[1]
    gargs, gspecs = _mod_args(rows, mod_p, mod_s, (5,), d)
    fargs, fspecs = _mod_args(rows, modf_p, modf_s, (1, 0), d)
    return pl.pallas_call(
        functools.partial(_final_kernel, n_prompt_tiles=rows.n_prompt_tiles),
        grid=(rows.n_tiles,),
        in_specs=[rows.all_rows(d), rows.all_rows(d)] + gspecs + fspecs
        + [pl.BlockSpec((1, d), lambda i: (0, 0))],
        out_specs=[rows.prompt_rows(d), rows.sample_rows(d, is_input=False)],
        out_shape=[jax.ShapeDtypeStruct((rows.n_p, d), F32), jax.ShapeDtypeStruct((rows.pad, d), F32)],
        compiler_params=_params("arbitrary"),
    )(x1, moe, *gargs, *fargs, norm_w)


def _align_cols(a):
    parts = []
    for name in _ALIGNED:
        off, w = SRC[name]
        part = a[..., off:off + w]
        wa = DST[name][1]
        if wa != w:
            part = jnp.pad(part, [(0, 0)] * (a.ndim - 1) + [(0, wa - w)])
        parts.append(part)
    return jnp.concatenate(parts, axis=-1)


def _rwkv_seg(a, name):
    off, w = SRC[name]
    return a[..., off - RWKV_SRC_BASE:off - RWKV_SRC_BASE + w]


def _rwkv_seg_padded(a, name):
    seg = _rwkv_seg(a, name)
    wa = DST[name][1]
    return jnp.pad(seg, [(0, 0)] * (a.ndim - 1) + [(0, wa - seg.shape[-1])])


def _proj_seg(p, name, padded=True):
    off, wa = DST[name]
    return p[..., off:off + (wa if padded else SRC[name][1])]


def _pad_rows(w, rows):
    return jnp.pad(w, ((0, rows - w.shape[0]), (0, 0)))


def kernel(x_prompt, x_sample, state_gla, state_rwkv, state_shift, c_prompt, c_sample, w_ada, b_ada, norm_mix, norm_ffn, w_in, gla_gate_w2, gla_gate_b, gla_norm, rwkv_mu, rwkv_w0, rwkv_w2, rwkv_a0, rwkv_a2, rwkv_g2, rwkv_k_k, rwkv_k_a, rwkv_r_k, rwkv_gn_w, rwkv_gn_b, w_out, w_router_group, w_router_expert, w_exp_gate, w_exp_up, w_exp_down, norm_final, w_ada_final, b_ada_final):
    assert w_ada.shape[0] == 1, "single-layer step"
    bp, t, d = x_prompt.shape
    bs = x_sample.shape[0]
    assert x_sample.shape[1] == 1
    rows = _Rows(bp, t, bs)
    rows_small = rows.retile(MAX_ROW_TILE // 2)
    n_p = rows.n_p
    hi = lax.Precision.HIGHEST
    pad_tile = lambda a: jnp.pad(a, ((0, rows.pad - a.shape[0]), (0, 0)))

    c_act = jax.nn.silu(jnp.concatenate([c_prompt, c_sample], axis=0))
    mod = matmul(c_act, w_ada[0]) + b_ada[0]
    mod_f = matmul(c_act, w_ada_final) + b_ada_final
    mod_p, mod_s = mod[:bp, None, :], pad_tile(mod[bp:])
    modf_p, modf_s = mod_f[:bp, None, :], pad_tile(mod_f[bp:])
    xp = x_prompt.reshape(n_p, d)
    xs = pad_tile(x_sample[:, 0, :])

    proj = in_projection(rows, xp, xs, mod_p, mod_s, norm_mix, _align_cols(w_in[0]).astype(BF16))
    proj_s = proj[n_p:n_p + bs]

    o_gla, gla_t_p = gla_prompt(rows, proj, _pad_rows(gla_gate_w2[0], LANES), gla_gate_b, gla_norm)
    new_gla_p = jnp.swapaxes(gla_t_p, -1, -2)
    heads = lambda a, n: a.reshape(a.shape[0], n, -1)
    gl_s = _proj_seg(proj_s, "gl", padded=False)
    logd_s = jax.nn.log_sigmoid(jnp.dot(gl_s.astype(BF16), gla_gate_w2[0].astype(BF16),
                                        preferred_element_type=F32) + gla_gate_b[0]) / GLA_GATE_NORMALIZER
    q_s = heads(_proj_seg(proj_s, "q") * (GLA_DK ** -0.5), GLA_HEADS)
    k_s, v_s, g_s = heads(_proj_seg(proj_s, "k"), GLA_HEADS), heads(_proj_seg(proj_s, "v"), GLA_HEADS), \
        heads(logd_s, GLA_HEADS)
    new_gla_s = jnp.exp(g_s)[..., None] * state_gla[0] + k_s[..., :, None] * v_s[..., None, :]
    o_s = jnp.einsum("bhd,bhde->bhe", q_s, new_gla_s, precision=hi)
    o_s = _rmsnorm(o_s, gla_norm[0], GLA_NORM_EPS) * heads(jax.nn.silu(_proj_seg(proj_s, "og")), GLA_HEADS)
    o_gla = lax.dynamic_update_slice(o_gla, pad_tile(o_s.reshape(bs, GLA_WIDTH)).astype(BF16), (n_p, 0))

    mu = rwkv_mu[0]
    mu_big = jnp.concatenate([_rwkv_seg(mu, n) for n in ("r", "k7", "v7")])[None, :]
    mu_small = jnp.concatenate([_rwkv_seg_padded(mu, n) for n in ("wl", "al", "gl7")])[None, :]
    vecs = jnp.concatenate([rwkv_w0, rwkv_a0, rwkv_k_k, rwkv_k_a, rwkv_r_k[0].reshape(1, RWKV_WIDTH),
                            rwkv_gn_w, rwkv_gn_b, jnp.zeros((1, RWKV_WIDTH), F32)], axis=0)
    o_rwkv, rwkv_t_p = rwkv7_prompt(rows, proj, mu_big, mu_small, vecs, _pad_rows(rwkv_w2[0], LANES),
                                    _pad_rows(rwkv_a2[0], LANES), rwkv_g2[0])
    new_rwkv_p = rwkv_t_p.reshape(bp, RWKV_HEAD, RWKV_HEADS, RWKV_HEAD).transpose(0, 2, 1, 3)
    unalign = lambda a: jnp.concatenate([_proj_seg(a, n, padded=False) for n in _RWKV_ORIG], axis=-1)
    new_shift_p = unalign(proj[t - 1:n_p:t])
    new_shift_s = unalign(proj_s)
    sx = lambda name: (lambda cur, prev, m: cur + (prev - cur) * m)(
        _proj_seg(proj_s, name, padded=False), _rwkv_seg(state_shift[0], name), _rwkv_seg(mu, name))
    bdot = lambda a, w: jnp.dot(a.astype(BF16), w.astype(BF16), preferred_element_type=F32)
    r_s, k7_s, v7_s = sx("r"), sx("k7"), sx("v7")
    w_pre = rwkv_w0[0] + bdot(jnp.tanh(sx("wl")), rwkv_w2[0])
    decay_s = jnp.exp(-jnp.exp(-jax.nn.softplus(-w_pre) - 0.5))
    a_s = jax.nn.sigmoid(rwkv_a0[0] + bdot(sx("al"), rwkv_a2[0]))
    g_s7 = bdot(jax.nn.sigmoid(sx("gl7")), rwkv_g2[0])
    hs = lambda z: z.reshape(bs, RWKV_HEADS, RWKV_HEAD)
    kk_s = hs(k7_s * rwkv_k_k[0])
    kk_s = kk_s / jnp.maximum(jnp.sqrt(jnp.sum(kk_s * kk_s, axis=-1, keepdims=True)), 1e-12)
    k7_s = k7_s * (1.0 + (a_s - 1.0) * rwkv_k_a[0])
    r_h, w_h, k_h, v_h, a_h = hs(r_s), hs(decay_s), hs(k7_s), hs(v7_s), hs(a_s)
    s_prev = state_rwkv[0]
    sa = jnp.einsum("bhij,bhj->bhi", s_prev, -kk_s, precision=hi)
    new_rwkv_s = s_prev * w_h[:, :, None, :] + sa[..., None] * (kk_s * a_h)[:, :, None, :] \
        + v_h[..., None] * k_h[:, :, None, :]
    y_s = jnp.einsum("bhij,bhj->bhi", new_rwkv_s, r_h, precision=hi)
    y_mu = jnp.mean(y_s, axis=-1, keepdims=True)
    y_var = jnp.mean(jnp.square(y_s - y_mu), axis=-1, keepdims=True)
    y_s = (y_s - y_mu) * lax.rsqrt(y_var + RWKV_GN_EPS) * rwkv_gn_w[0].reshape(RWKV_HEADS, RWKV_HEAD) \
        + rwkv_gn_b[0].reshape(RWKV_HEADS, RWKV_HEAD)
    bonus = jnp.sum(r_h * k_h * rwkv_r_k[0], axis=-1, keepdims=True) * v_h
    o_rs = (y_s + bonus).reshape(bs, RWKV_WIDTH) * g_s7
    o_rwkv = lax.dynamic_update_slice(o_rwkv, pad_tile(o_rs).astype(BF16), (n_p, 0))

    w_router = jnp.pad(jnp.concatenate([w_router_group[0], w_router_expert[0]], axis=-1),
                       ((0, 0), (0, LANES - N_GROUPS - N_EXPERTS)))
    x1, h2, logits = out_projection(rows_small, o_gla, o_rwkv, xp, xs, mod_p, mod_s, norm_ffn,
                                    w_out[0].astype(BF16), w_router)

    n_tok = n_p + bs
    expert, w_top = _route(logits[:n_tok])
    pos, src_token, row_w, tile_expert, tile_valid = _moe_plan(expert, w_top)
    x_sorted = jnp.take(h2, src_token, axis=0).astype(BF16)
    y_sorted = moe_experts(x_sorted, row_w, tile_expert, tile_valid,
                           w_exp_gate[0].reshape(N_EXPERTS, d, EXPERT_FF),
                           w_exp_up[0].reshape(N_EXPERTS, d, EXPERT_FF),
                           w_exp_down[0].reshape(N_EXPERTS, EXPERT_FF, d))
    moe = jnp.take(y_sorted, pos[:, 0], axis=0) + jnp.take(y_sorted, pos[:, 1], axis=0)
    moe = jnp.pad(moe, ((0, rows.rows - n_tok), (0, 0)))

    y_p, y_s_pad = final_norm(rows_small, x1, moe, mod_p, mod_s, modf_p, modf_s, norm_final[None, :])
    return (y_p.reshape(bp, t, d), y_s_pad[:bs, None, :], new_gla_p[None], new_rwkv_p[None], new_shift_p[None],
            new_gla_s[None], new_rwkv_s[None], new_shift_s[None])
```

```python
import collections
import functools

import jax
import jax.numpy as jnp
from jax import lax
from jax.experimental import pallas as pl
from jax.experimental.pallas import tpu as pltpu

F32 = jnp.float32
BF16 = jnp.bfloat16

D_MODEL = 2048
GLA_HEADS = 4
GLA_DK = 128
GLA_DV = 256
GLA_KEY_WIDTH = GLA_HEADS * GLA_DK
GLA_WIDTH = GLA_HEADS * GLA_DV
GLA_GATE_NORMALIZER = 16.0
RWKV_HEAD = 64
RWKV_HEADS = 16
RWKV_WIDTH = RWKV_HEAD * RWKV_HEADS
N_GROUPS = 4
EXPERTS_PER_GROUP = 8
N_EXPERTS = N_GROUPS * EXPERTS_PER_GROUP
EXPERT_FF = 512
NORM_EPS = 1e-6
GLA_NORM_EPS = 1e-5
RWKV_GN_EPS = 64e-5

LANES = 128
SUBLANES = 8
VMEM_LIMIT_BYTES = 56 * 1024 * 1024

CHUNK = 64
SUB = 16
RWKV_GROUP = 4
GROUP_W = RWKV_GROUP * RWKV_HEAD
MOE_TILE = 256
ROW_TILE = 256
NORM_SLAB = 128
K_SPLIT = 256

_ORIG = (("q", 512), ("k", 512), ("v", 1024), ("gl", 16), ("og", 1024),
         ("r", 1024), ("wl", 64), ("k7", 1024), ("v7", 1024), ("al", 64), ("gl7", 128))
_ALIGNED = ("q", "k", "v", "og", "r", "k7", "v7", "gl", "wl", "al", "gl7")
_RWKV_ORIG = ("r", "wl", "k7", "v7", "al", "gl7")


def _round_up(n, m):
    return (n + m - 1) // m * m


def _layouts():
    src, off = {}, 0
    for name, w in _ORIG:
        src[name] = (off, w)
        off += w
    dst, pos = {}, 0
    for name in _ALIGNED:
        wa = _round_up(src[name][1], LANES)
        dst[name] = (pos, wa)
        pos += wa
    return src, off, dst, pos


SRC, IN_COLS, DST, IN_COLS_ALIGNED = _layouts()
RWKV_SRC_BASE = SRC["r"][0]
BIG_W = 3 * RWKV_WIDTH
SMALL_W = 3 * LANES
assert DST["r"][0] == BIG_W and DST["k7"][0] == BIG_W + RWKV_WIDTH and DST["v7"][0] == BIG_W + 2 * RWKV_WIDTH
assert DST["v"][0] == RWKV_WIDTH and DST["og"][0] == 2 * RWKV_WIDTH and DST["k"][0] == GLA_KEY_WIDTH

NN = ((1,), (0,))
NT = ((1,), (1,))

_PromptShape = collections.namedtuple("_PromptShape", "bp t rows")


def _dot(a, b, dims=NN):
    return lax.dot_general(a, b, (dims, ((), ())), preferred_element_type=F32)


def _split2(x):
    hi = x.astype(BF16)
    lo = (x - hi.astype(F32)).astype(BF16)
    return hi, lo


def _dotp(a, b, dims=NN, passes=1):
    if passes == 1:
        return _dot(a.astype(BF16), b.astype(BF16), dims)
    ah, al = _split2(a)
    bh, bl = _split2(b)
    return _dot(ah, bh, dims) + (_dot(ah, bl, dims) + _dot(al, bh, dims))


def _cumsum_rows(x):
    n = x.shape[0]
    row = lax.broadcasted_iota(jnp.int32, (n, n), 0)
    col = lax.broadcasted_iota(jnp.int32, (n, n), 1)
    tri = (row >= col).astype(BF16)
    x1 = x.astype(BF16)
    r1 = x - x1.astype(F32)
    x2 = r1.astype(BF16)
    x3 = (r1 - x2.astype(F32)).astype(BF16)
    return _dot(tri, x1) + (_dot(tri, x2) + _dot(tri, x3))


def _softplus(z):
    return jnp.maximum(z, 0.0) + jnp.log1p(jnp.exp(-jnp.abs(z)))


def _rmsnorm(x, g, eps):
    return x * lax.rsqrt(jnp.mean(x * x, axis=-1, keepdims=True) + eps) * g


def _largest_tile(n, cap, mult=SUBLANES):
    if n <= cap:
        return n
    best = None
    for t in range(mult, cap + 1, mult):
        if n % t == 0:
            best = t
    assert best is not None, (n, cap)
    return best


def _params(*sem):
    return pltpu.CompilerParams(dimension_semantics=sem, vmem_limit_bytes=VMEM_LIMIT_BYTES)


def _mm3_kernel(a_ref, w_ref, o_ref):
    @pl.when(pl.program_id(0) == 0)
    def _():
        o_ref[...] = jnp.zeros_like(o_ref)

    o_ref[...] += _dotp(a_ref[...], w_ref[...], NN, 3)


def matmul3(a, w):
    m, k = a.shape
    n = w.shape[1]
    tk = _largest_tile(k, K_SPLIT, LANES)
    return pl.pallas_call(
        _mm3_kernel,
        grid=(k // tk,),
        in_specs=[pl.BlockSpec((m, tk), lambda s: (0, s)), pl.BlockSpec((tk, n), lambda s: (s, 0))],
        out_specs=pl.BlockSpec((m, n), lambda s: (0, 0)),
        out_shape=jax.ShapeDtypeStruct((m, n), F32),
        compiler_params=_params("arbitrary"),
    )(a, w)


def _prompt_mod(tiles_per_batch, width, col):
    return pl.BlockSpec((1, 1, width), lambda i, *_: (i // tiles_per_batch, 0, col))


def _inproj_kernel(x_ref, sc_ref, sh_ref, nw_ref, w_ref, o_ref, h_scr):
    @pl.when(pl.program_id(1) == 0)
    def _():
        for r in range(0, h_scr.shape[0], NORM_SLAB):
            sl = slice(r, r + NORM_SLAB)
            h = _rmsnorm(x_ref[sl, :], nw_ref[...], NORM_EPS) * (1.0 + sc_ref[0]) + sh_ref[0]
            h_scr[sl, :] = h.astype(BF16)

    o_ref[...] = _dot(h_scr[...], w_ref[...])


def in_projection(xp, t, mod_p, norm_w, w_aligned):
    n_p, d = xp.shape
    n = w_aligned.shape[1]
    tm = _largest_tile(t, 1024, NORM_SLAB)
    tn = _largest_tile(n, 512, LANES)
    per = t // tm
    return pl.pallas_call(
        _inproj_kernel,
        grid=(n_p // tm, n // tn),
        in_specs=[pl.BlockSpec((tm, d), lambda i, j: (i, 0)), _prompt_mod(per, d, 1), _prompt_mod(per, d, 0),
                  pl.BlockSpec((1, d), lambda i, j: (0, 0)), pl.BlockSpec((d, tn), lambda i, j: (0, j))],
        out_specs=pl.BlockSpec((tm, tn), lambda i, j: (i, j)),
        out_shape=jax.ShapeDtypeStruct((n_p, n), F32),
        scratch_shapes=[pltpu.VMEM((tm, d), BF16)],
        compiler_params=_params("parallel", "arbitrary"),
    )(xp, mod_p, mod_p, norm_w, w_aligned)


def _rwkv_chunk_kernel(big_ref, wl_ref, al_ref, gl_ref, mu_big_ref, mu_small_ref, vec_ref, w2_ref, a2_ref,
                       g2_ref, o_ref, s_ref, state, carry_big, carry_small, *, passes):
    c_idx = pl.program_id(1)
    n_tok = big_ref.shape[0]
    n_grp = RWKV_WIDTH // GROUP_W
    grp = [slice(g * GROUP_W, (g + 1) * GROUP_W) for g in range(n_grp)]

    @pl.when(c_idx == 0)
    def _():
        state[...] = jnp.zeros_like(state)
        carry_big[...] = jnp.zeros_like(carry_big)
        carry_small[...] = jnp.zeros_like(carry_small)

    first_row = lax.broadcasted_iota(jnp.int32, (n_tok, 1), 0) == 0

    def token_shift(cur, carry, mu):
        prev = jnp.where(first_row, carry[0:1, :], pltpu.roll(cur, 1, 0))
        carry[0:1, :] = cur[n_tok - 1:n_tok, :]
        return cur + (prev - cur) * mu

    xs_big = token_shift(big_ref[...], carry_big, mu_big_ref[...])
    small = jnp.concatenate([wl_ref[...], al_ref[...], gl_ref[...]], axis=1)
    xs_small = token_shift(small, carry_small, mu_small_ref[...])
    r = xs_big[:, :RWKV_WIDTH]
    k7 = xs_big[:, RWKV_WIDTH:2 * RWKV_WIDTH]
    v = xs_big[:, 2 * RWKV_WIDTH:]
    w0, a0, k_k, k_a, r_k, gn_w, gn_b = (vec_ref[i:i + 1, :] for i in range(7))
    w_pre = w0 + _dotp(jnp.tanh(xs_small[:, :LANES]), w2_ref[...])
    lw = -jnp.exp(-_softplus(-w_pre) - 0.5)
    a = jax.nn.sigmoid(a0 + _dotp(xs_small[:, LANES:2 * LANES], a2_ref[...]))
    gate = _dotp(jax.nn.sigmoid(xs_small[:, 2 * LANES:]), g2_ref[...])
    kk_raw = k7 * k_k
    k = k7 * (1.0 + (a - 1.0) * k_a)

    gi = lax.broadcasted_iota(jnp.int32, (GROUP_W, GROUP_W), 0) // RWKV_HEAD
    gj = lax.broadcasted_iota(jnp.int32, (GROUP_W, GROUP_W), 1) // RWKV_HEAD
    head_ones = (gi == gj).astype(BF16)

    def head_sums(parts):
        hi, lo = _split2(jnp.concatenate(parts, axis=0))
        out = _dot(hi, head_ones) + _dot(lo, head_ones)
        return [out[i * n_tok:(i + 1) * n_tok] for i in range(len(parts))]

    pre = [head_sums([kk_raw[:, sl] * kk_raw[:, sl], r[:, sl] * k[:, sl] * r_k[:, sl]]) for sl in grp]
    kk = jnp.concatenate([kk_raw[:, sl] / jnp.maximum(jnp.sqrt(pre[g][0]), 1e-12)
                          for g, sl in enumerate(grp)], axis=1)

    cum = _cumsum_rows(lw)
    cum_end = cum[n_tok - 1:n_tok, :]
    beta = kk * a
    g_inv = jnp.exp(-cum)
    g_end = jnp.exp(cum_end - cum)
    g_tot = jnp.exp(cum_end)
    a_hat = -kk * jnp.exp(cum - lw)
    r_hat = r * jnp.exp(cum)
    b_hat = beta * g_inv
    k_hat = k * g_inv
    b_end = beta * g_end
    k_end = k * g_end

    lane = lax.broadcasted_iota(jnp.int32, (n_tok, GROUP_W), 1)
    tok = lax.broadcasted_iota(jnp.int32, (n_tok, GROUP_W), 0)
    lane_head = lane // RWKV_HEAD
    src_tok = lane % RWKV_HEAD
    strict = tok > src_tok
    incl = tok >= src_tok

    def bd(y):
        return jnp.concatenate([jnp.where(lane_head == h, y, 0.0) for h in range(RWKV_GROUP)], axis=0)

    gs = range(n_grp)
    s0 = [state[:, sl] for sl in grp]
    lhs2 = [jnp.concatenate([a_hat[:, sl], r_hat[:, sl]], axis=0) for sl in grp]
    abrb = [_dotp(lhs2[g], bd(b_hat[:, grp[g]]), NT, passes) for g in gs]
    akrk = [_dotp(lhs2[g], bd(k_hat[:, grp[g]]), NT, passes) for g in gs]
    asrs = [_dotp(lhs2[g], bd(s0[g]), NT, passes) for g in gs]
    p = [jnp.where(strict, abrb[g][:n_tok], 0.0) for g in gs]
    ak = [jnp.where(strict, akrk[g][:n_tok], 0.0) for g in gs]
    rb = [jnp.where(incl, abrb[g][n_tok:], 0.0) for g in gs]
    rk = [jnp.where(incl, akrk[g][n_tok:], 0.0) for g in gs]
    bd_v = [bd(v[:, sl]) for sl in grp]
    x = [asrs[g][:n_tok] + _dotp(ak[g], bd_v[g], NN, passes) for g in gs]
    n_sq = n_tok.bit_length() - 1
    for it in range(n_sq):
        if it < n_sq - 1:
            both = [_dotp(p[g], jnp.concatenate([bd(p[g]), bd(x[g])], axis=1), NN, passes) for g in gs]
            x = [x[g] + both[g][:, GROUP_W:] for g in gs]
            p = [both[g][:, :GROUP_W] for g in gs]
        else:
            x = [x[g] + _dotp(p[g], bd(x[g]), NN, passes) for g in gs]
    y = [asrs[g][n_tok:] + _dotp(jnp.concatenate([rb[g], rk[g]], axis=1),
                                 jnp.concatenate([bd(x[g]), bd_v[g]], axis=0), NN, passes) for g in gs]
    full = [_dotp(jnp.concatenate([x[g], v[:, grp[g]]], axis=0).T,
                  jnp.concatenate([b_end[:, grp[g]], k_end[:, grp[g]]], axis=0), NN, passes) for g in gs]
    for g in gs:
        upd = s0[g] * g_tot[:, grp[g]]
        for h in range(RWKV_GROUP):
            upd = upd + jnp.where(lane_head == h, full[g][h * RWKV_HEAD:(h + 1) * RWKV_HEAD, :], 0.0)
        state[:, grp[g]] = upd

    inv_n = 1.0 / RWKV_HEAD
    for g, sl in enumerate(grp):
        dev = y[g] - head_sums([y[g]])[0] * inv_n
        var = head_sums([dev * dev])[0] * inv_n
        yn = dev * lax.rsqrt(var + RWKV_GN_EPS) * gn_w[:, sl] + gn_b[:, sl]
        o_ref[:, sl] = ((yn + pre[g][1] * v[:, sl]) * gate[:, sl]).astype(o_ref.dtype)

    @pl.when(c_idx == pl.num_programs(1) - 1)
    def _():
        s_ref[0] = state[...]


def rwkv7_prompt(rows, proj, mu_big, mu_small, vecs, w2, a2, g2, *, passes=1):
    assert CHUNK == RWKV_HEAD and rows.t % CHUNK == 0
    nc = rows.t // CHUNK
    row_blk = lambda b, c: b * nc + c
    small_col = lambda name: DST[name][0] // LANES
    const = lambda shape: pl.BlockSpec(shape, lambda b, c: (0, 0))
    return pl.pallas_call(
        functools.partial(_rwkv_chunk_kernel, passes=passes),
        grid=(rows.bp, nc),
        in_specs=[pl.BlockSpec((CHUNK, BIG_W), lambda b, c: (row_blk(b, c), 1)),
                  pl.BlockSpec((CHUNK, LANES), lambda b, c: (row_blk(b, c), small_col("wl"))),
                  pl.BlockSpec((CHUNK, LANES), lambda b, c: (row_blk(b, c), small_col("al"))),
                  pl.BlockSpec((CHUNK, LANES), lambda b, c: (row_blk(b, c), small_col("gl7"))),
                  const((1, BIG_W)), const((1, SMALL_W)), const((SUBLANES, RWKV_WIDTH)),
                  const((LANES, RWKV_WIDTH)), const((LANES, RWKV_WIDTH)), const((LANES, RWKV_WIDTH))],
        out_specs=[pl.BlockSpec((CHUNK, RWKV_WIDTH), lambda b, c: (row_blk(b, c), 0)),
                   pl.BlockSpec((1, RWKV_HEAD, RWKV_WIDTH), lambda b, c: (b, 0, 0))],
        out_shape=[jax.ShapeDtypeStruct((rows.rows, RWKV_WIDTH), BF16),
                   jax.ShapeDtypeStruct((rows.bp, RWKV_HEAD, RWKV_WIDTH), F32)],
        scratch_shapes=[pltpu.VMEM((RWKV_HEAD, RWKV_WIDTH), F32), pltpu.VMEM((SUBLANES, BIG_W), F32),
                        pltpu.VMEM((SUBLANES, SMALL_W), F32)],
        compiler_params=_params("parallel", "arbitrary"),
    )(proj, proj, proj, proj, mu_big, mu_small, vecs, w2, a2, g2)


def _gla_chunk_kernel(qk_ref, v_ref, og_ref, gl_ref, w2_ref, gb_ref, nw_ref, o_ref, s_ref, state):
    c_idx = pl.program_id(1)
    n_tok = qk_ref.shape[0]

    @pl.when(c_idx == 0)
    def _():
        state[...] = jnp.zeros_like(state)

    logd = -_softplus(-(_dotp(gl_ref[...], w2_ref[...]) + gb_ref[...])) * (1.0 / GLA_GATE_NORMALIZER)
    b_all = _cumsum_rows(logd)
    row_k = lax.broadcasted_iota(jnp.int32, (n_tok, GLA_DK), 0)
    row_s = lax.broadcasted_iota(jnp.int32, (SUB, GLA_DK), 0)

    for h in range(GLA_HEADS):
        ks = slice(h * GLA_DK, (h + 1) * GLA_DK)
        vs = slice(h * GLA_DV, (h + 1) * GLA_DV)
        q = qk_ref[:, ks] * (GLA_DK ** -0.5)
        k = qk_ref[:, GLA_KEY_WIDTH + h * GLA_DK:GLA_KEY_WIDTH + (h + 1) * GLA_DK]
        b, v = b_all[:, ks], v_ref[:, vs]
        b_last = b[n_tok - 1:n_tok, :]
        st = state[h]
        o_inter = _dotp(q * jnp.exp(b), st, NT)
        blocks = []
        for i in range(n_tok // SUB):
            lo = i * SUB
            qs, kb, bs, vb = q[lo:lo + SUB], k[lo:lo + SUB], b[lo:lo + SUB], v[lo:lo + SUB]
            acc = o_inter[lo:lo + SUB]
            if i > 0:
                b_ref = b[lo - 1:lo, :]
                q_hat = qs * jnp.exp(bs - b_ref)
                earlier = row_k < lo
                k_hat = jnp.where(earlier, k * jnp.exp(jnp.where(earlier, b_ref - b, 0.0)), 0.0)
                att = _dotp(q_hat, k_hat, NT)
                acc = acc + _dotp(att, v, NN)
            for j in range(SUB):
                valid = row_s >= j
                w_j = jnp.where(valid, kb[j:j + 1] * jnp.exp(jnp.where(valid, bs - bs[j:j + 1], 0.0)), 0.0)
                col = jnp.sum(qs * w_j, axis=1, keepdims=True)
                acc = acc + col * vb[j:j + 1]
            blocks.append(acc)
        o = jnp.concatenate(blocks, axis=0)
        og = og_ref[:, vs]
        o_ref[:, vs] = (_rmsnorm(o, nw_ref[...], GLA_NORM_EPS) * (og * jax.nn.sigmoid(og))).astype(o_ref.dtype)
        k_dec = k * jnp.exp(b_last - b)
        state[h] = st * jnp.exp(b_last) + _dotp(v.T, k_dec, NN)

    @pl.when(c_idx == pl.num_programs(1) - 1)
    def _():
        s_ref[0] = state[...]


def gla_prompt(rows, proj, gate_w2, gate_b, norm_w):
    nc = rows.t // CHUNK
    row_blk = lambda b, c: b * nc + c
    wide = lambda col: pl.BlockSpec((CHUNK, GLA_WIDTH), lambda b, c: (row_blk(b, c), col))
    const = lambda shape: pl.BlockSpec(shape, lambda b, c: (0, 0))
    return pl.pallas_call(
        _gla_chunk_kernel,
        grid=(rows.bp, nc),
        in_specs=[wide(0), wide(1), wide(2),
                  pl.BlockSpec((CHUNK, LANES), lambda b, c: (row_blk(b, c), DST["gl"][0] // LANES)),
                  const((LANES, GLA_KEY_WIDTH)), const((1, GLA_KEY_WIDTH)), const((1, GLA_DV))],
        out_specs=[wide(0), pl.BlockSpec((1, GLA_HEADS, GLA_DV, GLA_DK), lambda b, c: (b, 0, 0, 0))],
        out_shape=[jax.ShapeDtypeStruct((rows.rows, GLA_WIDTH), BF16),
                   jax.ShapeDtypeStruct((rows.bp, GLA_HEADS, GLA_DV, GLA_DK), F32)],
        scratch_shapes=[pltpu.VMEM((GLA_HEADS, GLA_DV, GLA_DK), F32)],
        compiler_params=_params("parallel", "arbitrary"),
    )(proj, proj, proj, proj, gate_w2, gate_b, norm_w)


def _outproj_kernel(og_ref, orw_ref, x_ref, gt_ref, sc_ref, sh_ref, nw_ref, wo_ref, wr_ref,
                    x1s_ref, h2s_ref, lgs_ref, x1_ref, h2_ref, lg_ref, *, n_prompt_tiles):
    i = pl.program_id(0)

    @pl.when(i < n_prompt_tiles)
    def _():
        half = og_ref.shape[1]
        mix = _dot(og_ref[...], wo_ref[:half, :]) + _dot(orw_ref[...], wo_ref[half:, :])
        x1 = x_ref[...] + gt_ref[0] * mix
        h2 = _rmsnorm(x1, nw_ref[...], NORM_EPS) * (1.0 + sc_ref[0]) + sh_ref[0]
        x1_ref[...] = x1
        h2_ref[...] = h2
        lg_ref[...] = _dotp(h2, wr_ref[...], NN, 3)

    @pl.when(i >= n_prompt_tiles)
    def _():
        x1_ref[...] = x1s_ref[...]
        h2_ref[...] = h2s_ref[...]
        lg_ref[...] = lgs_ref[...]


def out_projection(t, o_gla, o_rwkv, xp, mod_p, norm_w, w_out, w_router, x1_s, h2_s, logits_s):
    n_p, d = xp.shape
    pad_s = x1_s.shape[0]
    tm = ROW_TILE
    npt, per = n_p // tm, t // tm
    prompt = lambda width: pl.BlockSpec((tm, width), lambda i: (jnp.minimum(i, npt - 1), 0))
    pmod = lambda col: pl.BlockSpec((1, 1, d), lambda i: (jnp.minimum(i // per, mod_p.shape[0] - 1), 0, col))
    sample = lambda width: pl.BlockSpec((tm, width), lambda i: (jnp.maximum(i - npt, 0), 0))
    rows = lambda width: pl.BlockSpec((tm, width), lambda i: (i, 0))
    const = lambda a: pl.BlockSpec(a.shape, lambda i: (0, 0), pipeline_mode=pl.Buffered(1))
    n_rows = n_p + pad_s
    return pl.pallas_call(
        functools.partial(_outproj_kernel, n_prompt_tiles=npt),
        grid=(n_rows // tm,),
        in_specs=[prompt(o_gla.shape[1]), prompt(o_rwkv.shape[1]), prompt(d), pmod(2), pmod(4), pmod(3),
                  const(norm_w), const(w_out), const(w_router), sample(d), sample(d), sample(LANES)],
        out_specs=[rows(d), rows(d), rows(LANES)],
        out_shape=[jax.ShapeDtypeStruct((n_rows, d), F32), jax.ShapeDtypeStruct((n_rows, d), F32),
                   jax.ShapeDtypeStruct((n_rows, LANES), F32)],
        compiler_params=_params("arbitrary"),
    )(o_gla, o_rwkv, xp, mod_p, mod_p, mod_p, norm_w, w_out, w_router, x1_s, h2_s, logits_s)


def _moe_kernel(tile_expert_ref, n_valid_ref, src_ref, dst_ref, h2_hbm, rw_ref, wg_ref, wu_ref, wd_ref,
                y_hbm, xbuf, obuf, gather_sem, scatter_sem):
    del tile_expert_ref
    i = pl.program_id(0)
    n_tiles = pl.num_programs(0)
    n_valid = n_valid_ref[0]
    slot = i % 2

    def gather_copy(tile, s, r):
        tok = src_ref[tile * MOE_TILE + r]
        return pltpu.make_async_copy(h2_hbm.at[pl.ds(tok, 1), :], xbuf.at[s, pl.ds(r, 1), :], gather_sem.at[s])

    def scatter_copy(tile, s, r):
        row = dst_ref[tile * MOE_TILE + r]
        copy = pltpu.make_async_copy(obuf.at[s, pl.ds(r, 1), :], y_hbm.at[pl.ds(jnp.maximum(row, 0), 1), :],
                                     scatter_sem.at[s])
        return row >= 0, copy

    def for_rows(fn):
        def body(r, carry):
            fn(r)
            return carry
        lax.fori_loop(0, MOE_TILE, body, 0)

    def start_gather(tile, s):
        for_rows(lambda r: gather_copy(tile, s, r).start())

    def wait_gather(tile, s):
        for_rows(lambda r: gather_copy(tile, s, r).wait())

    def start_scatter(tile, s):
        def one(r):
            real, copy = scatter_copy(tile, s, r)
            pl.when(real)(copy.start)
        for_rows(one)

    def wait_scatter(tile, s):
        def one(r):
            real, copy = scatter_copy(tile, s, r)
            pl.when(real)(copy.wait)
        for_rows(one)

    @pl.when(i == 0)
    def _():
        start_gather(0, 0)

    @pl.when(i + 1 < n_valid)
    def _():
        start_gather(i + 1, 1 - slot)

    @pl.when(i < n_valid)
    def _():
        wait_gather(i, slot)

        @pl.when(i >= 2)
        def _():
            wait_scatter(i - 2, slot)

        x = xbuf[slot].astype(BF16)
        gate = _dot(x, wg_ref[0].astype(BF16))
        up = _dot(x, wu_ref[0].astype(BF16))
        hid = gate * jax.nn.sigmoid(gate) * up * rw_ref[...]
        obuf[slot] = _dot(hid.astype(BF16), wd_ref[0].astype(BF16))
        start_scatter(i, slot)

    @pl.when(i == n_tiles - 1)
    def _():
        @pl.when(n_valid >= 2)
        def _():
            wait_scatter(n_valid - 2, n_valid % 2)

        wait_scatter(n_valid - 1, (n_valid - 1) % 2)


def moe_experts(h2, plan, w_gate, w_up, w_down):
    n_rows, d = h2.shape
    _, _, ff = w_gate.shape
    n_tiles = plan["tile_expert"].shape[0]
    grid_spec = pltpu.PrefetchScalarGridSpec(
        num_scalar_prefetch=4,
        grid=(n_tiles,),
        in_specs=[pl.BlockSpec(memory_space=pl.ANY),
                  pl.BlockSpec((MOE_TILE, 1), lambda i, te, nv, src, dst: (i, 0)),
                  pl.BlockSpec((1, d, ff), lambda i, te, nv, src, dst: (te[i], 0, 0)),
                  pl.BlockSpec((1, d, ff), lambda i, te, nv, src, dst: (te[i], 0, 0)),
                  pl.BlockSpec((1, ff, d), lambda i, te, nv, src, dst: (te[i], 0, 0))],
        out_specs=pl.BlockSpec(memory_space=pl.ANY),
        scratch_shapes=[pltpu.VMEM((2, MOE_TILE, d), F32), pltpu.VMEM((2, MOE_TILE, d), F32),
                        pltpu.SemaphoreType.DMA((2,)), pltpu.SemaphoreType.DMA((2,))],
    )
    return pl.pallas_call(
        _moe_kernel,
        grid_spec=grid_spec,
        out_shape=jax.ShapeDtypeStruct((2 * n_rows, d), F32),
        compiler_params=_params("arbitrary"),
    )(plan["tile_expert"], plan["n_valid"], plan["src_row"], plan["dst_row"], h2, plan["row_w"],
      w_gate, w_up, w_down)


def _route(logits):
    p_group = jax.nn.softmax(logits[:, :N_GROUPS], axis=-1)
    g_sel = jnp.argmax(p_group, axis=-1)
    p_sel = jnp.take_along_axis(p_group, g_sel[:, None], axis=-1)
    logits_e = logits[:, N_GROUPS:N_GROUPS + N_EXPERTS].reshape(-1, N_GROUPS, EXPERTS_PER_GROUP)
    logits_e = jnp.take_along_axis(logits_e, g_sel[:, None, None], axis=1)[:, 0]
    i1 = jnp.argmax(logits_e, axis=-1)
    v1 = jnp.take_along_axis(logits_e, i1[:, None], axis=-1)
    masked = jnp.where(jnp.arange(EXPERTS_PER_GROUP)[None, :] == i1[:, None], -jnp.inf, logits_e)
    i2 = jnp.argmax(masked, axis=-1)
    v2 = jnp.take_along_axis(logits_e, i2[:, None], axis=-1)
    w_top = jax.nn.softmax(jnp.concatenate([v1, v2], axis=-1), axis=-1) * p_sel
    top_i = jnp.stack([i1, i2], axis=-1)
    expert = g_sel[:, None].astype(jnp.int32) * EXPERTS_PER_GROUP + top_i.astype(jnp.int32)
    return expert, w_top


def _moe_plan(expert, w_top):
    n = expert.shape[0]
    e_flat = expert.reshape(-1)
    ids = jnp.arange(N_EXPERTS, dtype=jnp.int32)
    one_hot = (e_flat[:, None] == ids[None, :]).astype(jnp.int32)
    counts = jnp.sum(one_hot, axis=0)
    padded = (counts + MOE_TILE - 1) // MOE_TILE * MOE_TILE
    ends = jnp.cumsum(padded)
    starts = ends - padded
    rank = jnp.sum(jnp.cumsum(one_hot, axis=0) * one_hot, axis=1) - 1
    pos = jnp.sum(one_hot * starts[None, :], axis=1) + rank
    p_max = _round_up(2 * n, MOE_TILE) + N_EXPERTS * MOE_TILE
    n_tiles = p_max // MOE_TILE
    pair = jnp.full((p_max,), -1, jnp.int32).at[pos].set(jnp.arange(2 * n, dtype=jnp.int32))
    real = pair >= 0
    pair0 = jnp.maximum(pair, 0)
    src_row = pair0 // 2
    dst_row = jnp.where(real, (pair0 % 2) * n + src_row, -1)
    row_w = jnp.where(real, w_top.reshape(-1)[pair0], 0.0)
    tile_start = jnp.arange(n_tiles, dtype=jnp.int32) * MOE_TILE
    n_valid = ends[-1] // MOE_TILE
    tile_expert = jnp.sum((tile_start[:, None] >= ends[None, :]).astype(jnp.int32), axis=1)
    tile_expert = jnp.where(tile_start < ends[-1], tile_expert, tile_expert[n_valid - 1])
    return dict(src_row=src_row, dst_row=dst_row, row_w=row_w[:, None], tile_expert=tile_expert,
                n_valid=n_valid.astype(jnp.int32)[None])


def _final_kernel(x1_ref, y0_ref, y1_ref, gtp_ref, gts_ref, scp_ref, scs_ref, shp_ref, shs_ref, nw_ref,
                  yp_ref, ys_ref, *, n_prompt_tiles):
    i = pl.program_id(0)

    def finish(gt, sc, sh):
        x2 = x1_ref[...] + gt * (y0_ref[...] + y1_ref[...])
        return _rmsnorm(x2, nw_ref[...], NORM_EPS) * (1.0 + sc) + sh

    @pl.when(i < n_prompt_tiles)
    def _():
        yp_ref[...] = finish(gtp_ref[0], scp_ref[0], shp_ref[0])

    @pl.when(i >= n_prompt_tiles)
    def _():
        ys_ref[...] = finish(gts_ref[...], scs_ref[...], shs_ref[...])


def final_norm(n_p, t, x1, y_pairs, mod_p, mod_s, modf_p, modf_s, norm_w):
    n_rows, d = x1.shape
    tile = ROW_TILE
    n_tiles, npt = n_rows // tile, n_p // tile
    per = t // tile
    rows = lambda off: pl.BlockSpec((tile, d), lambda i: (i + off, 0))
    prompt_out = pl.BlockSpec((tile, d), lambda i: (jnp.minimum(i, npt - 1), 0))
    sample = lambda col: pl.BlockSpec((tile, d), lambda i: (jnp.maximum(i - npt, 0), col))
    pmod = lambda col: pl.BlockSpec((1, 1, d), lambda i: (jnp.minimum(i // per, mod_p.shape[0] - 1), 0, col))
    return pl.pallas_call(
        functools.partial(_final_kernel, n_prompt_tiles=npt),
        grid=(n_tiles,),
        in_specs=[rows(0), rows(0), rows(n_tiles), pmod(5), sample(5), pmod(1), sample(1), pmod(0), sample(0),
                  pl.BlockSpec((1, d), lambda i: (0, 0))],
        out_specs=[prompt_out, sample(0)],
        out_shape=[jax.ShapeDtypeStruct((n_p, d), F32), jax.ShapeDtypeStruct((n_rows - n_p, d), F32)],
        compiler_params=_params("arbitrary"),
    )(x1, y_pairs, y_pairs, mod_p, mod_s, modf_p, modf_s, modf_p, modf_s, norm_w)


def _align_cols(a):
    parts = []
    for name in _ALIGNED:
        off, w = SRC[name]
        part = a[..., off:off + w]
        wa = DST[name][1]
        if wa != w:
            part = jnp.pad(part, [(0, 0)] * (a.ndim - 1) + [(0, wa - w)])
        parts.append(part)
    return jnp.concatenate(parts, axis=-1)


def _rwkv_seg(a, name):
    off, w = SRC[name]
    return a[..., off - RWKV_SRC_BASE:off - RWKV_SRC_BASE + w]


def _rwkv_seg_padded(a, name):
    seg = _rwkv_seg(a, name)
    wa = DST[name][1]
    return jnp.pad(seg, [(0, 0)] * (a.ndim - 1) + [(0, wa - seg.shape[-1])])


def _orig_seg(p, name):
    off, w = SRC[name]
    return p[..., off:off + w]


def _pad_rows(w, rows):
    return jnp.pad(w, ((0, rows - w.shape[0]), (0, 0)))


def kernel(x_prompt, x_sample, state_gla, state_rwkv, state_shift, c_prompt, c_sample, w_ada, b_ada, norm_mix, norm_ffn, w_in, gla_gate_w2, gla_gate_b, gla_norm, rwkv_mu, rwkv_w0, rwkv_w2, rwkv_a0, rwkv_a2, rwkv_g2, rwkv_k_k, rwkv_k_a, rwkv_r_k, rwkv_gn_w, rwkv_gn_b, w_out, w_router_group, w_router_expert, w_exp_gate, w_exp_up, w_exp_down, norm_final, w_ada_final, b_ada_final):
    assert w_ada.shape[0] == 1, "single-layer step"
    bp, t, d = x_prompt.shape
    bs = x_sample.shape[0]
    assert x_sample.shape[1] == 1 and t % ROW_TILE == 0
    n_p = bp * t
    pad_s = _round_up(bs, ROW_TILE)
    n_rows = n_p + pad_s
    prompt = _PromptShape(bp, t, n_p)
    hi = lax.Precision.HIGHEST
    pad_sample = lambda a: jnp.pad(a, ((0, pad_s - a.shape[0]), (0, 0)))
    hdot = lambda a, w: jnp.dot(a, w, precision=hi)

    c_act = jax.nn.silu(jnp.concatenate([c_prompt, c_sample], axis=0))
    mod = matmul3(c_act, w_ada[0]) + b_ada[0]
    mod_f = matmul3(c_act, w_ada_final) + b_ada_final
    mod_p, mod_s = mod[:bp, None, :], mod[bp:]
    modf_p, modf_s = mod_f[:bp, None, :], mod_f[bp:]
    sh1_s, sc1_s, gt1_s, sh2_s, sc2_s, _ = jnp.split(mod_s, 6, axis=-1)
    xp = x_prompt.reshape(n_p, d)
    xs = x_sample[:, 0, :]

    proj = in_projection(xp, t, mod_p, norm_mix, _align_cols(w_in[0]).astype(BF16))
    o_gla, gla_t_p = gla_prompt(prompt, proj, _pad_rows(gla_gate_w2[0], LANES), gla_gate_b, gla_norm)
    new_gla_p = jnp.swapaxes(gla_t_p, -1, -2)
    mu = rwkv_mu[0]
    mu_big = jnp.concatenate([_rwkv_seg(mu, n) for n in ("r", "k7", "v7")])[None, :]
    mu_small = jnp.concatenate([_rwkv_seg_padded(mu, n) for n in ("wl", "al", "gl7")])[None, :]
    vecs = jnp.concatenate([rwkv_w0, rwkv_a0, rwkv_k_k, rwkv_k_a, rwkv_r_k[0].reshape(1, RWKV_WIDTH),
                            rwkv_gn_w, rwkv_gn_b, jnp.zeros((1, RWKV_WIDTH), F32)], axis=0)
    o_rwkv, rwkv_t_p = rwkv7_prompt(prompt, proj, mu_big, mu_small, vecs, _pad_rows(rwkv_w2[0], LANES),
                                    _pad_rows(rwkv_a2[0], LANES), rwkv_g2[0])
    new_rwkv_p = rwkv_t_p.reshape(bp, RWKV_HEAD, RWKV_HEADS, RWKV_HEAD).transpose(0, 2, 1, 3)
    last = proj[t - 1:n_p:t]
    new_shift_p = jnp.concatenate([last[:, DST[n][0]:DST[n][0] + SRC[n][1]] for n in _RWKV_ORIG], axis=-1)

    h1_s = _rmsnorm(xs, norm_mix[0], NORM_EPS) * (1.0 + sc1_s) + sh1_s
    proj_s = matmul3(h1_s, w_in[0])
    heads = lambda a, n: a.reshape(a.shape[0], n, -1)
    logd_s = jax.nn.log_sigmoid(hdot(_orig_seg(proj_s, "gl"), gla_gate_w2[0]) + gla_gate_b[0]) \
        / GLA_GATE_NORMALIZER
    q_s = heads(_orig_seg(proj_s, "q") * (GLA_DK ** -0.5), GLA_HEADS)
    k_s, v_s, g_s = heads(_orig_seg(proj_s, "k"), GLA_HEADS), heads(_orig_seg(proj_s, "v"), GLA_HEADS), \
        heads(logd_s, GLA_HEADS)
    new_gla_s = jnp.exp(g_s)[..., None] * state_gla[0] + k_s[..., :, None] * v_s[..., None, :]
    o_s = jnp.einsum("bhd,bhde->bhe", q_s, new_gla_s, precision=hi)
    o_s = _rmsnorm(o_s, gla_norm[0], GLA_NORM_EPS) * heads(jax.nn.silu(_orig_seg(proj_s, "og")), GLA_HEADS)

    rp_s = proj_s[:, RWKV_SRC_BASE:]
    new_shift_s = rp_s
    xs7 = rp_s + (state_shift[0] - rp_s) * mu
    sx = lambda name: _rwkv_seg(xs7, name)
    r_s, k7_s, v7_s = sx("r"), sx("k7"), sx("v7")
    w_pre = rwkv_w0[0] + hdot(jnp.tanh(sx("wl")), rwkv_w2[0])
    decay_s = jnp.exp(-jnp.exp(-jax.nn.softplus(-w_pre) - 0.5))
    a_s = jax.nn.sigmoid(rwkv_a0[0] + hdot(sx("al"), rwkv_a2[0]))
    g_s7 = hdot(jax.nn.sigmoid(sx("gl7")), rwkv_g2[0])
    hs = lambda z: z.reshape(bs, RWKV_HEADS, RWKV_HEAD)
    kk_s = hs(k7_s * rwkv_k_k[0])
    kk_s = kk_s / jnp.maximum(jnp.sqrt(jnp.sum(kk_s * kk_s, axis=-1, keepdims=True)), 1e-12)
    k7_s = k7_s * (1.0 + (a_s - 1.0) * rwkv_k_a[0])
    r_h, w_h, k_h, v_h, a_h = hs(r_s), hs(decay_s), hs(k7_s), hs(v7_s), hs(a_s)
    s_prev = state_rwkv[0]
    sa = jnp.einsum("bhij,bhj->bhi", s_prev, -kk_s, precision=hi)
    new_rwkv_s = s_prev * w_h[:, :, None, :] + sa[..., None] * (kk_s * a_h)[:, :, None, :] \
        + v_h[..., None] * k_h[:, :, None, :]
    y_s = jnp.einsum("bhij,bhj->bhi", new_rwkv_s, r_h, precision=hi)
    y_mu = jnp.mean(y_s, axis=-1, keepdims=True)
    y_var = jnp.mean(jnp.square(y_s - y_mu), axis=-1, keepdims=True)
    y_s = (y_s - y_mu) * lax.rsqrt(y_var + RWKV_GN_EPS) * rwkv_gn_w[0].reshape(RWKV_HEADS, RWKV_HEAD) \
        + rwkv_gn_b[0].reshape(RWKV_HEADS, RWKV_HEAD)
    bonus = jnp.sum(r_h * k_h * rwkv_r_k[0], axis=-1, keepdims=True) * v_h
    o_rs = (y_s + bonus).reshape(bs, RWKV_WIDTH) * g_s7
    mix_s = matmul3(jnp.concatenate([o_s.reshape(bs, GLA_WIDTH), o_rs], axis=-1), w_out[0])
    x1_s = xs + gt1_s * mix_s
    h2_s = _rmsnorm(x1_s, norm_ffn[0], NORM_EPS) * (1.0 + sc2_s) + sh2_s
    w_router = jnp.pad(jnp.concatenate([w_router_group[0], w_router_expert[0]], axis=-1),
                       ((0, 0), (0, LANES - N_GROUPS - N_EXPERTS)))
    logits_s = hdot(h2_s, w_router)

    x1, h2, logits = out_projection(t, o_gla, o_rwkv, xp, mod_p, norm_ffn, w_out[0].astype(BF16), w_router,
                                    pad_sample(x1_s), pad_sample(h2_s), pad_sample(logits_s))

    expert, w_top = _route(logits)
    plan = _moe_plan(expert, w_top)
    y_pairs = moe_experts(h2, plan, w_exp_gate[0].reshape(N_EXPERTS, d, EXPERT_FF),
                          w_exp_up[0].reshape(N_EXPERTS, d, EXPERT_FF),
                          w_exp_down[0].reshape(N_EXPERTS, EXPERT_FF, d))

    y_p, y_s_pad = final_norm(n_p, t, x1, y_pairs, mod_p, pad_sample(mod_s), modf_p, pad_sample(modf_s),
                              norm_final[None, :])
    return (y_p.reshape(bp, t, d), y_s_pad[:bs, None, :], new_gla_p[None], new_rwkv_p[None], new_shift_p[None],
            new_gla_s[None], new_rwkv_s[None], new_shift_s[None])
```

```python
import collections
import functools

import jax
import jax.numpy as jnp
from jax import lax
from jax.experimental import pallas as pl
from jax.experimental.pallas import tpu as pltpu

F32 = jnp.float32
BF16 = jnp.bfloat16

D_MODEL = 2048
GLA_HEADS = 4
GLA_DK = 128
GLA_DV = 256
GLA_KEY_WIDTH = GLA_HEADS * GLA_DK
GLA_WIDTH = GLA_HEADS * GLA_DV
GLA_GATE_NORMALIZER = 16.0
RWKV_HEAD = 64
RWKV_HEADS = 16
RWKV_WIDTH = RWKV_HEAD * RWKV_HEADS
N_GROUPS = 4
EXPERTS_PER_GROUP = 8
N_EXPERTS = N_GROUPS * EXPERTS_PER_GROUP
EXPERT_FF = 512
NORM_EPS = 1e-6
GLA_NORM_EPS = 1e-5
RWKV_GN_EPS = 64e-5

LANES = 128
SUBLANES = 8
VMEM_LIMIT_BYTES = 56 * 1024 * 1024

CHUNK = 64
SUB = 16
RWKV_GROUP = 4
GROUP_W = RWKV_GROUP * RWKV_HEAD
MOE_TILE = 256
DMA_UNROLL = 8
ROW_TILE = 256
NORM_SLAB = 128
K_SPLIT = 256

_ORIG = (("q", 512), ("k", 512), ("v", 1024), ("gl", 16), ("og", 1024),
         ("r", 1024), ("wl", 64), ("k7", 1024), ("v7", 1024), ("al", 64), ("gl7", 128))
_ALIGNED = ("q", "k", "v", "og", "r", "k7", "v7", "gl", "wl", "al", "gl7")
_RWKV_ORIG = ("r", "wl", "k7", "v7", "al", "gl7")


def _round_up(n, m):
    return (n + m - 1) // m * m


def _layouts():
    src, off = {}, 0
    for name, w in _ORIG:
        src[name] = (off, w)
        off += w
    dst, pos = {}, 0
    for name in _ALIGNED:
        wa = _round_up(src[name][1], LANES)
        dst[name] = (pos, wa)
        pos += wa
    return src, off, dst, pos


SRC, IN_COLS, DST, IN_COLS_ALIGNED = _layouts()
RWKV_SRC_BASE = SRC["r"][0]
BIG_W = 3 * RWKV_WIDTH
SMALL_W = 3 * LANES
assert DST["r"][0] == BIG_W and DST["k7"][0] == BIG_W + RWKV_WIDTH and DST["v7"][0] == BIG_W + 2 * RWKV_WIDTH
assert DST["v"][0] == RWKV_WIDTH and DST["og"][0] == 2 * RWKV_WIDTH and DST["k"][0] == GLA_KEY_WIDTH

NN = ((1,), (0,))
NT = ((1,), (1,))

_PromptShape = collections.namedtuple("_PromptShape", "bp t rows")


def _dot(a, b, dims=NN):
    return lax.dot_general(a, b, (dims, ((), ())), preferred_element_type=F32)


def _split2(x):
    hi = x.astype(BF16)
    lo = (x - hi.astype(F32)).astype(BF16)
    return hi, lo


def _dotp(a, b, dims=NN, passes=1):
    if passes == 1:
        return _dot(a.astype(BF16), b.astype(BF16), dims)
    ah, al = _split2(a)
    bh, bl = _split2(b)
    return _dot(ah, bh, dims) + (_dot(ah, bl, dims) + _dot(al, bh, dims))


def _cumsum_rows(x):
    n = x.shape[0]
    row = lax.broadcasted_iota(jnp.int32, (n, n), 0)
    col = lax.broadcasted_iota(jnp.int32, (n, n), 1)
    tri = (row >= col).astype(BF16)
    x1 = x.astype(BF16)
    r1 = x - x1.astype(F32)
    x2 = r1.astype(BF16)
    x3 = (r1 - x2.astype(F32)).astype(BF16)
    return _dot(tri, x1) + (_dot(tri, x2) + _dot(tri, x3))


def _softplus(z):
    return jnp.maximum(z, 0.0) + jnp.log1p(jnp.exp(-jnp.abs(z)))


def _rmsnorm(x, g, eps):
    return x * lax.rsqrt(jnp.mean(x * x, axis=-1, keepdims=True) + eps) * g


def _largest_tile(n, cap, mult=SUBLANES):
    if n <= cap:
        return n
    best = None
    for t in range(mult, cap + 1, mult):
        if n % t == 0:
            best = t
    assert best is not None, (n, cap)
    return best


def _params(*sem):
    return pltpu.CompilerParams(dimension_semantics=sem, vmem_limit_bytes=VMEM_LIMIT_BYTES)


def _mm3_kernel(a_ref, w_ref, o_ref):
    @pl.when(pl.program_id(0) == 0)
    def _():
        o_ref[...] = jnp.zeros_like(o_ref)

    o_ref[...] += _dotp(a_ref[...], w_ref[...], NN, 3)


def matmul3(a, w):
    m, k = a.shape
    n = w.shape[1]
    tk = _largest_tile(k, K_SPLIT, LANES)
    return pl.pallas_call(
        _mm3_kernel,
        grid=(k // tk,),
        in_specs=[pl.BlockSpec((m, tk), lambda s: (0, s)), pl.BlockSpec((tk, n), lambda s: (s, 0))],
        out_specs=pl.BlockSpec((m, n), lambda s: (0, 0)),
        out_shape=jax.ShapeDtypeStruct((m, n), F32),
        compiler_params=_params("arbitrary"),
    )(a, w)


def _prompt_mod(tiles_per_batch, width, col):
    return pl.BlockSpec((1, 1, width), lambda i, *_: (i // tiles_per_batch, 0, col))


def _inproj_kernel(x_ref, sc_ref, sh_ref, nw_ref, w_ref, o_ref, h_scr):
    @pl.when(pl.program_id(1) == 0)
    def _():
        for r in range(0, h_scr.shape[0], NORM_SLAB):
            sl = slice(r, r + NORM_SLAB)
            h = _rmsnorm(x_ref[sl, :], nw_ref[...], NORM_EPS) * (1.0 + sc_ref[0]) + sh_ref[0]
            h_scr[sl, :] = h.astype(BF16)

    o_ref[...] = _dot(h_scr[...], w_ref[...])


def in_projection(xp, t, mod_p, norm_w, w_aligned):
    n_p, d = xp.shape
    n = w_aligned.shape[1]
    tm = _largest_tile(t, 1024, NORM_SLAB)
    tn = _largest_tile(n, 512, LANES)
    per = t // tm
    return pl.pallas_call(
        _inproj_kernel,
        grid=(n_p // tm, n // tn),
        in_specs=[pl.BlockSpec((tm, d), lambda i, j: (i, 0)), _prompt_mod(per, d, 1), _prompt_mod(per, d, 0),
                  pl.BlockSpec((1, d), lambda i, j: (0, 0)), pl.BlockSpec((d, tn), lambda i, j: (0, j))],
        out_specs=pl.BlockSpec((tm, tn), lambda i, j: (i, j)),
        out_shape=jax.ShapeDtypeStruct((n_p, n), F32),
        scratch_shapes=[pltpu.VMEM((tm, d), BF16)],
        compiler_params=_params("parallel", "arbitrary"),
    )(xp, mod_p, mod_p, norm_w, w_aligned)


def _rwkv_chunk_kernel(big_ref, wl_ref, al_ref, gl_ref, mu_big_ref, mu_small_ref, vec_ref, w2_ref, a2_ref,
                       g2_ref, o_ref, s_ref, state, carry_big, carry_small, *, passes):
    c_idx = pl.program_id(1)
    n_tok = big_ref.shape[0]
    n_grp = RWKV_WIDTH // GROUP_W
    grp = [slice(g * GROUP_W, (g + 1) * GROUP_W) for g in range(n_grp)]

    @pl.when(c_idx == 0)
    def _():
        state[...] = jnp.zeros_like(state)
        carry_big[...] = jnp.zeros_like(carry_big)
        carry_small[...] = jnp.zeros_like(carry_small)

    first_row = lax.broadcasted_iota(jnp.int32, (n_tok, 1), 0) == 0

    def token_shift(cur, carry, mu):
        prev = jnp.where(first_row, carry[0:1, :], pltpu.roll(cur, 1, 0))
        carry[0:1, :] = cur[n_tok - 1:n_tok, :]
        return cur + (prev - cur) * mu

    xs_big = token_shift(big_ref[...], carry_big, mu_big_ref[...])
    small = jnp.concatenate([wl_ref[...], al_ref[...], gl_ref[...]], axis=1)
    xs_small = token_shift(small, carry_small, mu_small_ref[...])
    r = xs_big[:, :RWKV_WIDTH]
    k7 = xs_big[:, RWKV_WIDTH:2 * RWKV_WIDTH]
    v = xs_big[:, 2 * RWKV_WIDTH:]
    w0, a0, k_k, k_a, r_k, gn_w, gn_b = (vec_ref[i:i + 1, :] for i in range(7))
    w_pre = w0 + _dotp(jnp.tanh(xs_small[:, :LANES]), w2_ref[...])
    lw = -jnp.exp(-_softplus(-w_pre) - 0.5)
    a = jax.nn.sigmoid(a0 + _dotp(xs_small[:, LANES:2 * LANES], a2_ref[...]))
    gate = _dotp(jax.nn.sigmoid(xs_small[:, 2 * LANES:]), g2_ref[...])
    kk_raw = k7 * k_k
    k = k7 * (1.0 + (a - 1.0) * k_a)

    gi = lax.broadcasted_iota(jnp.int32, (GROUP_W, GROUP_W), 0) // RWKV_HEAD
    gj = lax.broadcasted_iota(jnp.int32, (GROUP_W, GROUP_W), 1) // RWKV_HEAD
    head_ones = (gi == gj).astype(BF16)

    def head_sums(parts):
        hi, lo = _split2(jnp.concatenate(parts, axis=0))
        out = _dot(hi, head_ones) + _dot(lo, head_ones)
        return [out[i * n_tok:(i + 1) * n_tok] for i in range(len(parts))]

    pre = [head_sums([kk_raw[:, sl] * kk_raw[:, sl], r[:, sl] * k[:, sl] * r_k[:, sl]]) for sl in grp]
    kk = jnp.concatenate([kk_raw[:, sl] / jnp.maximum(jnp.sqrt(pre[g][0]), 1e-12)
                          for g, sl in enumerate(grp)], axis=1)

    cum = _cumsum_rows(lw)
    cum_end = cum[n_tok - 1:n_tok, :]
    beta = kk * a
    g_inv = jnp.exp(-cum)
    g_end = jnp.exp(cum_end - cum)
    g_tot = jnp.exp(cum_end)
    a_hat = -kk * jnp.exp(cum - lw)
    r_hat = r * jnp.exp(cum)
    b_hat = beta * g_inv
    k_hat = k * g_inv
    b_end = beta * g_end
    k_end = k * g_end

    lane = lax.broadcasted_iota(jnp.int32, (n_tok, GROUP_W), 1)
    tok = lax.broadcasted_iota(jnp.int32, (n_tok, GROUP_W), 0)
    lane_head = lane // RWKV_HEAD
    src_tok = lane % RWKV_HEAD
    strict = tok > src_tok
    incl = tok >= src_tok

    def bd(y):
        return jnp.concatenate([jnp.where(lane_head == h, y, 0.0) for h in range(RWKV_GROUP)], axis=0)

    gs = range(n_grp)
    s0 = [state[:, sl] for sl in grp]
    lhs2 = [jnp.concatenate([a_hat[:, sl], r_hat[:, sl]], axis=0) for sl in grp]
    abrb = [_dotp(lhs2[g], bd(b_hat[:, grp[g]]), NT, passes) for g in gs]
    akrk = [_dotp(lhs2[g], bd(k_hat[:, grp[g]]), NT, passes) for g in gs]
    asrs = [_dotp(lhs2[g], bd(s0[g]), NT, passes) for g in gs]
    p = [jnp.where(strict, abrb[g][:n_tok], 0.0) for g in gs]
    ak = [jnp.where(strict, akrk[g][:n_tok], 0.0) for g in gs]
    rb = [jnp.where(incl, abrb[g][n_tok:], 0.0) for g in gs]
    rk = [jnp.where(incl, akrk[g][n_tok:], 0.0) for g in gs]
    bd_v = [bd(v[:, sl]) for sl in grp]
    x = [asrs[g][:n_tok] + _dotp(ak[g], bd_v[g], NN, passes) for g in gs]
    n_sq = n_tok.bit_length() - 1
    for it in range(n_sq):
        if it < n_sq - 1:
            both = [_dotp(p[g], jnp.concatenate([bd(p[g]), bd(x[g])], axis=1), NN, passes) for g in gs]
            x = [x[g] + both[g][:, GROUP_W:] for g in gs]
            p = [both[g][:, :GROUP_W] for g in gs]
        else:
            x = [x[g] + _dotp(p[g], bd(x[g]), NN, passes) for g in gs]
    y = [asrs[g][n_tok:] + _dotp(jnp.concatenate([rb[g], rk[g]], axis=1),
                                 jnp.concatenate([bd(x[g]), bd_v[g]], axis=0), NN, passes) for g in gs]
    full = [_dotp(jnp.concatenate([x[g], v[:, grp[g]]], axis=0).T,
                  jnp.concatenate([b_end[:, grp[g]], k_end[:, grp[g]]], axis=0), NN, passes) for g in gs]
    for g in gs:
        upd = s0[g] * g_tot[:, grp[g]]
        for h in range(RWKV_GROUP):
            upd = upd + jnp.where(lane_head == h, full[g][h * RWKV_HEAD:(h + 1) * RWKV_HEAD, :], 0.0)
        state[:, grp[g]] = upd

    inv_n = 1.0 / RWKV_HEAD
    for g, sl in enumerate(grp):
        dev = y[g] - head_sums([y[g]])[0] * inv_n
        var = head_sums([dev * dev])[0] * inv_n
        yn = dev * lax.rsqrt(var + RWKV_GN_EPS) * gn_w[:, sl] + gn_b[:, sl]
        o_ref[:, sl] = ((yn + pre[g][1] * v[:, sl]) * gate[:, sl]).astype(o_ref.dtype)

    @pl.when(c_idx == pl.num_programs(1) - 1)
    def _():
        s_ref[0] = state[...]


def rwkv7_prompt(rows, proj, mu_big, mu_small, vecs, w2, a2, g2, *, passes=1):
    assert CHUNK == RWKV_HEAD and rows.t % CHUNK == 0
    nc = rows.t // CHUNK
    row_blk = lambda b, c: b * nc + c
    small_col = lambda name: DST[name][0] // LANES
    const = lambda shape: pl.BlockSpec(shape, lambda b, c: (0, 0))
    return pl.pallas_call(
        functools.partial(_rwkv_chunk_kernel, passes=passes),
        grid=(rows.bp, nc),
        in_specs=[pl.BlockSpec((CHUNK, BIG_W), lambda b, c: (row_blk(b, c), 1)),
                  pl.BlockSpec((CHUNK, LANES), lambda b, c: (row_blk(b, c), small_col("wl"))),
                  pl.BlockSpec((CHUNK, LANES), lambda b, c: (row_blk(b, c), small_col("al"))),
                  pl.BlockSpec((CHUNK, LANES), lambda b, c: (row_blk(b, c), small_col("gl7"))),
                  const((1, BIG_W)), const((1, SMALL_W)), const((SUBLANES, RWKV_WIDTH)),
                  const((LANES, RWKV_WIDTH)), const((LANES, RWKV_WIDTH)), const((LANES, RWKV_WIDTH))],
        out_specs=[pl.BlockSpec((CHUNK, RWKV_WIDTH), lambda b, c: (row_blk(b, c), 0)),
                   pl.BlockSpec((1, RWKV_HEAD, RWKV_WIDTH), lambda b, c: (b, 0, 0))],
        out_shape=[jax.ShapeDtypeStruct((rows.rows, RWKV_WIDTH), BF16),
                   jax.ShapeDtypeStruct((rows.bp, RWKV_HEAD, RWKV_WIDTH), F32)],
        scratch_shapes=[pltpu.VMEM((RWKV_HEAD, RWKV_WIDTH), F32), pltpu.VMEM((SUBLANES, BIG_W), F32),
                        pltpu.VMEM((SUBLANES, SMALL_W), F32)],
        compiler_params=_params("parallel", "arbitrary"),
    )(proj, proj, proj, proj, mu_big, mu_small, vecs, w2, a2, g2)


def _gla_chunk_kernel(qk_ref, v_ref, og_ref, gl_ref, w2_ref, gb_ref, nw_ref, o_ref, s_ref, state):
    c_idx = pl.program_id(1)
    n_tok = qk_ref.shape[0]

    @pl.when(c_idx == 0)
    def _():
        state[...] = jnp.zeros_like(state)

    logd = -_softplus(-(_dotp(gl_ref[...], w2_ref[...]) + gb_ref[...])) * (1.0 / GLA_GATE_NORMALIZER)
    b_all = _cumsum_rows(logd)
    row_k = lax.broadcasted_iota(jnp.int32, (n_tok, GLA_DK), 0)
    row_s = lax.broadcasted_iota(jnp.int32, (SUB, GLA_DK), 0)

    for h in range(GLA_HEADS):
        ks = slice(h * GLA_DK, (h + 1) * GLA_DK)
        vs = slice(h * GLA_DV, (h + 1) * GLA_DV)
        q = qk_ref[:, ks] * (GLA_DK ** -0.5)
        k = qk_ref[:, GLA_KEY_WIDTH + h * GLA_DK:GLA_KEY_WIDTH + (h + 1) * GLA_DK]
        b, v = b_all[:, ks], v_ref[:, vs]
        b_last = b[n_tok - 1:n_tok, :]
        st = state[h]
        o_inter = _dotp(q * jnp.exp(b), st, NT)
        blocks = []
        for i in range(n_tok // SUB):
            lo = i * SUB
            qs, kb, bs, vb = q[lo:lo + SUB], k[lo:lo + SUB], b[lo:lo + SUB], v[lo:lo + SUB]
            acc = o_inter[lo:lo + SUB]
            if i > 0:
                b_ref = b[lo - 1:lo, :]
                q_hat = qs * jnp.exp(bs - b_ref)
                earlier = row_k < lo
                k_hat = jnp.where(earlier, k * jnp.exp(jnp.where(earlier, b_ref - b, 0.0)), 0.0)
                att = _dotp(q_hat, k_hat, NT)
                acc = acc + _dotp(att, v, NN)
            for j in range(SUB):
                valid = row_s >= j
                w_j = jnp.where(valid, kb[j:j + 1] * jnp.exp(jnp.where(valid, bs - bs[j:j + 1], 0.0)), 0.0)
                col = jnp.sum(qs * w_j, axis=1, keepdims=True)
                acc = acc + col * vb[j:j + 1]
            blocks.append(acc)
        o = jnp.concatenate(blocks, axis=0)
        og = og_ref[:, vs]
        o_ref[:, vs] = (_rmsnorm(o, nw_ref[...], GLA_NORM_EPS) * (og * jax.nn.sigmoid(og))).astype(o_ref.dtype)
        k_dec = k * jnp.exp(b_last - b)
        state[h] = st * jnp.exp(b_last) + _dotp(v.T, k_dec, NN)

    @pl.when(c_idx == pl.num_programs(1) - 1)
    def _():
        s_ref[0] = state[...]


def gla_prompt(rows, proj, gate_w2, gate_b, norm_w):
    nc = rows.t // CHUNK
    row_blk = lambda b, c: b * nc + c
    wide = lambda col: pl.BlockSpec((CHUNK, GLA_WIDTH), lambda b, c: (row_blk(b, c), col))
    const = lambda shape: pl.BlockSpec(shape, lambda b, c: (0, 0))
    return pl.pallas_call(
        _gla_chunk_kernel,
        grid=(rows.bp, nc),
        in_specs=[wide(0), wide(1), wide(2),
                  pl.BlockSpec((CHUNK, LANES), lambda b, c: (row_blk(b, c), DST["gl"][0] // LANES)),
                  const((LANES, GLA_KEY_WIDTH)), const((1, GLA_KEY_WIDTH)), const((1, GLA_DV))],
        out_specs=[wide(0), pl.BlockSpec((1, GLA_HEADS, GLA_DV, GLA_DK), lambda b, c: (b, 0, 0, 0))],
        out_shape=[jax.ShapeDtypeStruct((rows.rows, GLA_WIDTH), BF16),
                   jax.ShapeDtypeStruct((rows.bp, GLA_HEADS, GLA_DV, GLA_DK), F32)],
        scratch_shapes=[pltpu.VMEM((GLA_HEADS, GLA_DV, GLA_DK), F32)],
        compiler_params=_params("parallel", "arbitrary"),
    )(proj, proj, proj, proj, gate_w2, gate_b, norm_w)


def _outproj_kernel(og_ref, orw_ref, x_ref, gt_ref, sc_ref, sh_ref, nw_ref, wo_ref, wr_ref,
                    x1s_ref, h2s_ref, lgs_ref, x1_ref, h2_ref, lg_ref, *, n_prompt_tiles):
    i = pl.program_id(0)

    @pl.when(i < n_prompt_tiles)
    def _():
        half = og_ref.shape[1]
        mix = _dot(og_ref[...], wo_ref[:half, :]) + _dot(orw_ref[...], wo_ref[half:, :])
        x1 = x_ref[...] + gt_ref[0] * mix
        h2 = _rmsnorm(x1, nw_ref[...], NORM_EPS) * (1.0 + sc_ref[0]) + sh_ref[0]
        x1_ref[...] = x1
        h2_ref[...] = h2
        lg_ref[...] = _dotp(h2, wr_ref[...], NN, 3)

    @pl.when(i >= n_prompt_tiles)
    def _():
        x1_ref[...] = x1s_ref[...]
        h2_ref[...] = h2s_ref[...]
        lg_ref[...] = lgs_ref[...]


def out_projection(t, o_gla, o_rwkv, xp, mod_p, norm_w, w_out, w_router, x1_s, h2_s, logits_s):
    n_p, d = xp.shape
    pad_s = x1_s.shape[0]
    tm = ROW_TILE
    npt, per = n_p // tm, t // tm
    prompt = lambda width: pl.BlockSpec((tm, width), lambda i: (jnp.minimum(i, npt - 1), 0))
    pmod = lambda col: pl.BlockSpec((1, 1, d), lambda i: (jnp.minimum(i // per, mod_p.shape[0] - 1), 0, col))
    sample = lambda width: pl.BlockSpec((tm, width), lambda i: (jnp.maximum(i - npt, 0), 0))
    rows = lambda width: pl.BlockSpec((tm, width), lambda i: (i, 0))
    const = lambda a: pl.BlockSpec(a.shape, lambda i: (0, 0), pipeline_mode=pl.Buffered(1))
    n_rows = n_p + pad_s
    return pl.pallas_call(
        functools.partial(_outproj_kernel, n_prompt_tiles=npt),
        grid=(n_rows // tm,),
        in_specs=[prompt(o_gla.shape[1]), prompt(o_rwkv.shape[1]), prompt(d), pmod(2), pmod(4), pmod(3),
                  const(norm_w), const(w_out), const(w_router), sample(d), sample(d), sample(LANES)],
        out_specs=[rows(d), rows(d), rows(LANES)],
        out_shape=[jax.ShapeDtypeStruct((n_rows, d), F32), jax.ShapeDtypeStruct((n_rows, d), F32),
                   jax.ShapeDtypeStruct((n_rows, LANES), F32)],
        compiler_params=_params("arbitrary"),
    )(o_gla, o_rwkv, xp, mod_p, mod_p, mod_p, norm_w, w_out, w_router, x1_s, h2_s, logits_s)


def _moe_kernel(tile_expert_ref, n_valid_ref, n_real_ref, src_ref, dst_ref, h2_hbm, rw_ref, wg_ref, wu_ref, wd_ref,
                y_hbm, xbuf, obuf, gather_sem, scatter_sem):
    del tile_expert_ref
    i = pl.program_id(0)
    n_tiles = pl.num_programs(0)
    n_valid = n_valid_ref[0]
    slot = i % 2

    def for_rows(n, fn):
        def group(g, carry):
            for u in range(DMA_UNROLL):
                fn(g * DMA_UNROLL + u)
            return carry

        def single(r, carry):
            fn(r)
            return carry
        full = n // DMA_UNROLL
        lax.fori_loop(0, full, group, 0)
        lax.fori_loop(full * DMA_UNROLL, n, single, 0)

    def start_gather(tile, s):
        def one(r):
            tok = src_ref[tile * MOE_TILE + r]
            pltpu.make_async_copy(h2_hbm.at[pl.ds(tok, 1), :], xbuf.at[s, pl.ds(r, 1), :],
                                  gather_sem.at[s]).start()
        for_rows(MOE_TILE, one)

    def wait_gather(s):
        pltpu.make_async_copy(h2_hbm.at[pl.ds(0, MOE_TILE), :], xbuf.at[s], gather_sem.at[s]).wait()

    def scatter_row(tile, s, r):
        row = dst_ref[tile * MOE_TILE + r]
        return pltpu.make_async_copy(obuf.at[s, pl.ds(r, 1), :], y_hbm.at[pl.ds(row, 1), :], scatter_sem.at[s])

    def start_scatter(tile, s):
        for_rows(n_real_ref[tile], lambda r: scatter_row(tile, s, r).start())

    def wait_scatter(tile, s):
        n_real = n_real_ref[tile]

        @pl.when(n_real == MOE_TILE)
        def _():
            pltpu.make_async_copy(obuf.at[s], y_hbm.at[pl.ds(0, MOE_TILE), :], scatter_sem.at[s]).wait()

        @pl.when(n_real < MOE_TILE)
        def _():
            for_rows(n_real, lambda r: scatter_row(tile, s, r).wait())

    @pl.when(i == 0)
    def _():
        start_gather(0, 0)

    @pl.when(i + 1 < n_valid)
    def _():
        start_gather(i + 1, 1 - slot)

    @pl.when(i < n_valid)
    def _():
        wait_gather(slot)

        @pl.when(i >= 2)
        def _():
            wait_scatter(i - 2, slot)

        x = xbuf[slot].astype(BF16)
        gate = _dot(x, wg_ref[0].astype(BF16))
        up = _dot(x, wu_ref[0].astype(BF16))
        hid = gate * jax.nn.sigmoid(gate) * up * rw_ref[...]
        obuf[slot] = _dot(hid.astype(BF16), wd_ref[0].astype(BF16))
        start_scatter(i, slot)

    @pl.when(i == n_tiles - 1)
    def _():
        @pl.when(n_valid >= 2)
        def _():
            wait_scatter(n_valid - 2, n_valid % 2)

        wait_scatter(n_valid - 1, (n_valid - 1) % 2)


def moe_experts(h2, plan, w_gate, w_up, w_down):
    n_rows, d = h2.shape
    _, _, ff = w_gate.shape
    n_tiles = plan["tile_expert"].shape[0]
    grid_spec = pltpu.PrefetchScalarGridSpec(
        num_scalar_prefetch=5,
        grid=(n_tiles,),
        in_specs=[pl.BlockSpec(memory_space=pl.ANY),
                  pl.BlockSpec((MOE_TILE, 1), lambda i, te, *_: (i, 0)),
                  pl.BlockSpec((1, d, ff), lambda i, te, *_: (te[i], 0, 0)),
                  pl.BlockSpec((1, d, ff), lambda i, te, *_: (te[i], 0, 0)),
                  pl.BlockSpec((1, ff, d), lambda i, te, *_: (te[i], 0, 0))],
        out_specs=pl.BlockSpec(memory_space=pl.ANY),
        scratch_shapes=[pltpu.VMEM((2, MOE_TILE, d), F32), pltpu.VMEM((2, MOE_TILE, d), F32),
                        pltpu.SemaphoreType.DMA((2,)), pltpu.SemaphoreType.DMA((2,))],
    )
    return pl.pallas_call(
        _moe_kernel,
        grid_spec=grid_spec,
        out_shape=jax.ShapeDtypeStruct((2 * n_rows, d), F32),
        compiler_params=_params("arbitrary"),
    )(plan["tile_expert"], plan["n_valid"], plan["n_real"], plan["src_row"], plan["dst_row"], h2, plan["row_w"],
      w_gate, w_up, w_down)


def _route(logits):
    p_group = jax.nn.softmax(logits[:, :N_GROUPS], axis=-1)
    g_sel = jnp.argmax(p_group, axis=-1)
    p_sel = jnp.take_along_axis(p_group, g_sel[:, None], axis=-1)
    logits_e = logits[:, N_GROUPS:N_GROUPS + N_EXPERTS].reshape(-1, N_GROUPS, EXPERTS_PER_GROUP)
    logits_e = jnp.take_along_axis(logits_e, g_sel[:, None, None], axis=1)[:, 0]
    i1 = jnp.argmax(logits_e, axis=-1)
    v1 = jnp.take_along_axis(logits_e, i1[:, None], axis=-1)
    masked = jnp.where(jnp.arange(EXPERTS_PER_GROUP)[None, :] == i1[:, None], -jnp.inf, logits_e)
    i2 = jnp.argmax(masked, axis=-1)
    v2 = jnp.take_along_axis(logits_e, i2[:, None], axis=-1)
    w_top = jax.nn.softmax(jnp.concatenate([v1, v2], axis=-1), axis=-1) * p_sel
    top_i = jnp.stack([i1, i2], axis=-1)
    expert = g_sel[:, None].astype(jnp.int32) * EXPERTS_PER_GROUP + top_i.astype(jnp.int32)
    return expert, w_top


def _moe_plan(expert, w_top):
    n = expert.shape[0]
    e_flat = expert.reshape(-1)
    ids = jnp.arange(N_EXPERTS, dtype=jnp.int32)
    one_hot = (e_flat[:, None] == ids[None, :]).astype(jnp.int32)
    counts = jnp.sum(one_hot, axis=0)
    padded = (counts + MOE_TILE - 1) // MOE_TILE * MOE_TILE
    ends = jnp.cumsum(padded)
    starts = ends - padded
    rank = jnp.sum(jnp.cumsum(one_hot, axis=0) * one_hot, axis=1) - 1
    pos = jnp.sum(one_hot * starts[None, :], axis=1) + rank
    p_max = _round_up(2 * n, MOE_TILE) + N_EXPERTS * MOE_TILE
    n_tiles = p_max // MOE_TILE
    pair = jnp.full((p_max,), -1, jnp.int32).at[pos].set(jnp.arange(2 * n, dtype=jnp.int32))
    real = pair >= 0
    pair0 = jnp.maximum(pair, 0)
    src_row = pair0 // 2
    dst_row = (pair0 % 2) * n + src_row
    row_w = jnp.where(real, w_top.reshape(-1)[pair0], 0.0)
    tile_start = jnp.arange(n_tiles, dtype=jnp.int32) * MOE_TILE
    n_valid = ends[-1] // MOE_TILE
    tile_expert = jnp.sum((tile_start[:, None] >= ends[None, :]).astype(jnp.int32), axis=1)
    tile_expert = jnp.where(tile_start < ends[-1], tile_expert, tile_expert[n_valid - 1])
    n_real = jnp.clip((starts + counts)[tile_expert] - tile_start, 0, MOE_TILE)
    n_real = jnp.where(tile_start < ends[-1], n_real, 0).astype(jnp.int32)
    return dict(src_row=src_row, dst_row=dst_row, row_w=row_w[:, None], tile_expert=tile_expert,
                n_valid=n_valid.astype(jnp.int32)[None], n_real=n_real)


def _final_kernel(x1_ref, y0_ref, y1_ref, gtp_ref, gts_ref, scp_ref, scs_ref, shp_ref, shs_ref, nw_ref,
                  yp_ref, ys_ref, *, n_prompt_tiles):
    i = pl.program_id(0)

    def finish(gt, sc, sh):
        x2 = x1_ref[...] + gt * (y0_ref[...] + y1_ref[...])
        return _rmsnorm(x2, nw_ref[...], NORM_EPS) * (1.0 + sc) + sh

    @pl.when(i < n_prompt_tiles)
    def _():
        yp_ref[...] = finish(gtp_ref[0], scp_ref[0], shp_ref[0])

    @pl.when(i >= n_prompt_tiles)
    def _():
        ys_ref[...] = finish(gts_ref[...], scs_ref[...], shs_ref[...])


def final_norm(n_p, t, x1, y_pairs, mod_p, mod_s, modf_p, modf_s, norm_w):
    n_rows, d = x1.shape
    tile = ROW_TILE
    n_tiles, npt = n_rows // tile, n_p // tile
    per = t // tile
    rows = lambda off: pl.BlockSpec((tile, d), lambda i: (i + off, 0))
    prompt_out = pl.BlockSpec((tile, d), lambda i: (jnp.minimum(i, npt - 1), 0))
    sample = lambda col: pl.BlockSpec((tile, d), lambda i: (jnp.maximum(i - npt, 0), col))
    pmod = lambda col: pl.BlockSpec((1, 1, d), lambda i: (jnp.minimum(i // per, mod_p.shape[0] - 1), 0, col))
    return pl.pallas_call(
        functools.partial(_final_kernel, n_prompt_tiles=npt),
        grid=(n_tiles,),
        in_specs=[rows(0), rows(0), rows(n_tiles), pmod(5), sample(5), pmod(1), sample(1), pmod(0), sample(0),
                  pl.BlockSpec((1, d), lambda i: (0, 0))],
        out_specs=[prompt_out, sample(0)],
        out_shape=[jax.ShapeDtypeStruct((n_p, d), F32), jax.ShapeDtypeStruct((n_rows - n_p, d), F32)],
        compiler_params=_params("arbitrary"),
    )(x1, y_pairs, y_pairs, mod_p, mod_s, modf_p, modf_s, modf_p, modf_s, norm_w)


def _align_cols(a):
    parts = []
    for name in _ALIGNED:
        off, w = SRC[name]
        part = a[..., off:off + w]
        wa = DST[name][1]
        if wa != w:
            part = jnp.pad(part, [(0, 0)] * (a.ndim - 1) + [(0, wa - w)])
        parts.append(part)
    return jnp.concatenate(parts, axis=-1)


def _rwkv_seg(a, name):
    off, w = SRC[name]
    return a[..., off - RWKV_SRC_BASE:off - RWKV_SRC_BASE + w]


def _rwkv_seg_padded(a, name):
    seg = _rwkv_seg(a, name)
    wa = DST[name][1]
    return jnp.pad(seg, [(0, 0)] * (a.ndim - 1) + [(0, wa - seg.shape[-1])])


def _orig_seg(p, name):
    off, w = SRC[name]
    return p[..., off:off + w]


def _pad_rows(w, rows):
    return jnp.pad(w, ((0, rows - w.shape[0]), (0, 0)))


def kernel(x_prompt, x_sample, state_gla, state_rwkv, state_shift, c_prompt, c_sample, w_ada, b_ada, norm_mix, norm_ffn, w_in, gla_gate_w2, gla_gate_b, gla_norm, rwkv_mu, rwkv_w0, rwkv_w2, rwkv_a0, rwkv_a2, rwkv_g2, rwkv_k_k, rwkv_k_a, rwkv_r_k, rwkv_gn_w, rwkv_gn_b, w_out, w_router_group, w_router_expert, w_exp_gate, w_exp_up, w_exp_down, norm_final, w_ada_final, b_ada_final):
    assert w_ada.shape[0] == 1, "single-layer step"
    bp, t, d = x_prompt.shape
    bs = x_sample.shape[0]
    assert x_sample.shape[1] == 1 and t % ROW_TILE == 0
    n_p = bp * t
    pad_s = _round_up(bs, ROW_TILE)
    n_rows = n_p + pad_s
    prompt = _PromptShape(bp, t, n_p)
    hi = lax.Precision.HIGHEST
    pad_sample = lambda a: jnp.pad(a, ((0, pad_s - a.shape[0]), (0, 0)))
    hdot = lambda a, w: jnp.dot(a, w, precision=hi)

    c_act = jax.nn.silu(jnp.concatenate([c_prompt, c_sample], axis=0))
    mod = matmul3(c_act, w_ada[0]) + b_ada[0]
    mod_f = matmul3(c_act, w_ada_final) + b_ada_final
    mod_p, mod_s = mod[:bp, None, :], mod[bp:]
    modf_p, modf_s = mod_f[:bp, None, :], mod_f[bp:]
    sh1_s, sc1_s, gt1_s, sh2_s, sc2_s, _ = jnp.split(mod_s, 6, axis=-1)
    xp = x_prompt.reshape(n_p, d)
    xs = x_sample[:, 0, :]

    proj = in_projection(xp, t, mod_p, norm_mix, _align_cols(w_in[0]).astype(BF16))
    o_gla, gla_t_p = gla_prompt(prompt, proj, _pad_rows(gla_gate_w2[0], LANES), gla_gate_b, gla_norm)
    new_gla_p = jnp.swapaxes(gla_t_p, -1, -2)
    mu = rwkv_mu[0]
    mu_big = jnp.concatenate([_rwkv_seg(mu, n) for n in ("r", "k7", "v7")])[None, :]
    mu_small = jnp.concatenate([_rwkv_seg_padded(mu, n) for n in ("wl", "al", "gl7")])[None, :]
    vecs = jnp.concatenate([rwkv_w0, rwkv_a0, rwkv_k_k, rwkv_k_a, rwkv_r_k[0].reshape(1, RWKV_WIDTH),
                            rwkv_gn_w, rwkv_gn_b, jnp.zeros((1, RWKV_WIDTH), F32)], axis=0)
    o_rwkv, rwkv_t_p = rwkv7_prompt(prompt, proj, mu_big, mu_small, vecs, _pad_rows(rwkv_w2[0], LANES),
                                    _pad_rows(rwkv_a2[0], LANES), rwkv_g2[0])
    new_rwkv_p = rwkv_t_p.reshape(bp, RWKV_HEAD, RWKV_HEADS, RWKV_HEAD).transpose(0, 2, 1, 3)
    last = proj[t - 1:n_p:t]
    new_shift_p = jnp.concatenate([last[:, DST[n][0]:DST[n][0] + SRC[n][1]] for n in _RWKV_ORIG], axis=-1)

    h1_s = _rmsnorm(xs, norm_mix[0], NORM_EPS) * (1.0 + sc1_s) + sh1_s
    proj_s = matmul3(h1_s, w_in[0])
    heads = lambda a, n: a.reshape(a.shape[0], n, -1)
    logd_s = jax.nn.log_sigmoid(hdot(_orig_seg(proj_s, "gl"), gla_gate_w2[0]) + gla_gate_b[0]) \
        / GLA_GATE_NORMALIZER
    q_s = heads(_orig_seg(proj_s, "q") * (GLA_DK ** -0.5), GLA_HEADS)
    k_s, v_s, g_s = heads(_orig_seg(proj_s, "k"), GLA_HEADS), heads(_orig_seg(proj_s, "v"), GLA_HEADS), \
        heads(logd_s, GLA_HEADS)
    new_gla_s = jnp.exp(g_s)[..., None] * state_gla[0] + k_s[..., :, None] * v_s[..., None, :]
    o_s = jnp.einsum("bhd,bhde->bhe", q_s, new_gla_s, precision=hi)
    o_s = _rmsnorm(o_s, gla_norm[0], GLA_NORM_EPS) * heads(jax.nn.silu(_orig_seg(proj_s, "og")), GLA_HEADS)

    rp_s = proj_s[:, RWKV_SRC_BASE:]
    new_shift_s = rp_s
    xs7 = rp_s + (state_shift[0] - rp_s) * mu
    sx = lambda name: _rwkv_seg(xs7, name)
    r_s, k7_s, v7_s = sx("r"), sx("k7"), sx("v7")
    w_pre = rwkv_w0[0] + hdot(jnp.tanh(sx("wl")), rwkv_w2[0])
    decay_s = jnp.exp(-jnp.exp(-jax.nn.softplus(-w_pre) - 0.5))
    a_s = jax.nn.sigmoid(rwkv_a0[0] + hdot(sx("al"), rwkv_a2[0]))
    g_s7 = hdot(jax.nn.sigmoid(sx("gl7")), rwkv_g2[0])
    hs = lambda z: z.reshape(bs, RWKV_HEADS, RWKV_HEAD)
    kk_s = hs(k7_s * rwkv_k_k[0])
    kk_s = kk_s / jnp.maximum(jnp.sqrt(jnp.sum(kk_s * kk_s, axis=-1, keepdims=True)), 1e-12)
    k7_s = k7_s * (1.0 + (a_s - 1.0) * rwkv_k_a[0])
    r_h, w_h, k_h, v_h, a_h = hs(r_s), hs(decay_s), hs(k7_s), hs(v7_s), hs(a_s)
    s_prev = state_rwkv[0]
    sa = jnp.einsum("bhij,bhj->bhi", s_prev, -kk_s, precision=hi)
    new_rwkv_s = s_prev * w_h[:, :, None, :] + sa[..., None] * (kk_s * a_h)[:, :, None, :] \
        + v_h[..., None] * k_h[:, :, None, :]
    y_s = jnp.einsum("bhij,bhj->bhi", new_rwkv_s, r_h, precision=hi)
    y_mu = jnp.mean(y_s, axis=-1, keepdims=True)
    y_var = jnp.mean(jnp.square(y_s - y_mu), axis=-1, keepdims=True)
    y_s = (y_s - y_mu) * lax.rsqrt(y_var + RWKV_GN_EPS) * rwkv_gn_w[0].reshape(RWKV_HEADS, RWKV_HEAD) \
        + rwkv_gn_b[0].reshape(RWKV_HEADS, RWKV_HEAD)
    bonus = jnp.sum(r_h * k_h * rwkv_r_k[0], axis=-1, keepdims=True) * v_h
    o_rs = (y_s + bonus).reshape(bs, RWKV_WIDTH) * g_s7
    mix_s = matmul3(jnp.concatenate([o_s.reshape(bs, GLA_WIDTH), o_rs], axis=-1), w_out[0])
    x1_s = xs + gt1_s * mix_s
    h2_s = _rmsnorm(x1_s, norm_ffn[0], NORM_EPS) * (1.0 + sc2_s) + sh2_s
    w_router = jnp.pad(jnp.concatenate([w_router_group[0], w_router_expert[0]], axis=-1),
                       ((0, 0), (0, LANES - N_GROUPS - N_EXPERTS)))
    logits_s = hdot(h2_s, w_router)

    x1, h2, logits = out_projection(t, o_gla, o_rwkv, xp, mod_p, norm_ffn, w_out[0].astype(BF16), w_router,
                                    pad_sample(x1_s), pad_sample(h2_s), pad_sample(logits_s))

    expert, w_top = _route(logits)
    plan = _moe_plan(expert, w_top)
    y_pairs = moe_experts(h2, plan, w_exp_gate[0].reshape(N_EXPERTS, d, EXPERT_FF),
                          w_exp_up[0].reshape(N_EXPERTS, d, EXPERT_FF),
                          w_exp_down[0].reshape(N_EXPERTS, EXPERT_FF, d))

    y_p, y_s_pad = final_norm(n_p, t, x1, y_pairs, mod_p, pad_sample(mod_s), modf_p, pad_sample(modf_s),
                              norm_final[None, :])
    return (y_p.reshape(bp, t, d), y_s_pad[:bs, None, :], new_gla_p[None], new_rwkv_p[None], new_shift_p[None],
            new_gla_s[None], new_rwkv_s[None], new_shift_s[None])
```

```python
import collections
import functools

import jax
import jax.numpy as jnp
from jax import lax
from jax.experimental import pallas as pl
from jax.experimental.pallas import tpu as pltpu

F32 = jnp.float32
BF16 = jnp.bfloat16

D_MODEL = 2048
GLA_HEADS = 4
GLA_DK = 128
GLA_DV = 256
GLA_KEY_WIDTH = GLA_HEADS * GLA_DK
GLA_WIDTH = GLA_HEADS * GLA_DV
GLA_GATE_NORMALIZER = 16.0
RWKV_HEAD = 64
RWKV_HEADS = 16
RWKV_WIDTH = RWKV_HEAD * RWKV_HEADS
N_GROUPS = 4
EXPERTS_PER_GROUP = 8
N_EXPERTS = N_GROUPS * EXPERTS_PER_GROUP
EXPERT_FF = 512
NORM_EPS = 1e-6
GLA_NORM_EPS = 1e-5
RWKV_GN_EPS = 64e-5

LANES = 128
SUBLANES = 8
VMEM_LIMIT_BYTES = 56 * 1024 * 1024

CHUNK = 64
SUB = 16
RWKV_GROUP = 2
GROUP_W = RWKV_GROUP * RWKV_HEAD
GLA_SEQS = 2
LOG2E = 1.4426950408889634
RWKV_SEQS = 4
MOE_TILE = 256
DMA_UNROLL = 8
ROW_TILE = 256
NORM_SLAB = 128
K_SPLIT = 256

_ORIG = (("q", 512), ("k", 512), ("v", 1024), ("gl", 16), ("og", 1024),
         ("r", 1024), ("wl", 64), ("k7", 1024), ("v7", 1024), ("al", 64), ("gl7", 128))
_ALIGNED = ("q", "k", "v", "og", "r", "k7", "v7", "gl", "wl", "al", "gl7")
_RWKV_ORIG = ("r", "wl", "k7", "v7", "al", "gl7")


def _round_up(n, m):
    return (n + m - 1) // m * m


def _layouts():
    src, off = {}, 0
    for name, w in _ORIG:
        src[name] = (off, w)
        off += w
    dst, pos = {}, 0
    for name in _ALIGNED:
        wa = _round_up(src[name][1], LANES)
        dst[name] = (pos, wa)
        pos += wa
    return src, off, dst, pos


SRC, IN_COLS, DST, IN_COLS_ALIGNED = _layouts()
RWKV_SRC_BASE = SRC["r"][0]
BIG_W = 3 * RWKV_WIDTH
SMALL_W = 3 * LANES
assert DST["r"][0] == BIG_W and DST["k7"][0] == BIG_W + RWKV_WIDTH and DST["v7"][0] == BIG_W + 2 * RWKV_WIDTH
assert DST["v"][0] == RWKV_WIDTH and DST["og"][0] == 2 * RWKV_WIDTH and DST["k"][0] == GLA_KEY_WIDTH

NN = ((1,), (0,))
NT = ((1,), (1,))

_PromptShape = collections.namedtuple("_PromptShape", "bp t rows")


def _dot(a, b, dims=NN):
    return lax.dot_general(a, b, (dims, ((), ())), preferred_element_type=F32)


def _split2(x):
    hi = x.astype(BF16)
    lo = (x - hi.astype(F32)).astype(BF16)
    return hi, lo


def _dotp(a, b, dims=NN, passes=1):
    if passes == 1:
        return _dot(a.astype(BF16), b.astype(BF16), dims)
    ah, al = _split2(a)
    bh, bl = _split2(b)
    return _dot(ah, bh, dims) + (_dot(ah, bl, dims) + _dot(al, bh, dims))


def _cumsum_rows(x):
    n = x.shape[0]
    row = lax.broadcasted_iota(jnp.int32, (n, n), 0)
    col = lax.broadcasted_iota(jnp.int32, (n, n), 1)
    tri = (row >= col).astype(BF16)
    x1 = x.astype(BF16)
    r1 = x - x1.astype(F32)
    x2 = r1.astype(BF16)
    x3 = (r1 - x2.astype(F32)).astype(BF16)
    return _dot(tri, x1) + (_dot(tri, x2) + _dot(tri, x3))


def _softplus(z):
    return jnp.maximum(z, 0.0) + jnp.log1p(jnp.exp(-jnp.abs(z)))


def _rmsnorm(x, g, eps):
    return x * lax.rsqrt(jnp.mean(x * x, axis=-1, keepdims=True) + eps) * g


def _largest_tile(n, cap, mult=SUBLANES):
    if n <= cap:
        return n
    best = None
    for t in range(mult, cap + 1, mult):
        if n % t == 0:
            best = t
    assert best is not None, (n, cap)
    return best


def _params(*sem):
    return pltpu.CompilerParams(dimension_semantics=sem, vmem_limit_bytes=VMEM_LIMIT_BYTES)


def _mm3_kernel(a_ref, w_ref, o_ref):
    @pl.when(pl.program_id(0) == 0)
    def _():
        o_ref[...] = jnp.zeros_like(o_ref)

    o_ref[...] += _dotp(a_ref[...], w_ref[...], NN, 3)


def matmul3(a, w):
    m, k = a.shape
    n = w.shape[1]
    tk = _largest_tile(k, K_SPLIT, LANES)
    return pl.pallas_call(
        _mm3_kernel,
        grid=(k // tk,),
        in_specs=[pl.BlockSpec((m, tk), lambda s: (0, s)), pl.BlockSpec((tk, n), lambda s: (s, 0))],
        out_specs=pl.BlockSpec((m, n), lambda s: (0, 0)),
        out_shape=jax.ShapeDtypeStruct((m, n), F32),
        compiler_params=_params("arbitrary"),
    )(a, w)


def _prompt_mod(tiles_per_batch, width, col):
    return pl.BlockSpec((1, 1, width), lambda i, *_: (i // tiles_per_batch, 0, col))


def _inproj_kernel(x_ref, sc_ref, sh_ref, nw_ref, w_ref, o_ref, h_scr):
    @pl.when(pl.program_id(1) == 0)
    def _():
        for r in range(0, h_scr.shape[0], NORM_SLAB):
            sl = slice(r, r + NORM_SLAB)
            h = _rmsnorm(x_ref[sl, :], nw_ref[...], NORM_EPS) * (1.0 + sc_ref[0]) + sh_ref[0]
            h_scr[sl, :] = h.astype(BF16)

    o_ref[...] = _dot(h_scr[...], w_ref[...])


def in_projection(xp, t, mod_p, norm_w, w_aligned):
    n_p, d = xp.shape
    n = w_aligned.shape[1]
    tm = _largest_tile(t, 1024, NORM_SLAB)
    tn = _largest_tile(n, 512, LANES)
    per = t // tm
    return pl.pallas_call(
        _inproj_kernel,
        grid=(n_p // tm, n // tn),
        in_specs=[pl.BlockSpec((tm, d), lambda i, j: (i, 0)), _prompt_mod(per, d, 1), _prompt_mod(per, d, 0),
                  pl.BlockSpec((1, d), lambda i, j: (0, 0)), pl.BlockSpec((d, tn), lambda i, j: (0, j))],
        out_specs=pl.BlockSpec((tm, tn), lambda i, j: (i, j)),
        out_shape=jax.ShapeDtypeStruct((n_p, n), F32),
        scratch_shapes=[pltpu.VMEM((tm, d), BF16)],
        compiler_params=_params("parallel", "arbitrary"),
    )(xp, mod_p, mod_p, norm_w, w_aligned)


def _rwkv_chunk_kernel(big_ref, wl_ref, al_ref, gl_ref, mu_big_ref, mu_small_ref, vec_ref, w2_ref, a2_ref,
                       g2_ref, o_ref, s_ref, state, carry_big, carry_small, *, passes):
    c_idx = pl.program_id(1)
    n_seq, n_tok = big_ref.shape[0], big_ref.shape[1]
    n_grp = RWKV_WIDTH // GROUP_W
    grp = [slice(g * GROUP_W, (g + 1) * GROUP_W) for g in range(n_grp)]

    @pl.when(c_idx == 0)
    def _():
        state[...] = jnp.zeros_like(state)
        carry_big[...] = jnp.zeros_like(carry_big)
        carry_small[...] = jnp.zeros_like(carry_small)

    first_row = lax.broadcasted_iota(jnp.int32, (n_tok, 1), 0) == 0
    w0, a0, k_k, k_a, r_k, gn_w, gn_b = (vec_ref[i:i + 1, :] for i in range(7))

    gi = lax.broadcasted_iota(jnp.int32, (GROUP_W, GROUP_W), 0) // RWKV_HEAD
    gj = lax.broadcasted_iota(jnp.int32, (GROUP_W, GROUP_W), 1) // RWKV_HEAD
    head_ones = (gi == gj).astype(BF16)

    def head_sums(parts):
        hi, lo = _split2(jnp.concatenate(parts, axis=0))
        out = _dot(hi, head_ones) + _dot(lo, head_ones)
        return [out[i * n_tok:(i + 1) * n_tok] for i in range(len(parts))]

    def prepare(n):
        def token_shift(cur, carry, mu):
            prev = jnp.where(first_row, carry[n, 0:1, :], pltpu.roll(cur, 1, 0))
            carry[n, 0:1, :] = cur[n_tok - 1:n_tok, :]
            return cur + (prev - cur) * mu

        xs_big = token_shift(big_ref[n], carry_big, mu_big_ref[...])
        small = jnp.concatenate([wl_ref[n], al_ref[n], gl_ref[n]], axis=1)
        xs_small = token_shift(small, carry_small, mu_small_ref[...])
        r = xs_big[:, :RWKV_WIDTH]
        k7 = xs_big[:, RWKV_WIDTH:2 * RWKV_WIDTH]
        v = xs_big[:, 2 * RWKV_WIDTH:]
        w_pre = w0 + _dotp(jnp.tanh(xs_small[:, :LANES]), w2_ref[...])
        lw = -jnp.exp(-_softplus(-w_pre) - 0.5)
        a = jax.nn.sigmoid(a0 + _dotp(xs_small[:, LANES:2 * LANES], a2_ref[...]))
        gate = _dotp(jax.nn.sigmoid(xs_small[:, 2 * LANES:]), g2_ref[...])
        kk_raw = k7 * k_k
        k = k7 * (1.0 + (a - 1.0) * k_a)
        sums = [head_sums([kk_raw[:, sl] * kk_raw[:, sl], r[:, sl] * k[:, sl] * r_k[:, sl]]) for sl in grp]
        kk = jnp.concatenate([kk_raw[:, sl] / jnp.maximum(jnp.sqrt(sums[g][0]), 1e-12)
                              for g, sl in enumerate(grp)], axis=1)
        bonus = jnp.concatenate([sums[g][1] for g in range(n_grp)], axis=1) * v
        cum = _cumsum_rows(lw)
        cum_end = cum[n_tok - 1:n_tok, :]
        beta = kk * a
        g_inv = jnp.exp(-cum)
        g_end = jnp.exp(cum_end - cum)
        return dict(v=v, gate=gate, bonus=bonus, g_tot=jnp.exp(cum_end),
                    a_hat=-kk * jnp.exp(cum - lw), r_hat=r * jnp.exp(cum), b_hat=beta * g_inv,
                    k_hat=k * g_inv, b_end=beta * g_end, k_end=k * g_end)

    seqs = [prepare(n) for n in range(n_seq)]
    units = [(n, sl) for n in range(n_seq) for sl in grp]
    part = lambda name: [seqs[n][name][:, sl] for n, sl in units]

    lane = lax.broadcasted_iota(jnp.int32, (n_tok, GROUP_W), 1)
    tok = lax.broadcasted_iota(jnp.int32, (n_tok, GROUP_W), 0)
    lane_head = lane // RWKV_HEAD
    src_tok = lane % RWKV_HEAD
    strict = tok > src_tok
    incl = tok >= src_tok

    def bd(y):
        return jnp.concatenate([jnp.where(lane_head == h, y, 0.0) for h in range(RWKV_GROUP)], axis=0)

    gs = range(len(units))
    s0 = [state[n, :, sl] for n, sl in units]
    v_u, b_hat, k_hat, b_end, k_end = part("v"), part("b_hat"), part("k_hat"), part("b_end"), part("k_end")
    lhs2 = [jnp.concatenate([a, r], axis=0) for a, r in zip(part("a_hat"), part("r_hat"))]
    abrb = [_dotp(lhs2[g], bd(b_hat[g]), NT, passes) for g in gs]
    akrk = [_dotp(lhs2[g], bd(k_hat[g]), NT, passes) for g in gs]
    asrs = [_dotp(lhs2[g], bd(s0[g]), NT, passes) for g in gs]
    p = [jnp.where(strict, abrb[g][:n_tok], 0.0) for g in gs]
    ak = [jnp.where(strict, akrk[g][:n_tok], 0.0) for g in gs]
    rb = [jnp.where(incl, abrb[g][n_tok:], 0.0) for g in gs]
    rk = [jnp.where(incl, akrk[g][n_tok:], 0.0) for g in gs]
    bd_v = [bd(v_u[g]) for g in gs]
    x = [asrs[g][:n_tok] + _dotp(ak[g], bd_v[g], NN, passes) for g in gs]
    n_sq = n_tok.bit_length() - 1
    for it in range(n_sq):
        if it < n_sq - 1:
            both = [_dotp(p[g], jnp.concatenate([bd(p[g]), bd(x[g])], axis=1), NN, passes) for g in gs]
            x = [x[g] + both[g][:, GROUP_W:] for g in gs]
            p = [both[g][:, :GROUP_W] for g in gs]
        else:
            x = [x[g] + _dotp(p[g], bd(x[g]), NN, passes) for g in gs]
    y = [asrs[g][n_tok:] + _dotp(jnp.concatenate([rb[g], rk[g]], axis=1),
                                 jnp.concatenate([bd(x[g]), bd_v[g]], axis=0), NN, passes) for g in gs]
    full = [_dotp(jnp.concatenate([x[g], v_u[g]], axis=0).T,
                  jnp.concatenate([b_end[g], k_end[g]], axis=0), NN, passes) for g in gs]
    g_tot = part("g_tot")
    for g, (n, sl) in enumerate(units):
        upd = s0[g] * g_tot[g]
        for h in range(RWKV_GROUP):
            upd = upd + jnp.where(lane_head == h, full[g][h * RWKV_HEAD:(h + 1) * RWKV_HEAD, :], 0.0)
        state[n, :, sl] = upd

    inv_n = 1.0 / RWKV_HEAD
    mean = [head_sums([y[g]])[0] * inv_n for g in gs]
    dev = [y[g] - mean[g] for g in gs]
    var = [head_sums([dev[g] * dev[g]])[0] * inv_n for g in gs]
    bonus, gate = part("bonus"), part("gate")
    for g, (n, sl) in enumerate(units):
        yn = dev[g] * lax.rsqrt(var[g] + RWKV_GN_EPS) * gn_w[:, sl] + gn_b[:, sl]
        o_ref[n, :, sl] = ((yn + bonus[g]) * gate[g]).astype(o_ref.dtype)

    @pl.when(c_idx == pl.num_programs(1) - 1)
    def _():
        s_ref[...] = state[...]


def rwkv7_prompt(rows, proj, mu_big, mu_small, vecs, w2, a2, g2, *, passes=1):
    assert CHUNK == RWKV_HEAD and rows.t % CHUNK == 0
    nc = rows.t // CHUNK
    n_seq = _largest_tile(rows.bp, RWKV_SEQS, 1)
    proj3 = proj.reshape(rows.bp, rows.t, proj.shape[1])
    small_col = lambda name: DST[name][0] // LANES
    tok = lambda width, col: pl.BlockSpec((n_seq, CHUNK, width), lambda b, c: (b, c, col))
    const = lambda shape: pl.BlockSpec(shape, lambda b, c: (0, 0))
    out, final_state = pl.pallas_call(
        functools.partial(_rwkv_chunk_kernel, passes=passes),
        grid=(rows.bp // n_seq, nc),
        in_specs=[tok(BIG_W, 1), tok(LANES, small_col("wl")), tok(LANES, small_col("al")),
                  tok(LANES, small_col("gl7")),
                  const((1, BIG_W)), const((1, SMALL_W)), const((SUBLANES, RWKV_WIDTH)),
                  const((LANES, RWKV_WIDTH)), const((LANES, RWKV_WIDTH)), const((LANES, RWKV_WIDTH))],
        out_specs=[tok(RWKV_WIDTH, 0),
                   pl.BlockSpec((n_seq, RWKV_HEAD, RWKV_WIDTH), lambda b, c: (b, 0, 0))],
        out_shape=[jax.ShapeDtypeStruct((rows.bp, rows.t, RWKV_WIDTH), BF16),
                   jax.ShapeDtypeStruct((rows.bp, RWKV_HEAD, RWKV_WIDTH), F32)],
        scratch_shapes=[pltpu.VMEM((n_seq, RWKV_HEAD, RWKV_WIDTH), F32),
                        pltpu.VMEM((n_seq, SUBLANES, BIG_W), F32), pltpu.VMEM((n_seq, SUBLANES, SMALL_W), F32)],
        compiler_params=_params("parallel", "arbitrary"),
    )(proj3, proj3, proj3, proj3, mu_big, mu_small, vecs, w2, a2, g2)
    return out.reshape(rows.rows, RWKV_WIDTH), final_state


def _gla_chunk_kernel(qk_ref, v_ref, og_ref, gl_ref, w2_ref, gb_ref, nw_ref, o_ref, s_ref, state):
    c_idx = pl.program_id(1)
    n_seq, n_tok = qk_ref.shape[0], qk_ref.shape[1]
    n_sub = n_tok // SUB

    @pl.when(c_idx == 0)
    def _():
        state[...] = jnp.zeros_like(state)

    row_k = lax.broadcasted_iota(jnp.int32, (n_tok, GLA_DK), 0)
    att_row = lax.broadcasted_iota(jnp.int32, (SUB, n_tok), 0)
    att_col = lax.broadcasted_iota(jnp.int32, (SUB, n_tok), 1)
    own_col = [jnp.where((att_col >= i * SUB) & (att_col - i * SUB <= att_row), att_col - i * SUB, -1)
               for i in range(n_sub)]

    def cum_log2_decay(n):
        logd = -_softplus(-(_dotp(gl_ref[n], w2_ref[...]) + gb_ref[...])) * (LOG2E / GLA_GATE_NORMALIZER)
        return _cumsum_rows(logd)

    units = [(n, h) for n in range(n_seq) for h in range(GLA_HEADS)]
    us = range(len(units))
    ks = lambda h: slice(h * GLA_DK, (h + 1) * GLA_DK)
    vs = lambda h: slice(h * GLA_DV, (h + 1) * GLA_DV)
    b_seq = [cum_log2_decay(n) for n in range(n_seq)]
    q = [qk_ref[n, :, ks(h)] * (GLA_DK ** -0.5) for n, h in units]
    k = [qk_ref[n, :, GLA_KEY_WIDTH + h * GLA_DK:GLA_KEY_WIDTH + (h + 1) * GLA_DK] for n, h in units]
    b = [b_seq[n][:, ks(h)] for n, h in units]
    v = [v_ref[n, :, vs(h)] for n, h in units]
    st = [state[n, h] for n, h in units]
    o_inter = [_dotp(q[u] * jnp.exp2(b[u]), st[u], NT) for u in us]
    blocks = [[] for _ in us]
    for i in range(n_sub):
        lo = i * SUB
        rows = slice(lo, lo + SUB)
        if i > 0:
            att = [_dotp(q[u][rows] * jnp.exp2(b[u][rows] - b[u][lo - 1:lo]),
                         jnp.where(row_k < lo, k[u] * jnp.exp2(b[u][lo - 1:lo] - b[u]), 0.0), NT) for u in us]
        else:
            att = [jnp.zeros((SUB, n_tok), F32) for _ in us]
        for j in range(SUB):
            tok = slice(lo + j, lo + j + 1)
            col = [jnp.sum(q[u][rows] * (k[u][tok] * jnp.exp2(b[u][rows] - b[u][tok])), axis=1, keepdims=True)
                   for u in us]
            att = [jnp.where(own_col[i] == j, col[u], att[u]) for u in us]
        for u in us:
            blocks[u].append(o_inter[u][rows] + _dotp(att[u], v[u], NN))
    for u, (n, h) in enumerate(units):
        o = jnp.concatenate(blocks[u], axis=0)
        og = og_ref[n, :, vs(h)]
        o_ref[n, :, vs(h)] = (_rmsnorm(o, nw_ref[...], GLA_NORM_EPS)
                              * (og * jax.nn.sigmoid(og))).astype(o_ref.dtype)
        b_last = b[u][n_tok - 1:n_tok, :]
        state[n, h] = st[u] * jnp.exp2(b_last) + _dotp(v[u].T, k[u] * jnp.exp2(b_last - b[u]), NN)

    @pl.when(c_idx == pl.num_programs(1) - 1)
    def _():
        s_ref[...] = state[...]


def gla_prompt(rows, proj, gate_w2, gate_b, norm_w):
    nc = rows.t // CHUNK
    n_seq = _largest_tile(rows.bp, GLA_SEQS, 1)
    proj3 = proj.reshape(rows.bp, rows.t, proj.shape[1])
    tok = lambda width, col: pl.BlockSpec((n_seq, CHUNK, width), lambda b, c: (b, c, col))
    const = lambda shape: pl.BlockSpec(shape, lambda b, c: (0, 0))
    out, final_state = pl.pallas_call(
        _gla_chunk_kernel,
        grid=(rows.bp // n_seq, nc),
        in_specs=[tok(GLA_WIDTH, 0), tok(GLA_WIDTH, 1), tok(GLA_WIDTH, 2), tok(LANES, DST["gl"][0] // LANES),
                  const((LANES, GLA_KEY_WIDTH)), const((1, GLA_KEY_WIDTH)), const((1, GLA_DV))],
        out_specs=[tok(GLA_WIDTH, 0),
                   pl.BlockSpec((n_seq, GLA_HEADS, GLA_DV, GLA_DK), lambda b, c: (b, 0, 0, 0))],
        out_shape=[jax.ShapeDtypeStruct((rows.bp, rows.t, GLA_WIDTH), BF16),
                   jax.ShapeDtypeStruct((rows.bp, GLA_HEADS, GLA_DV, GLA_DK), F32)],
        scratch_shapes=[pltpu.VMEM((n_seq, GLA_HEADS, GLA_DV, GLA_DK), F32)],
        compiler_params=_params("parallel", "arbitrary"),
    )(proj3, proj3, proj3, proj3, gate_w2, gate_b, norm_w)
    return out.reshape(rows.rows, GLA_WIDTH), final_state


def _outproj_kernel(og_ref, orw_ref, x_ref, gt_ref, sc_ref, sh_ref, nw_ref, wo_ref, wr_ref,
                    x1s_ref, h2s_ref, lgs_ref, x1_ref, h2_ref, lg_ref, *, n_prompt_tiles):
    i = pl.program_id(0)

    @pl.when(i < n_prompt_tiles)
    def _():
        half = og_ref.shape[1]
        mix = _dot(og_ref[...], wo_ref[:half, :]) + _dot(orw_ref[...], wo_ref[half:, :])
        x1 = x_ref[...] + gt_ref[0] * mix
        h2 = _rmsnorm(x1, nw_ref[...], NORM_EPS) * (1.0 + sc_ref[0]) + sh_ref[0]
        x1_ref[...] = x1
        h2_ref[...] = h2
        lg_ref[...] = _dotp(h2, wr_ref[...], NN, 3)

    @pl.when(i >= n_prompt_tiles)
    def _():
        x1_ref[...] = x1s_ref[...]
        h2_ref[...] = h2s_ref[...]
        lg_ref[...] = lgs_ref[...]


def out_projection(t, o_gla, o_rwkv, xp, mod_p, norm_w, w_out, w_router, x1_s, h2_s, logits_s):
    n_p, d = xp.shape
    pad_s = x1_s.shape[0]
    tm = ROW_TILE
    npt, per = n_p // tm, t // tm
    prompt = lambda width: pl.BlockSpec((tm, width), lambda i: (jnp.minimum(i, npt - 1), 0))
    pmod = lambda col: pl.BlockSpec((1, 1, d), lambda i: (jnp.minimum(i // per, mod_p.shape[0] - 1), 0, col))
    sample = lambda width: pl.BlockSpec((tm, width), lambda i: (jnp.maximum(i - npt, 0), 0))
    rows = lambda width: pl.BlockSpec((tm, width), lambda i: (i, 0))
    const = lambda a: pl.BlockSpec(a.shape, lambda i: (0, 0), pipeline_mode=pl.Buffered(1))
    n_rows = n_p + pad_s
    return pl.pallas_call(
        functools.partial(_outproj_kernel, n_prompt_tiles=npt),
        grid=(n_rows // tm,),
        in_specs=[prompt(o_gla.shape[1]), prompt(o_rwkv.shape[1]), prompt(d), pmod(2), pmod(4), pmod(3),
                  const(norm_w), const(w_out), const(w_router), sample(d), sample(d), sample(LANES)],
        out_specs=[rows(d), rows(d), rows(LANES)],
        out_shape=[jax.ShapeDtypeStruct((n_rows, d), F32), jax.ShapeDtypeStruct((n_rows, d), F32),
                   jax.ShapeDtypeStruct((n_rows, LANES), F32)],
        compiler_params=_params("arbitrary"),
    )(o_gla, o_rwkv, xp, mod_p, mod_p, mod_p, norm_w, w_out, w_router, x1_s, h2_s, logits_s)


def _moe_kernel(tile_expert_ref, n_valid_ref, n_real_ref, src_ref, dst_ref, h2_hbm, rw_ref, wg_ref, wu_ref, wd_ref,
                y_hbm, xbuf, obuf, gather_sem, scatter_sem):
    del tile_expert_ref
    i = pl.program_id(0)
    n_tiles = pl.num_programs(0)
    n_valid = n_valid_ref[0]
    slot = i % 2

    def for_rows(n, fn):
        def group(g, carry):
            for u in range(DMA_UNROLL):
                fn(g * DMA_UNROLL + u)
            return carry

        def single(r, carry):
            fn(r)
            return carry
        full = n // DMA_UNROLL
        lax.fori_loop(0, full, group, 0)
        lax.fori_loop(full * DMA_UNROLL, n, single, 0)

    def start_gather(tile, s):
        def one(r):
            tok = src_ref[tile * MOE_TILE + r]
            pltpu.make_async_copy(h2_hbm.at[pl.ds(tok, 1), :], xbuf.at[s, pl.ds(r, 1), :],
                                  gather_sem.at[s]).start()
        for_rows(MOE_TILE, one)

    def wait_gather(s):
        pltpu.make_async_copy(h2_hbm.at[pl.ds(0, MOE_TILE), :], xbuf.at[s], gather_sem.at[s]).wait()

    def scatter_row(tile, s, r):
        row = dst_ref[tile * MOE_TILE + r]
        return pltpu.make_async_copy(obuf.at[s, pl.ds(r, 1), :], y_hbm.at[pl.ds(row, 1), :], scatter_sem.at[s])

    def start_scatter(tile, s):
        for_rows(n_real_ref[tile], lambda r: scatter_row(tile, s, r).start())

    def wait_scatter(tile, s):
        n_real = n_real_ref[tile]

        @pl.when(n_real == MOE_TILE)
        def _():
            pltpu.make_async_copy(obuf.at[s], y_hbm.at[pl.ds(0, MOE_TILE), :], scatter_sem.at[s]).wait()

        @pl.when(n_real < MOE_TILE)
        def _():
            for_rows(n_real, lambda r: scatter_row(tile, s, r).wait())

    @pl.when(i == 0)
    def _():
        start_gather(0, 0)

    @pl.when(i + 1 < n_valid)
    def _():
        start_gather(i + 1, 1 - slot)

    @pl.when(i < n_valid)
    def _():
        wait_gather(slot)

        @pl.when(i >= 2)
        def _():
            wait_scatter(i - 2, slot)

        x = xbuf[slot].astype(BF16)
        gate = _dot(x, wg_ref[0].astype(BF16))
        up = _dot(x, wu_ref[0].astype(BF16))
        hid = gate * jax.nn.sigmoid(gate) * up * rw_ref[...]
        obuf[slot] = _dot(hid.astype(BF16), wd_ref[0].astype(BF16))
        start_scatter(i, slot)

    @pl.when(i == n_tiles - 1)
    def _():
        @pl.when(n_valid >= 2)
        def _():
            wait_scatter(n_valid - 2, n_valid % 2)

        wait_scatter(n_valid - 1, (n_valid - 1) % 2)


def moe_experts(h2, plan, w_gate, w_up, w_down):
    n_rows, d = h2.shape
    _, _, ff = w_gate.shape
    n_tiles = plan["tile_expert"].shape[0]
    grid_spec = pltpu.PrefetchScalarGridSpec(
        num_scalar_prefetch=5,
        grid=(n_tiles,),
        in_specs=[pl.BlockSpec(memory_space=pl.ANY),
                  pl.BlockSpec((MOE_TILE, 1), lambda i, te, *_: (i, 0)),
                  pl.BlockSpec((1, d, ff), lambda i, te, *_: (te[i], 0, 0)),
                  pl.BlockSpec((1, d, ff), lambda i, te, *_: (te[i], 0, 0)),
                  pl.BlockSpec((1, ff, d), lambda i, te, *_: (te[i], 0, 0))],
        out_specs=pl.BlockSpec(memory_space=pl.ANY),
        scratch_shapes=[pltpu.VMEM((2, MOE_TILE, d), F32), pltpu.VMEM((2, MOE_TILE, d), F32),
                        pltpu.SemaphoreType.DMA((2,)), pltpu.SemaphoreType.DMA((2,))],
    )
    return pl.pallas_call(
        _moe_kernel,
        grid_spec=grid_spec,
        out_shape=jax.ShapeDtypeStruct((2 * n_rows, d), F32),
        compiler_params=_params("arbitrary"),
    )(plan["tile_expert"], plan["n_valid"], plan["n_real"], plan["src_row"], plan["dst_row"], h2, plan["row_w"],
      w_gate, w_up, w_down)


def _route(logits):
    p_group = jax.nn.softmax(logits[:, :N_GROUPS], axis=-1)
    g_sel = jnp.argmax(p_group, axis=-1)
    p_sel = jnp.take_along_axis(p_group, g_sel[:, None], axis=-1)
    logits_e = logits[:, N_GROUPS:N_GROUPS + N_EXPERTS].reshape(-1, N_GROUPS, EXPERTS_PER_GROUP)
    logits_e = jnp.take_along_axis(logits_e, g_sel[:, None, None], axis=1)[:, 0]
    i1 = jnp.argmax(logits_e, axis=-1)
    v1 = jnp.take_along_axis(logits_e, i1[:, None], axis=-1)
    masked = jnp.where(jnp.arange(EXPERTS_PER_GROUP)[None, :] == i1[:, None], -jnp.inf, logits_e)
    i2 = jnp.argmax(masked, axis=-1)
    v2 = jnp.take_along_axis(logits_e, i2[:, None], axis=-1)
    w_top = jax.nn.softmax(jnp.concatenate([v1, v2], axis=-1), axis=-1) * p_sel
    top_i = jnp.stack([i1, i2], axis=-1)
    expert = g_sel[:, None].astype(jnp.int32) * EXPERTS_PER_GROUP + top_i.astype(jnp.int32)
    return expert, w_top


def _moe_plan(expert, w_top):
    n = expert.shape[0]
    e_flat = expert.reshape(-1)
    ids = jnp.arange(N_EXPERTS, dtype=jnp.int32)
    one_hot = (e_flat[:, None] == ids[None, :]).astype(jnp.int32)
    counts = jnp.sum(one_hot, axis=0)
    padded = (counts + MOE_TILE - 1) // MOE_TILE * MOE_TILE
    ends = jnp.cumsum(padded)
    starts = ends - padded
    rank = jnp.sum(jnp.cumsum(one_hot, axis=0) * one_hot, axis=1) - 1
    pos = jnp.sum(one_hot * starts[None, :], axis=1) + rank
    p_max = _round_up(2 * n, MOE_TILE) + N_EXPERTS * MOE_TILE
    n_tiles = p_max // MOE_TILE
    pair = jnp.full((p_max,), -1, jnp.int32).at[pos].set(jnp.arange(2 * n, dtype=jnp.int32))
    real = pair >= 0
    pair0 = jnp.maximum(pair, 0)
    src_row = pair0 // 2
    dst_row = (pair0 % 2) * n + src_row
    row_w = jnp.where(real, w_top.reshape(-1)[pair0], 0.0)
    tile_start = jnp.arange(n_tiles, dtype=jnp.int32) * MOE_TILE
    n_valid = ends[-1] // MOE_TILE
    tile_expert = jnp.sum((tile_start[:, None] >= ends[None, :]).astype(jnp.int32), axis=1)
    tile_expert = jnp.where(tile_start < ends[-1], tile_expert, tile_expert[n_valid - 1])
    n_real = jnp.clip((starts + counts)[tile_expert] - tile_start, 0, MOE_TILE)
    n_real = jnp.where(tile_start < ends[-1], n_real, 0).astype(jnp.int32)
    return dict(src_row=src_row, dst_row=dst_row, row_w=row_w[:, None], tile_expert=tile_expert,
                n_valid=n_valid.astype(jnp.int32)[None], n_real=n_real)


def _final_kernel(x1_ref, y0_ref, y1_ref, gtp_ref, gts_ref, scp_ref, scs_ref, shp_ref, shs_ref, nw_ref,
                  yp_ref, ys_ref, *, n_prompt_tiles):
    i = pl.program_id(0)

    def finish(gt, sc, sh):
        x2 = x1_ref[...] + gt * (y0_ref[...] + y1_ref[...])
        return _rmsnorm(x2, nw_ref[...], NORM_EPS) * (1.0 + sc) + sh

    @pl.when(i < n_prompt_tiles)
    def _():
        yp_ref[...] = finish(gtp_ref[0], scp_ref[0], shp_ref[0])

    @pl.when(i >= n_prompt_tiles)
    def _():
        ys_ref[...] = finish(gts_ref[...], scs_ref[...], shs_ref[...])


def final_norm(n_p, t, x1, y_pairs, mod_p, mod_s, modf_p, modf_s, norm_w):
    n_rows, d = x1.shape
    tile = ROW_TILE
    n_tiles, npt = n_rows // tile, n_p // tile
    per = t // tile
    rows = lambda off: pl.BlockSpec((tile, d), lambda i: (i + off, 0))
    prompt_out = pl.BlockSpec((tile, d), lambda i: (jnp.minimum(i, npt - 1), 0))
    sample = lambda col: pl.BlockSpec((tile, d), lambda i: (jnp.maximum(i - npt, 0), col))
    pmod = lambda col: pl.BlockSpec((1, 1, d), lambda i: (jnp.minimum(i // per, mod_p.shape[0] - 1), 0, col))
    return pl.pallas_call(
        functools.partial(_final_kernel, n_prompt_tiles=npt),
        grid=(n_tiles,),
        in_specs=[rows(0), rows(0), rows(n_tiles), pmod(5), sample(5), pmod(1), sample(1), pmod(0), sample(0),
                  pl.BlockSpec((1, d), lambda i: (0, 0))],
        out_specs=[prompt_out, sample(0)],
        out_shape=[jax.ShapeDtypeStruct((n_p, d), F32), jax.ShapeDtypeStruct((n_rows - n_p, d), F32)],
        compiler_params=_params("arbitrary"),
    )(x1, y_pairs, y_pairs, mod_p, mod_s, modf_p, modf_s, modf_p, modf_s, norm_w)


def _align_cols(a):
    parts = []
    for name in _ALIGNED:
        off, w = SRC[name]
        part = a[..., off:off + w]
        wa = DST[name][1]
        if wa != w:
            part = jnp.pad(part, [(0, 0)] * (a.ndim - 1) + [(0, wa - w)])
        parts.append(part)
    return jnp.concatenate(parts, axis=-1)


def _rwkv_seg(a, name):
    off, w = SRC[name]
    return a[..., off - RWKV_SRC_BASE:off - RWKV_SRC_BASE + w]


def _rwkv_seg_padded(a, name):
    seg = _rwkv_seg(a, name)
    wa = DST[name][1]
    return jnp.pad(seg, [(0, 0)] * (a.ndim - 1) + [(0, wa - seg.shape[-1])])


def _orig_seg(p, name):
    off, w = SRC[name]
    return p[..., off:off + w]


def _pad_rows(w, rows):
    return jnp.pad(w, ((0, rows - w.shape[0]), (0, 0)))


def kernel(x_prompt, x_sample, state_gla, state_rwkv, state_shift, c_prompt, c_sample, w_ada, b_ada, norm_mix, norm_ffn, w_in, gla_gate_w2, gla_gate_b, gla_norm, rwkv_mu, rwkv_w0, rwkv_w2, rwkv_a0, rwkv_a2, rwkv_g2, rwkv_k_k, rwkv_k_a, rwkv_r_k, rwkv_gn_w, rwkv_gn_b, w_out, w_router_group, w_router_expert, w_exp_gate, w_exp_up, w_exp_down, norm_final, w_ada_final, b_ada_final):
    assert w_ada.shape[0] == 1, "single-layer step"
    bp, t, d = x_prompt.shape
    bs = x_sample.shape[0]
    assert x_sample.shape[1] == 1 and t % ROW_TILE == 0
    n_p = bp * t
    pad_s = _round_up(bs, ROW_TILE)
    n_rows = n_p + pad_s
    prompt = _PromptShape(bp, t, n_p)
    hi = lax.Precision.HIGHEST
    pad_sample = lambda a: jnp.pad(a, ((0, pad_s - a.shape[0]), (0, 0)))
    hdot = lambda a, w: jnp.dot(a, w, precision=hi)

    c_act = jax.nn.silu(jnp.concatenate([c_prompt, c_sample], axis=0))
    mod = matmul3(c_act, w_ada[0]) + b_ada[0]
    mod_f = matmul3(c_act, w_ada_final) + b_ada_final
    mod_p, mod_s = mod[:bp, None, :], mod[bp:]
    modf_p, modf_s = mod_f[:bp, None, :], mod_f[bp:]
    sh1_s, sc1_s, gt1_s, sh2_s, sc2_s, _ = jnp.split(mod_s, 6, axis=-1)
    xp = x_prompt.reshape(n_p, d)
    xs = x_sample[:, 0, :]

    proj = in_projection(xp, t, mod_p, norm_mix, _align_cols(w_in[0].astype(BF16)))
    o_gla, gla_t_p = gla_prompt(prompt, proj, _pad_rows(gla_gate_w2[0], LANES), gla_gate_b, gla_norm)
    new_gla_p = jnp.swapaxes(gla_t_p, -1, -2)
    mu = rwkv_mu[0]
    mu_big = jnp.concatenate([_rwkv_seg(mu, n) for n in ("r", "k7", "v7")])[None, :]
    mu_small = jnp.concatenate([_rwkv_seg_padded(mu, n) for n in ("wl", "al", "gl7")])[None, :]
    vecs = jnp.concatenate([rwkv_w0, rwkv_a0, rwkv_k_k, rwkv_k_a, rwkv_r_k[0].reshape(1, RWKV_WIDTH),
                            rwkv_gn_w, rwkv_gn_b, jnp.zeros((1, RWKV_WIDTH), F32)], axis=0)
    o_rwkv, rwkv_t_p = rwkv7_prompt(prompt, proj, mu_big, mu_small, vecs, _pad_rows(rwkv_w2[0], LANES),
                                    _pad_rows(rwkv_a2[0], LANES), rwkv_g2[0])
    new_rwkv_p = rwkv_t_p.reshape(bp, RWKV_HEAD, RWKV_HEADS, RWKV_HEAD).transpose(0, 2, 1, 3)
    last = proj[t - 1:n_p:t]
    new_shift_p = jnp.concatenate([last[:, DST[n][0]:DST[n][0] + SRC[n][1]] for n in _RWKV_ORIG], axis=-1)

    h1_s = _rmsnorm(xs, norm_mix[0], NORM_EPS) * (1.0 + sc1_s) + sh1_s
    proj_s = matmul3(h1_s, w_in[0])
    heads = lambda a, n: a.reshape(a.shape[0], n, -1)
    logd_s = jax.nn.log_sigmoid(hdot(_orig_seg(proj_s, "gl"), gla_gate_w2[0]) + gla_gate_b[0]) \
        / GLA_GATE_NORMALIZER
    q_s = heads(_orig_seg(proj_s, "q") * (GLA_DK ** -0.5), GLA_HEADS)
    k_s, v_s, g_s = heads(_orig_seg(proj_s, "k"), GLA_HEADS), heads(_orig_seg(proj_s, "v"), GLA_HEADS), \
        heads(logd_s, GLA_HEADS)
    new_gla_s = jnp.exp(g_s)[..., None] * state_gla[0] + k_s[..., :, None] * v_s[..., None, :]
    o_s = jnp.einsum("bhd,bhde->bhe", q_s, new_gla_s, precision=hi)
    o_s = _rmsnorm(o_s, gla_norm[0], GLA_NORM_EPS) * heads(jax.nn.silu(_orig_seg(proj_s, "og")), GLA_HEADS)

    rp_s = proj_s[:, RWKV_SRC_BASE:]
    new_shift_s = rp_s
    xs7 = rp_s + (state_shift[0] - rp_s) * mu
    sx = lambda name: _rwkv_seg(xs7, name)
    r_s, k7_s, v7_s = sx("r"), sx("k7"), sx("v7")
    w_pre = rwkv_w0[0] + hdot(jnp.tanh(sx("wl")), rwkv_w2[0])
    decay_s = jnp.exp(-jnp.exp(-jax.nn.softplus(-w_pre) - 0.5))
    a_s = jax.nn.sigmoid(rwkv_a0[0] + hdot(sx("al"), rwkv_a2[0]))
    g_s7 = hdot(jax.nn.sigmoid(sx("gl7")), rwkv_g2[0])
    hs = lambda z: z.reshape(bs, RWKV_HEADS, RWKV_HEAD)
    kk_s = hs(k7_s * rwkv_k_k[0])
    kk_s = kk_s / jnp.maximum(jnp.sqrt(jnp.sum(kk_s * kk_s, axis=-1, keepdims=True)), 1e-12)
    k7_s = k7_s * (1.0 + (a_s - 1.0) * rwkv_k_a[0])
    r_h, w_h, k_h, v_h, a_h = hs(r_s), hs(decay_s), hs(k7_s), hs(v7_s), hs(a_s)
    s_prev = state_rwkv[0]
    sa = jnp.einsum("bhij,bhj->bhi", s_prev, -kk_s, precision=hi)
    new_rwkv_s = s_prev * w_h[:, :, None, :] + sa[..., None] * (kk_s * a_h)[:, :, None, :] \
        + v_h[..., None] * k_h[:, :, None, :]
    y_s = jnp.einsum("bhij,bhj->bhi", new_rwkv_s, r_h, precision=hi)
    y_mu = jnp.mean(y_s, axis=-1, keepdims=True)
    y_var = jnp.mean(jnp.square(y_s - y_mu), axis=-1, keepdims=True)
    y_s = (y_s - y_mu) * lax.rsqrt(y_var + RWKV_GN_EPS) * rwkv_gn_w[0].reshape(RWKV_HEADS, RWKV_HEAD) \
        + rwkv_gn_b[0].reshape(RWKV_HEADS, RWKV_HEAD)
    bonus = jnp.sum(r_h * k_h * rwkv_r_k[0], axis=-1, keepdims=True) * v_h
    o_rs = (y_s + bonus).reshape(bs, RWKV_WIDTH) * g_s7
    mix_s = matmul3(jnp.concatenate([o_s.reshape(bs, GLA_WIDTH), o_rs], axis=-1), w_out[0])
    x1_s = xs + gt1_s * mix_s
    h2_s = _rmsnorm(x1_s, norm_ffn[0], NORM_EPS) * (1.0 + sc2_s) + sh2_s
    w_router = jnp.pad(jnp.concatenate([w_router_group[0], w_router_expert[0]], axis=-1),
                       ((0, 0), (0, LANES - N_GROUPS - N_EXPERTS)))
    logits_s = hdot(h2_s, w_router)

    x1, h2, logits = out_projection(t, o_gla, o_rwkv, xp, mod_p, norm_ffn, w_out[0].astype(BF16), w_router,
                                    pad_sample(x1_s), pad_sample(h2_s), pad_sample(logits_s))

    expert, w_top = _route(logits)
    plan = _moe_plan(expert, w_top)
    y_pairs = moe_experts(h2, plan, w_exp_gate[0].reshape(N_EXPERTS, d, EXPERT_FF),
                          w_exp_up[0].reshape(N_EXPERTS, d, EXPERT_FF),
                          w_exp_down[0].reshape(N_EXPERTS, EXPERT_FF, d))

    y_p, y_s_pad = final_norm(n_p, t, x1, y_pairs, mod_p, pad_sample(mod_s), modf_p, pad_sample(modf_s),
                              norm_final[None, :])
    return (y_p.reshape(bp, t, d), y_s_pad[:bs, None, :], new_gla_p[None], new_rwkv_p[None], new_shift_p[None],
            new_gla_s[None], new_rwkv_s[None], new_shift_s[None])
```

```python
import collections
import functools

import jax
import jax.numpy as jnp
from jax import lax
from jax.experimental import pallas as pl
from jax.experimental.pallas import tpu as pltpu

F32 = jnp.float32
BF16 = jnp.bfloat16

D_MODEL = 2048
GLA_HEADS = 4
GLA_DK = 128
GLA_DV = 256
GLA_KEY_WIDTH = GLA_HEADS * GLA_DK
GLA_WIDTH = GLA_HEADS * GLA_DV
GLA_GATE_NORMALIZER = 16.0
RWKV_HEAD = 64
RWKV_HEADS = 16
RWKV_WIDTH = RWKV_HEAD * RWKV_HEADS
N_GROUPS = 4
EXPERTS_PER_GROUP = 8
N_EXPERTS = N_GROUPS * EXPERTS_PER_GROUP
EXPERT_FF = 512
NORM_EPS = 1e-6
GLA_NORM_EPS = 1e-5
RWKV_GN_EPS = 64e-5

LANES = 128
SUBLANES = 8
VMEM_LIMIT_BYTES = 56 * 1024 * 1024

CHUNK = 64
SUB = 16
RWKV_GROUP = 2
GROUP_W = RWKV_GROUP * RWKV_HEAD
GLA_SEQS = 2
LOG2E = 1.4426950408889634
RWKV_SEQS = 4
MOE_TILE = 256
DMA_UNROLL = 8
ROW_TILE = 256
NORM_SLAB = 128
K_SPLIT = 256

_ORIG = (("q", 512), ("k", 512), ("v", 1024), ("gl", 16), ("og", 1024),
         ("r", 1024), ("wl", 64), ("k7", 1024), ("v7", 1024), ("al", 64), ("gl7", 128))
_ALIGNED = ("q", "k", "v", "og", "r", "k7", "v7", "gl", "wl", "al", "gl7")
_RWKV_ORIG = ("r", "wl", "k7", "v7", "al", "gl7")


def _round_up(n, m):
    return (n + m - 1) // m * m


def _layouts():
    src, off = {}, 0
    for name, w in _ORIG:
        src[name] = (off, w)
        off += w
    dst, pos = {}, 0
    for name in _ALIGNED:
        wa = _round_up(src[name][1], LANES)
        dst[name] = (pos, wa)
        pos += wa
    return src, off, dst, pos


SRC, IN_COLS, DST, IN_COLS_ALIGNED = _layouts()
RWKV_SRC_BASE = SRC["r"][0]
BIG_W = 3 * RWKV_WIDTH
SMALL_W = 3 * LANES
assert DST["r"][0] == BIG_W and DST["k7"][0] == BIG_W + RWKV_WIDTH and DST["v7"][0] == BIG_W + 2 * RWKV_WIDTH
assert DST["v"][0] == RWKV_WIDTH and DST["og"][0] == 2 * RWKV_WIDTH and DST["k"][0] == GLA_KEY_WIDTH

NN = ((1,), (0,))
NT = ((1,), (1,))

_PromptShape = collections.namedtuple("_PromptShape", "bp t rows")


def _dot(a, b, dims=NN):
    return lax.dot_general(a, b, (dims, ((), ())), preferred_element_type=F32)


def _split2(x):
    hi = x.astype(BF16)
    lo = (x - hi.astype(F32)).astype(BF16)
    return hi, lo


def _dotp(a, b, dims=NN, passes=1):
    if passes == 1:
        return _dot(a.astype(BF16), b.astype(BF16), dims)
    ah, al = _split2(a)
    bh, bl = _split2(b)
    return _dot(ah, bh, dims) + (_dot(ah, bl, dims) + _dot(al, bh, dims))


def _cumsum_rows(x):
    n = x.shape[0]
    row = lax.broadcasted_iota(jnp.int32, (n, n), 0)
    col = lax.broadcasted_iota(jnp.int32, (n, n), 1)
    tri = (row >= col).astype(BF16)
    x1 = x.astype(BF16)
    r1 = x - x1.astype(F32)
    x2 = r1.astype(BF16)
    x3 = (r1 - x2.astype(F32)).astype(BF16)
    return _dot(tri, x1) + (_dot(tri, x2) + _dot(tri, x3))


def _softplus(z):
    return jnp.maximum(z, 0.0) + jnp.log1p(jnp.exp(-jnp.abs(z)))


def _rmsnorm(x, g, eps):
    return x * lax.rsqrt(jnp.mean(x * x, axis=-1, keepdims=True) + eps) * g


def _largest_tile(n, cap, mult=SUBLANES):
    if n <= cap:
        return n
    best = None
    for t in range(mult, cap + 1, mult):
        if n % t == 0:
            best = t
    assert best is not None, (n, cap)
    return best


def _params(*sem):
    return pltpu.CompilerParams(dimension_semantics=sem, vmem_limit_bytes=VMEM_LIMIT_BYTES)


def _mm3_kernel(a_ref, w_ref, o_ref, *, dims):
    @pl.when(pl.program_id(0) == 0)
    def _():
        o_ref[...] = jnp.zeros_like(o_ref)

    o_ref[...] += _dotp(a_ref[...], w_ref[...], dims, 3)


def matmul3(a, w, w_transposed=False):
    m, k = a.shape
    n = w.shape[0] if w_transposed else w.shape[1]
    tk = _largest_tile(k, K_SPLIT, LANES)
    w_spec = pl.BlockSpec((n, tk), lambda s: (0, s)) if w_transposed else pl.BlockSpec((tk, n), lambda s: (s, 0))
    return pl.pallas_call(
        functools.partial(_mm3_kernel, dims=NT if w_transposed else NN),
        grid=(k // tk,),
        in_specs=[pl.BlockSpec((m, tk), lambda s: (0, s)), w_spec],
        out_specs=pl.BlockSpec((m, n), lambda s: (0, 0)),
        out_shape=jax.ShapeDtypeStruct((m, n), F32),
        compiler_params=_params("arbitrary"),
    )(a, w)


def _prompt_mod(tiles_per_batch, width, col):
    return pl.BlockSpec((1, 1, width), lambda i, *_: (i // tiles_per_batch, 0, col))


def _inproj_kernel(x_ref, sc_ref, sh_ref, nw_ref, w_ref, o_ref, h_scr):
    @pl.when(pl.program_id(1) == 0)
    def _():
        for r in range(0, h_scr.shape[0], NORM_SLAB):
            sl = slice(r, r + NORM_SLAB)
            h = _rmsnorm(x_ref[sl, :], nw_ref[...], NORM_EPS) * (1.0 + sc_ref[0]) + sh_ref[0]
            h_scr[sl, :] = h.astype(BF16)

    o_ref[...] = _dot(h_scr[...], w_ref[...], NT)


def in_projection(xp, t, mod_p, norm_w, w_aligned):
    n_p, d = xp.shape
    n = w_aligned.shape[0]
    tm = _largest_tile(t, 1024, NORM_SLAB)
    tn = _largest_tile(n, 512, LANES)
    per = t // tm
    return pl.pallas_call(
        _inproj_kernel,
        grid=(n_p // tm, n // tn),
        in_specs=[pl.BlockSpec((tm, d), lambda i, j: (i, 0)), _prompt_mod(per, d, 1), _prompt_mod(per, d, 0),
                  pl.BlockSpec((1, d), lambda i, j: (0, 0)), pl.BlockSpec((tn, d), lambda i, j: (j, 0))],
        out_specs=pl.BlockSpec((tm, tn), lambda i, j: (i, j)),
        out_shape=jax.ShapeDtypeStruct((n_p, n), F32),
        scratch_shapes=[pltpu.VMEM((tm, d), BF16)],
        compiler_params=_params("parallel", "arbitrary"),
    )(xp, mod_p, mod_p, norm_w, w_aligned)


def _rwkv_chunk_kernel(big_ref, wl_ref, al_ref, gl_ref, mu_big_ref, mu_small_ref, vec_ref, w2_ref, a2_ref,
                       g2_ref, o_ref, s_ref, state, carry_big, carry_small, *, passes):
    c_idx = pl.program_id(1)
    n_seq, n_tok = big_ref.shape[0], big_ref.shape[1]
    n_grp = RWKV_WIDTH // GROUP_W
    grp = [slice(g * GROUP_W, (g + 1) * GROUP_W) for g in range(n_grp)]

    @pl.when(c_idx == 0)
    def _():
        state[...] = jnp.zeros_like(state)
        carry_big[...] = jnp.zeros_like(carry_big)
        carry_small[...] = jnp.zeros_like(carry_small)

    first_row = lax.broadcasted_iota(jnp.int32, (n_tok, 1), 0) == 0
    w0, a0, k_k, k_a, r_k, gn_w, gn_b = (vec_ref[i:i + 1, :] for i in range(7))

    gi = lax.broadcasted_iota(jnp.int32, (GROUP_W, GROUP_W), 0) // RWKV_HEAD
    gj = lax.broadcasted_iota(jnp.int32, (GROUP_W, GROUP_W), 1) // RWKV_HEAD
    head_ones = (gi == gj).astype(BF16)

    def head_sums(parts):
        hi, lo = _split2(jnp.concatenate(parts, axis=0))
        out = _dot(hi, head_ones) + _dot(lo, head_ones)
        return [out[i * n_tok:(i + 1) * n_tok] for i in range(len(parts))]

    def prepare(n):
        def token_shift(cur, carry, mu):
            prev = jnp.where(first_row, carry[n, 0:1, :], pltpu.roll(cur, 1, 0))
            carry[n, 0:1, :] = cur[n_tok - 1:n_tok, :]
            return cur + (prev - cur) * mu

        xs_big = token_shift(big_ref[n], carry_big, mu_big_ref[...])
        small = jnp.concatenate([wl_ref[n], al_ref[n], gl_ref[n]], axis=1)
        xs_small = token_shift(small, carry_small, mu_small_ref[...])
        r = xs_big[:, :RWKV_WIDTH]
        k7 = xs_big[:, RWKV_WIDTH:2 * RWKV_WIDTH]
        v = xs_big[:, 2 * RWKV_WIDTH:]
        w_pre = w0 + _dotp(jnp.tanh(xs_small[:, :LANES]), w2_ref[...])
        lw = -jnp.exp(-_softplus(-w_pre) - 0.5)
        a = jax.nn.sigmoid(a0 + _dotp(xs_small[:, LANES:2 * LANES], a2_ref[...]))
        gate = _dotp(jax.nn.sigmoid(xs_small[:, 2 * LANES:]), g2_ref[...])
        kk_raw = k7 * k_k
        k = k7 * (1.0 + (a - 1.0) * k_a)
        sums = [head_sums([kk_raw[:, sl] * kk_raw[:, sl], r[:, sl] * k[:, sl] * r_k[:, sl]]) for sl in grp]
        kk = jnp.concatenate([kk_raw[:, sl] / jnp.maximum(jnp.sqrt(sums[g][0]), 1e-12)
                              for g, sl in enumerate(grp)], axis=1)
        bonus = jnp.concatenate([sums[g][1] for g in range(n_grp)], axis=1) * v
        cum = _cumsum_rows(lw)
        cum_end = cum[n_tok - 1:n_tok, :]
        beta = kk * a
        g_inv = jnp.exp(-cum)
        g_end = jnp.exp(cum_end - cum)
        return dict(v=v, gate=gate, bonus=bonus, g_tot=jnp.exp(cum_end),
                    a_hat=-kk * jnp.exp(cum - lw), r_hat=r * jnp.exp(cum), b_hat=beta * g_inv,
                    k_hat=k * g_inv, b_end=beta * g_end, k_end=k * g_end)

    seqs = [prepare(n) for n in range(n_seq)]
    units = [(n, sl) for n in range(n_seq) for sl in grp]
    part = lambda name: [seqs[n][name][:, sl] for n, sl in units]

    lane = lax.broadcasted_iota(jnp.int32, (n_tok, GROUP_W), 1)
    tok = lax.broadcasted_iota(jnp.int32, (n_tok, GROUP_W), 0)
    lane_head = lane // RWKV_HEAD
    src_tok = lane % RWKV_HEAD
    strict = tok > src_tok
    incl = tok >= src_tok

    def bd(y):
        return jnp.concatenate([jnp.where(lane_head == h, y, 0.0) for h in range(RWKV_GROUP)], axis=0)

    gs = range(len(units))
    s0 = [state[n, :, sl] for n, sl in units]
    v_u, b_hat, k_hat, b_end, k_end = part("v"), part("b_hat"), part("k_hat"), part("b_end"), part("k_end")
    lhs2 = [jnp.concatenate([a, r], axis=0) for a, r in zip(part("a_hat"), part("r_hat"))]
    abrb = [_dotp(lhs2[g], bd(b_hat[g]), NT, passes) for g in gs]
    akrk = [_dotp(lhs2[g], bd(k_hat[g]), NT, passes) for g in gs]
    asrs = [_dotp(lhs2[g], bd(s0[g]), NT, passes) for g in gs]
    p = [jnp.where(strict, abrb[g][:n_tok], 0.0) for g in gs]
    ak = [jnp.where(strict, akrk[g][:n_tok], 0.0) for g in gs]
    rb = [jnp.where(incl, abrb[g][n_tok:], 0.0) for g in gs]
    rk = [jnp.where(incl, akrk[g][n_tok:], 0.0) for g in gs]
    bd_v = [bd(v_u[g]) for g in gs]
    x = [asrs[g][:n_tok] + _dotp(ak[g], bd_v[g], NN, passes) for g in gs]
    n_sq = n_tok.bit_length() - 1
    for it in range(n_sq):
        if it < n_sq - 1:
            both = [_dotp(p[g], jnp.concatenate([bd(p[g]), bd(x[g])], axis=1), NN, passes) for g in gs]
            x = [x[g] + both[g][:, GROUP_W:] for g in gs]
            p = [both[g][:, :GROUP_W] for g in gs]
        else:
            x = [x[g] + _dotp(p[g], bd(x[g]), NN, passes) for g in gs]
    y = [asrs[g][n_tok:] + _dotp(jnp.concatenate([rb[g], rk[g]], axis=1),
                                 jnp.concatenate([bd(x[g]), bd_v[g]], axis=0), NN, passes) for g in gs]
    full = [_dotp(jnp.concatenate([x[g], v_u[g]], axis=0).T,
                  jnp.concatenate([b_end[g], k_end[g]], axis=0), NN, passes) for g in gs]
    g_tot = part("g_tot")
    for g, (n, sl) in enumerate(units):
        upd = s0[g] * g_tot[g]
        for h in range(RWKV_GROUP):
            upd = upd + jnp.where(lane_head == h, full[g][h * RWKV_HEAD:(h + 1) * RWKV_HEAD, :], 0.0)
        state[n, :, sl] = upd

    inv_n = 1.0 / RWKV_HEAD
    mean = [head_sums([y[g]])[0] * inv_n for g in gs]
    dev = [y[g] - mean[g] for g in gs]
    var = [head_sums([dev[g] * dev[g]])[0] * inv_n for g in gs]
    bonus, gate = part("bonus"), part("gate")
    for g, (n, sl) in enumerate(units):
        yn = dev[g] * lax.rsqrt(var[g] + RWKV_GN_EPS) * gn_w[:, sl] + gn_b[:, sl]
        o_ref[n, :, sl] = ((yn + bonus[g]) * gate[g]).astype(o_ref.dtype)

    @pl.when(c_idx == pl.num_programs(1) - 1)
    def _():
        s_ref[...] = state[...]


def rwkv7_prompt(rows, proj, mu_big, mu_small, vecs, w2, a2, g2, *, passes=1):
    assert CHUNK == RWKV_HEAD and rows.t % CHUNK == 0
    nc = rows.t // CHUNK
    n_seq = _largest_tile(rows.bp, RWKV_SEQS, 1)
    proj3 = proj.reshape(rows.bp, rows.t, proj.shape[1])
    small_col = lambda name: DST[name][0] // LANES
    tok = lambda width, col: pl.BlockSpec((n_seq, CHUNK, width), lambda b, c: (b, c, col))
    const = lambda shape: pl.BlockSpec(shape, lambda b, c: (0, 0))
    out, final_state = pl.pallas_call(
        functools.partial(_rwkv_chunk_kernel, passes=passes),
        grid=(rows.bp // n_seq, nc),
        in_specs=[tok(BIG_W, 1), tok(LANES, small_col("wl")), tok(LANES, small_col("al")),
                  tok(LANES, small_col("gl7")),
                  const((1, BIG_W)), const((1, SMALL_W)), const((SUBLANES, RWKV_WIDTH)),
                  const((LANES, RWKV_WIDTH)), const((LANES, RWKV_WIDTH)), const((LANES, RWKV_WIDTH))],
        out_specs=[tok(RWKV_WIDTH, 0),
                   pl.BlockSpec((n_seq, RWKV_HEAD, RWKV_WIDTH), lambda b, c: (b, 0, 0))],
        out_shape=[jax.ShapeDtypeStruct((rows.bp, rows.t, RWKV_WIDTH), BF16),
                   jax.ShapeDtypeStruct((rows.bp, RWKV_HEAD, RWKV_WIDTH), F32)],
        scratch_shapes=[pltpu.VMEM((n_seq, RWKV_HEAD, RWKV_WIDTH), F32),
                        pltpu.VMEM((n_seq, SUBLANES, BIG_W), F32), pltpu.VMEM((n_seq, SUBLANES, SMALL_W), F32)],
        compiler_params=_params("parallel", "arbitrary"),
    )(proj3, proj3, proj3, proj3, mu_big, mu_small, vecs, w2, a2, g2)
    return out.reshape(rows.rows, RWKV_WIDTH), final_state


def _gla_chunk_kernel(qk_ref, v_ref, og_ref, gl_ref, w2_ref, gb_ref, nw_ref, o_ref, s_ref, state):
    c_idx = pl.program_id(1)
    n_seq, n_tok = qk_ref.shape[0], qk_ref.shape[1]
    n_sub = n_tok // SUB

    @pl.when(c_idx == 0)
    def _():
        state[...] = jnp.zeros_like(state)

    row_k = lax.broadcasted_iota(jnp.int32, (n_tok, GLA_DK), 0)
    att_row = lax.broadcasted_iota(jnp.int32, (SUB, n_tok), 0)
    att_col = lax.broadcasted_iota(jnp.int32, (SUB, n_tok), 1)
    own_col = [jnp.where((att_col >= i * SUB) & (att_col - i * SUB <= att_row), att_col - i * SUB, -1)
               for i in range(n_sub)]

    def cum_log2_decay(n):
        logd = -_softplus(-(_dotp(gl_ref[n], w2_ref[...]) + gb_ref[...])) * (LOG2E / GLA_GATE_NORMALIZER)
        return _cumsum_rows(logd)

    units = [(n, h) for n in range(n_seq) for h in range(GLA_HEADS)]
    us = range(len(units))
    ks = lambda h: slice(h * GLA_DK, (h + 1) * GLA_DK)
    vs = lambda h: slice(h * GLA_DV, (h + 1) * GLA_DV)
    b_seq = [cum_log2_decay(n) for n in range(n_seq)]
    q = [qk_ref[n, :, ks(h)] * (GLA_DK ** -0.5) for n, h in units]
    k = [qk_ref[n, :, GLA_KEY_WIDTH + h * GLA_DK:GLA_KEY_WIDTH + (h + 1) * GLA_DK] for n, h in units]
    b = [b_seq[n][:, ks(h)] for n, h in units]
    v = [v_ref[n, :, vs(h)] for n, h in units]
    st = [state[n, h] for n, h in units]
    o_inter = [_dotp(q[u] * jnp.exp2(b[u]), st[u], NT) for u in us]
    blocks = [[] for _ in us]
    for i in range(n_sub):
        lo = i * SUB
        rows = slice(lo, lo + SUB)
        if i > 0:
            att = [_dotp(q[u][rows] * jnp.exp2(b[u][rows] - b[u][lo - 1:lo]),
                         jnp.where(row_k < lo, k[u] * jnp.exp2(b[u][lo - 1:lo] - b[u]), 0.0), NT) for u in us]
        else:
            att = [jnp.zeros((SUB, n_tok), F32) for _ in us]
        for j in range(SUB):
            tok = slice(lo + j, lo + j + 1)
            col = [jnp.sum(q[u][rows] * (k[u][tok] * jnp.exp2(b[u][rows] - b[u][tok])), axis=1, keepdims=True)
                   for u in us]
            att = [jnp.where(own_col[i] == j, col[u], att[u]) for u in us]
        for u in us:
            blocks[u].append(o_inter[u][rows] + _dotp(att[u], v[u], NN))
    for u, (n, h) in enumerate(units):
        o = jnp.concatenate(blocks[u], axis=0)
        og = og_ref[n, :, vs(h)]
        o_ref[n, :, vs(h)] = (_rmsnorm(o, nw_ref[...], GLA_NORM_EPS)
                              * (og * jax.nn.sigmoid(og))).astype(o_ref.dtype)
        b_last = b[u][n_tok - 1:n_tok, :]
        state[n, h] = st[u] * jnp.exp2(b_last) + _dotp(v[u].T, k[u] * jnp.exp2(b_last - b[u]), NN)

    @pl.when(c_idx == pl.num_programs(1) - 1)
    def _():
        s_ref[...] = state[...]


def gla_prompt(rows, proj, gate_w2, gate_b, norm_w):
    nc = rows.t // CHUNK
    n_seq = _largest_tile(rows.bp, GLA_SEQS, 1)
    proj3 = proj.reshape(rows.bp, rows.t, proj.shape[1])
    tok = lambda width, col: pl.BlockSpec((n_seq, CHUNK, width), lambda b, c: (b, c, col))
    const = lambda shape: pl.BlockSpec(shape, lambda b, c: (0, 0))
    out, final_state = pl.pallas_call(
        _gla_chunk_kernel,
        grid=(rows.bp // n_seq, nc),
        in_specs=[tok(GLA_WIDTH, 0), tok(GLA_WIDTH, 1), tok(GLA_WIDTH, 2), tok(LANES, DST["gl"][0] // LANES),
                  const((LANES, GLA_KEY_WIDTH)), const((1, GLA_KEY_WIDTH)), const((1, GLA_DV))],
        out_specs=[tok(GLA_WIDTH, 0),
                   pl.BlockSpec((n_seq, GLA_HEADS, GLA_DV, GLA_DK), lambda b, c: (b, 0, 0, 0))],
        out_shape=[jax.ShapeDtypeStruct((rows.bp, rows.t, GLA_WIDTH), BF16),
                   jax.ShapeDtypeStruct((rows.bp, GLA_HEADS, GLA_DV, GLA_DK), F32)],
        scratch_shapes=[pltpu.VMEM((n_seq, GLA_HEADS, GLA_DV, GLA_DK), F32)],
        compiler_params=_params("parallel", "arbitrary"),
    )(proj3, proj3, proj3, proj3, gate_w2, gate_b, norm_w)
    return out.reshape(rows.rows, GLA_WIDTH), final_state


def _outproj_kernel(og_ref, orw_ref, x_ref, gt_ref, sc_ref, sh_ref, nw_ref, wo_ref, wr_ref,
                    x1s_ref, h2s_ref, lgs_ref, x1_ref, h2_ref, lg_ref, *, n_prompt_tiles):
    i = pl.program_id(0)

    @pl.when(i < n_prompt_tiles)
    def _():
        half = og_ref.shape[1]
        mix = _dot(og_ref[...], wo_ref[:half, :]) + _dot(orw_ref[...], wo_ref[half:, :])
        x1 = x_ref[...] + gt_ref[0] * mix
        h2 = _rmsnorm(x1, nw_ref[...], NORM_EPS) * (1.0 + sc_ref[0]) + sh_ref[0]
        x1_ref[...] = x1
        h2_ref[...] = h2
        lg_ref[...] = _dotp(h2, wr_ref[...], NN, 3)

    @pl.when(i >= n_prompt_tiles)
    def _():
        x1_ref[...] = x1s_ref[...]
        h2_ref[...] = h2s_ref[...]
        lg_ref[...] = lgs_ref[...]


def out_projection(t, o_gla, o_rwkv, xp, mod_p, norm_w, w_out, w_router, x1_s, h2_s, logits_s):
    n_p, d = xp.shape
    pad_s = x1_s.shape[0]
    tm = ROW_TILE
    npt, per = n_p // tm, t // tm
    prompt = lambda width: pl.BlockSpec((tm, width), lambda i: (jnp.minimum(i, npt - 1), 0))
    pmod = lambda col: pl.BlockSpec((1, 1, d), lambda i: (jnp.minimum(i // per, mod_p.shape[0] - 1), 0, col))
    sample = lambda width: pl.BlockSpec((tm, width), lambda i: (jnp.maximum(i - npt, 0), 0))
    rows = lambda width: pl.BlockSpec((tm, width), lambda i: (i, 0))
    const = lambda a: pl.BlockSpec(a.shape, lambda i: (0, 0), pipeline_mode=pl.Buffered(1))
    n_rows = n_p + pad_s
    return pl.pallas_call(
        functools.partial(_outproj_kernel, n_prompt_tiles=npt),
        grid=(n_rows // tm,),
        in_specs=[prompt(o_gla.shape[1]), prompt(o_rwkv.shape[1]), prompt(d), pmod(2), pmod(4), pmod(3),
                  const(norm_w), const(w_out), const(w_router), sample(d), sample(d), sample(LANES)],
        out_specs=[rows(d), rows(d), rows(LANES)],
        out_shape=[jax.ShapeDtypeStruct((n_rows, d), F32), jax.ShapeDtypeStruct((n_rows, d), F32),
                   jax.ShapeDtypeStruct((n_rows, LANES), F32)],
        compiler_params=_params("arbitrary"),
    )(o_gla, o_rwkv, xp, mod_p, mod_p, mod_p, norm_w, w_out, w_router, x1_s, h2_s, logits_s)


def _moe_kernel(tile_expert_ref, n_valid_ref, n_real_ref, src_ref, dst_ref, h2_hbm, wg_ref, wu_ref, wd_ref,
                y_hbm, xbuf, obuf, gather_sem, scatter_sem):
    del tile_expert_ref
    i = pl.program_id(0)
    n_tiles = pl.num_programs(0)
    n_valid = n_valid_ref[0]
    slot = i % 2

    def for_rows(n, fn):
        def group(g, carry):
            for u in range(DMA_UNROLL):
                fn(g * DMA_UNROLL + u)
            return carry

        def single(r, carry):
            fn(r)
            return carry
        full = n // DMA_UNROLL
        lax.fori_loop(0, full, group, 0)
        lax.fori_loop(full * DMA_UNROLL, n, single, 0)

    def start_gather(tile, s):
        def one(r):
            tok = src_ref[tile * MOE_TILE + r]
            pltpu.make_async_copy(h2_hbm.at[pl.ds(tok, 1), :], xbuf.at[s, pl.ds(r, 1), :],
                                  gather_sem.at[s]).start()
        for_rows(MOE_TILE, one)

    def wait_gather(s):
        pltpu.make_async_copy(h2_hbm.at[pl.ds(0, MOE_TILE), :], xbuf.at[s], gather_sem.at[s]).wait()

    def scatter_row(tile, s, r):
        row = dst_ref[tile * MOE_TILE + r]
        return pltpu.make_async_copy(obuf.at[s, pl.ds(r, 1), :], y_hbm.at[pl.ds(row, 1), :], scatter_sem.at[s])

    def start_scatter(tile, s):
        for_rows(n_real_ref[tile], lambda r: scatter_row(tile, s, r).start())

    def wait_scatter(tile, s):
        n_real = n_real_ref[tile]

        @pl.when(n_real == MOE_TILE)
        def _():
            pltpu.make_async_copy(obuf.at[s], y_hbm.at[pl.ds(0, MOE_TILE), :], scatter_sem.at[s]).wait()

        @pl.when(n_real < MOE_TILE)
        def _():
            for_rows(n_real, lambda r: scatter_row(tile, s, r).wait())

    @pl.when(i == 0)
    def _():
        start_gather(0, 0)

    @pl.when(i + 1 < n_valid)
    def _():
        start_gather(i + 1, 1 - slot)

    @pl.when(i < n_valid)
    def _():
        wait_gather(slot)

        @pl.when(i >= 2)
        def _():
            wait_scatter(i - 2, slot)

        x = xbuf[slot].astype(BF16)
        gate = _dot(x, wg_ref[0].astype(BF16))
        up = _dot(x, wu_ref[0].astype(BF16))
        hid = gate * jax.nn.sigmoid(gate) * up
        obuf[slot] = _dot(hid.astype(BF16), wd_ref[0].astype(BF16))
        start_scatter(i, slot)

    @pl.when(i == n_tiles - 1)
    def _():
        @pl.when(n_valid >= 2)
        def _():
            wait_scatter(n_valid - 2, n_valid % 2)

        wait_scatter(n_valid - 1, (n_valid - 1) % 2)


def moe_experts(h2, plan, w_gate, w_up, w_down):
    n_rows, d = h2.shape
    _, _, ff = w_gate.shape
    n_tiles = plan["tile_expert"].shape[0]
    grid_spec = pltpu.PrefetchScalarGridSpec(
        num_scalar_prefetch=5,
        grid=(n_tiles,),
        in_specs=[pl.BlockSpec(memory_space=pl.ANY),
                  pl.BlockSpec((1, d, ff), lambda i, te, *_: (te[i], 0, 0)),
                  pl.BlockSpec((1, d, ff), lambda i, te, *_: (te[i], 0, 0)),
                  pl.BlockSpec((1, ff, d), lambda i, te, *_: (te[i], 0, 0))],
        out_specs=pl.BlockSpec(memory_space=pl.ANY),
        scratch_shapes=[pltpu.VMEM((2, MOE_TILE, d), F32), pltpu.VMEM((2, MOE_TILE, d), F32),
                        pltpu.SemaphoreType.DMA((2,)), pltpu.SemaphoreType.DMA((2,))],
    )
    return pl.pallas_call(
        _moe_kernel,
        grid_spec=grid_spec,
        out_shape=jax.ShapeDtypeStruct((2 * n_rows, d), F32),
        compiler_params=_params("arbitrary"),
    )(plan["tile_expert"], plan["n_valid"], plan["n_real"], plan["src_row"], plan["dst_row"], h2,
      w_gate, w_up, w_down)


def _route(logits):
    p_group = jax.nn.softmax(logits[:, :N_GROUPS], axis=-1)
    g_sel = jnp.argmax(p_group, axis=-1)
    p_sel = jnp.take_along_axis(p_group, g_sel[:, None], axis=-1)
    logits_e = logits[:, N_GROUPS:N_GROUPS + N_EXPERTS].reshape(-1, N_GROUPS, EXPERTS_PER_GROUP)
    logits_e = jnp.take_along_axis(logits_e, g_sel[:, None, None], axis=1)[:, 0]
    i1 = jnp.argmax(logits_e, axis=-1)
    v1 = jnp.take_along_axis(logits_e, i1[:, None], axis=-1)
    masked = jnp.where(jnp.arange(EXPERTS_PER_GROUP)[None, :] == i1[:, None], -jnp.inf, logits_e)
    i2 = jnp.argmax(masked, axis=-1)
    v2 = jnp.take_along_axis(logits_e, i2[:, None], axis=-1)
    w_top = jax.nn.softmax(jnp.concatenate([v1, v2], axis=-1), axis=-1) * p_sel
    top_i = jnp.stack([i1, i2], axis=-1)
    expert = g_sel[:, None].astype(jnp.int32) * EXPERTS_PER_GROUP + top_i.astype(jnp.int32)
    return expert, w_top


def _moe_plan(expert):
    n = expert.shape[0]
    e_flat = expert.reshape(-1)
    ids = jnp.arange(N_EXPERTS, dtype=jnp.int32)
    one_hot = (e_flat[:, None] == ids[None, :]).astype(jnp.int32)
    counts = jnp.sum(one_hot, axis=0)
    padded = (counts + MOE_TILE - 1) // MOE_TILE * MOE_TILE
    ends = jnp.cumsum(padded)
    starts = ends - padded
    rank = jnp.sum(jnp.cumsum(one_hot, axis=0) * one_hot, axis=1) - 1
    pos = jnp.sum(one_hot * starts[None, :], axis=1) + rank
    p_max = _round_up(2 * n, MOE_TILE) + N_EXPERTS * MOE_TILE
    n_tiles = p_max // MOE_TILE
    pair = jnp.full((p_max,), -1, jnp.int32).at[pos].set(jnp.arange(2 * n, dtype=jnp.int32))
    pair0 = jnp.maximum(pair, 0)
    src_row = pair0 // 2
    dst_row = (pair0 % 2) * n + src_row
    tile_start = jnp.arange(n_tiles, dtype=jnp.int32) * MOE_TILE
    n_valid = ends[-1] // MOE_TILE
    tile_expert = jnp.sum((tile_start[:, None] >= ends[None, :]).astype(jnp.int32), axis=1)
    tile_expert = jnp.where(tile_start < ends[-1], tile_expert, tile_expert[n_valid - 1])
    n_real = jnp.clip((starts + counts)[tile_expert] - tile_start, 0, MOE_TILE)
    n_real = jnp.where(tile_start < ends[-1], n_real, 0).astype(jnp.int32)
    return dict(src_row=src_row, dst_row=dst_row, tile_expert=tile_expert,
                n_valid=n_valid.astype(jnp.int32)[None], n_real=n_real)


def _final_kernel(x1_ref, y0_ref, y1_ref, wt_ref, gtp_ref, gts_ref, scp_ref, scs_ref, shp_ref, shs_ref, nw_ref,
                  yp_ref, ys_ref, *, n_prompt_tiles):
    i = pl.program_id(0)

    def finish(gt, sc, sh):
        moe = wt_ref[:, 0:1] * y0_ref[...] + wt_ref[:, 1:2] * y1_ref[...]
        x2 = x1_ref[...] + gt * moe
        return _rmsnorm(x2, nw_ref[...], NORM_EPS) * (1.0 + sc) + sh

    @pl.when(i < n_prompt_tiles)
    def _():
        yp_ref[...] = finish(gtp_ref[0], scp_ref[0], shp_ref[0])

    @pl.when(i >= n_prompt_tiles)
    def _():
        ys_ref[...] = finish(gts_ref[...], scs_ref[...], shs_ref[...])


def final_norm(n_p, t, x1, y_pairs, w_top, mod_p, mod_s, modf_p, modf_s, norm_w):
    n_rows, d = x1.shape
    tile = ROW_TILE
    n_tiles, npt = n_rows // tile, n_p // tile
    per = t // tile
    rows = lambda off: pl.BlockSpec((tile, d), lambda i: (i + off, 0))
    prompt_out = pl.BlockSpec((tile, d), lambda i: (jnp.minimum(i, npt - 1), 0))
    sample = lambda col: pl.BlockSpec((tile, d), lambda i: (jnp.maximum(i - npt, 0), col))
    pmod = lambda col: pl.BlockSpec((1, 1, d), lambda i: (jnp.minimum(i // per, mod_p.shape[0] - 1), 0, col))
    return pl.pallas_call(
        functools.partial(_final_kernel, n_prompt_tiles=npt),
        grid=(n_tiles,),
        in_specs=[rows(0), rows(0), rows(n_tiles), pl.BlockSpec((tile, w_top.shape[1]), lambda i: (i, 0)),
                  pmod(5), sample(5), pmod(1), sample(1), pmod(0), sample(0),
                  pl.BlockSpec((1, d), lambda i: (0, 0))],
        out_specs=[prompt_out, sample(0)],
        out_shape=[jax.ShapeDtypeStruct((n_p, d), F32), jax.ShapeDtypeStruct((n_rows - n_p, d), F32)],
        compiler_params=_params("arbitrary"),
    )(x1, y_pairs, y_pairs, w_top, mod_p, mod_s, modf_p, modf_s, modf_p, modf_s, norm_w)


def _align_rows(wt):
    parts = []
    for name in _ALIGNED:
        off, w = SRC[name]
        part = wt[off:off + w]
        wa = DST[name][1]
        if wa != w:
            part = jnp.pad(part, ((0, wa - w), (0, 0)))
        parts.append(part)
    return jnp.concatenate(parts, axis=0)


def _rwkv_seg(a, name):
    off, w = SRC[name]
    return a[..., off - RWKV_SRC_BASE:off - RWKV_SRC_BASE + w]


def _rwkv_seg_padded(a, name):
    seg = _rwkv_seg(a, name)
    wa = DST[name][1]
    return jnp.pad(seg, [(0, 0)] * (a.ndim - 1) + [(0, wa - seg.shape[-1])])


def _orig_seg(p, name):
    off, w = SRC[name]
    return p[..., off:off + w]


def _pad_rows(w, rows):
    return jnp.pad(w, ((0, rows - w.shape[0]), (0, 0)))


def kernel(x_prompt, x_sample, state_gla, state_rwkv, state_shift, c_prompt, c_sample, w_ada, b_ada, norm_mix, norm_ffn, w_in, gla_gate_w2, gla_gate_b, gla_norm, rwkv_mu, rwkv_w0, rwkv_w2, rwkv_a0, rwkv_a2, rwkv_g2, rwkv_k_k, rwkv_k_a, rwkv_r_k, rwkv_gn_w, rwkv_gn_b, w_out, w_router_group, w_router_expert, w_exp_gate, w_exp_up, w_exp_down, norm_final, w_ada_final, b_ada_final):
    assert w_ada.shape[0] == 1, "single-layer step"
    bp, t, d = x_prompt.shape
    bs = x_sample.shape[0]
    assert x_sample.shape[1] == 1 and t % ROW_TILE == 0
    n_p = bp * t
    pad_s = _round_up(bs, ROW_TILE)
    n_rows = n_p + pad_s
    prompt = _PromptShape(bp, t, n_p)
    hi = lax.Precision.HIGHEST
    pad_sample = lambda a: jnp.pad(a, ((0, pad_s - a.shape[0]), (0, 0)))
    hdot = lambda a, w: jnp.dot(a, w, precision=hi)

    c_act = jax.nn.silu(jnp.concatenate([c_prompt, c_sample], axis=0))
    mod = matmul3(c_act, w_ada[0]) + b_ada[0]
    mod_f = matmul3(c_act, w_ada_final) + b_ada_final
    mod_p, mod_s = mod[:bp, None, :], mod[bp:]
    modf_p, modf_s = mod_f[:bp, None, :], mod_f[bp:]
    sh1_s, sc1_s, gt1_s, sh2_s, sc2_s, _ = jnp.split(mod_s, 6, axis=-1)
    xp = x_prompt.reshape(n_p, d)
    xs = x_sample[:, 0, :]

    w_in_t = jnp.swapaxes(w_in[0], 0, 1)
    proj = in_projection(xp, t, mod_p, norm_mix, _align_rows(w_in_t.astype(BF16)))
    o_gla, gla_t_p = gla_prompt(prompt, proj, _pad_rows(gla_gate_w2[0], LANES), gla_gate_b, gla_norm)
    new_gla_p = jnp.swapaxes(gla_t_p, -1, -2)
    mu = rwkv_mu[0]
    mu_big = jnp.concatenate([_rwkv_seg(mu, n) for n in ("r", "k7", "v7")])[None, :]
    mu_small = jnp.concatenate([_rwkv_seg_padded(mu, n) for n in ("wl", "al", "gl7")])[None, :]
    vecs = jnp.concatenate([rwkv_w0, rwkv_a0, rwkv_k_k, rwkv_k_a, rwkv_r_k[0].reshape(1, RWKV_WIDTH),
                            rwkv_gn_w, rwkv_gn_b, jnp.zeros((1, RWKV_WIDTH), F32)], axis=0)
    o_rwkv, rwkv_t_p = rwkv7_prompt(prompt, proj, mu_big, mu_small, vecs, _pad_rows(rwkv_w2[0], LANES),
                                    _pad_rows(rwkv_a2[0], LANES), rwkv_g2[0])
    new_rwkv_p = rwkv_t_p.reshape(bp, RWKV_HEAD, RWKV_HEADS, RWKV_HEAD).transpose(0, 2, 1, 3)
    last = jnp.stack([proj[(b + 1) * t - 1] for b in range(bp)])
    new_shift_p = jnp.concatenate([last[:, DST[n][0]:DST[n][0] + SRC[n][1]] for n in _RWKV_ORIG], axis=-1)

    h1_s = _rmsnorm(xs, norm_mix[0], NORM_EPS) * (1.0 + sc1_s) + sh1_s
    proj_s = matmul3(h1_s, w_in_t, w_transposed=True)
    heads = lambda a, n: a.reshape(a.shape[0], n, -1)
    logd_s = jax.nn.log_sigmoid(hdot(_orig_seg(proj_s, "gl"), gla_gate_w2[0]) + gla_gate_b[0]) \
        / GLA_GATE_NORMALIZER
    q_s = heads(_orig_seg(proj_s, "q") * (GLA_DK ** -0.5), GLA_HEADS)
    k_s, v_s, g_s = heads(_orig_seg(proj_s, "k"), GLA_HEADS), heads(_orig_seg(proj_s, "v"), GLA_HEADS), \
        heads(logd_s, GLA_HEADS)
    new_gla_s = jnp.exp(g_s)[..., None] * state_gla[0] + k_s[..., :, None] * v_s[..., None, :]
    o_s = jnp.einsum("bhd,bhde->bhe", q_s, new_gla_s, precision=hi)
    o_s = _rmsnorm(o_s, gla_norm[0], GLA_NORM_EPS) * heads(jax.nn.silu(_orig_seg(proj_s, "og")), GLA_HEADS)

    rp_s = proj_s[:, RWKV_SRC_BASE:]
    new_shift_s = rp_s
    xs7 = rp_s + (state_shift[0] - rp_s) * mu
    sx = lambda name: _rwkv_seg(xs7, name)
    r_s, k7_s, v7_s = sx("r"), sx("k7"), sx("v7")
    w_pre = rwkv_w0[0] + hdot(jnp.tanh(sx("wl")), rwkv_w2[0])
    decay_s = jnp.exp(-jnp.exp(-jax.nn.softplus(-w_pre) - 0.5))
    a_s = jax.nn.sigmoid(rwkv_a0[0] + hdot(sx("al"), rwkv_a2[0]))
    g_s7 = hdot(jax.nn.sigmoid(sx("gl7")), rwkv_g2[0])
    hs = lambda z: z.reshape(bs, RWKV_HEADS, RWKV_HEAD)
    kk_s = hs(k7_s * rwkv_k_k[0])
    kk_s = kk_s / jnp.maximum(jnp.sqrt(jnp.sum(kk_s * kk_s, axis=-1, keepdims=True)), 1e-12)
    k7_s = k7_s * (1.0 + (a_s - 1.0) * rwkv_k_a[0])
    r_h, w_h, k_h, v_h, a_h = hs(r_s), hs(decay_s), hs(k7_s), hs(v7_s), hs(a_s)
    s_prev = state_rwkv[0]
    sa = jnp.einsum("bhij,bhj->bhi", s_prev, -kk_s, precision=hi)
    new_rwkv_s = s_prev * w_h[:, :, None, :] + sa[..., None] * (kk_s * a_h)[:, :, None, :] \
        + v_h[..., None] * k_h[:, :, None, :]
    y_s = jnp.einsum("bhij,bhj->bhi", new_rwkv_s, r_h, precision=hi)
    y_mu = jnp.mean(y_s, axis=-1, keepdims=True)
    y_var = jnp.mean(jnp.square(y_s - y_mu), axis=-1, keepdims=True)
    y_s = (y_s - y_mu) * lax.rsqrt(y_var + RWKV_GN_EPS) * rwkv_gn_w[0].reshape(RWKV_HEADS, RWKV_HEAD) \
        + rwkv_gn_b[0].reshape(RWKV_HEADS, RWKV_HEAD)
    bonus = jnp.sum(r_h * k_h * rwkv_r_k[0], axis=-1, keepdims=True) * v_h
    o_rs = (y_s + bonus).reshape(bs, RWKV_WIDTH) * g_s7
    mix_s = matmul3(jnp.concatenate([o_s.reshape(bs, GLA_WIDTH), o_rs], axis=-1), w_out[0])
    x1_s = xs + gt1_s * mix_s
    h2_s = _rmsnorm(x1_s, norm_ffn[0], NORM_EPS) * (1.0 + sc2_s) + sh2_s
    w_router = jnp.pad(jnp.concatenate([w_router_group[0], w_router_expert[0]], axis=-1),
                       ((0, 0), (0, LANES - N_GROUPS - N_EXPERTS)))
    logits_s = hdot(h2_s, w_router)

    x1, h2, logits = out_projection(t, o_gla, o_rwkv, xp, mod_p, norm_ffn, w_out[0].astype(BF16), w_router,
                                    pad_sample(x1_s), pad_sample(h2_s), pad_sample(logits_s))

    expert, w_top = _route(logits)
    plan = _moe_plan(expert)
    y_pairs = moe_experts(h2, plan, w_exp_gate[0].reshape(N_EXPERTS, d, EXPERT_FF),
                          w_exp_up[0].reshape(N_EXPERTS, d, EXPERT_FF),
                          w_exp_down[0].reshape(N_EXPERTS, EXPERT_FF, d))

    y_p, y_s_pad = final_norm(n_p, t, x1, y_pairs, w_top, mod_p, pad_sample(mod_s), modf_p,
                              pad_sample(modf_s), norm_final[None, :])
    return (y_p.reshape(bp, t, d), y_s_pad[:bs, None, :], new_gla_p[None], new_rwkv_p[None], new_shift_p[None],
            new_gla_s[None], new_rwkv_s[None], new_shift_s[None])
```

```python
import collections
import functools

import jax
import jax.numpy as jnp
from jax import lax
from jax.experimental import pallas as pl
from jax.experimental.pallas import tpu as pltpu

F32 = jnp.float32
BF16 = jnp.bfloat16

D_MODEL = 2048
GLA_HEADS = 4
GLA_DK = 128
GLA_DV = 256
GLA_KEY_WIDTH = GLA_HEADS * GLA_DK
GLA_WIDTH = GLA_HEADS * GLA_DV
GLA_GATE_NORMALIZER = 16.0
RWKV_HEAD = 64
RWKV_HEADS = 16
RWKV_WIDTH = RWKV_HEAD * RWKV_HEADS
N_GROUPS = 4
EXPERTS_PER_GROUP = 8
N_EXPERTS = N_GROUPS * EXPERTS_PER_GROUP
EXPERT_FF = 512
NORM_EPS = 1e-6
GLA_NORM_EPS = 1e-5
RWKV_GN_EPS = 64e-5

LANES = 128
SUBLANES = 8
VMEM_LIMIT_BYTES = 56 * 1024 * 1024

CHUNK = 64
SUB = 16
RWKV_GROUP = 2
GROUP_W = RWKV_GROUP * RWKV_HEAD
GLA_SEQS = 2
LOG2E = 1.4426950408889634
RWKV_SEQS = 4
MOE_TILE = 256
DMA_UNROLL = 8
ROW_TILE = 256
NORM_SLAB = 128
K_SPLIT = 256

_ORIG = (("q", 512), ("k", 512), ("v", 1024), ("gl", 16), ("og", 1024),
         ("r", 1024), ("wl", 64), ("k7", 1024), ("v7", 1024), ("al", 64), ("gl7", 128))
_ALIGNED = ("q", "k", "v", "og", "r", "k7", "v7", "gl", "wl", "al", "gl7")
_RWKV_ORIG = ("r", "wl", "k7", "v7", "al", "gl7")


def _round_up(n, m):
    return (n + m - 1) // m * m


def _layouts():
    src, off = {}, 0
    for name, w in _ORIG:
        src[name] = (off, w)
        off += w
    dst, pos = {}, 0
    for name in _ALIGNED:
        wa = _round_up(src[name][1], LANES)
        dst[name] = (pos, wa)
        pos += wa
    return src, off, dst, pos


SRC, IN_COLS, DST, IN_COLS_ALIGNED = _layouts()
RWKV_SRC_BASE = SRC["r"][0]
BIG_W = 3 * RWKV_WIDTH
SMALL_W = 3 * LANES
assert DST["r"][0] == BIG_W and DST["k7"][0] == BIG_W + RWKV_WIDTH and DST["v7"][0] == BIG_W + 2 * RWKV_WIDTH
assert DST["v"][0] == RWKV_WIDTH and DST["og"][0] == 2 * RWKV_WIDTH and DST["k"][0] == GLA_KEY_WIDTH

NN = ((1,), (0,))
NT = ((1,), (1,))

_PromptShape = collections.namedtuple("_PromptShape", "bp t rows")


def _dot(a, b, dims=NN):
    return lax.dot_general(a, b, (dims, ((), ())), preferred_element_type=F32)


def _split2(x):
    hi = x.astype(BF16)
    lo = (x - hi.astype(F32)).astype(BF16)
    return hi, lo


def _dotp(a, b, dims=NN, passes=1):
    if passes == 1:
        return _dot(a.astype(BF16), b.astype(BF16), dims)
    ah, al = _split2(a)
    bh, bl = _split2(b)
    return _dot(ah, bh, dims) + (_dot(ah, bl, dims) + _dot(al, bh, dims))


def _cumsum_rows(x):
    n = x.shape[0]
    row = lax.broadcasted_iota(jnp.int32, (n, n), 0)
    col = lax.broadcasted_iota(jnp.int32, (n, n), 1)
    tri = (row >= col).astype(BF16)
    x1 = x.astype(BF16)
    r1 = x - x1.astype(F32)
    x2 = r1.astype(BF16)
    x3 = (r1 - x2.astype(F32)).astype(BF16)
    return _dot(tri, x1) + (_dot(tri, x2) + _dot(tri, x3))


def _softplus(z):
    return jnp.maximum(z, 0.0) + jnp.log1p(jnp.exp(-jnp.abs(z)))


def _rmsnorm(x, g, eps):
    return x * lax.rsqrt(jnp.mean(x * x, axis=-1, keepdims=True) + eps) * g


def _largest_tile(n, cap, mult=SUBLANES):
    if n <= cap:
        return n
    best = None
    for t in range(mult, cap + 1, mult):
        if n % t == 0:
            best = t
    assert best is not None, (n, cap)
    return best


def _params(*sem):
    return pltpu.CompilerParams(dimension_semantics=sem, vmem_limit_bytes=VMEM_LIMIT_BYTES)


def _mm3_kernel(a_ref, w_ref, o_ref, *, dims):
    @pl.when(pl.program_id(0) == 0)
    def _():
        o_ref[...] = jnp.zeros_like(o_ref)

    o_ref[...] += _dotp(a_ref[...], w_ref[...], dims, 3)


def matmul3(a, w, w_transposed=False):
    m, k = a.shape
    n = w.shape[0] if w_transposed else w.shape[1]
    tk = _largest_tile(k, K_SPLIT, LANES)
    w_spec = pl.BlockSpec((n, tk), lambda s: (0, s)) if w_transposed else pl.BlockSpec((tk, n), lambda s: (s, 0))
    return pl.pallas_call(
        functools.partial(_mm3_kernel, dims=NT if w_transposed else NN),
        grid=(k // tk,),
        in_specs=[pl.BlockSpec((m, tk), lambda s: (0, s)), w_spec],
        out_specs=pl.BlockSpec((m, n), lambda s: (0, 0)),
        out_shape=jax.ShapeDtypeStruct((m, n), F32),
        compiler_params=_params("arbitrary"),
    )(a, w)


def _prompt_mod(tiles_per_batch, width, col):
    return pl.BlockSpec((1, 1, width), lambda i, *_: (i // tiles_per_batch, 0, col))


def _inproj_kernel(x_ref, sc_ref, sh_ref, nw_ref, w_ref, o_ref, h_scr):
    @pl.when(pl.program_id(1) == 0)
    def _():
        for r in range(0, h_scr.shape[0], NORM_SLAB):
            sl = slice(r, r + NORM_SLAB)
            h = _rmsnorm(x_ref[sl, :], nw_ref[...], NORM_EPS) * (1.0 + sc_ref[0]) + sh_ref[0]
            h_scr[sl, :] = h.astype(BF16)

    o_ref[...] = _dot(h_scr[...], w_ref[...], NT)


def in_projection(xp, t, mod_p, norm_w, w_aligned):
    n_p, d = xp.shape
    n = w_aligned.shape[0]
    tm = _largest_tile(t, 1024, NORM_SLAB)
    tn = _largest_tile(n, 512, LANES)
    per = t // tm
    return pl.pallas_call(
        _inproj_kernel,
        grid=(n_p // tm, n // tn),
        in_specs=[pl.BlockSpec((tm, d), lambda i, j: (i, 0)), _prompt_mod(per, d, 1), _prompt_mod(per, d, 0),
                  pl.BlockSpec((1, d), lambda i, j: (0, 0)), pl.BlockSpec((tn, d), lambda i, j: (j, 0))],
        out_specs=pl.BlockSpec((tm, tn), lambda i, j: (i, j)),
        out_shape=jax.ShapeDtypeStruct((n_p, n), F32),
        scratch_shapes=[pltpu.VMEM((tm, d), BF16)],
        compiler_params=_params("parallel", "arbitrary"),
    )(xp, mod_p, mod_p, norm_w, w_aligned)


def _rwkv_chunk_kernel(big_ref, wl_ref, al_ref, gl_ref, mu_big_ref, mu_small_ref, vec_ref, w2_ref, a2_ref,
                       g2_ref, o_ref, s_ref, state, carry_big, carry_small, *, passes):
    c_idx = pl.program_id(1)
    n_seq, n_tok = big_ref.shape[0], big_ref.shape[1]
    n_grp = RWKV_WIDTH // GROUP_W
    grp = [slice(g * GROUP_W, (g + 1) * GROUP_W) for g in range(n_grp)]

    @pl.when(c_idx == 0)
    def _():
        state[...] = jnp.zeros_like(state)
        carry_big[...] = jnp.zeros_like(carry_big)
        carry_small[...] = jnp.zeros_like(carry_small)

    first_row = lax.broadcasted_iota(jnp.int32, (n_tok, 1), 0) == 0
    w0, a0, k_k, k_a, r_k, gn_w, gn_b = (vec_ref[i:i + 1, :] for i in range(7))

    gi = lax.broadcasted_iota(jnp.int32, (GROUP_W, GROUP_W), 0) // RWKV_HEAD
    gj = lax.broadcasted_iota(jnp.int32, (GROUP_W, GROUP_W), 1) // RWKV_HEAD
    head_ones = (gi == gj).astype(BF16)

    def head_sums(parts):
        hi, lo = _split2(jnp.concatenate(parts, axis=0))
        out = _dot(hi, head_ones) + _dot(lo, head_ones)
        return [out[i * n_tok:(i + 1) * n_tok] for i in range(len(parts))]

    def prepare(n):
        def token_shift(cur, carry, mu):
            prev = jnp.where(first_row, carry[n, 0:1, :], pltpu.roll(cur, 1, 0))
            carry[n, 0:1, :] = cur[n_tok - 1:n_tok, :]
            return cur + (prev - cur) * mu

        xs_big = token_shift(big_ref[n], carry_big, mu_big_ref[...])
        small = jnp.concatenate([wl_ref[n], al_ref[n], gl_ref[n]], axis=1)
        xs_small = token_shift(small, carry_small, mu_small_ref[...])
        r = xs_big[:, :RWKV_WIDTH]
        k7 = xs_big[:, RWKV_WIDTH:2 * RWKV_WIDTH]
        v = xs_big[:, 2 * RWKV_WIDTH:]
        w_pre = w0 + _dotp(jnp.tanh(xs_small[:, :LANES]), w2_ref[...])
        lw = -jnp.exp(-_softplus(-w_pre) - 0.5)
        a = jax.nn.sigmoid(a0 + _dotp(xs_small[:, LANES:2 * LANES], a2_ref[...]))
        gate = _dotp(jax.nn.sigmoid(xs_small[:, 2 * LANES:]), g2_ref[...])
        kk_raw = k7 * k_k
        k = k7 * (1.0 + (a - 1.0) * k_a)
        sums = [head_sums([kk_raw[:, sl] * kk_raw[:, sl], r[:, sl] * k[:, sl] * r_k[:, sl]]) for sl in grp]
        kk = jnp.concatenate([kk_raw[:, sl] / jnp.maximum(jnp.sqrt(sums[g][0]), 1e-12)
                              for g, sl in enumerate(grp)], axis=1)
        bonus = jnp.concatenate([sums[g][1] for g in range(n_grp)], axis=1) * v
        cum = _cumsum_rows(lw)
        cum_end = cum[n_tok - 1:n_tok, :]
        beta = kk * a
        g_inv = jnp.exp(-cum)
        g_end = jnp.exp(cum_end - cum)
        return dict(v=v, gate=gate, bonus=bonus, g_tot=jnp.exp(cum_end),
                    a_hat=-kk * jnp.exp(cum - lw), r_hat=r * jnp.exp(cum), b_hat=beta * g_inv,
                    k_hat=k * g_inv, b_end=beta * g_end, k_end=k * g_end)

    seqs = [prepare(n) for n in range(n_seq)]
    units = [(n, sl) for n in range(n_seq) for sl in grp]
    part = lambda name: [seqs[n][name][:, sl] for n, sl in units]

    lane = lax.broadcasted_iota(jnp.int32, (n_tok, GROUP_W), 1)
    tok = lax.broadcasted_iota(jnp.int32, (n_tok, GROUP_W), 0)
    lane_head = lane // RWKV_HEAD
    src_tok = lane % RWKV_HEAD
    strict = tok > src_tok
    incl = tok >= src_tok

    def bd(y):
        return jnp.concatenate([jnp.where(lane_head == h, y, 0.0) for h in range(RWKV_GROUP)], axis=0)

    gs = range(len(units))
    s0 = [state[n, :, sl] for n, sl in units]
    v_u, b_hat, k_hat, b_end, k_end = part("v"), part("b_hat"), part("k_hat"), part("b_end"), part("k_end")
    lhs2 = [jnp.concatenate([a, r], axis=0) for a, r in zip(part("a_hat"), part("r_hat"))]
    abrb = [_dotp(lhs2[g], bd(b_hat[g]), NT, passes) for g in gs]
    akrk = [_dotp(lhs2[g], bd(k_hat[g]), NT, passes) for g in gs]
    asrs = [_dotp(lhs2[g], bd(s0[g]), NT, passes) for g in gs]
    p = [jnp.where(strict, abrb[g][:n_tok], 0.0) for g in gs]
    ak = [jnp.where(strict, akrk[g][:n_tok], 0.0) for g in gs]
    rb = [jnp.where(incl, abrb[g][n_tok:], 0.0) for g in gs]
    rk = [jnp.where(incl, akrk[g][n_tok:], 0.0) for g in gs]
    bd_v = [bd(v_u[g]) for g in gs]
    x = [asrs[g][:n_tok] + _dotp(ak[g], bd_v[g], NN, passes) for g in gs]
    n_sq = n_tok.bit_length() - 1
    for it in range(n_sq):
        if it < n_sq - 1:
            both = [_dotp(p[g], jnp.concatenate([bd(p[g]), bd(x[g])], axis=1), NN, passes) for g in gs]
            x = [x[g] + both[g][:, GROUP_W:] for g in gs]
            p = [both[g][:, :GROUP_W] for g in gs]
        else:
            x = [x[g] + _dotp(p[g], bd(x[g]), NN, passes) for g in gs]
    y = [asrs[g][n_tok:] + _dotp(jnp.concatenate([rb[g], rk[g]], axis=1),
                                 jnp.concatenate([bd(x[g]), bd_v[g]], axis=0), NN, passes) for g in gs]
    full = [_dotp(jnp.concatenate([x[g], v_u[g]], axis=0).T,
                  jnp.concatenate([b_end[g], k_end[g]], axis=0), NN, passes) for g in gs]
    g_tot = part("g_tot")
    for g, (n, sl) in enumerate(units):
        upd = s0[g] * g_tot[g]
        for h in range(RWKV_GROUP):
            upd = upd + jnp.where(lane_head == h, full[g][h * RWKV_HEAD:(h + 1) * RWKV_HEAD, :], 0.0)
        state[n, :, sl] = upd

    inv_n = 1.0 / RWKV_HEAD
    mean = [head_sums([y[g]])[0] * inv_n for g in gs]
    dev = [y[g] - mean[g] for g in gs]
    var = [head_sums([dev[g] * dev[g]])[0] * inv_n for g in gs]
    bonus, gate = part("bonus"), part("gate")
    for g, (n, sl) in enumerate(units):
        yn = dev[g] * lax.rsqrt(var[g] + RWKV_GN_EPS) * gn_w[:, sl] + gn_b[:, sl]
        o_ref[n, :, sl] = ((yn + bonus[g]) * gate[g]).astype(o_ref.dtype)

    @pl.when(c_idx == pl.num_programs(1) - 1)
    def _():
        s_ref[...] = state[...]


def rwkv7_prompt(rows, proj, mu_big, mu_small, vecs, w2, a2, g2, *, passes=1):
    assert CHUNK == RWKV_HEAD and rows.t % CHUNK == 0
    nc = rows.t // CHUNK
    n_seq = _largest_tile(rows.bp, RWKV_SEQS, 1)
    proj3 = proj.reshape(rows.bp, rows.t, proj.shape[1])
    small_col = lambda name: DST[name][0] // LANES
    tok = lambda width, col: pl.BlockSpec((n_seq, CHUNK, width), lambda b, c: (b, c, col))
    const = lambda shape: pl.BlockSpec(shape, lambda b, c: (0, 0))
    out, final_state = pl.pallas_call(
        functools.partial(_rwkv_chunk_kernel, passes=passes),
        grid=(rows.bp // n_seq, nc),
        in_specs=[tok(BIG_W, 1), tok(LANES, small_col("wl")), tok(LANES, small_col("al")),
                  tok(LANES, small_col("gl7")),
                  const((1, BIG_W)), const((1, SMALL_W)), const((SUBLANES, RWKV_WIDTH)),
                  const((LANES, RWKV_WIDTH)), const((LANES, RWKV_WIDTH)), const((LANES, RWKV_WIDTH))],
        out_specs=[tok(RWKV_WIDTH, 0),
                   pl.BlockSpec((n_seq, RWKV_HEAD, RWKV_WIDTH), lambda b, c: (b, 0, 0))],
        out_shape=[jax.ShapeDtypeStruct((rows.bp, rows.t, RWKV_WIDTH), BF16),
                   jax.ShapeDtypeStruct((rows.bp, RWKV_HEAD, RWKV_WIDTH), F32)],
        scratch_shapes=[pltpu.VMEM((n_seq, RWKV_HEAD, RWKV_WIDTH), F32),
                        pltpu.VMEM((n_seq, SUBLANES, BIG_W), F32), pltpu.VMEM((n_seq, SUBLANES, SMALL_W), F32)],
        compiler_params=_params("parallel", "arbitrary"),
    )(proj3, proj3, proj3, proj3, mu_big, mu_small, vecs, w2, a2, g2)
    return out.reshape(rows.rows, RWKV_WIDTH), final_state


def _gla_chunk_kernel(qk_ref, v_ref, og_ref, gl_ref, w2_ref, gb_ref, nw_ref, o_ref, s_ref, state):
    c_idx = pl.program_id(1)
    n_seq, n_tok = qk_ref.shape[0], qk_ref.shape[1]
    n_sub = n_tok // SUB

    @pl.when(c_idx == 0)
    def _():
        state[...] = jnp.zeros_like(state)

    row_k = lax.broadcasted_iota(jnp.int32, (n_tok, GLA_DK), 0)
    att_row = lax.broadcasted_iota(jnp.int32, (SUB, n_tok), 0)
    att_col = lax.broadcasted_iota(jnp.int32, (SUB, n_tok), 1)
    own_col = [jnp.where((att_col >= i * SUB) & (att_col - i * SUB <= att_row), att_col - i * SUB, -1)
               for i in range(n_sub)]

    def cum_log2_decay(n):
        logd = -_softplus(-(_dotp(gl_ref[n], w2_ref[...]) + gb_ref[...])) * (LOG2E / GLA_GATE_NORMALIZER)
        return _cumsum_rows(logd)

    units = [(n, h) for n in range(n_seq) for h in range(GLA_HEADS)]
    us = range(len(units))
    ks = lambda h: slice(h * GLA_DK, (h + 1) * GLA_DK)
    vs = lambda h: slice(h * GLA_DV, (h + 1) * GLA_DV)
    b_seq = [cum_log2_decay(n) for n in range(n_seq)]
    q = [qk_ref[n, :, ks(h)] * (GLA_DK ** -0.5) for n, h in units]
    k = [qk_ref[n, :, GLA_KEY_WIDTH + h * GLA_DK:GLA_KEY_WIDTH + (h + 1) * GLA_DK] for n, h in units]
    b = [b_seq[n][:, ks(h)] for n, h in units]
    v = [v_ref[n, :, vs(h)] for n, h in units]
    st = [state[n, h] for n, h in units]
    o_inter = [_dotp(q[u] * jnp.exp2(b[u]), st[u], NT) for u in us]
    blocks = [[] for _ in us]
    for i in range(n_sub):
        lo = i * SUB
        rows = slice(lo, lo + SUB)
        if i > 0:
            att = [_dotp(q[u][rows] * jnp.exp2(b[u][rows] - b[u][lo - 1:lo]),
                         jnp.where(row_k < lo, k[u] * jnp.exp2(b[u][lo - 1:lo] - b[u]), 0.0), NT) for u in us]
        else:
            att = [jnp.zeros((SUB, n_tok), F32) for _ in us]
        for j in range(SUB):
            tok = slice(lo + j, lo + j + 1)
            col = [jnp.sum(q[u][rows] * (k[u][tok] * jnp.exp2(b[u][rows] - b[u][tok])), axis=1, keepdims=True)
                   for u in us]
            att = [jnp.where(own_col[i] == j, col[u], att[u]) for u in us]
        for u in us:
            blocks[u].append(o_inter[u][rows] + _dotp(att[u], v[u], NN))
    for u, (n, h) in enumerate(units):
        o = jnp.concatenate(blocks[u], axis=0)
        og = og_ref[n, :, vs(h)]
        o_ref[n, :, vs(h)] = (_rmsnorm(o, nw_ref[...], GLA_NORM_EPS)
                              * (og * jax.nn.sigmoid(og))).astype(o_ref.dtype)
        b_last = b[u][n_tok - 1:n_tok, :]
        state[n, h] = st[u] * jnp.exp2(b_last) + _dotp(v[u].T, k[u] * jnp.exp2(b_last - b[u]), NN)

    @pl.when(c_idx == pl.num_programs(1) - 1)
    def _():
        s_ref[...] = state[...]


def gla_prompt(rows, proj, gate_w2, gate_b, norm_w):
    nc = rows.t // CHUNK
    n_seq = _largest_tile(rows.bp, GLA_SEQS, 1)
    proj3 = proj.reshape(rows.bp, rows.t, proj.shape[1])
    tok = lambda width, col: pl.BlockSpec((n_seq, CHUNK, width), lambda b, c: (b, c, col))
    const = lambda shape: pl.BlockSpec(shape, lambda b, c: (0, 0))
    out, final_state = pl.pallas_call(
        _gla_chunk_kernel,
        grid=(rows.bp // n_seq, nc),
        in_specs=[tok(GLA_WIDTH, 0), tok(GLA_WIDTH, 1), tok(GLA_WIDTH, 2), tok(LANES, DST["gl"][0] // LANES),
                  const((LANES, GLA_KEY_WIDTH)), const((1, GLA_KEY_WIDTH)), const((1, GLA_DV))],
        out_specs=[tok(GLA_WIDTH, 0),
                   pl.BlockSpec((n_seq, GLA_HEADS, GLA_DV, GLA_DK), lambda b, c: (b, 0, 0, 0))],
        out_shape=[jax.ShapeDtypeStruct((rows.bp, rows.t, GLA_WIDTH), BF16),
                   jax.ShapeDtypeStruct((rows.bp, GLA_HEADS, GLA_DV, GLA_DK), F32)],
        scratch_shapes=[pltpu.VMEM((n_seq, GLA_HEADS, GLA_DV, GLA_DK), F32)],
        compiler_params=_params("parallel", "arbitrary"),
    )(proj3, proj3, proj3, proj3, gate_w2, gate_b, norm_w)
    return out.reshape(rows.rows, GLA_WIDTH), final_state


def _outproj_kernel(og_ref, orw_ref, x_ref, gt_ref, sc_ref, sh_ref, nw_ref, wo_ref, wr_ref,
                    x1s_ref, h2s_ref, lgs_ref, x1_ref, h2_ref, lg_ref, *, n_prompt_tiles):
    i = pl.program_id(0)

    @pl.when(i < n_prompt_tiles)
    def _():
        half = og_ref.shape[1]
        mix = _dot(og_ref[...], wo_ref[:half, :]) + _dot(orw_ref[...], wo_ref[half:, :])
        x1 = x_ref[...] + gt_ref[0] * mix
        h2 = _rmsnorm(x1, nw_ref[...], NORM_EPS) * (1.0 + sc_ref[0]) + sh_ref[0]
        x1_ref[...] = x1
        h2_ref[...] = h2
        lg_ref[...] = _dotp(h2, wr_ref[...], NN, 3)

    @pl.when(i >= n_prompt_tiles)
    def _():
        x1_ref[...] = x1s_ref[...]
        h2_ref[...] = h2s_ref[...]
        lg_ref[...] = lgs_ref[...]


def out_projection(t, o_gla, o_rwkv, xp, mod_p, norm_w, w_out, w_router, x1_s, h2_s, logits_s):
    n_p, d = xp.shape
    pad_s = x1_s.shape[0]
    tm = ROW_TILE
    npt, per = n_p // tm, t // tm
    prompt = lambda width: pl.BlockSpec((tm, width), lambda i: (jnp.minimum(i, npt - 1), 0))
    pmod = lambda col: pl.BlockSpec((1, 1, d), lambda i: (jnp.minimum(i // per, mod_p.shape[0] - 1), 0, col))
    sample = lambda width: pl.BlockSpec((tm, width), lambda i: (jnp.maximum(i - npt, 0), 0))
    rows = lambda width: pl.BlockSpec((tm, width), lambda i: (i, 0))
    const = lambda a: pl.BlockSpec(a.shape, lambda i: (0, 0), pipeline_mode=pl.Buffered(1))
    n_rows = n_p + pad_s
    return pl.pallas_call(
        functools.partial(_outproj_kernel, n_prompt_tiles=npt),
        grid=(n_rows // tm,),
        in_specs=[prompt(o_gla.shape[1]), prompt(o_rwkv.shape[1]), prompt(d), pmod(2), pmod(4), pmod(3),
                  const(norm_w), const(w_out), const(w_router), sample(d), sample(d), sample(LANES)],
        out_specs=[rows(d), rows(d), rows(LANES)],
        out_shape=[jax.ShapeDtypeStruct((n_rows, d), F32), jax.ShapeDtypeStruct((n_rows, d), F32),
                   jax.ShapeDtypeStruct((n_rows, LANES), F32)],
        compiler_params=_params("arbitrary"),
    )(o_gla, o_rwkv, xp, mod_p, mod_p, mod_p, norm_w, w_out, w_router, x1_s, h2_s, logits_s)


def _moe_kernel(tile_expert_ref, n_valid_ref, n_real_ref, src_ref, dst_ref, h2_hbm, wg_ref, wu_ref, wd_ref,
                y_hbm, x_even, x_odd, o_even, o_odd, gather_sem, scatter_sem):
    del tile_expert_ref
    i = pl.program_id(0)
    n_tiles = pl.num_programs(0)
    n_valid = n_valid_ref[0]
    xbufs, obufs = (x_even, x_odd), (o_even, o_odd)

    def for_rows(n, fn):
        def group(g, carry):
            for u in range(DMA_UNROLL):
                fn(g * DMA_UNROLL + u)
            return carry

        def single(r, carry):
            fn(r)
            return carry
        full = n // DMA_UNROLL
        lax.fori_loop(0, full, group, 0)
        lax.fori_loop(full * DMA_UNROLL, n, single, 0)

    def gather_row(tile, p, r):
        tok = src_ref[tile * MOE_TILE + r]
        return pltpu.make_async_copy(h2_hbm.at[pl.ds(tok, 1), :], xbufs[p].at[pl.ds(r, 1), :], gather_sem.at[p])

    def wait_gather(p):
        pltpu.make_async_copy(h2_hbm.at[pl.ds(0, MOE_TILE), :], xbufs[p], gather_sem.at[p]).wait()

    def scatter_row(tile, p, r):
        row = dst_ref[tile * MOE_TILE + r]
        return pltpu.make_async_copy(obufs[p].at[pl.ds(r, 1), :], y_hbm.at[pl.ds(row, 1), :], scatter_sem.at[p])

    def start_scatter(tile, p):
        for_rows(n_real_ref[tile], lambda r: scatter_row(tile, p, r).start())

    def wait_scatter(tile, p):
        n_real = n_real_ref[tile]

        @pl.when(n_real == MOE_TILE)
        def _():
            pltpu.make_async_copy(obufs[p], y_hbm.at[pl.ds(0, MOE_TILE), :], scatter_sem.at[p]).wait()

        @pl.when(n_real < MOE_TILE)
        def _():
            for_rows(n_real, lambda r: scatter_row(tile, p, r).wait())

    @pl.when(i == 0)
    def _():
        for_rows(MOE_TILE, lambda r: gather_row(0, 0, r).start())

    def step(p):
        wait_gather(p)

        @pl.when(i >= 2)
        def _():
            wait_scatter(i - 2, p)

        nxt = jnp.minimum(i + 1, n_valid - 1)
        for r in range(MOE_TILE):
            gather_row(nxt, 1 - p, r).start()
        x = xbufs[p][...].astype(BF16)
        gate = _dot(x, wg_ref[0].astype(BF16))
        up = _dot(x, wu_ref[0].astype(BF16))
        hid = gate * jax.nn.sigmoid(gate) * up
        obufs[p][...] = _dot(hid.astype(BF16), wd_ref[0].astype(BF16))
        start_scatter(i, p)

    for p in (0, 1):
        pl.when((i < n_valid) & (i % 2 == p))(functools.partial(step, p))

    @pl.when(i == n_tiles - 1)
    def _():
        for p in (0, 1):
            pl.when(n_valid % 2 == p)(functools.partial(wait_gather, p))
            pl.when((n_valid >= 2) & (n_valid % 2 == p))(functools.partial(wait_scatter, n_valid - 2, p))
            pl.when((n_valid - 1) % 2 == p)(functools.partial(wait_scatter, n_valid - 1, p))


def moe_experts(h2, plan, w_gate, w_up, w_down):
    n_rows, d = h2.shape
    _, _, ff = w_gate.shape
    n_tiles = plan["tile_expert"].shape[0]
    grid_spec = pltpu.PrefetchScalarGridSpec(
        num_scalar_prefetch=5,
        grid=(n_tiles,),
        in_specs=[pl.BlockSpec(memory_space=pl.ANY),
                  pl.BlockSpec((1, d, ff), lambda i, te, *_: (te[i], 0, 0)),
                  pl.BlockSpec((1, d, ff), lambda i, te, *_: (te[i], 0, 0)),
                  pl.BlockSpec((1, ff, d), lambda i, te, *_: (te[i], 0, 0))],
        out_specs=pl.BlockSpec(memory_space=pl.ANY),
        scratch_shapes=[pltpu.VMEM((MOE_TILE, d), F32)] * 4
        + [pltpu.SemaphoreType.DMA((2,)), pltpu.SemaphoreType.DMA((2,))],
    )
    return pl.pallas_call(
        _moe_kernel,
        grid_spec=grid_spec,
        out_shape=jax.ShapeDtypeStruct((2 * n_rows, d), F32),
        compiler_params=_params("arbitrary"),
    )(plan["tile_expert"], plan["n_valid"], plan["n_real"], plan["src_row"], plan["dst_row"], h2,
      w_gate, w_up, w_down)


def _route(logits):
    p_group = jax.nn.softmax(logits[:, :N_GROUPS], axis=-1)
    g_sel = jnp.argmax(p_group, axis=-1)
    p_sel = jnp.take_along_axis(p_group, g_sel[:, None], axis=-1)
    logits_e = logits[:, N_GROUPS:N_GROUPS + N_EXPERTS].reshape(-1, N_GROUPS, EXPERTS_PER_GROUP)
    logits_e = jnp.take_along_axis(logits_e, g_sel[:, None, None], axis=1)[:, 0]
    i1 = jnp.argmax(logits_e, axis=-1)
    v1 = jnp.take_along_axis(logits_e, i1[:, None], axis=-1)
    masked = jnp.where(jnp.arange(EXPERTS_PER_GROUP)[None, :] == i1[:, None], -jnp.inf, logits_e)
    i2 = jnp.argmax(masked, axis=-1)
    v2 = jnp.take_along_axis(logits_e, i2[:, None], axis=-1)
    w_top = jax.nn.softmax(jnp.concatenate([v1, v2], axis=-1), axis=-1) * p_sel
    top_i = jnp.stack([i1, i2], axis=-1)
    expert = g_sel[:, None].astype(jnp.int32) * EXPERTS_PER_GROUP + top_i.astype(jnp.int32)
    return expert, w_top


def _moe_plan(expert):
    n = expert.shape[0]
    e_flat = expert.reshape(-1)
    ids = jnp.arange(N_EXPERTS, dtype=jnp.int32)
    one_hot = (e_flat[:, None] == ids[None, :]).astype(jnp.int32)
    counts = jnp.sum(one_hot, axis=0)
    padded = (counts + MOE_TILE - 1) // MOE_TILE * MOE_TILE
    ends = jnp.cumsum(padded)
    starts = ends - padded
    rank = jnp.sum(jnp.cumsum(one_hot, axis=0) * one_hot, axis=1) - 1
    pos = jnp.sum(one_hot * starts[None, :], axis=1) + rank
    p_max = _round_up(2 * n, MOE_TILE) + N_EXPERTS * MOE_TILE
    n_tiles = p_max // MOE_TILE
    pair = jnp.full((p_max,), -1, jnp.int32).at[pos].set(jnp.arange(2 * n, dtype=jnp.int32))
    pair0 = jnp.maximum(pair, 0)
    src_row = pair0 // 2
    dst_row = (pair0 % 2) * n + src_row
    tile_start = jnp.arange(n_tiles, dtype=jnp.int32) * MOE_TILE
    n_valid = ends[-1] // MOE_TILE
    tile_expert = jnp.sum((tile_start[:, None] >= ends[None, :]).astype(jnp.int32), axis=1)
    tile_expert = jnp.where(tile_start < ends[-1], tile_expert, tile_expert[n_valid - 1])
    n_real = jnp.clip((starts + counts)[tile_expert] - tile_start, 0, MOE_TILE)
    n_real = jnp.where(tile_start < ends[-1], n_real, 0).astype(jnp.int32)
    return dict(src_row=src_row, dst_row=dst_row, tile_expert=tile_expert,
                n_valid=n_valid.astype(jnp.int32)[None], n_real=n_real)


def _final_kernel(x1_ref, y0_ref, y1_ref, wt_ref, gtp_ref, gts_ref, scp_ref, scs_ref, shp_ref, shs_ref, nw_ref,
                  yp_ref, ys_ref, *, n_prompt_tiles):
    i = pl.program_id(0)

    def finish(gt, sc, sh):
        moe = wt_ref[:, 0:1] * y0_ref[...] + wt_ref[:, 1:2] * y1_ref[...]
        x2 = x1_ref[...] + gt * moe
        return _rmsnorm(x2, nw_ref[...], NORM_EPS) * (1.0 + sc) + sh

    @pl.when(i < n_prompt_tiles)
    def _():
        yp_ref[...] = finish(gtp_ref[0], scp_ref[0], shp_ref[0])

    @pl.when(i >= n_prompt_tiles)
    def _():
        ys_ref[...] = finish(gts_ref[...], scs_ref[...], shs_ref[...])


def final_norm(n_p, t, x1, y_pairs, w_top, mod_p, mod_s, modf_p, modf_s, norm_w):
    n_rows, d = x1.shape
    tile = ROW_TILE
    n_tiles, npt = n_rows // tile, n_p // tile
    per = t // tile
    rows = lambda off: pl.BlockSpec((tile, d), lambda i: (i + off, 0))
    prompt_out = pl.BlockSpec((tile, d), lambda i: (jnp.minimum(i, npt - 1), 0))
    sample = lambda col: pl.BlockSpec((tile, d), lambda i: (jnp.maximum(i - npt, 0), col))
    pmod = lambda col: pl.BlockSpec((1, 1, d), lambda i: (jnp.minimum(i // per, mod_p.shape[0] - 1), 0, col))
    return pl.pallas_call(
        functools.partial(_final_kernel, n_prompt_tiles=npt),
        grid=(n_tiles,),
        in_specs=[rows(0), rows(0), rows(n_tiles), pl.BlockSpec((tile, w_top.shape[1]), lambda i: (i, 0)),
                  pmod(5), sample(5), pmod(1), sample(1), pmod(0), sample(0),
                  pl.BlockSpec((1, d), lambda i: (0, 0))],
        out_specs=[prompt_out, sample(0)],
        out_shape=[jax.ShapeDtypeStruct((n_p, d), F32), jax.ShapeDtypeStruct((n_rows - n_p, d), F32)],
        compiler_params=_params("arbitrary"),
    )(x1, y_pairs, y_pairs, w_top, mod_p, mod_s, modf_p, modf_s, modf_p, modf_s, norm_w)


def _align_rows(wt):
    parts = []
    for name in _ALIGNED:
        off, w = SRC[name]
        part = wt[off:off + w]
        wa = DST[name][1]
        if wa != w:
            part = jnp.pad(part, ((0, wa - w), (0, 0)))
        parts.append(part)
    return jnp.concatenate(parts, axis=0)


def _rwkv_seg(a, name):
    off, w = SRC[name]
    return a[..., off - RWKV_SRC_BASE:off - RWKV_SRC_BASE + w]


def _rwkv_seg_padded(a, name):
    seg = _rwkv_seg(a, name)
    wa = DST[name][1]
    return jnp.pad(seg, [(0, 0)] * (a.ndim - 1) + [(0, wa - seg.shape[-1])])


def _orig_seg(p, name):
    off, w = SRC[name]
    return p[..., off:off + w]


def _pad_rows(w, rows):
    return jnp.pad(w, ((0, rows - w.shape[0]), (0, 0)))


def kernel(x_prompt, x_sample, state_gla, state_rwkv, state_shift, c_prompt, c_sample, w_ada, b_ada, norm_mix, norm_ffn, w_in, gla_gate_w2, gla_gate_b, gla_norm, rwkv_mu, rwkv_w0, rwkv_w2, rwkv_a0, rwkv_a2, rwkv_g2, rwkv_k_k, rwkv_k_a, rwkv_r_k, rwkv_gn_w, rwkv_gn_b, w_out, w_router_group, w_router_expert, w_exp_gate, w_exp_up, w_exp_down, norm_final, w_ada_final, b_ada_final):
    assert w_ada.shape[0] == 1, "single-layer step"
    bp, t, d = x_prompt.shape
    bs = x_sample.shape[0]
    assert x_sample.shape[1] == 1 and t % ROW_TILE == 0
    n_p = bp * t
    pad_s = _round_up(bs, ROW_TILE)
    n_rows = n_p + pad_s
    prompt = _PromptShape(bp, t, n_p)
    hi = lax.Precision.HIGHEST
    pad_sample = lambda a: jnp.pad(a, ((0, pad_s - a.shape[0]), (0, 0)))
    hdot = lambda a, w: jnp.dot(a, w, precision=hi)

    c_act = jax.nn.silu(jnp.concatenate([c_prompt, c_sample], axis=0))
    mod = matmul3(c_act, w_ada[0]) + b_ada[0]
    mod_f = matmul3(c_act, w_ada_final) + b_ada_final
    mod_p, mod_s = mod[:bp, None, :], mod[bp:]
    modf_p, modf_s = mod_f[:bp, None, :], mod_f[bp:]
    sh1_s, sc1_s, gt1_s, sh2_s, sc2_s, _ = jnp.split(mod_s, 6, axis=-1)
    xp = x_prompt.reshape(n_p, d)
    xs = x_sample[:, 0, :]

    w_in_t = jnp.swapaxes(w_in[0], 0, 1)
    proj = in_projection(xp, t, mod_p, norm_mix, _align_rows(w_in_t.astype(BF16)))
    o_gla, gla_t_p = gla_prompt(prompt, proj, _pad_rows(gla_gate_w2[0], LANES), gla_gate_b, gla_norm)
    new_gla_p = jnp.swapaxes(gla_t_p, -1, -2)
    mu = rwkv_mu[0]
    mu_big = jnp.concatenate([_rwkv_seg(mu, n) for n in ("r", "k7", "v7")])[None, :]
    mu_small = jnp.concatenate([_rwkv_seg_padded(mu, n) for n in ("wl", "al", "gl7")])[None, :]
    vecs = jnp.concatenate([rwkv_w0, rwkv_a0, rwkv_k_k, rwkv_k_a, rwkv_r_k[0].reshape(1, RWKV_WIDTH),
                            rwkv_gn_w, rwkv_gn_b, jnp.zeros((1, RWKV_WIDTH), F32)], axis=0)
    o_rwkv, rwkv_t_p = rwkv7_prompt(prompt, proj, mu_big, mu_small, vecs, _pad_rows(rwkv_w2[0], LANES),
                                    _pad_rows(rwkv_a2[0], LANES), rwkv_g2[0])
    new_rwkv_p = rwkv_t_p.reshape(bp, RWKV_HEAD, RWKV_HEADS, RWKV_HEAD).transpose(0, 2, 1, 3)
    last = jnp.stack([proj[(b + 1) * t - 1] for b in range(bp)])
    new_shift_p = jnp.concatenate([last[:, DST[n][0]:DST[n][0] + SRC[n][1]] for n in _RWKV_ORIG], axis=-1)

    h1_s = _rmsnorm(xs, norm_mix[0], NORM_EPS) * (1.0 + sc1_s) + sh1_s
    proj_s = matmul3(h1_s, w_in_t, w_transposed=True)
    heads = lambda a, n: a.reshape(a.shape[0], n, -1)
    logd_s = jax.nn.log_sigmoid(hdot(_orig_seg(proj_s, "gl"), gla_gate_w2[0]) + gla_gate_b[0]) \
        / GLA_GATE_NORMALIZER
    q_s = heads(_orig_seg(proj_s, "q") * (GLA_DK ** -0.5), GLA_HEADS)
    k_s, v_s, g_s = heads(_orig_seg(proj_s, "k"), GLA_HEADS), heads(_orig_seg(proj_s, "v"), GLA_HEADS), \
        heads(logd_s, GLA_HEADS)
    new_gla_s = jnp.exp(g_s)[..., None] * state_gla[0] + k_s[..., :, None] * v_s[..., None, :]
    o_s = jnp.einsum("bhd,bhde->bhe", q_s, new_gla_s, precision=hi)
    o_s = _rmsnorm(o_s, gla_norm[0], GLA_NORM_EPS) * heads(jax.nn.silu(_orig_seg(proj_s, "og")), GLA_HEADS)

    rp_s = proj_s[:, RWKV_SRC_BASE:]
    new_shift_s = rp_s
    xs7 = rp_s + (state_shift[0] - rp_s) * mu
    sx = lambda name: _rwkv_seg(xs7, name)
    r_s, k7_s, v7_s = sx("r"), sx("k7"), sx("v7")
    w_pre = rwkv_w0[0] + hdot(jnp.tanh(sx("wl")), rwkv_w2[0])
    decay_s = jnp.exp(-jnp.exp(-jax.nn.softplus(-w_pre) - 0.5))
    a_s = jax.nn.sigmoid(rwkv_a0[0] + hdot(sx("al"), rwkv_a2[0]))
    g_s7 = hdot(jax.nn.sigmoid(sx("gl7")), rwkv_g2[0])
    hs = lambda z: z.reshape(bs, RWKV_HEADS, RWKV_HEAD)
    kk_s = hs(k7_s * rwkv_k_k[0])
    kk_s = kk_s / jnp.maximum(jnp.sqrt(jnp.sum(kk_s * kk_s, axis=-1, keepdims=True)), 1e-12)
    k7_s = k7_s * (1.0 + (a_s - 1.0) * rwkv_k_a[0])
    r_h, w_h, k_h, v_h, a_h = hs(r_s), hs(decay_s), hs(k7_s), hs(v7_s), hs(a_s)
    s_prev = state_rwkv[0]
    sa = jnp.einsum("bhij,bhj->bhi", s_prev, -kk_s, precision=hi)
    new_rwkv_s = s_prev * w_h[:, :, None, :] + sa[..., None] * (kk_s * a_h)[:, :, None, :] \
        + v_h[..., None] * k_h[:, :, None, :]
    y_s = jnp.einsum("bhij,bhj->bhi", new_rwkv_s, r_h, precision=hi)
    y_mu = jnp.mean(y_s, axis=-1, keepdims=True)
    y_var = jnp.mean(jnp.square(y_s - y_mu), axis=-1, keepdims=True)
    y_s = (y_s - y_mu) * lax.rsqrt(y_var + RWKV_GN_EPS) * rwkv_gn_w[0].reshape(RWKV_HEADS, RWKV_HEAD) \
        + rwkv_gn_b[0].reshape(RWKV_HEADS, RWKV_HEAD)
    bonus = jnp.sum(r_h * k_h * rwkv_r_k[0], axis=-1, keepdims=True) * v_h
    o_rs = (y_s + bonus).reshape(bs, RWKV_WIDTH) * g_s7
    mix_s = matmul3(jnp.concatenate([o_s.reshape(bs, GLA_WIDTH), o_rs], axis=-1), w_out[0])
    x1_s = xs + gt1_s * mix_s
    h2_s = _rmsnorm(x1_s, norm_ffn[0], NORM_EPS) * (1.0 + sc2_s) + sh2_s
    w_router = jnp.pad(jnp.concatenate([w_router_group[0], w_router_expert[0]], axis=-1),
                       ((0, 0), (0, LANES - N_GROUPS - N_EXPERTS)))
    logits_s = hdot(h2_s, w_router)

    x1, h2, logits = out_projection(t, o_gla, o_rwkv, xp, mod_p, norm_ffn, w_out[0].astype(BF16), w_router,
                                    pad_sample(x1_s), pad_sample(h2_s), pad_sample(logits_s))

    expert, w_top = _route(logits)
    plan = _moe_plan(expert)
    y_pairs = moe_experts(h2, plan, w_exp_gate[0].reshape(N_EXPERTS, d, EXPERT_FF),
                          w_exp_up[0].reshape(N_EXPERTS, d, EXPERT_FF),
                          w_exp_down[0].reshape(N_EXPERTS, EXPERT_FF, d))

    y_p, y_s_pad = final_norm(n_p, t, x1, y_pairs, w_top, mod_p, pad_sample(mod_s), modf_p,
                              pad_sample(modf_s), norm_final[None, :])
    return (y_p.reshape(bp, t, d), y_s_pad[:bs, None, :], new_gla_p[None], new_rwkv_p[None], new_shift_p[None],
            new_gla_s[None], new_rwkv_s[None], new_shift_s[None])
```

```python
import collections
import functools

import jax
import jax.numpy as jnp
from jax import lax
from jax.experimental import pallas as pl
from jax.experimental.pallas import tpu as pltpu

F32 = jnp.float32
BF16 = jnp.bfloat16

D_MODEL = 2048
GLA_HEADS = 4
GLA_DK = 128
GLA_DV = 256
GLA_KEY_WIDTH = GLA_HEADS * GLA_DK
GLA_WIDTH = GLA_HEADS * GLA_DV
GLA_GATE_NORMALIZER = 16.0
RWKV_HEAD = 64
RWKV_HEADS = 16
RWKV_WIDTH = RWKV_HEAD * RWKV_HEADS
N_GROUPS = 4
EXPERTS_PER_GROUP = 8
N_EXPERTS = N_GROUPS * EXPERTS_PER_GROUP
EXPERT_FF = 512
NORM_EPS = 1e-6
GLA_NORM_EPS = 1e-5
RWKV_GN_EPS = 64e-5

LANES = 128
SUBLANES = 8
VMEM_LIMIT_BYTES = 56 * 1024 * 1024

CHUNK = 64
SUB = 16
RWKV_GROUP = 2
GROUP_W = RWKV_GROUP * RWKV_HEAD
GLA_SEQS = 2
LOG2E = 1.4426950408889634
RWKV_SEQS = 4
MOE_TILE = 256
DMA_UNROLL = 8
ROW_DMA_PRIORITY = 1
ROW_TILE = 256
NORM_SLAB = 128
K_SPLIT = 256

_ORIG = (("q", 512), ("k", 512), ("v", 1024), ("gl", 16), ("og", 1024),
         ("r", 1024), ("wl", 64), ("k7", 1024), ("v7", 1024), ("al", 64), ("gl7", 128))
_ALIGNED = ("q", "k", "v", "og", "r", "k7", "v7", "gl", "wl", "al", "gl7")
_RWKV_ORIG = ("r", "wl", "k7", "v7", "al", "gl7")


def _round_up(n, m):
    return (n + m - 1) // m * m


def _layouts():
    src, off = {}, 0
    for name, w in _ORIG:
        src[name] = (off, w)
        off += w
    dst, pos = {}, 0
    for name in _ALIGNED:
        wa = _round_up(src[name][1], LANES)
        dst[name] = (pos, wa)
        pos += wa
    return src, off, dst, pos


SRC, IN_COLS, DST, IN_COLS_ALIGNED = _layouts()
RWKV_SRC_BASE = SRC["r"][0]
BIG_W = 3 * RWKV_WIDTH
SMALL_W = 3 * LANES
assert DST["r"][0] == BIG_W and DST["k7"][0] == BIG_W + RWKV_WIDTH and DST["v7"][0] == BIG_W + 2 * RWKV_WIDTH
assert DST["v"][0] == RWKV_WIDTH and DST["og"][0] == 2 * RWKV_WIDTH and DST["k"][0] == GLA_KEY_WIDTH

NN = ((1,), (0,))
NT = ((1,), (1,))

_PromptShape = collections.namedtuple("_PromptShape", "bp t rows")


def _dot(a, b, dims=NN):
    return lax.dot_general(a, b, (dims, ((), ())), preferred_element_type=F32)


def _split2(x):
    hi = x.astype(BF16)
    lo = (x - hi.astype(F32)).astype(BF16)
    return hi, lo


def _dotp(a, b, dims=NN, passes=1):
    if passes == 1:
        return _dot(a.astype(BF16), b.astype(BF16), dims)
    ah, al = _split2(a)
    bh, bl = _split2(b)
    return _dot(ah, bh, dims) + (_dot(ah, bl, dims) + _dot(al, bh, dims))


def _cumsum_rows(x):
    n = x.shape[0]
    row = lax.broadcasted_iota(jnp.int32, (n, n), 0)
    col = lax.broadcasted_iota(jnp.int32, (n, n), 1)
    tri = (row >= col).astype(BF16)
    x1 = x.astype(BF16)
    r1 = x - x1.astype(F32)
    x2 = r1.astype(BF16)
    x3 = (r1 - x2.astype(F32)).astype(BF16)
    return _dot(tri, x1) + (_dot(tri, x2) + _dot(tri, x3))


def _softplus(z):
    return jnp.maximum(z, 0.0) + jnp.log1p(jnp.exp(-jnp.abs(z)))


def _rmsnorm(x, g, eps):
    return x * lax.rsqrt(jnp.mean(x * x, axis=-1, keepdims=True) + eps) * g


def _largest_tile(n, cap, mult=SUBLANES):
    if n <= cap:
        return n
    best = None
    for t in range(mult, cap + 1, mult):
        if n % t == 0:
            best = t
    assert best is not None, (n, cap)
    return best


def _params(*sem):
    return pltpu.CompilerParams(dimension_semantics=sem, vmem_limit_bytes=VMEM_LIMIT_BYTES)


def _mm3_kernel(a_ref, w_ref, o_ref, *, dims):
    @pl.when(pl.program_id(0) == 0)
    def _():
        o_ref[...] = jnp.zeros_like(o_ref)

    o_ref[...] += _dotp(a_ref[...], w_ref[...], dims, 3)


def matmul3(a, w, w_transposed=False):
    m, k = a.shape
    n = w.shape[0] if w_transposed else w.shape[1]
    tk = _largest_tile(k, K_SPLIT, LANES)
    w_spec = pl.BlockSpec((n, tk), lambda s: (0, s)) if w_transposed else pl.BlockSpec((tk, n), lambda s: (s, 0))
    return pl.pallas_call(
        functools.partial(_mm3_kernel, dims=NT if w_transposed else NN),
        grid=(k // tk,),
        in_specs=[pl.BlockSpec((m, tk), lambda s: (0, s)), w_spec],
        out_specs=pl.BlockSpec((m, n), lambda s: (0, 0)),
        out_shape=jax.ShapeDtypeStruct((m, n), F32),
        compiler_params=_params("arbitrary"),
    )(a, w)


def _prompt_mod(tiles_per_batch, width, col):
    return pl.BlockSpec((1, 1, width), lambda i, *_: (i // tiles_per_batch, 0, col))


def _inproj_kernel(x_ref, sc_ref, sh_ref, nw_ref, w_ref, o_ref, h_scr):
    @pl.when(pl.program_id(1) == 0)
    def _():
        for r in range(0, h_scr.shape[0], NORM_SLAB):
            sl = slice(r, r + NORM_SLAB)
            h = _rmsnorm(x_ref[sl, :], nw_ref[...], NORM_EPS) * (1.0 + sc_ref[0]) + sh_ref[0]
            h_scr[sl, :] = h.astype(BF16)

    o_ref[...] = _dot(h_scr[...], w_ref[...], NT)


def in_projection(xp, t, mod_p, norm_w, w_aligned):
    n_p, d = xp.shape
    n = w_aligned.shape[0]
    tm = _largest_tile(t, 1024, NORM_SLAB)
    tn = _largest_tile(n, 512, LANES)
    per = t // tm
    return pl.pallas_call(
        _inproj_kernel,
        grid=(n_p // tm, n // tn),
        in_specs=[pl.BlockSpec((tm, d), lambda i, j: (i, 0)), _prompt_mod(per, d, 1), _prompt_mod(per, d, 0),
                  pl.BlockSpec((1, d), lambda i, j: (0, 0)), pl.BlockSpec((tn, d), lambda i, j: (j, 0))],
        out_specs=pl.BlockSpec((tm, tn), lambda i, j: (i, j)),
        out_shape=jax.ShapeDtypeStruct((n_p, n), F32),
        scratch_shapes=[pltpu.VMEM((tm, d), BF16)],
        compiler_params=_params("parallel", "arbitrary"),
    )(xp, mod_p, mod_p, norm_w, w_aligned)


def _rwkv_chunk_kernel(big_ref, wl_ref, al_ref, gl_ref, mu_big_ref, mu_small_ref, vec_ref, w2_ref, a2_ref,
                       g2_ref, o_ref, s_ref, state, carry_big, carry_small, *, passes):
    c_idx = pl.program_id(1)
    n_seq, n_tok = big_ref.shape[0], big_ref.shape[1]
    n_grp = RWKV_WIDTH // GROUP_W
    grp = [slice(g * GROUP_W, (g + 1) * GROUP_W) for g in range(n_grp)]

    @pl.when(c_idx == 0)
    def _():
        state[...] = jnp.zeros_like(state)
        carry_big[...] = jnp.zeros_like(carry_big)
        carry_small[...] = jnp.zeros_like(carry_small)

    first_row = lax.broadcasted_iota(jnp.int32, (n_tok, 1), 0) == 0
    w0, a0, k_k, k_a, r_k, gn_w, gn_b = (vec_ref[i:i + 1, :] for i in range(7))

    gi = lax.broadcasted_iota(jnp.int32, (GROUP_W, GROUP_W), 0) // RWKV_HEAD
    gj = lax.broadcasted_iota(jnp.int32, (GROUP_W, GROUP_W), 1) // RWKV_HEAD
    head_ones = (gi == gj).astype(BF16)

    def head_sums(parts):
        hi, lo = _split2(jnp.concatenate(parts, axis=0))
        out = _dot(hi, head_ones) + _dot(lo, head_ones)
        return [out[i * n_tok:(i + 1) * n_tok] for i in range(len(parts))]

    def prepare(n):
        def token_shift(cur, carry, mu):
            prev = jnp.where(first_row, carry[n, 0:1, :], pltpu.roll(cur, 1, 0))
            carry[n, 0:1, :] = cur[n_tok - 1:n_tok, :]
            return cur + (prev - cur) * mu

        xs_big = token_shift(big_ref[n], carry_big, mu_big_ref[...])
        small = jnp.concatenate([wl_ref[n], al_ref[n], gl_ref[n]], axis=1)
        xs_small = token_shift(small, carry_small, mu_small_ref[...])
        r = xs_big[:, :RWKV_WIDTH]
        k7 = xs_big[:, RWKV_WIDTH:2 * RWKV_WIDTH]
        v = xs_big[:, 2 * RWKV_WIDTH:]
        w_pre = w0 + _dotp(jnp.tanh(xs_small[:, :LANES]), w2_ref[...])
        lw = -jnp.exp(-_softplus(-w_pre) - 0.5)
        a = jax.nn.sigmoid(a0 + _dotp(xs_small[:, LANES:2 * LANES], a2_ref[...]))
        gate = _dotp(jax.nn.sigmoid(xs_small[:, 2 * LANES:]), g2_ref[...])
        kk_raw = k7 * k_k
        k = k7 * (1.0 + (a - 1.0) * k_a)
        sums = [head_sums([kk_raw[:, sl] * kk_raw[:, sl], r[:, sl] * k[:, sl] * r_k[:, sl]]) for sl in grp]
        kk = jnp.concatenate([kk_raw[:, sl] / jnp.maximum(jnp.sqrt(sums[g][0]), 1e-12)
                              for g, sl in enumerate(grp)], axis=1)
        bonus = jnp.concatenate([sums[g][1] for g in range(n_grp)], axis=1) * v
        cum = _cumsum_rows(lw)
        cum_end = cum[n_tok - 1:n_tok, :]
        beta = kk * a
        g_inv = jnp.exp(-cum)
        g_end = jnp.exp(cum_end - cum)
        return dict(v=v, gate=gate, bonus=bonus, g_tot=jnp.exp(cum_end),
                    a_hat=-kk * jnp.exp(cum - lw), r_hat=r * jnp.exp(cum), b_hat=beta * g_inv,
                    k_hat=k * g_inv, b_end=beta * g_end, k_end=k * g_end)

    seqs = [prepare(n) for n in range(n_seq)]
    units = [(n, sl) for n in range(n_seq) for sl in grp]
    part = lambda name: [seqs[n][name][:, sl] for n, sl in units]

    lane = lax.broadcasted_iota(jnp.int32, (n_tok, GROUP_W), 1)
    tok = lax.broadcasted_iota(jnp.int32, (n_tok, GROUP_W), 0)
    lane_head = lane // RWKV_HEAD
    src_tok = lane % RWKV_HEAD
    strict = tok > src_tok
    incl = tok >= src_tok

    def bd(y):
        return jnp.concatenate([jnp.where(lane_head == h, y, 0.0) for h in range(RWKV_GROUP)], axis=0)

    gs = range(len(units))
    s0 = [state[n, :, sl] for n, sl in units]
    v_u, b_hat, k_hat, b_end, k_end = part("v"), part("b_hat"), part("k_hat"), part("b_end"), part("k_end")
    lhs2 = [jnp.concatenate([a, r], axis=0) for a, r in zip(part("a_hat"), part("r_hat"))]
    abrb = [_dotp(lhs2[g], bd(b_hat[g]), NT, passes) for g in gs]
    akrk = [_dotp(lhs2[g], bd(k_hat[g]), NT, passes) for g in gs]
    asrs = [_dotp(lhs2[g], bd(s0[g]), NT, passes) for g in gs]
    p = [jnp.where(strict, abrb[g][:n_tok], 0.0) for g in gs]
    ak = [jnp.where(strict, akrk[g][:n_tok], 0.0) for g in gs]
    rb = [jnp.where(incl, abrb[g][n_tok:], 0.0) for g in gs]
    rk = [jnp.where(incl, akrk[g][n_tok:], 0.0) for g in gs]
    bd_v = [bd(v_u[g]) for g in gs]
    x = [asrs[g][:n_tok] + _dotp(ak[g], bd_v[g], NN, passes) for g in gs]
    n_sq = n_tok.bit_length() - 1
    for it in range(n_sq):
        if it < n_sq - 1:
            both = [_dotp(p[g], jnp.concatenate([bd(p[g]), bd(x[g])], axis=1), NN, passes) for g in gs]
            x = [x[g] + both[g][:, GROUP_W:] for g in gs]
            p = [both[g][:, :GROUP_W] for g in gs]
        else:
            x = [x[g] + _dotp(p[g], bd(x[g]), NN, passes) for g in gs]
    y = [asrs[g][n_tok:] + _dotp(jnp.concatenate([rb[g], rk[g]], axis=1),
                                 jnp.concatenate([bd(x[g]), bd_v[g]], axis=0), NN, passes) for g in gs]
    full = [_dotp(jnp.concatenate([x[g], v_u[g]], axis=0).T,
                  jnp.concatenate([b_end[g], k_end[g]], axis=0), NN, passes) for g in gs]
    g_tot = part("g_tot")
    for g, (n, sl) in enumerate(units):
        upd = s0[g] * g_tot[g]
        for h in range(RWKV_GROUP):
            upd = upd + jnp.where(lane_head == h, full[g][h * RWKV_HEAD:(h + 1) * RWKV_HEAD, :], 0.0)
        state[n, :, sl] = upd

    inv_n = 1.0 / RWKV_HEAD
    mean = [head_sums([y[g]])[0] * inv_n for g in gs]
    dev = [y[g] - mean[g] for g in gs]
    var = [head_sums([dev[g] * dev[g]])[0] * inv_n for g in gs]
    bonus, gate = part("bonus"), part("gate")
    for g, (n, sl) in enumerate(units):
        yn = dev[g] * lax.rsqrt(var[g] + RWKV_GN_EPS) * gn_w[:, sl] + gn_b[:, sl]
        o_ref[n, :, sl] = ((yn + bonus[g]) * gate[g]).astype(o_ref.dtype)

    @pl.when(c_idx == pl.num_programs(1) - 1)
    def _():
        s_ref[...] = state[...]


def rwkv7_prompt(rows, proj, mu_big, mu_small, vecs, w2, a2, g2, *, passes=1):
    assert CHUNK == RWKV_HEAD and rows.t % CHUNK == 0
    nc = rows.t // CHUNK
    n_seq = _largest_tile(rows.bp, RWKV_SEQS, 1)
    proj3 = proj.reshape(rows.bp, rows.t, proj.shape[1])
    small_col = lambda name: DST[name][0] // LANES
    tok = lambda width, col: pl.BlockSpec((n_seq, CHUNK, width), lambda b, c: (b, c, col))
    const = lambda shape: pl.BlockSpec(shape, lambda b, c: (0, 0))
    out, final_state = pl.pallas_call(
        functools.partial(_rwkv_chunk_kernel, passes=passes),
        grid=(rows.bp // n_seq, nc),
        in_specs=[tok(BIG_W, 1), tok(LANES, small_col("wl")), tok(LANES, small_col("al")),
                  tok(LANES, small_col("gl7")),
                  const((1, BIG_W)), const((1, SMALL_W)), const((SUBLANES, RWKV_WIDTH)),
                  const((LANES, RWKV_WIDTH)), const((LANES, RWKV_WIDTH)), const((LANES, RWKV_WIDTH))],
        out_specs=[tok(RWKV_WIDTH, 0),
                   pl.BlockSpec((n_seq, RWKV_HEAD, RWKV_WIDTH), lambda b, c: (b, 0, 0))],
        out_shape=[jax.ShapeDtypeStruct((rows.bp, rows.t, RWKV_WIDTH), BF16),
                   jax.ShapeDtypeStruct((rows.bp, RWKV_HEAD, RWKV_WIDTH), F32)],
        scratch_shapes=[pltpu.VMEM((n_seq, RWKV_HEAD, RWKV_WIDTH), F32),
                        pltpu.VMEM((n_seq, SUBLANES, BIG_W), F32), pltpu.VMEM((n_seq, SUBLANES, SMALL_W), F32)],
        compiler_params=_params("parallel", "arbitrary"),
    )(proj3, proj3, proj3, proj3, mu_big, mu_small, vecs, w2, a2, g2)
    return out.reshape(rows.rows, RWKV_WIDTH), final_state


def _gla_chunk_kernel(qk_ref, v_ref, og_ref, gl_ref, w2_ref, gb_ref, nw_ref, o_ref, s_ref, state):
    c_idx = pl.program_id(1)
    n_seq, n_tok = qk_ref.shape[0], qk_ref.shape[1]
    n_sub = n_tok // SUB

    @pl.when(c_idx == 0)
    def _():
        state[...] = jnp.zeros_like(state)

    row_k = lax.broadcasted_iota(jnp.int32, (n_tok, GLA_DK), 0)
    att_row = lax.broadcasted_iota(jnp.int32, (SUB, n_tok), 0)
    att_col = lax.broadcasted_iota(jnp.int32, (SUB, n_tok), 1)
    own_col = [jnp.where((att_col >= i * SUB) & (att_col - i * SUB <= att_row), att_col - i * SUB, -1)
               for i in range(n_sub)]

    def cum_log2_decay(n):
        logd = -_softplus(-(_dotp(gl_ref[n], w2_ref[...]) + gb_ref[...])) * (LOG2E / GLA_GATE_NORMALIZER)
        return _cumsum_rows(logd)

    units = [(n, h) for n in range(n_seq) for h in range(GLA_HEADS)]
    us = range(len(units))
    ks = lambda h: slice(h * GLA_DK, (h + 1) * GLA_DK)
    vs = lambda h: slice(h * GLA_DV, (h + 1) * GLA_DV)
    b_seq = [cum_log2_decay(n) for n in range(n_seq)]
    q = [qk_ref[n, :, ks(h)] * (GLA_DK ** -0.5) for n, h in units]
    k = [qk_ref[n, :, GLA_KEY_WIDTH + h * GLA_DK:GLA_KEY_WIDTH + (h + 1) * GLA_DK] for n, h in units]
    b = [b_seq[n][:, ks(h)] for n, h in units]
    v = [v_ref[n, :, vs(h)] for n, h in units]
    st = [state[n, h] for n, h in units]
    o_inter = [_dotp(q[u] * jnp.exp2(b[u]), st[u], NT) for u in us]
    blocks = [[] for _ in us]
    for i in range(n_sub):
        lo = i * SUB
        rows = slice(lo, lo + SUB)
        if i > 0:
            att = [_dotp(q[u][rows] * jnp.exp2(b[u][rows] - b[u][lo - 1:lo]),
                         jnp.where(row_k < lo, k[u] * jnp.exp2(b[u][lo - 1:lo] - b[u]), 0.0), NT) for u in us]
        else:
            att = [jnp.zeros((SUB, n_tok), F32) for _ in us]
        for j in range(SUB):
            tok = slice(lo + j, lo + j + 1)
            col = [jnp.sum(q[u][rows] * (k[u][tok] * jnp.exp2(b[u][rows] - b[u][tok])), axis=1, keepdims=True)
                   for u in us]
            att = [jnp.where(own_col[i] == j, col[u], att[u]) for u in us]
        for u in us:
            blocks[u].append(o_inter[u][rows] + _dotp(att[u], v[u], NN))
    for u, (n, h) in enumerate(units):
        o = jnp.concatenate(blocks[u], axis=0)
        og = og_ref[n, :, vs(h)]
        o_ref[n, :, vs(h)] = (_rmsnorm(o, nw_ref[...], GLA_NORM_EPS)
                              * (og * jax.nn.sigmoid(og))).astype(o_ref.dtype)
        b_last = b[u][n_tok - 1:n_tok, :]
        state[n, h] = st[u] * jnp.exp2(b_last) + _dotp(v[u].T, k[u] * jnp.exp2(b_last - b[u]), NN)

    @pl.when(c_idx == pl.num_programs(1) - 1)
    def _():
        s_ref[...] = state[...]


def gla_prompt(rows, proj, gate_w2, gate_b, norm_w):
    nc = rows.t // CHUNK
    n_seq = _largest_tile(rows.bp, GLA_SEQS, 1)
    proj3 = proj.reshape(rows.bp, rows.t, proj.shape[1])
    tok = lambda width, col: pl.BlockSpec((n_seq, CHUNK, width), lambda b, c: (b, c, col))
    const = lambda shape: pl.BlockSpec(shape, lambda b, c: (0, 0))
    out, final_state = pl.pallas_call(
        _gla_chunk_kernel,
        grid=(rows.bp // n_seq, nc),
        in_specs=[tok(GLA_WIDTH, 0), tok(GLA_WIDTH, 1), tok(GLA_WIDTH, 2), tok(LANES, DST["gl"][0] // LANES),
                  const((LANES, GLA_KEY_WIDTH)), const((1, GLA_KEY_WIDTH)), const((1, GLA_DV))],
        out_specs=[tok(GLA_WIDTH, 0),
                   pl.BlockSpec((n_seq, GLA_HEADS, GLA_DV, GLA_DK), lambda b, c: (b, 0, 0, 0))],
        out_shape=[jax.ShapeDtypeStruct((rows.bp, rows.t, GLA_WIDTH), BF16),
                   jax.ShapeDtypeStruct((rows.bp, GLA_HEADS, GLA_DV, GLA_DK), F32)],
        scratch_shapes=[pltpu.VMEM((n_seq, GLA_HEADS, GLA_DV, GLA_DK), F32)],
        compiler_params=_params("parallel", "arbitrary"),
    )(proj3, proj3, proj3, proj3, gate_w2, gate_b, norm_w)
    return out.reshape(rows.rows, GLA_WIDTH), final_state


def _outproj_kernel(og_ref, orw_ref, x_ref, gt_ref, sc_ref, sh_ref, nw_ref, wo_ref, wr_ref,
                    x1s_ref, h2s_ref, lgs_ref, x1_ref, h2_ref, lg_ref, *, n_prompt_tiles):
    i = pl.program_id(0)

    @pl.when(i < n_prompt_tiles)
    def _():
        half = og_ref.shape[1]
        mix = _dot(og_ref[...], wo_ref[:half, :]) + _dot(orw_ref[...], wo_ref[half:, :])
        x1 = x_ref[...] + gt_ref[0] * mix
        h2 = _rmsnorm(x1, nw_ref[...], NORM_EPS) * (1.0 + sc_ref[0]) + sh_ref[0]
        x1_ref[...] = x1
        h2_ref[...] = h2
        lg_ref[...] = _dotp(h2, wr_ref[...], NN, 3)

    @pl.when(i >= n_prompt_tiles)
    def _():
        x1_ref[...] = x1s_ref[...]
        h2_ref[...] = h2s_ref[...]
        lg_ref[...] = lgs_ref[...]


def out_projection(t, o_gla, o_rwkv, xp, mod_p, norm_w, w_out, w_router, x1_s, h2_s, logits_s):
    n_p, d = xp.shape
    pad_s = x1_s.shape[0]
    tm = ROW_TILE
    npt, per = n_p // tm, t // tm
    prompt = lambda width: pl.BlockSpec((tm, width), lambda i: (jnp.minimum(i, npt - 1), 0))
    pmod = lambda col: pl.BlockSpec((1, 1, d), lambda i: (jnp.minimum(i // per, mod_p.shape[0] - 1), 0, col))
    sample = lambda width: pl.BlockSpec((tm, width), lambda i: (jnp.maximum(i - npt, 0), 0))
    rows = lambda width: pl.BlockSpec((tm, width), lambda i: (i, 0))
    const = lambda a: pl.BlockSpec(a.shape, lambda i: (0, 0), pipeline_mode=pl.Buffered(1))
    n_rows = n_p + pad_s
    return pl.pallas_call(
        functools.partial(_outproj_kernel, n_prompt_tiles=npt),
        grid=(n_rows // tm,),
        in_specs=[prompt(o_gla.shape[1]), prompt(o_rwkv.shape[1]), prompt(d), pmod(2), pmod(4), pmod(3),
                  const(norm_w), const(w_out), const(w_router), sample(d), sample(d), sample(LANES)],
        out_specs=[rows(d), rows(d), rows(LANES)],
        out_shape=[jax.ShapeDtypeStruct((n_rows, d), F32), jax.ShapeDtypeStruct((n_rows, d), F32),
                   jax.ShapeDtypeStruct((n_rows, LANES), F32)],
        compiler_params=_params("arbitrary"),
    )(o_gla, o_rwkv, xp, mod_p, mod_p, mod_p, norm_w, w_out, w_router, x1_s, h2_s, logits_s)


def _moe_kernel(tile_expert_ref, n_valid_ref, n_real_ref, src_ref, dst_ref, h2_hbm, wg_ref, wu_ref, wd_ref,
                y_hbm, x_even, x_odd, o_even, o_odd, gather_sem, scatter_sem):
    del tile_expert_ref
    i = pl.program_id(0)
    n_tiles = pl.num_programs(0)
    n_valid = n_valid_ref[0]
    xbufs, obufs = (x_even, x_odd), (o_even, o_odd)

    def for_rows(n, fn):
        def group(g, carry):
            for u in range(DMA_UNROLL):
                fn(g * DMA_UNROLL + u)
            return carry

        def single(r, carry):
            fn(r)
            return carry
        full = n // DMA_UNROLL
        lax.fori_loop(0, full, group, 0)
        lax.fori_loop(full * DMA_UNROLL, n, single, 0)

    def gather_row(tile, p, r):
        tok = src_ref[tile * MOE_TILE + r]
        return pltpu.make_async_copy(h2_hbm.at[pl.ds(tok, 1), :], xbufs[p].at[pl.ds(r, 1), :], gather_sem.at[p])

    def wait_gather(p):
        pltpu.make_async_copy(h2_hbm.at[pl.ds(0, MOE_TILE), :], xbufs[p], gather_sem.at[p]).wait()

    def scatter_row(tile, p, r):
        row = dst_ref[tile * MOE_TILE + r]
        return pltpu.make_async_copy(obufs[p].at[pl.ds(r, 1), :], y_hbm.at[pl.ds(row, 1), :], scatter_sem.at[p])

    def start_scatter(tile, p):
        for_rows(n_real_ref[tile], lambda r: scatter_row(tile, p, r).start(priority=ROW_DMA_PRIORITY))

    def wait_scatter(tile, p):
        n_real = n_real_ref[tile]

        @pl.when(n_real == MOE_TILE)
        def _():
            pltpu.make_async_copy(obufs[p], y_hbm.at[pl.ds(0, MOE_TILE), :], scatter_sem.at[p]).wait()

        @pl.when(n_real < MOE_TILE)
        def _():
            for_rows(n_real, lambda r: scatter_row(tile, p, r).wait())

    @pl.when(i == 0)
    def _():
        for_rows(MOE_TILE, lambda r: gather_row(0, 0, r).start(priority=ROW_DMA_PRIORITY))

    def step(p):
        wait_gather(p)

        @pl.when(i >= 2)
        def _():
            wait_scatter(i - 2, p)

        nxt = jnp.minimum(i + 1, n_valid - 1)
        for r in range(MOE_TILE):
            gather_row(nxt, 1 - p, r).start(priority=ROW_DMA_PRIORITY)
        x = xbufs[p][...].astype(BF16)
        gate = _dot(x, wg_ref[0].astype(BF16))
        up = _dot(x, wu_ref[0].astype(BF16))
        hid = gate * jax.nn.sigmoid(gate) * up
        obufs[p][...] = _dot(hid.astype(BF16), wd_ref[0].astype(BF16))
        start_scatter(i, p)

    for p in (0, 1):
        pl.when((i < n_valid) & (i % 2 == p))(functools.partial(step, p))

    @pl.when(i == n_tiles - 1)
    def _():
        for p in (0, 1):
            pl.when(n_valid % 2 == p)(functools.partial(wait_gather, p))
            pl.when((n_valid >= 2) & (n_valid % 2 == p))(functools.partial(wait_scatter, n_valid - 2, p))
            pl.when((n_valid - 1) % 2 == p)(functools.partial(wait_scatter, n_valid - 1, p))


def moe_experts(h2, plan, w_gate, w_up, w_down):
    n_rows, d = h2.shape
    _, _, ff = w_gate.shape
    n_tiles = plan["tile_expert"].shape[0]
    grid_spec = pltpu.PrefetchScalarGridSpec(
        num_scalar_prefetch=5,
        grid=(n_tiles,),
        in_specs=[pl.BlockSpec(memory_space=pl.ANY),
                  pl.BlockSpec((1, d, ff), lambda i, te, *_: (te[i], 0, 0)),
                  pl.BlockSpec((1, d, ff), lambda i, te, *_: (te[i], 0, 0)),
                  pl.BlockSpec((1, ff, d), lambda i, te, *_: (te[i], 0, 0))],
        out_specs=pl.BlockSpec(memory_space=pl.ANY),
        scratch_shapes=[pltpu.VMEM((MOE_TILE, d), F32)] * 4
        + [pltpu.SemaphoreType.DMA((2,)), pltpu.SemaphoreType.DMA((2,))],
    )
    return pl.pallas_call(
        _moe_kernel,
        grid_spec=grid_spec,
        out_shape=jax.ShapeDtypeStruct((2 * n_rows, d), F32),
        compiler_params=_params("arbitrary"),
    )(plan["tile_expert"], plan["n_valid"], plan["n_real"], plan["src_row"], plan["dst_row"], h2,
      w_gate, w_up, w_down)


def _route(logits):
    p_group = jax.nn.softmax(logits[:, :N_GROUPS], axis=-1)
    g_sel = jnp.argmax(p_group, axis=-1)
    p_sel = jnp.take_along_axis(p_group, g_sel[:, None], axis=-1)
    logits_e = logits[:, N_GROUPS:N_GROUPS + N_EXPERTS].reshape(-1, N_GROUPS, EXPERTS_PER_GROUP)
    logits_e = jnp.take_along_axis(logits_e, g_sel[:, None, None], axis=1)[:, 0]
    i1 = jnp.argmax(logits_e, axis=-1)
    v1 = jnp.take_along_axis(logits_e, i1[:, None], axis=-1)
    masked = jnp.where(jnp.arange(EXPERTS_PER_GROUP)[None, :] == i1[:, None], -jnp.inf, logits_e)
    i2 = jnp.argmax(masked, axis=-1)
    v2 = jnp.take_along_axis(logits_e, i2[:, None], axis=-1)
    w_top = jax.nn.softmax(jnp.concatenate([v1, v2], axis=-1), axis=-1) * p_sel
    top_i = jnp.stack([i1, i2], axis=-1)
    expert = g_sel[:, None].astype(jnp.int32) * EXPERTS_PER_GROUP + top_i.astype(jnp.int32)
    return expert, w_top


def _moe_plan(expert):
    n = expert.shape[0]
    e_flat = expert.reshape(-1)
    ids = jnp.arange(N_EXPERTS, dtype=jnp.int32)
    one_hot = (e_flat[:, None] == ids[None, :]).astype(jnp.int32)
    counts = jnp.sum(one_hot, axis=0)
    padded = (counts + MOE_TILE - 1) // MOE_TILE * MOE_TILE
    ends = jnp.cumsum(padded)
    starts = ends - padded
    rank = jnp.sum(jnp.cumsum(one_hot, axis=0) * one_hot, axis=1) - 1
    pos = jnp.sum(one_hot * starts[None, :], axis=1) + rank
    p_max = _round_up(2 * n, MOE_TILE) + N_EXPERTS * MOE_TILE
    n_tiles = p_max // MOE_TILE
    pair = jnp.full((p_max,), -1, jnp.int32).at[pos].set(jnp.arange(2 * n, dtype=jnp.int32))
    pair0 = jnp.maximum(pair, 0)
    src_row = pair0 // 2
    dst_row = (pair0 % 2) * n + src_row
    tile_start = jnp.arange(n_tiles, dtype=jnp.int32) * MOE_TILE
    n_valid = ends[-1] // MOE_TILE
    tile_expert = jnp.sum((tile_start[:, None] >= ends[None, :]).astype(jnp.int32), axis=1)
    tile_expert = jnp.where(tile_start < ends[-1], tile_expert, tile_expert[n_valid - 1])
    n_real = jnp.clip((starts + counts)[tile_expert] - tile_start, 0, MOE_TILE)
    n_real = jnp.where(tile_start < ends[-1], n_real, 0).astype(jnp.int32)
    return dict(src_row=src_row, dst_row=dst_row, tile_expert=tile_expert,
                n_valid=n_valid.astype(jnp.int32)[None], n_real=n_real)


def _final_kernel(x1_ref, y0_ref, y1_ref, wt_ref, gtp_ref, gts_ref, scp_ref, scs_ref, shp_ref, shs_ref, nw_ref,
                  yp_ref, ys_ref, *, n_prompt_tiles):
    i = pl.program_id(0)

    def finish(gt, sc, sh):
        moe = wt_ref[:, 0:1] * y0_ref[...] + wt_ref[:, 1:2] * y1_ref[...]
        x2 = x1_ref[...] + gt * moe
        return _rmsnorm(x2, nw_ref[...], NORM_EPS) * (1.0 + sc) + sh

    @pl.when(i < n_prompt_tiles)
    def _():
        yp_ref[...] = finish(gtp_ref[0], scp_ref[0], shp_ref[0])

    @pl.when(i >= n_prompt_tiles)
    def _():
        ys_ref[...] = finish(gts_ref[...], scs_ref[...], shs_ref[...])


def final_norm(n_p, t, x1, y_pairs, w_top, mod_p, mod_s, modf_p, modf_s, norm_w):
    n_rows, d = x1.shape
    tile = ROW_TILE
    n_tiles, npt = n_rows // tile, n_p // tile
    per = t // tile
    rows = lambda off: pl.BlockSpec((tile, d), lambda i: (i + off, 0))
    prompt_out = pl.BlockSpec((tile, d), lambda i: (jnp.minimum(i, npt - 1), 0))
    sample = lambda col: pl.BlockSpec((tile, d), lambda i: (jnp.maximum(i - npt, 0), col))
    pmod = lambda col: pl.BlockSpec((1, 1, d), lambda i: (jnp.minimum(i // per, mod_p.shape[0] - 1), 0, col))
    return pl.pallas_call(
        functools.partial(_final_kernel, n_prompt_tiles=npt),
        grid=(n_tiles,),
        in_specs=[rows(0), rows(0), rows(n_tiles), pl.BlockSpec((tile, w_top.shape[1]), lambda i: (i, 0)),
                  pmod(5), sample(5), pmod(1), sample(1), pmod(0), sample(0),
                  pl.BlockSpec((1, d), lambda i: (0, 0))],
        out_specs=[prompt_out, sample(0)],
        out_shape=[jax.ShapeDtypeStruct((n_p, d), F32), jax.ShapeDtypeStruct((n_rows - n_p, d), F32)],
        compiler_params=_params("arbitrary"),
    )(x1, y_pairs, y_pairs, w_top, mod_p, mod_s, modf_p, modf_s, modf_p, modf_s, norm_w)


def _align_rows(wt):
    parts = []
    for name in _ALIGNED:
        off, w = SRC[name]
        part = wt[off:off + w]
        wa = DST[name][1]
        if wa != w:
            part = jnp.pad(part, ((0, wa - w), (0, 0)))
        parts.append(part)
    return jnp.concatenate(parts, axis=0)


def _rwkv_seg(a, name):
    off, w = SRC[name]
    return a[..., off - RWKV_SRC_BASE:off - RWKV_SRC_BASE + w]


def _rwkv_seg_padded(a, name):
    seg = _rwkv_seg(a, name)
    wa = DST[name][1]
    return jnp.pad(seg, [(0, 0)] * (a.ndim - 1) + [(0, wa - seg.shape[-1])])


def _orig_seg(p, name):
    off, w = SRC[name]
    return p[..., off:off + w]


def _pad_rows(w, rows):
    return jnp.pad(w, ((0, rows - w.shape[0]), (0, 0)))


def kernel(x_prompt, x_sample, state_gla, state_rwkv, state_shift, c_prompt, c_sample, w_ada, b_ada, norm_mix, norm_ffn, w_in, gla_gate_w2, gla_gate_b, gla_norm, rwkv_mu, rwkv_w0, rwkv_w2, rwkv_a0, rwkv_a2, rwkv_g2, rwkv_k_k, rwkv_k_a, rwkv_r_k, rwkv_gn_w, rwkv_gn_b, w_out, w_router_group, w_router_expert, w_exp_gate, w_exp_up, w_exp_down, norm_final, w_ada_final, b_ada_final):
    assert w_ada.shape[0] == 1, "single-layer step"
    bp, t, d = x_prompt.shape
    bs = x_sample.shape[0]
    assert x_sample.shape[1] == 1 and t % ROW_TILE == 0
    n_p = bp * t
    pad_s = _round_up(bs, ROW_TILE)
    n_rows = n_p + pad_s
    prompt = _PromptShape(bp, t, n_p)
    hi = lax.Precision.HIGHEST
    pad_sample = lambda a: jnp.pad(a, ((0, pad_s - a.shape[0]), (0, 0)))
    hdot = lambda a, w: jnp.dot(a, w, precision=hi)

    c_act = jax.nn.silu(jnp.concatenate([c_prompt, c_sample], axis=0))
    mod = matmul3(c_act, w_ada[0]) + b_ada[0]
    mod_f = matmul3(c_act, w_ada_final) + b_ada_final
    mod_p, mod_s = mod[:bp, None, :], mod[bp:]
    modf_p, modf_s = mod_f[:bp, None, :], mod_f[bp:]
    sh1_s, sc1_s, gt1_s, sh2_s, sc2_s, _ = jnp.split(mod_s, 6, axis=-1)
    xp = x_prompt.reshape(n_p, d)
    xs = x_sample[:, 0, :]

    w_in_t = jnp.swapaxes(w_in[0], 0, 1)
    proj = in_projection(xp, t, mod_p, norm_mix, _align_rows(w_in_t.astype(BF16)))
    o_gla, gla_t_p = gla_prompt(prompt, proj, _pad_rows(gla_gate_w2[0], LANES), gla_gate_b, gla_norm)
    new_gla_p = jnp.swapaxes(gla_t_p, -1, -2)
    mu = rwkv_mu[0]
    mu_big = jnp.concatenate([_rwkv_seg(mu, n) for n in ("r", "k7", "v7")])[None, :]
    mu_small = jnp.concatenate([_rwkv_seg_padded(mu, n) for n in ("wl", "al", "gl7")])[None, :]
    vecs = jnp.concatenate([rwkv_w0, rwkv_a0, rwkv_k_k, rwkv_k_a, rwkv_r_k[0].reshape(1, RWKV_WIDTH),
                            rwkv_gn_w, rwkv_gn_b, jnp.zeros((1, RWKV_WIDTH), F32)], axis=0)
    o_rwkv, rwkv_t_p = rwkv7_prompt(prompt, proj, mu_big, mu_small, vecs, _pad_rows(rwkv_w2[0], LANES),
                                    _pad_rows(rwkv_a2[0], LANES), rwkv_g2[0])
    new_rwkv_p = rwkv_t_p.reshape(bp, RWKV_HEAD, RWKV_HEADS, RWKV_HEAD).transpose(0, 2, 1, 3)
    last = jnp.stack([proj[(b + 1) * t - 1] for b in range(bp)])
    new_shift_p = jnp.concatenate([last[:, DST[n][0]:DST[n][0] + SRC[n][1]] for n in _RWKV_ORIG], axis=-1)

    h1_s = _rmsnorm(xs, norm_mix[0], NORM_EPS) * (1.0 + sc1_s) + sh1_s
    proj_s = matmul3(h1_s, w_in_t, w_transposed=True)
    heads = lambda a, n: a.reshape(a.shape[0], n, -1)
    logd_s = jax.nn.log_sigmoid(hdot(_orig_seg(proj_s, "gl"), gla_gate_w2[0]) + gla_gate_b[0]) \
        / GLA_GATE_NORMALIZER
    q_s = heads(_orig_seg(proj_s, "q") * (GLA_DK ** -0.5), GLA_HEADS)
    k_s, v_s, g_s = heads(_orig_seg(proj_s, "k"), GLA_HEADS), heads(_orig_seg(proj_s, "v"), GLA_HEADS), \
        heads(logd_s, GLA_HEADS)
    new_gla_s = jnp.exp(g_s)[..., None] * state_gla[0] + k_s[..., :, None] * v_s[..., None, :]
    o_s = jnp.einsum("bhd,bhde->bhe", q_s, new_gla_s, precision=hi)
    o_s = _rmsnorm(o_s, gla_norm[0], GLA_NORM_EPS) * heads(jax.nn.silu(_orig_seg(proj_s, "og")), GLA_HEADS)

    rp_s = proj_s[:, RWKV_SRC_BASE:]
    new_shift_s = rp_s
    xs7 = rp_s + (state_shift[0] - rp_s) * mu
    sx = lambda name: _rwkv_seg(xs7, name)
    r_s, k7_s, v7_s = sx("r"), sx("k7"), sx("v7")
    w_pre = rwkv_w0[0] + hdot(jnp.tanh(sx("wl")), rwkv_w2[0])
    decay_s = jnp.exp(-jnp.exp(-jax.nn.softplus(-w_pre) - 0.5))
    a_s = jax.nn.sigmoid(rwkv_a0[0] + hdot(sx("al"), rwkv_a2[0]))
    g_s7 = hdot(jax.nn.sigmoid(sx("gl7")), rwkv_g2[0])
    hs = lambda z: z.reshape(bs, RWKV_HEADS, RWKV_HEAD)
    kk_s = hs(k7_s * rwkv_k_k[0])
    kk_s = kk_s / jnp.maximum(jnp.sqrt(jnp.sum(kk_s * kk_s, axis=-1, keepdims=True)), 1e-12)
    k7_s = k7_s * (1.0 + (a_s - 1.0) * rwkv_k_a[0])
    r_h, w_h, k_h, v_h, a_h = hs(r_s), hs(decay_s), hs(k7_s), hs(v7_s), hs(a_s)
    s_prev = state_rwkv[0]
    sa = jnp.einsum("bhij,bhj->bhi", s_prev, -kk_s, precision=hi)
    new_rwkv_s = s_prev * w_h[:, :, None, :] + sa[..., None] * (kk_s * a_h)[:, :, None, :] \
        + v_h[..., None] * k_h[:, :, None, :]
    y_s = jnp.einsum("bhij,bhj->bhi", new_rwkv_s, r_h, precision=hi)
    y_mu = jnp.mean(y_s, axis=-1, keepdims=True)
    y_var = jnp.mean(jnp.square(y_s - y_mu), axis=-1, keepdims=True)
    y_s = (y_s - y_mu) * lax.rsqrt(y_var + RWKV_GN_EPS) * rwkv_gn_w[0].reshape(RWKV_HEADS, RWKV_HEAD) \
        + rwkv_gn_b[0].reshape(RWKV_HEADS, RWKV_HEAD)
    bonus = jnp.sum(r_h * k_h * rwkv_r_k[0], axis=-1, keepdims=True) * v_h
    o_rs = (y_s + bonus).reshape(bs, RWKV_WIDTH) * g_s7
    mix_s = matmul3(jnp.concatenate([o_s.reshape(bs, GLA_WIDTH), o_rs], axis=-1), w_out[0])
    x1_s = xs + gt1_s * mix_s
    h2_s = _rmsnorm(x1_s, norm_ffn[0], NORM_EPS) * (1.0 + sc2_s) + sh2_s
    w_router = jnp.pad(jnp.concatenate([w_router_group[0], w_router_expert[0]], axis=-1),
                       ((0, 0), (0, LANES - N_GROUPS - N_EXPERTS)))
    logits_s = hdot(h2_s, w_router)

    x1, h2, logits = out_projection(t, o_gla, o_rwkv, xp, mod_p, norm_ffn, w_out[0].astype(BF16), w_router,
                                    pad_sample(x1_s), pad_sample(h2_s), pad_sample(logits_s))

    expert, w_top = _route(logits)
    plan = _moe_plan(expert)
    y_pairs = moe_experts(h2, plan, w_exp_gate[0].reshape(N_EXPERTS, d, EXPERT_FF),
                          w_exp_up[0].reshape(N_EXPERTS, d, EXPERT_FF),
                          w_exp_down[0].reshape(N_EXPERTS, EXPERT_FF, d))

    y_p, y_s_pad = final_norm(n_p, t, x1, y_pairs, w_top, mod_p, pad_sample(mod_s), modf_p,
                              pad_sample(modf_s), norm_final[None, :])
    return (y_p.reshape(bp, t, d), y_s_pad[:bs, None, :], new_gla_p[None], new_rwkv_p[None], new_shift_p[None],
            new_gla_s[None], new_rwkv_s[None], new_shift_s[None])
```

```python
import collections
import functools

import jax
import jax.numpy as jnp
from jax import lax
from jax.experimental import pallas as pl
from jax.experimental.pallas import tpu as pltpu

F32 = jnp.float32
BF16 = jnp.bfloat16

D_MODEL = 2048
GLA_HEADS = 4
GLA_DK = 128
GLA_DV = 256
GLA_KEY_WIDTH = GLA_HEADS * GLA_DK
GLA_WIDTH = GLA_HEADS * GLA_DV
GLA_GATE_NORMALIZER = 16.0
RWKV_HEAD = 64
RWKV_HEADS = 16
RWKV_WIDTH = RWKV_HEAD * RWKV_HEADS
N_GROUPS = 4
EXPERTS_PER_GROUP = 8
N_EXPERTS = N_GROUPS * EXPERTS_PER_GROUP
EXPERT_FF = 512
NORM_EPS = 1e-6
GLA_NORM_EPS = 1e-5
RWKV_GN_EPS = 64e-5

LANES = 128
SUBLANES = 8
VMEM_LIMIT_BYTES = 56 * 1024 * 1024

CHUNK = 64
SUB = 16
RWKV_GROUP = 2
GROUP_W = RWKV_GROUP * RWKV_HEAD
GLA_SEQS = 2
LOG2E = 1.4426950408889634
RWKV_SEQS = 4
MOE_TILE = 256
DMA_UNROLL = 8
ROW_TILE = 256
NORM_SLAB = 128
K_SPLIT = 256

_ORIG = (("q", 512), ("k", 512), ("v", 1024), ("gl", 16), ("og", 1024),
         ("r", 1024), ("wl", 64), ("k7", 1024), ("v7", 1024), ("al", 64), ("gl7", 128))
_ALIGNED = ("q", "k", "v", "og", "r", "k7", "v7", "gl", "wl", "al", "gl7")
_RWKV_ORIG = ("r", "wl", "k7", "v7", "al", "gl7")


def _round_up(n, m):
    return (n + m - 1) // m * m


def _layouts():
    src, off = {}, 0
    for name, w in _ORIG:
        src[name] = (off, w)
        off += w
    dst, pos = {}, 0
    for name in _ALIGNED:
        wa = _round_up(src[name][1], LANES)
        dst[name] = (pos, wa)
        pos += wa
    return src, off, dst, pos


SRC, IN_COLS, DST, IN_COLS_ALIGNED = _layouts()
RWKV_SRC_BASE = SRC["r"][0]
BIG_W = 3 * RWKV_WIDTH
SMALL_W = 3 * LANES
assert DST["r"][0] == BIG_W and DST["k7"][0] == BIG_W + RWKV_WIDTH and DST["v7"][0] == BIG_W + 2 * RWKV_WIDTH
assert DST["v"][0] == RWKV_WIDTH and DST["og"][0] == 2 * RWKV_WIDTH and DST["k"][0] == GLA_KEY_WIDTH

NN = ((1,), (0,))
NT = ((1,), (1,))

_PromptShape = collections.namedtuple("_PromptShape", "bp t rows")


def _dot(a, b, dims=NN):
    return lax.dot_general(a, b, (dims, ((), ())), preferred_element_type=F32)


def _split2(x):
    hi = x.astype(BF16)
    lo = (x - hi.astype(F32)).astype(BF16)
    return hi, lo


def _dotp(a, b, dims=NN, passes=1):
    if passes == 1:
        return _dot(a.astype(BF16), b.astype(BF16), dims)
    ah, al = _split2(a)
    bh, bl = _split2(b)
    return _dot(ah, bh, dims) + (_dot(ah, bl, dims) + _dot(al, bh, dims))


def _cumsum_rows(x):
    n = x.shape[0]
    row = lax.broadcasted_iota(jnp.int32, (n, n), 0)
    col = lax.broadcasted_iota(jnp.int32, (n, n), 1)
    tri = (row >= col).astype(BF16)
    x1 = x.astype(BF16)
    r1 = x - x1.astype(F32)
    x2 = r1.astype(BF16)
    x3 = (r1 - x2.astype(F32)).astype(BF16)
    return _dot(tri, x1) + (_dot(tri, x2) + _dot(tri, x3))


def _softplus(z):
    return jnp.maximum(z, 0.0) + jnp.log1p(jnp.exp(-jnp.abs(z)))


def _rmsnorm(x, g, eps):
    return x * lax.rsqrt(jnp.mean(x * x, axis=-1, keepdims=True) + eps) * g


def _largest_tile(n, cap, mult=SUBLANES):
    if n <= cap:
        return n
    best = None
    for t in range(mult, cap + 1, mult):
        if n % t == 0:
            best = t
    assert best is not None, (n, cap)
    return best


def _params(*sem):
    return pltpu.CompilerParams(dimension_semantics=sem, vmem_limit_bytes=VMEM_LIMIT_BYTES)


def _mm3_kernel(a_ref, w_ref, o_ref, *, dims):
    @pl.when(pl.program_id(0) == 0)
    def _():
        o_ref[...] = jnp.zeros_like(o_ref)

    o_ref[...] += _dotp(a_ref[...], w_ref[...], dims, 3)


def matmul3(a, w, w_transposed=False):
    m, k = a.shape
    n = w.shape[0] if w_transposed else w.shape[1]
    tk = _largest_tile(k, K_SPLIT, LANES)
    w_spec = pl.BlockSpec((n, tk), lambda s: (0, s)) if w_transposed else pl.BlockSpec((tk, n), lambda s: (s, 0))
    return pl.pallas_call(
        functools.partial(_mm3_kernel, dims=NT if w_transposed else NN),
        grid=(k // tk,),
        in_specs=[pl.BlockSpec((m, tk), lambda s: (0, s)), w_spec],
        out_specs=pl.BlockSpec((m, n), lambda s: (0, 0)),
        out_shape=jax.ShapeDtypeStruct((m, n), F32),
        compiler_params=_params("arbitrary"),
    )(a, w)


def _prompt_mod(tiles_per_batch, width, col):
    return pl.BlockSpec((1, 1, width), lambda i, *_: (i // tiles_per_batch, 0, col))


def _inproj_kernel(x_ref, sc_ref, sh_ref, nw_ref, w_ref, o_ref, h_scr):
    @pl.when(pl.program_id(1) == 0)
    def _():
        for r in range(0, h_scr.shape[0], NORM_SLAB):
            sl = slice(r, r + NORM_SLAB)
            h = _rmsnorm(x_ref[sl, :], nw_ref[...], NORM_EPS) * (1.0 + sc_ref[0]) + sh_ref[0]
            h_scr[sl, :] = h.astype(BF16)

    o_ref[...] = _dot(h_scr[...], w_ref[...], NT)


def in_projection(xp, t, mod_p, norm_w, w_aligned):
    n_p, d = xp.shape
    n = w_aligned.shape[0]
    tm = _largest_tile(t, 1024, NORM_SLAB)
    tn = _largest_tile(n, 512, LANES)
    per = t // tm
    return pl.pallas_call(
        _inproj_kernel,
        grid=(n_p // tm, n // tn),
        in_specs=[pl.BlockSpec((tm, d), lambda i, j: (i, 0)), _prompt_mod(per, d, 1), _prompt_mod(per, d, 0),
                  pl.BlockSpec((1, d), lambda i, j: (0, 0)), pl.BlockSpec((tn, d), lambda i, j: (j, 0))],
        out_specs=pl.BlockSpec((tm, tn), lambda i, j: (i, j)),
        out_shape=jax.ShapeDtypeStruct((n_p, n), F32),
        scratch_shapes=[pltpu.VMEM((tm, d), BF16)],
        compiler_params=_params("parallel", "arbitrary"),
    )(xp, mod_p, mod_p, norm_w, w_aligned)


def _rwkv_chunk_kernel(big_ref, wl_ref, al_ref, gl_ref, mu_big_ref, mu_small_ref, vec_ref, w2_ref, a2_ref,
                       g2_ref, o_ref, s_ref, state, carry_big, carry_small, *, passes):
    c_idx = pl.program_id(1)
    n_seq, n_tok = big_ref.shape[0], big_ref.shape[1]
    n_grp = RWKV_WIDTH // GROUP_W
    grp = [slice(g * GROUP_W, (g + 1) * GROUP_W) for g in range(n_grp)]

    @pl.when(c_idx == 0)
    def _():
        state[...] = jnp.zeros_like(state)
        carry_big[...] = jnp.zeros_like(carry_big)
        carry_small[...] = jnp.zeros_like(carry_small)

    first_row = lax.broadcasted_iota(jnp.int32, (n_tok, 1), 0) == 0
    w0, a0, k_k, k_a, r_k, gn_w, gn_b = (vec_ref[i:i + 1, :] for i in range(7))

    gi = lax.broadcasted_iota(jnp.int32, (GROUP_W, GROUP_W), 0) // RWKV_HEAD
    gj = lax.broadcasted_iota(jnp.int32, (GROUP_W, GROUP_W), 1) // RWKV_HEAD
    head_ones = (gi == gj).astype(BF16)

    def head_sums(parts):
        hi, lo = _split2(jnp.concatenate(parts, axis=0))
        out = _dot(hi, head_ones) + _dot(lo, head_ones)
        return [out[i * n_tok:(i + 1) * n_tok] for i in range(len(parts))]

    def prepare(n):
        def token_shift(cur, carry, mu):
            prev = jnp.where(first_row, carry[n, 0:1, :], pltpu.roll(cur, 1, 0))
            carry[n, 0:1, :] = cur[n_tok - 1:n_tok, :]
            return cur + (prev - cur) * mu

        xs_big = token_shift(big_ref[n], carry_big, mu_big_ref[...])
        small = jnp.concatenate([wl_ref[n], al_ref[n], gl_ref[n]], axis=1)
        xs_small = token_shift(small, carry_small, mu_small_ref[...])
        r = xs_big[:, :RWKV_WIDTH]
        k7 = xs_big[:, RWKV_WIDTH:2 * RWKV_WIDTH]
        v = xs_big[:, 2 * RWKV_WIDTH:]
        w_pre = w0 + _dotp(jnp.tanh(xs_small[:, :LANES]), w2_ref[...])
        lw = -jnp.exp(-_softplus(-w_pre) - 0.5)
        a = jax.nn.sigmoid(a0 + _dotp(xs_small[:, LANES:2 * LANES], a2_ref[...]))
        gate = _dotp(jax.nn.sigmoid(xs_small[:, 2 * LANES:]), g2_ref[...])
        kk_raw = k7 * k_k
        k = k7 * (1.0 + (a - 1.0) * k_a)
        sums = [head_sums([kk_raw[:, sl] * kk_raw[:, sl], r[:, sl] * k[:, sl] * r_k[:, sl]]) for sl in grp]
        kk = jnp.concatenate([kk_raw[:, sl] / jnp.maximum(jnp.sqrt(sums[g][0]), 1e-12)
                              for g, sl in enumerate(grp)], axis=1)
        bonus = jnp.concatenate([sums[g][1] for g in range(n_grp)], axis=1) * v
        cum = _cumsum_rows(lw)
        cum_end = cum[n_tok - 1:n_tok, :]
        beta = kk * a
        g_inv = jnp.exp(-cum)
        g_end = jnp.exp(cum_end - cum)
        return dict(v=v, gate=gate, bonus=bonus, g_tot=jnp.exp(cum_end),
                    a_hat=-kk * jnp.exp(cum - lw), r_hat=r * jnp.exp(cum), b_hat=beta * g_inv,
                    k_hat=k * g_inv, b_end=beta * g_end, k_end=k * g_end)

    seqs = [prepare(n) for n in range(n_seq)]
    units = [(n, sl) for n in range(n_seq) for sl in grp]
    part = lambda name: [seqs[n][name][:, sl] for n, sl in units]

    lane = lax.broadcasted_iota(jnp.int32, (n_tok, GROUP_W), 1)
    tok = lax.broadcasted_iota(jnp.int32, (n_tok, GROUP_W), 0)
    lane_head = lane // RWKV_HEAD
    src_tok = lane % RWKV_HEAD
    strict = tok > src_tok
    incl = tok >= src_tok

    def bd(y):
        return jnp.concatenate([jnp.where(lane_head == h, y, 0.0) for h in range(RWKV_GROUP)], axis=0)

    gs = range(len(units))
    s0 = [state[n, :, sl] for n, sl in units]
    v_u, b_hat, k_hat, b_end, k_end = part("v"), part("b_hat"), part("k_hat"), part("b_end"), part("k_end")
    lhs2 = [jnp.concatenate([a, r], axis=0) for a, r in zip(part("a_hat"), part("r_hat"))]
    abrb = [_dotp(lhs2[g], bd(b_hat[g]), NT, passes) for g in gs]
    akrk = [_dotp(lhs2[g], bd(k_hat[g]), NT, passes) for g in gs]
    asrs = [_dotp(lhs2[g], bd(s0[g]), NT, passes) for g in gs]
    p = [jnp.where(strict, abrb[g][:n_tok], 0.0) for g in gs]
    ak = [jnp.where(strict, akrk[g][:n_tok], 0.0) for g in gs]
    rb = [jnp.where(incl, abrb[g][n_tok:], 0.0) for g in gs]
    rk = [jnp.where(incl, akrk[g][n_tok:], 0.0) for g in gs]
    bd_v = [bd(v_u[g]) for g in gs]
    x = [asrs[g][:n_tok] + _dotp(ak[g], bd_v[g], NN, passes) for g in gs]
    n_sq = n_tok.bit_length() - 1
    for it in range(n_sq):
        if it < n_sq - 1:
            both = [_dotp(p[g], jnp.concatenate([bd(p[g]), bd(x[g])], axis=1), NN, passes) for g in gs]
            x = [x[g] + both[g][:, GROUP_W:] for g in gs]
            p = [both[g][:, :GROUP_W] for g in gs]
        else:
            x = [x[g] + _dotp(p[g], bd(x[g]), NN, passes) for g in gs]
    y = [asrs[g][n_tok:] + _dotp(jnp.concatenate([rb[g], rk[g]], axis=1),
                                 jnp.concatenate([bd(x[g]), bd_v[g]], axis=0), NN, passes) for g in gs]
    full = [_dotp(jnp.concatenate([x[g], v_u[g]], axis=0).T,
                  jnp.concatenate([b_end[g], k_end[g]], axis=0), NN, passes) for g in gs]
    g_tot = part("g_tot")
    for g, (n, sl) in enumerate(units):
        upd = s0[g] * g_tot[g]
        for h in range(RWKV_GROUP):
            upd = upd + jnp.where(lane_head == h, full[g][h * RWKV_HEAD:(h + 1) * RWKV_HEAD, :], 0.0)
        state[n, :, sl] = upd

    inv_n = 1.0 / RWKV_HEAD
    mean = [head_sums([y[g]])[0] * inv_n for g in gs]
    dev = [y[g] - mean[g] for g in gs]
    var = [head_sums([dev[g] * dev[g]])[0] * inv_n for g in gs]
    bonus, gate = part("bonus"), part("gate")
    for g, (n, sl) in enumerate(units):
        yn = dev[g] * lax.rsqrt(var[g] + RWKV_GN_EPS) * gn_w[:, sl] + gn_b[:, sl]
        o_ref[n, :, sl] = ((yn + bonus[g]) * gate[g]).astype(o_ref.dtype)

    @pl.when(c_idx == pl.num_programs(1) - 1)
    def _():
        s_ref[...] = state[...]


def rwkv7_prompt(rows, proj, mu_big, mu_small, vecs, w2, a2, g2, *, passes=1):
    assert CHUNK == RWKV_HEAD and rows.t % CHUNK == 0
    nc = rows.t // CHUNK
    n_seq = _largest_tile(rows.bp, RWKV_SEQS, 1)
    proj3 = proj.reshape(rows.bp, rows.t, proj.shape[1])
    small_col = lambda name: DST[name][0] // LANES
    tok = lambda width, col: pl.BlockSpec((n_seq, CHUNK, width), lambda b, c: (b, c, col))
    const = lambda shape: pl.BlockSpec(shape, lambda b, c: (0, 0))
    out, final_state = pl.pallas_call(
        functools.partial(_rwkv_chunk_kernel, passes=passes),
        grid=(rows.bp // n_seq, nc),
        in_specs=[tok(BIG_W, 1), tok(LANES, small_col("wl")), tok(LANES, small_col("al")),
                  tok(LANES, small_col("gl7")),
                  const((1, BIG_W)), const((1, SMALL_W)), const((SUBLANES, RWKV_WIDTH)),
                  const((LANES, RWKV_WIDTH)), const((LANES, RWKV_WIDTH)), const((LANES, RWKV_WIDTH))],
        out_specs=[tok(RWKV_WIDTH, 0),
                   pl.BlockSpec((n_seq, RWKV_HEAD, RWKV_WIDTH), lambda b, c: (b, 0, 0))],
        out_shape=[jax.ShapeDtypeStruct((rows.bp, rows.t, RWKV_WIDTH), BF16),
                   jax.ShapeDtypeStruct((rows.bp, RWKV_HEAD, RWKV_WIDTH), F32)],
        scratch_shapes=[pltpu.VMEM((n_seq, RWKV_HEAD, RWKV_WIDTH), F32),
                        pltpu.VMEM((n_seq, SUBLANES, BIG_W), F32), pltpu.VMEM((n_seq, SUBLANES, SMALL_W), F32)],
        compiler_params=_params("parallel", "arbitrary"),
    )(proj3, proj3, proj3, proj3, mu_big, mu_small, vecs, w2, a2, g2)
    return out.reshape(rows.rows, RWKV_WIDTH), final_state


def _gla_chunk_kernel(qk_ref, v_ref, og_ref, gl_ref, w2_ref, gb_ref, nw_ref, o_ref, s_ref, state):
    c_idx = pl.program_id(1)
    n_seq, n_tok = qk_ref.shape[0], qk_ref.shape[1]
    n_sub = n_tok // SUB

    @pl.when(c_idx == 0)
    def _():
        state[...] = jnp.zeros_like(state)

    row_k = lax.broadcasted_iota(jnp.int32, (n_tok, GLA_DK), 0)
    att_row = lax.broadcasted_iota(jnp.int32, (SUB, n_tok), 0)
    att_col = lax.broadcasted_iota(jnp.int32, (SUB, n_tok), 1)
    own_col = [jnp.where((att_col >= i * SUB) & (att_col - i * SUB <= att_row), att_col - i * SUB, -1)
               for i in range(n_sub)]

    def cum_log2_decay(n):
        logd = -_softplus(-(_dotp(gl_ref[n], w2_ref[...]) + gb_ref[...])) * (LOG2E / GLA_GATE_NORMALIZER)
        return _cumsum_rows(logd)

    units = [(n, h) for n in range(n_seq) for h in range(GLA_HEADS)]
    us = range(len(units))
    ks = lambda h: slice(h * GLA_DK, (h + 1) * GLA_DK)
    vs = lambda h: slice(h * GLA_DV, (h + 1) * GLA_DV)
    b_seq = [cum_log2_decay(n) for n in range(n_seq)]
    q = [qk_ref[n, :, ks(h)] * (GLA_DK ** -0.5) for n, h in units]
    k = [qk_ref[n, :, GLA_KEY_WIDTH + h * GLA_DK:GLA_KEY_WIDTH + (h + 1) * GLA_DK] for n, h in units]
    b = [b_seq[n][:, ks(h)] for n, h in units]
    v = [v_ref[n, :, vs(h)] for n, h in units]
    st = [state[n, h] for n, h in units]
    o_inter = [_dotp(q[u] * jnp.exp2(b[u]), st[u], NT) for u in us]
    blocks = [[] for _ in us]
    for i in range(n_sub):
        lo = i * SUB
        rows = slice(lo, lo + SUB)
        if i > 0:
            att = [_dotp(q[u][rows] * jnp.exp2(b[u][rows] - b[u][lo - 1:lo]),
                         jnp.where(row_k < lo, k[u] * jnp.exp2(b[u][lo - 1:lo] - b[u]), 0.0), NT) for u in us]
        else:
            att = [jnp.zeros((SUB, n_tok), F32) for _ in us]
        for j in range(SUB):
            tok = slice(lo + j, lo + j + 1)
            col = [jnp.sum(q[u][rows] * (k[u][tok] * jnp.exp2(b[u][rows] - b[u][tok])), axis=1, keepdims=True)
                   for u in us]
            att = [jnp.where(own_col[i] == j, col[u], att[u]) for u in us]
        for u in us:
            blocks[u].append(o_inter[u][rows] + _dotp(att[u], v[u], NN))
    for u, (n, h) in enumerate(units):
        o = jnp.concatenate(blocks[u], axis=0)
        og = og_ref[n, :, vs(h)]
        o_ref[n, :, vs(h)] = (_rmsnorm(o, nw_ref[...], GLA_NORM_EPS)
                              * (og * jax.nn.sigmoid(og))).astype(o_ref.dtype)
        b_last = b[u][n_tok - 1:n_tok, :]
        state[n, h] = st[u] * jnp.exp2(b_last) + _dotp(v[u].T, k[u] * jnp.exp2(b_last - b[u]), NN)

    @pl.when(c_idx == pl.num_programs(1) - 1)
    def _():
        s_ref[...] = state[...]


def gla_prompt(rows, proj, gate_w2, gate_b, norm_w):
    nc = rows.t // CHUNK
    n_seq = _largest_tile(rows.bp, GLA_SEQS, 1)
    proj3 = proj.reshape(rows.bp, rows.t, proj.shape[1])
    tok = lambda width, col: pl.BlockSpec((n_seq, CHUNK, width), lambda b, c: (b, c, col))
    const = lambda shape: pl.BlockSpec(shape, lambda b, c: (0, 0))
    out, final_state = pl.pallas_call(
        _gla_chunk_kernel,
        grid=(rows.bp // n_seq, nc),
        in_specs=[tok(GLA_WIDTH, 0), tok(GLA_WIDTH, 1), tok(GLA_WIDTH, 2), tok(LANES, DST["gl"][0] // LANES),
                  const((LANES, GLA_KEY_WIDTH)), const((1, GLA_KEY_WIDTH)), const((1, GLA_DV))],
        out_specs=[tok(GLA_WIDTH, 0),
                   pl.BlockSpec((n_seq, GLA_HEADS, GLA_DV, GLA_DK), lambda b, c: (b, 0, 0, 0))],
        out_shape=[jax.ShapeDtypeStruct((rows.bp, rows.t, GLA_WIDTH), BF16),
                   jax.ShapeDtypeStruct((rows.bp, GLA_HEADS, GLA_DV, GLA_DK), F32)],
        scratch_shapes=[pltpu.VMEM((n_seq, GLA_HEADS, GLA_DV, GLA_DK), F32)],
        compiler_params=_params("parallel", "arbitrary"),
    )(proj3, proj3, proj3, proj3, gate_w2, gate_b, norm_w)
    return out.reshape(rows.rows, GLA_WIDTH), final_state


def _outproj_kernel(og_ref, orw_ref, x_ref, gt_ref, sc_ref, sh_ref, nw_ref, wo_ref, wr_ref,
                    x1s_ref, h2s_ref, lgs_ref, x1_ref, h2_ref, lg_ref, *, n_prompt_tiles):
    i = pl.program_id(0)

    @pl.when(i < n_prompt_tiles)
    def _():
        half = og_ref.shape[1]
        mix = _dot(og_ref[...], wo_ref[:half, :]) + _dot(orw_ref[...], wo_ref[half:, :])
        x1 = x_ref[...] + gt_ref[0] * mix
        h2 = _rmsnorm(x1, nw_ref[...], NORM_EPS) * (1.0 + sc_ref[0]) + sh_ref[0]
        x1_ref[...] = x1
        h2_ref[...] = h2
        lg_ref[...] = _dotp(h2, wr_ref[...], NN, 3)

    @pl.when(i >= n_prompt_tiles)
    def _():
        x1_ref[...] = x1s_ref[...]
        h2_ref[...] = h2s_ref[...]
        lg_ref[...] = lgs_ref[...]


def out_projection(t, o_gla, o_rwkv, xp, mod_p, norm_w, w_out, w_router, x1_s, h2_s, logits_s):
    n_p, d = xp.shape
    pad_s = x1_s.shape[0]
    tm = ROW_TILE
    npt, per = n_p // tm, t // tm
    prompt = lambda width: pl.BlockSpec((tm, width), lambda i: (jnp.minimum(i, npt - 1), 0))
    pmod = lambda col: pl.BlockSpec((1, 1, d), lambda i: (jnp.minimum(i // per, mod_p.shape[0] - 1), 0, col))
    sample = lambda width: pl.BlockSpec((tm, width), lambda i: (jnp.maximum(i - npt, 0), 0))
    rows = lambda width: pl.BlockSpec((tm, width), lambda i: (i, 0))
    const = lambda a: pl.BlockSpec(a.shape, lambda i: (0, 0), pipeline_mode=pl.Buffered(1))
    n_rows = n_p + pad_s
    return pl.pallas_call(
        functools.partial(_outproj_kernel, n_prompt_tiles=npt),
        grid=(n_rows // tm,),
        in_specs=[prompt(o_gla.shape[1]), prompt(o_rwkv.shape[1]), prompt(d), pmod(2), pmod(4), pmod(3),
                  const(norm_w), const(w_out), const(w_router), sample(d), sample(d), sample(LANES)],
        out_specs=[rows(d), rows(d), rows(LANES)],
        out_shape=[jax.ShapeDtypeStruct((n_rows, d), F32), jax.ShapeDtypeStruct((n_rows, d), F32),
                   jax.ShapeDtypeStruct((n_rows, LANES), F32)],
        compiler_params=_params("arbitrary"),
    )(o_gla, o_rwkv, xp, mod_p, mod_p, mod_p, norm_w, w_out, w_router, x1_s, h2_s, logits_s)


def _moe_kernel(tile_expert_ref, n_valid_ref, n_real_ref, src_ref, dst_ref, h2_hbm, wg_ref, wu_ref, wd_ref,
                y_hbm, xbuf, obuf, gather_sem, scatter_sem):
    del tile_expert_ref
    i = pl.program_id(0)
    n_tiles = pl.num_programs(0)
    n_valid = n_valid_ref[0]
    slot = i % 2

    def for_rows(n, fn):
        def group(g, carry):
            for u in range(DMA_UNROLL):
                fn(g * DMA_UNROLL + u)
            return carry

        def single(r, carry):
            fn(r)
            return carry
        full = n // DMA_UNROLL
        lax.fori_loop(0, full, group, 0)
        lax.fori_loop(full * DMA_UNROLL, n, single, 0)

    def gather_row(tile, s, r):
        tok = src_ref[tile * MOE_TILE + r]
        return pltpu.make_async_copy(h2_hbm.at[pl.ds(tok, 1), :], xbuf.at[s, pl.ds(r, 1), :], gather_sem.at[s])

    def scatter_row(tile, s, r):
        row = dst_ref[tile * MOE_TILE + r]
        return pltpu.make_async_copy(obuf.at[s, pl.ds(r, 1), :], y_hbm.at[pl.ds(row, 1), :], scatter_sem.at[s])

    def start_rows(row_copy, tile, s):
        for_rows(n_real_ref[tile], lambda r: row_copy(tile, s, r).start())

    def wait_rows(row_copy, whole_tile_copy, tile, s):
        n_real = n_real_ref[tile]

        @pl.when(n_real == MOE_TILE)
        def _():
            whole_tile_copy(s).wait()

        @pl.when(n_real < MOE_TILE)
        def _():
            for_rows(n_real, lambda r: row_copy(tile, s, r).wait())

    whole_gather = lambda s: pltpu.make_async_copy(h2_hbm.at[pl.ds(0, MOE_TILE), :], xbuf.at[s], gather_sem.at[s])
    whole_scatter = lambda s: pltpu.make_async_copy(obuf.at[s], y_hbm.at[pl.ds(0, MOE_TILE), :], scatter_sem.at[s])

    @pl.when(i == 0)
    def _():
        xbuf[...] = jnp.zeros_like(xbuf)
        start_rows(gather_row, 0, 0)

    @pl.when(i + 1 < n_valid)
    def _():
        start_rows(gather_row, i + 1, 1 - slot)

    @pl.when(i < n_valid)
    def _():
        wait_rows(gather_row, whole_gather, i, slot)

        @pl.when(i >= 2)
        def _():
            wait_rows(scatter_row, whole_scatter, i - 2, slot)

        x = xbuf[slot].astype(BF16)
        gate = _dot(x, wg_ref[0].astype(BF16))
        up = _dot(x, wu_ref[0].astype(BF16))
        hid = gate * jax.nn.sigmoid(gate) * up
        obuf[slot] = _dot(hid.astype(BF16), wd_ref[0].astype(BF16))
        start_rows(scatter_row, i, slot)

    @pl.when(i == n_tiles - 1)
    def _():
        @pl.when(n_valid >= 2)
        def _():
            wait_rows(scatter_row, whole_scatter, n_valid - 2, n_valid % 2)

        wait_rows(scatter_row, whole_scatter, n_valid - 1, (n_valid - 1) % 2)


def moe_experts(h2, plan, w_gate, w_up, w_down):
    n_rows, d = h2.shape
    _, _, ff = w_gate.shape
    n_tiles = plan["tile_expert"].shape[0]
    grid_spec = pltpu.PrefetchScalarGridSpec(
        num_scalar_prefetch=5,
        grid=(n_tiles,),
        in_specs=[pl.BlockSpec(memory_space=pl.ANY),
                  pl.BlockSpec((1, d, ff), lambda i, te, *_: (te[i], 0, 0)),
                  pl.BlockSpec((1, d, ff), lambda i, te, *_: (te[i], 0, 0)),
                  pl.BlockSpec((1, ff, d), lambda i, te, *_: (te[i], 0, 0))],
        out_specs=pl.BlockSpec(memory_space=pl.ANY),
        scratch_shapes=[pltpu.VMEM((2, MOE_TILE, d), F32), pltpu.VMEM((2, MOE_TILE, d), F32),
                        pltpu.SemaphoreType.DMA((2,)), pltpu.SemaphoreType.DMA((2,))],
    )
    return pl.pallas_call(
        _moe_kernel,
        grid_spec=grid_spec,
        out_shape=jax.ShapeDtypeStruct((2 * n_rows, d), F32),
        compiler_params=_params("arbitrary"),
    )(plan["tile_expert"], plan["n_valid"], plan["n_real"], plan["src_row"], plan["dst_row"], h2,
      w_gate, w_up, w_down)


def _route_kernel(lg_ref, idx_ref, wt_ref, cnt_ref, carry):
    i = pl.program_id(0)

    @pl.when(i == 0)
    def _():
        carry[...] = jnp.zeros_like(carry)

    lg = lg_ref[...]
    n, width = lg.shape
    lane = lax.broadcasted_iota(jnp.int32, (n, width), 1)
    row_max = lambda a: jnp.max(a, axis=1, keepdims=True)
    first_lane = lambda hit: jnp.min(jnp.where(hit, lane, width), axis=1, keepdims=True)

    is_group = lane < N_GROUPS
    g_exp = jnp.where(is_group, jnp.exp(lg - row_max(jnp.where(is_group, lg, -jnp.inf))), 0.0)
    p_group = g_exp / jnp.sum(g_exp, axis=1, keepdims=True)
    p_sel = row_max(p_group)
    g_sel = first_lane(is_group & (p_group == p_sel))
    lo = N_GROUPS + g_sel * EXPERTS_PER_GROUP
    in_group = (lane >= lo) & (lane < lo + EXPERTS_PER_GROUP)
    cand = jnp.where(in_group, lg, -jnp.inf)
    v1 = row_max(cand)
    l1 = first_lane(in_group & (cand == v1))
    cand2 = jnp.where(lane == l1, -jnp.inf, cand)
    v2 = row_max(cand2)
    l2 = first_lane(in_group & (lane != l1) & (cand2 == v2))
    t = jnp.exp(v2 - v1)
    w1 = p_sel / (1.0 + t)
    w2 = p_sel * t / (1.0 + t)
    e1, e2 = l1 - N_GROUPS, l2 - N_GROUPS

    hit1, hit2 = lane == e1, lane == e2
    both = (hit1 | hit2).astype(BF16)
    r_i = lax.broadcasted_iota(jnp.int32, (n, n), 0)
    c_i = lax.broadcasted_iota(jnp.int32, (n, n), 1)
    before = carry[0:1, :] + _dot((r_i > c_i).astype(BF16), both)
    rank1 = jnp.sum(jnp.where(hit1, before, 0.0), axis=1, keepdims=True).astype(jnp.int32)
    rank2 = jnp.sum(jnp.where(hit2, before, 0.0), axis=1, keepdims=True).astype(jnp.int32)
    carry[0:1, :] = carry[0:1, :] + jnp.sum(both.astype(F32), axis=0, keepdims=True)

    idx_ref[...] = jnp.where(lane == 0, e1, jnp.where(lane == 1, e2, jnp.where(lane == 2, rank1, rank2)))
    wt_ref[...] = jnp.where(lane == 0, w1, w2)

    @pl.when(i == pl.num_programs(0) - 1)
    def _():
        cnt_ref[...] = carry[...]


def route(logits):
    n_rows, width = logits.shape
    tile = ROW_TILE
    rows = pl.BlockSpec((tile, width), lambda i: (i, 0))
    idx, wt, cnt = pl.pallas_call(
        _route_kernel,
        grid=(n_rows // tile,),
        in_specs=[rows],
        out_specs=[rows, rows, pl.BlockSpec((SUBLANES, width), lambda i: (0, 0))],
        out_shape=[jax.ShapeDtypeStruct((n_rows, width), jnp.int32), jax.ShapeDtypeStruct((n_rows, width), F32),
                   jax.ShapeDtypeStruct((SUBLANES, width), F32)],
        scratch_shapes=[pltpu.VMEM((SUBLANES, width), F32)],
        compiler_params=_params("arbitrary"),
    )(logits)
    return idx[:, 0:2], idx[:, 2:4], wt[:, 0:2], cnt[0, :N_EXPERTS].astype(jnp.int32)


def _moe_plan(expert, rank, counts):
    n = expert.shape[0]
    e_flat = expert.reshape(-1)
    ids = jnp.arange(N_EXPERTS, dtype=jnp.int32)
    one_hot = (e_flat[:, None] == ids[None, :]).astype(jnp.int32)
    padded = (counts + MOE_TILE - 1) // MOE_TILE * MOE_TILE
    ends = jnp.cumsum(padded)
    starts = ends - padded
    pos = jnp.sum(one_hot * starts[None, :], axis=1) + rank.reshape(-1)
    p_max = _round_up(2 * n, MOE_TILE) + N_EXPERTS * MOE_TILE
    n_tiles = p_max // MOE_TILE
    pair = jnp.full((p_max,), -1, jnp.int32).at[pos].set(jnp.arange(2 * n, dtype=jnp.int32))
    pair0 = jnp.maximum(pair, 0)
    src_row = pair0 // 2
    dst_row = (pair0 % 2) * n + src_row
    tile_start = jnp.arange(n_tiles, dtype=jnp.int32) * MOE_TILE
    n_valid = ends[-1] // MOE_TILE
    tile_expert = jnp.sum((tile_start[:, None] >= ends[None, :]).astype(jnp.int32), axis=1)
    tile_expert = jnp.where(tile_start < ends[-1], tile_expert, tile_expert[n_valid - 1])
    n_real = jnp.clip((starts + counts)[tile_expert] - tile_start, 0, MOE_TILE)
    n_real = jnp.where(tile_start < ends[-1], n_real, 0).astype(jnp.int32)
    return dict(src_row=src_row, dst_row=dst_row, tile_expert=tile_expert,
                n_valid=n_valid.astype(jnp.int32)[None], n_real=n_real)


def _final_kernel(x1_ref, y0_ref, y1_ref, wt_ref, gtp_ref, gts_ref, scp_ref, scs_ref, shp_ref, shs_ref, nw_ref,
                  yp_ref, ys_ref, *, n_prompt_tiles):
    i = pl.program_id(0)

    def finish(gt, sc, sh):
        moe = wt_ref[:, 0:1] * y0_ref[...] + wt_ref[:, 1:2] * y1_ref[...]
        x2 = x1_ref[...] + gt * moe
        return _rmsnorm(x2, nw_ref[...], NORM_EPS) * (1.0 + sc) + sh

    @pl.when(i < n_prompt_tiles)
    def _():
        yp_ref[...] = finish(gtp_ref[0], scp_ref[0], shp_ref[0])

    @pl.when(i >= n_prompt_tiles)
    def _():
        ys_ref[...] = finish(gts_ref[...], scs_ref[...], shs_ref[...])


def final_norm(n_p, t, x1, y_pairs, w_top, mod_p, mod_s, modf_p, modf_s, norm_w):
    n_rows, d = x1.shape
    tile = ROW_TILE
    n_tiles, npt = n_rows // tile, n_p // tile
    per = t // tile
    rows = lambda off: pl.BlockSpec((tile, d), lambda i: (i + off, 0))
    prompt_out = pl.BlockSpec((tile, d), lambda i: (jnp.minimum(i, npt - 1), 0))
    sample = lambda col: pl.BlockSpec((tile, d), lambda i: (jnp.maximum(i - npt, 0), col))
    pmod = lambda col: pl.BlockSpec((1, 1, d), lambda i: (jnp.minimum(i // per, mod_p.shape[0] - 1), 0, col))
    return pl.pallas_call(
        functools.partial(_final_kernel, n_prompt_tiles=npt),
        grid=(n_tiles,),
        in_specs=[rows(0), rows(0), rows(n_tiles), pl.BlockSpec((tile, w_top.shape[1]), lambda i: (i, 0)),
                  pmod(5), sample(5), pmod(1), sample(1), pmod(0), sample(0),
                  pl.BlockSpec((1, d), lambda i: (0, 0))],
        out_specs=[prompt_out, sample(0)],
        out_shape=[jax.ShapeDtypeStruct((n_p, d), F32), jax.ShapeDtypeStruct((n_rows - n_p, d), F32)],
        compiler_params=_params("arbitrary"),
    )(x1, y_pairs, y_pairs, w_top, mod_p, mod_s, modf_p, modf_s, modf_p, modf_s, norm_w)


def _align_rows(wt):
    parts = []
    for name in _ALIGNED:
        off, w = SRC[name]
        part = wt[off:off + w]
        wa = DST[name][1]
        if wa != w:
            part = jnp.pad(part, ((0, wa - w), (0, 0)))
        parts.append(part)
    return jnp.concatenate(parts, axis=0)


def _rwkv_seg(a, name):
    off, w = SRC[name]
    return a[..., off - RWKV_SRC_BASE:off - RWKV_SRC_BASE + w]


def _rwkv_seg_padded(a, name):
    seg = _rwkv_seg(a, name)
    wa = DST[name][1]
    return jnp.pad(seg, [(0, 0)] * (a.ndim - 1) + [(0, wa - seg.shape[-1])])


def _orig_seg(p, name):
    off, w = SRC[name]
    return p[..., off:off + w]


def _pad_rows(w, rows):
    return jnp.pad(w, ((0, rows - w.shape[0]), (0, 0)))


def kernel(x_prompt, x_sample, state_gla, state_rwkv, state_shift, c_prompt, c_sample, w_ada, b_ada, norm_mix, norm_ffn, w_in, gla_gate_w2, gla_gate_b, gla_norm, rwkv_mu, rwkv_w0, rwkv_w2, rwkv_a0, rwkv_a2, rwkv_g2, rwkv_k_k, rwkv_k_a, rwkv_r_k, rwkv_gn_w, rwkv_gn_b, w_out, w_router_group, w_router_expert, w_exp_gate, w_exp_up, w_exp_down, norm_final, w_ada_final, b_ada_final):
    assert w_ada.shape[0] == 1, "single-layer step"
    bp, t, d = x_prompt.shape
    bs = x_sample.shape[0]
    assert x_sample.shape[1] == 1 and t % ROW_TILE == 0
    n_p = bp * t
    pad_s = _round_up(bs, ROW_TILE)
    n_rows = n_p + pad_s
    prompt = _PromptShape(bp, t, n_p)
    hi = lax.Precision.HIGHEST
    pad_sample = lambda a: jnp.pad(a, ((0, pad_s - a.shape[0]), (0, 0)))
    hdot = lambda a, w: jnp.dot(a, w, precision=hi)

    c_act = jax.nn.silu(jnp.concatenate([c_prompt, c_sample], axis=0))
    mod = matmul3(c_act, w_ada[0]) + b_ada[0]
    mod_f = matmul3(c_act, w_ada_final) + b_ada_final
    mod_p, mod_s = mod[:bp, None, :], mod[bp:]
    modf_p, modf_s = mod_f[:bp, None, :], mod_f[bp:]
    sh1_s, sc1_s, gt1_s, sh2_s, sc2_s, _ = jnp.split(mod_s, 6, axis=-1)
    xp = x_prompt.reshape(n_p, d)
    xs = x_sample[:, 0, :]

    w_in_t = jnp.swapaxes(w_in[0], 0, 1)
    proj = in_projection(xp, t, mod_p, norm_mix, _align_rows(w_in_t.astype(BF16)))
    o_gla, gla_t_p = gla_prompt(prompt, proj, _pad_rows(gla_gate_w2[0], LANES), gla_gate_b, gla_norm)
    new_gla_p = jnp.swapaxes(gla_t_p, -1, -2)
    mu = rwkv_mu[0]
    mu_big = jnp.concatenate([_rwkv_seg(mu, n) for n in ("r", "k7", "v7")])[None, :]
    mu_small = jnp.concatenate([_rwkv_seg_padded(mu, n) for n in ("wl", "al", "gl7")])[None, :]
    vecs = jnp.concatenate([rwkv_w0, rwkv_a0, rwkv_k_k, rwkv_k_a, rwkv_r_k[0].reshape(1, RWKV_WIDTH),
                            rwkv_gn_w, rwkv_gn_b, jnp.zeros((1, RWKV_WIDTH), F32)], axis=0)
    o_rwkv, rwkv_t_p = rwkv7_prompt(prompt, proj, mu_big, mu_small, vecs, _pad_rows(rwkv_w2[0], LANES),
                                    _pad_rows(rwkv_a2[0], LANES), rwkv_g2[0])
    new_rwkv_p = rwkv_t_p.reshape(bp, RWKV_HEAD, RWKV_HEADS, RWKV_HEAD).transpose(0, 2, 1, 3)
    last = jnp.stack([proj[(b + 1) * t - 1] for b in range(bp)])
    new_shift_p = jnp.concatenate([last[:, DST[n][0]:DST[n][0] + SRC[n][1]] for n in _RWKV_ORIG], axis=-1)

    h1_s = _rmsnorm(xs, norm_mix[0], NORM_EPS) * (1.0 + sc1_s) + sh1_s
    proj_s = matmul3(h1_s, w_in_t, w_transposed=True)
    heads = lambda a, n: a.reshape(a.shape[0], n, -1)
    logd_s = jax.nn.log_sigmoid(hdot(_orig_seg(proj_s, "gl"), gla_gate_w2[0]) + gla_gate_b[0]) \
        / GLA_GATE_NORMALIZER
    q_s = heads(_orig_seg(proj_s, "q") * (GLA_DK ** -0.5), GLA_HEADS)
    k_s, v_s, g_s = heads(_orig_seg(proj_s, "k"), GLA_HEADS), heads(_orig_seg(proj_s, "v"), GLA_HEADS), \
        heads(logd_s, GLA_HEADS)
    new_gla_s = jnp.exp(g_s)[..., None] * state_gla[0] + k_s[..., :, None] * v_s[..., None, :]
    o_s = jnp.einsum("bhd,bhde->bhe", q_s, new_gla_s, precision=hi)
    o_s = _rmsnorm(o_s, gla_norm[0], GLA_NORM_EPS) * heads(jax.nn.silu(_orig_seg(proj_s, "og")), GLA_HEADS)

    rp_s = proj_s[:, RWKV_SRC_BASE:]
    new_shift_s = rp_s
    xs7 = rp_s + (state_shift[0] - rp_s) * mu
    sx = lambda name: _rwkv_seg(xs7, name)
    r_s, k7_s, v7_s = sx("r"), sx("k7"), sx("v7")
    w_pre = rwkv_w0[0] + hdot(jnp.tanh(sx("wl")), rwkv_w2[0])
    decay_s = jnp.exp(-jnp.exp(-jax.nn.softplus(-w_pre) - 0.5))
    a_s = jax.nn.sigmoid(rwkv_a0[0] + hdot(sx("al"), rwkv_a2[0]))
    g_s7 = hdot(jax.nn.sigmoid(sx("gl7")), rwkv_g2[0])
    hs = lambda z: z.reshape(bs, RWKV_HEADS, RWKV_HEAD)
    kk_s = hs(k7_s * rwkv_k_k[0])
    kk_s = kk_s / jnp.maximum(jnp.sqrt(jnp.sum(kk_s * kk_s, axis=-1, keepdims=True)), 1e-12)
    k7_s = k7_s * (1.0 + (a_s - 1.0) * rwkv_k_a[0])
    r_h, w_h, k_h, v_h, a_h = hs(r_s), hs(decay_s), hs(k7_s), hs(v7_s), hs(a_s)
    s_prev = state_rwkv[0]
    sa = jnp.einsum("bhij,bhj->bhi", s_prev, -kk_s, precision=hi)
    new_rwkv_s = s_prev * w_h[:, :, None, :] + sa[..., None] * (kk_s * a_h)[:, :, None, :] \
        + v_h[..., None] * k_h[:, :, None, :]
    y_s = jnp.einsum("bhij,bhj->bhi", new_rwkv_s, r_h, precision=hi)
    y_mu = jnp.mean(y_s, axis=-1, keepdims=True)
    y_var = jnp.mean(jnp.square(y_s - y_mu), axis=-1, keepdims=True)
    y_s = (y_s - y_mu) * lax.rsqrt(y_var + RWKV_GN_EPS) * rwkv_gn_w[0].reshape(RWKV_HEADS, RWKV_HEAD) \
        + rwkv_gn_b[0].reshape(RWKV_HEADS, RWKV_HEAD)
    bonus = jnp.sum(r_h * k_h * rwkv_r_k[0], axis=-1, keepdims=True) * v_h
    o_rs = (y_s + bonus).reshape(bs, RWKV_WIDTH) * g_s7
    mix_s = matmul3(jnp.concatenate([o_s.reshape(bs, GLA_WIDTH), o_rs], axis=-1), w_out[0])
    x1_s = xs + gt1_s * mix_s
    h2_s = _rmsnorm(x1_s, norm_ffn[0], NORM_EPS) * (1.0 + sc2_s) + sh2_s
    w_router = jnp.pad(jnp.concatenate([w_router_group[0], w_router_expert[0]], axis=-1),
                       ((0, 0), (0, LANES - N_GROUPS - N_EXPERTS)))
    logits_s = hdot(h2_s, w_router)

    x1, h2, logits = out_projection(t, o_gla, o_rwkv, xp, mod_p, norm_ffn, w_out[0].astype(BF16), w_router,
                                    pad_sample(x1_s), pad_sample(h2_s), pad_sample(logits_s))

    expert, rank, w_top, counts = route(logits)
    plan = _moe_plan(expert, rank, counts)
    y_pairs = moe_experts(h2, plan, w_exp_gate[0].reshape(N_EXPERTS, d, EXPERT_FF),
                          w_exp_up[0].reshape(N_EXPERTS, d, EXPERT_FF),
                          w_exp_down[0].reshape(N_EXPERTS, EXPERT_FF, d))

    y_p, y_s_pad = final_norm(n_p, t, x1, y_pairs, w_top, mod_p, pad_sample(mod_s), modf_p,
                              pad_sample(modf_s), norm_final[None, :])
    return (y_p.reshape(bp, t, d), y_s_pad[:bs, None, :], new_gla_p[None], new_rwkv_p[None], new_shift_p[None],
            new_gla_s[None], new_rwkv_s[None], new_shift_s[None])
```

```python
import collections
import functools

import jax
import jax.numpy as jnp
from jax import lax
from jax.experimental import pallas as pl
from jax.experimental.pallas import tpu as pltpu

F32 = jnp.float32
BF16 = jnp.bfloat16

D_MODEL = 2048
GLA_HEADS = 4
GLA_DK = 128
GLA_DV = 256
GLA_KEY_WIDTH = GLA_HEADS * GLA_DK
GLA_WIDTH = GLA_HEADS * GLA_DV
GLA_GATE_NORMALIZER = 16.0
RWKV_HEAD = 64
RWKV_HEADS = 16
RWKV_WIDTH = RWKV_HEAD * RWKV_HEADS
N_GROUPS = 4
EXPERTS_PER_GROUP = 8
N_EXPERTS = N_GROUPS * EXPERTS_PER_GROUP
EXPERT_FF = 512
NORM_EPS = 1e-6
GLA_NORM_EPS = 1e-5
RWKV_GN_EPS = 64e-5

LANES = 128
SUBLANES = 8
VMEM_LIMIT_BYTES = 56 * 1024 * 1024

CHUNK = 64
SUB = 16
RWKV_GROUP = 2
GROUP_W = RWKV_GROUP * RWKV_HEAD
GLA_SEQS = 2
LOG2E = 1.4426950408889634
RWKV_SEQS = 4
MOE_TILE = 256
DMA_UNROLL = 8
SAMPLE_BLOCK = 8
ROW_TILE = 256
NORM_SLAB = 128
K_SPLIT = 256

_ORIG = (("q", 512), ("k", 512), ("v", 1024), ("gl", 16), ("og", 1024),
         ("r", 1024), ("wl", 64), ("k7", 1024), ("v7", 1024), ("al", 64), ("gl7", 128))
_ALIGNED = ("q", "k", "v", "og", "r", "k7", "v7", "gl", "wl", "al", "gl7")
_RWKV_ORIG = ("r", "wl", "k7", "v7", "al", "gl7")


def _round_up(n, m):
    return (n + m - 1) // m * m


def _layouts():
    src, off = {}, 0
    for name, w in _ORIG:
        src[name] = (off, w)
        off += w
    dst, pos = {}, 0
    for name in _ALIGNED:
        wa = _round_up(src[name][1], LANES)
        dst[name] = (pos, wa)
        pos += wa
    return src, off, dst, pos


SRC, IN_COLS, DST, IN_COLS_ALIGNED = _layouts()
RWKV_SRC_BASE = SRC["r"][0]
BIG_W = 3 * RWKV_WIDTH
SMALL_W = 3 * LANES
assert DST["r"][0] == BIG_W and DST["k7"][0] == BIG_W + RWKV_WIDTH and DST["v7"][0] == BIG_W + 2 * RWKV_WIDTH
assert DST["v"][0] == RWKV_WIDTH and DST["og"][0] == 2 * RWKV_WIDTH and DST["k"][0] == GLA_KEY_WIDTH

NN = ((1,), (0,))
NT = ((1,), (1,))

_PromptShape = collections.namedtuple("_PromptShape", "bp t rows")


def _dot(a, b, dims=NN):
    return lax.dot_general(a, b, (dims, ((), ())), preferred_element_type=F32)


def _split2(x):
    hi = x.astype(BF16)
    lo = (x - hi.astype(F32)).astype(BF16)
    return hi, lo


def _dotp(a, b, dims=NN, passes=1):
    if passes == 1:
        return _dot(a.astype(BF16), b.astype(BF16), dims)
    ah, al = _split2(a)
    bh, bl = _split2(b)
    return _dot(ah, bh, dims) + (_dot(ah, bl, dims) + _dot(al, bh, dims))


def _cumsum_rows(x):
    n = x.shape[0]
    row = lax.broadcasted_iota(jnp.int32, (n, n), 0)
    col = lax.broadcasted_iota(jnp.int32, (n, n), 1)
    tri = (row >= col).astype(BF16)
    x1 = x.astype(BF16)
    r1 = x - x1.astype(F32)
    x2 = r1.astype(BF16)
    x3 = (r1 - x2.astype(F32)).astype(BF16)
    return _dot(tri, x1) + (_dot(tri, x2) + _dot(tri, x3))


def _softplus(z):
    return jnp.maximum(z, 0.0) + jnp.log1p(jnp.exp(-jnp.abs(z)))


def _rmsnorm(x, g, eps):
    return x * lax.rsqrt(jnp.mean(x * x, axis=-1, keepdims=True) + eps) * g


def _largest_tile(n, cap, mult=SUBLANES):
    if n <= cap:
        return n
    best = None
    for t in range(mult, cap + 1, mult):
        if n % t == 0:
            best = t
    assert best is not None, (n, cap)
    return best


def _params(*sem):
    return pltpu.CompilerParams(dimension_semantics=sem, vmem_limit_bytes=VMEM_LIMIT_BYTES)


def _mm3_kernel(a_ref, w_ref, o_ref, *, dims):
    @pl.when(pl.program_id(0) == 0)
    def _():
        o_ref[...] = jnp.zeros_like(o_ref)

    o_ref[...] += _dotp(a_ref[...], w_ref[...], dims, 3)


def matmul3(a, w, w_transposed=False):
    m, k = a.shape
    n = w.shape[0] if w_transposed else w.shape[1]
    tk = _largest_tile(k, K_SPLIT, LANES)
    w_spec = pl.BlockSpec((n, tk), lambda s: (0, s)) if w_transposed else pl.BlockSpec((tk, n), lambda s: (s, 0))
    return pl.pallas_call(
        functools.partial(_mm3_kernel, dims=NT if w_transposed else NN),
        grid=(k // tk,),
        in_specs=[pl.BlockSpec((m, tk), lambda s: (0, s)), w_spec],
        out_specs=pl.BlockSpec((m, n), lambda s: (0, 0)),
        out_shape=jax.ShapeDtypeStruct((m, n), F32),
        compiler_params=_params("arbitrary"),
    )(a, w)


def _prompt_mod(tiles_per_batch, width, col):
    return pl.BlockSpec((1, 1, width), lambda i, *_: (i // tiles_per_batch, 0, col))


def _inproj_kernel(x_ref, sc_ref, sh_ref, nw_ref, w_ref, o_ref, h_scr):
    @pl.when(pl.program_id(1) == 0)
    def _():
        for r in range(0, h_scr.shape[0], NORM_SLAB):
            sl = slice(r, r + NORM_SLAB)
            h = _rmsnorm(x_ref[sl, :], nw_ref[...], NORM_EPS) * (1.0 + sc_ref[0]) + sh_ref[0]
            h_scr[sl, :] = h.astype(BF16)

    o_ref[...] = _dot(h_scr[...], w_ref[...], NT)


def in_projection(xp, t, mod_p, norm_w, w_aligned):
    n_p, d = xp.shape
    n = w_aligned.shape[0]
    tm = _largest_tile(t, 1024, NORM_SLAB)
    tn = _largest_tile(n, 512, LANES)
    per = t // tm
    return pl.pallas_call(
        _inproj_kernel,
        grid=(n_p // tm, n // tn),
        in_specs=[pl.BlockSpec((tm, d), lambda i, j: (i, 0)), _prompt_mod(per, d, 1), _prompt_mod(per, d, 0),
                  pl.BlockSpec((1, d), lambda i, j: (0, 0)), pl.BlockSpec((tn, d), lambda i, j: (j, 0))],
        out_specs=pl.BlockSpec((tm, tn), lambda i, j: (i, j)),
        out_shape=jax.ShapeDtypeStruct((n_p, n), F32),
        scratch_shapes=[pltpu.VMEM((tm, d), BF16)],
        compiler_params=_params("parallel", "arbitrary"),
    )(xp, mod_p, mod_p, norm_w, w_aligned)


def _rwkv_chunk_kernel(big_ref, wl_ref, al_ref, gl_ref, mu_big_ref, mu_small_ref, vec_ref, w2_ref, a2_ref,
                       g2_ref, o_ref, s_ref, state, carry_big, carry_small, *, passes):
    c_idx = pl.program_id(1)
    n_seq, n_tok = big_ref.shape[0], big_ref.shape[1]
    n_grp = RWKV_WIDTH // GROUP_W
    grp = [slice(g * GROUP_W, (g + 1) * GROUP_W) for g in range(n_grp)]

    @pl.when(c_idx == 0)
    def _():
        state[...] = jnp.zeros_like(state)
        carry_big[...] = jnp.zeros_like(carry_big)
        carry_small[...] = jnp.zeros_like(carry_small)

    first_row = lax.broadcasted_iota(jnp.int32, (n_tok, 1), 0) == 0
    w0, a0, k_k, k_a, r_k, gn_w, gn_b = (vec_ref[i:i + 1, :] for i in range(7))

    gi = lax.broadcasted_iota(jnp.int32, (GROUP_W, GROUP_W), 0) // RWKV_HEAD
    gj = lax.broadcasted_iota(jnp.int32, (GROUP_W, GROUP_W), 1) // RWKV_HEAD
    head_ones = (gi == gj).astype(BF16)

    def head_sums(parts):
        hi, lo = _split2(jnp.concatenate(parts, axis=0))
        out = _dot(hi, head_ones) + _dot(lo, head_ones)
        return [out[i * n_tok:(i + 1) * n_tok] for i in range(len(parts))]

    def prepare(n):
        def token_shift(cur, carry, mu):
            prev = jnp.where(first_row, carry[n, 0:1, :], pltpu.roll(cur, 1, 0))
            carry[n, 0:1, :] = cur[n_tok - 1:n_tok, :]
            return cur + (prev - cur) * mu

        xs_big = token_shift(big_ref[n], carry_big, mu_big_ref[...])
        small = jnp.concatenate([wl_ref[n], al_ref[n], gl_ref[n]], axis=1)
        xs_small = token_shift(small, carry_small, mu_small_ref[...])
        r = xs_big[:, :RWKV_WIDTH]
        k7 = xs_big[:, RWKV_WIDTH:2 * RWKV_WIDTH]
        v = xs_big[:, 2 * RWKV_WIDTH:]
        w_pre = w0 + _dotp(jnp.tanh(xs_small[:, :LANES]), w2_ref[...])
        lw = -jnp.exp(-_softplus(-w_pre) - 0.5)
        a = jax.nn.sigmoid(a0 + _dotp(xs_small[:, LANES:2 * LANES], a2_ref[...]))
        gate = _dotp(jax.nn.sigmoid(xs_small[:, 2 * LANES:]), g2_ref[...])
        kk_raw = k7 * k_k
        k = k7 * (1.0 + (a - 1.0) * k_a)
        sums = [head_sums([kk_raw[:, sl] * kk_raw[:, sl], r[:, sl] * k[:, sl] * r_k[:, sl]]) for sl in grp]
        kk = jnp.concatenate([kk_raw[:, sl] / jnp.maximum(jnp.sqrt(sums[g][0]), 1e-12)
                              for g, sl in enumerate(grp)], axis=1)
        bonus = jnp.concatenate([sums[g][1] for g in range(n_grp)], axis=1) * v
        cum = _cumsum_rows(lw)
        cum_end = cum[n_tok - 1:n_tok, :]
        beta = kk * a
        g_inv = jnp.exp(-cum)
        g_end = jnp.exp(cum_end - cum)
        return dict(v=v, gate=gate, bonus=bonus, g_tot=jnp.exp(cum_end),
                    a_hat=-kk * jnp.exp(cum - lw), r_hat=r * jnp.exp(cum), b_hat=beta * g_inv,
                    k_hat=k * g_inv, b_end=beta * g_end, k_end=k * g_end)

    seqs = [prepare(n) for n in range(n_seq)]
    units = [(n, sl) for n in range(n_seq) for sl in grp]
    part = lambda name: [seqs[n][name][:, sl] for n, sl in units]

    lane = lax.broadcasted_iota(jnp.int32, (n_tok, GROUP_W), 1)
    tok = lax.broadcasted_iota(jnp.int32, (n_tok, GROUP_W), 0)
    lane_head = lane // RWKV_HEAD
    src_tok = lane % RWKV_HEAD
    strict = tok > src_tok
    incl = tok >= src_tok

    def bd(y):
        return jnp.concatenate([jnp.where(lane_head == h, y, 0.0) for h in range(RWKV_GROUP)], axis=0)

    gs = range(len(units))
    s0 = [state[n, :, sl] for n, sl in units]
    v_u, b_hat, k_hat, b_end, k_end = part("v"), part("b_hat"), part("k_hat"), part("b_end"), part("k_end")
    lhs2 = [jnp.concatenate([a, r], axis=0) for a, r in zip(part("a_hat"), part("r_hat"))]
    abrb = [_dotp(lhs2[g], bd(b_hat[g]), NT, passes) for g in gs]
    akrk = [_dotp(lhs2[g], bd(k_hat[g]), NT, passes) for g in gs]
    asrs = [_dotp(lhs2[g], bd(s0[g]), NT, passes) for g in gs]
    p = [jnp.where(strict, abrb[g][:n_tok], 0.0) for g in gs]
    ak = [jnp.where(strict, akrk[g][:n_tok], 0.0) for g in gs]
    rb = [jnp.where(incl, abrb[g][n_tok:], 0.0) for g in gs]
    rk = [jnp.where(incl, akrk[g][n_tok:], 0.0) for g in gs]
    bd_v = [bd(v_u[g]) for g in gs]
    x = [asrs[g][:n_tok] + _dotp(ak[g], bd_v[g], NN, passes) for g in gs]
    n_sq = n_tok.bit_length() - 1
    for it in range(n_sq):
        if it < n_sq - 1:
            both = [_dotp(p[g], jnp.concatenate([bd(p[g]), bd(x[g])], axis=1), NN, passes) for g in gs]
            x = [x[g] + both[g][:, GROUP_W:] for g in gs]
            p = [both[g][:, :GROUP_W] for g in gs]
        else:
            x = [x[g] + _dotp(p[g], bd(x[g]), NN, passes) for g in gs]
    y = [asrs[g][n_tok:] + _dotp(jnp.concatenate([rb[g], rk[g]], axis=1),
                                 jnp.concatenate([bd(x[g]), bd_v[g]], axis=0), NN, passes) for g in gs]
    full = [_dotp(jnp.concatenate([x[g], v_u[g]], axis=0).T,
                  jnp.concatenate([b_end[g], k_end[g]], axis=0), NN, passes) for g in gs]
    g_tot = part("g_tot")
    for g, (n, sl) in enumerate(units):
        upd = s0[g] * g_tot[g]
        for h in range(RWKV_GROUP):
            upd = upd + jnp.where(lane_head == h, full[g][h * RWKV_HEAD:(h + 1) * RWKV_HEAD, :], 0.0)
        state[n, :, sl] = upd

    inv_n = 1.0 / RWKV_HEAD
    mean = [head_sums([y[g]])[0] * inv_n for g in gs]
    dev = [y[g] - mean[g] for g in gs]
    var = [head_sums([dev[g] * dev[g]])[0] * inv_n for g in gs]
    bonus, gate = part("bonus"), part("gate")
    for g, (n, sl) in enumerate(units):
        yn = dev[g] * lax.rsqrt(var[g] + RWKV_GN_EPS) * gn_w[:, sl] + gn_b[:, sl]
        o_ref[n, :, sl] = ((yn + bonus[g]) * gate[g]).astype(o_ref.dtype)

    @pl.when(c_idx == pl.num_programs(1) - 1)
    def _():
        s_ref[...] = state[...]


def rwkv7_prompt(rows, proj, mu_big, mu_small, vecs, w2, a2, g2, *, passes=1):
    assert CHUNK == RWKV_HEAD and rows.t % CHUNK == 0
    nc = rows.t // CHUNK
    n_seq = _largest_tile(rows.bp, RWKV_SEQS, 1)
    proj3 = proj.reshape(rows.bp, rows.t, proj.shape[1])
    small_col = lambda name: DST[name][0] // LANES
    tok = lambda width, col: pl.BlockSpec((n_seq, CHUNK, width), lambda b, c: (b, c, col))
    const = lambda shape: pl.BlockSpec(shape, lambda b, c: (0, 0))
    out, final_state = pl.pallas_call(
        functools.partial(_rwkv_chunk_kernel, passes=passes),
        grid=(rows.bp // n_seq, nc),
        in_specs=[tok(BIG_W, 1), tok(LANES, small_col("wl")), tok(LANES, small_col("al")),
                  tok(LANES, small_col("gl7")),
                  const((1, BIG_W)), const((1, SMALL_W)), const((SUBLANES, RWKV_WIDTH)),
                  const((LANES, RWKV_WIDTH)), const((LANES, RWKV_WIDTH)), const((LANES, RWKV_WIDTH))],
        out_specs=[tok(RWKV_WIDTH, 0),
                   pl.BlockSpec((n_seq, RWKV_HEAD, RWKV_WIDTH), lambda b, c: (b, 0, 0))],
        out_shape=[jax.ShapeDtypeStruct((rows.bp, rows.t, RWKV_WIDTH), BF16),
                   jax.ShapeDtypeStruct((rows.bp, RWKV_HEAD, RWKV_WIDTH), F32)],
        scratch_shapes=[pltpu.VMEM((n_seq, RWKV_HEAD, RWKV_WIDTH), F32),
                        pltpu.VMEM((n_seq, SUBLANES, BIG_W), F32), pltpu.VMEM((n_seq, SUBLANES, SMALL_W), F32)],
        compiler_params=_params("parallel", "arbitrary"),
    )(proj3, proj3, proj3, proj3, mu_big, mu_small, vecs, w2, a2, g2)
    return out.reshape(rows.rows, RWKV_WIDTH), final_state


def _gla_chunk_kernel(qk_ref, v_ref, og_ref, gl_ref, w2_ref, gb_ref, nw_ref, o_ref, s_ref, state):
    c_idx = pl.program_id(1)
    n_seq, n_tok = qk_ref.shape[0], qk_ref.shape[1]
    n_sub = n_tok // SUB

    @pl.when(c_idx == 0)
    def _():
        state[...] = jnp.zeros_like(state)

    row_k = lax.broadcasted_iota(jnp.int32, (n_tok, GLA_DK), 0)
    att_row = lax.broadcasted_iota(jnp.int32, (SUB, n_tok), 0)
    att_col = lax.broadcasted_iota(jnp.int32, (SUB, n_tok), 1)
    own_col = [jnp.where((att_col >= i * SUB) & (att_col - i * SUB <= att_row), att_col - i * SUB, -1)
               for i in range(n_sub)]

    def cum_log2_decay(n):
        logd = -_softplus(-(_dotp(gl_ref[n], w2_ref[...]) + gb_ref[...])) * (LOG2E / GLA_GATE_NORMALIZER)
        return _cumsum_rows(logd)

    units = [(n, h) for n in range(n_seq) for h in range(GLA_HEADS)]
    us = range(len(units))
    ks = lambda h: slice(h * GLA_DK, (h + 1) * GLA_DK)
    vs = lambda h: slice(h * GLA_DV, (h + 1) * GLA_DV)
    b_seq = [cum_log2_decay(n) for n in range(n_seq)]
    q = [qk_ref[n, :, ks(h)] * (GLA_DK ** -0.5) for n, h in units]
    k = [qk_ref[n, :, GLA_KEY_WIDTH + h * GLA_DK:GLA_KEY_WIDTH + (h + 1) * GLA_DK] for n, h in units]
    b = [b_seq[n][:, ks(h)] for n, h in units]
    v = [v_ref[n, :, vs(h)] for n, h in units]
    st = [state[n, h] for n, h in units]
    o_inter = [_dotp(q[u] * jnp.exp2(b[u]), st[u], NT) for u in us]
    blocks = [[] for _ in us]
    for i in range(n_sub):
        lo = i * SUB
        rows = slice(lo, lo + SUB)
        if i > 0:
            att = [_dotp(q[u][rows] * jnp.exp2(b[u][rows] - b[u][lo - 1:lo]),
                         jnp.where(row_k < lo, k[u] * jnp.exp2(b[u][lo - 1:lo] - b[u]), 0.0), NT) for u in us]
        else:
            att = [jnp.zeros((SUB, n_tok), F32) for _ in us]
        for j in range(SUB):
            tok = slice(lo + j, lo + j + 1)
            col = [jnp.sum(q[u][rows] * (k[u][tok] * jnp.exp2(b[u][rows] - b[u][tok])), axis=1, keepdims=True)
                   for u in us]
            att = [jnp.where(own_col[i] == j, col[u], att[u]) for u in us]
        for u in us:
            blocks[u].append(o_inter[u][rows] + _dotp(att[u], v[u], NN))
    for u, (n, h) in enumerate(units):
        o = jnp.concatenate(blocks[u], axis=0)
        og = og_ref[n, :, vs(h)]
        o_ref[n, :, vs(h)] = (_rmsnorm(o, nw_ref[...], GLA_NORM_EPS)
                              * (og * jax.nn.sigmoid(og))).astype(o_ref.dtype)
        b_last = b[u][n_tok - 1:n_tok, :]
        state[n, h] = st[u] * jnp.exp2(b_last) + _dotp(v[u].T, k[u] * jnp.exp2(b_last - b[u]), NN)

    @pl.when(c_idx == pl.num_programs(1) - 1)
    def _():
        s_ref[...] = state[...]


def gla_prompt(rows, proj, gate_w2, gate_b, norm_w):
    nc = rows.t // CHUNK
    n_seq = _largest_tile(rows.bp, GLA_SEQS, 1)
    proj3 = proj.reshape(rows.bp, rows.t, proj.shape[1])
    tok = lambda width, col: pl.BlockSpec((n_seq, CHUNK, width), lambda b, c: (b, c, col))
    const = lambda shape: pl.BlockSpec(shape, lambda b, c: (0, 0))
    out, final_state = pl.pallas_call(
        _gla_chunk_kernel,
        grid=(rows.bp // n_seq, nc),
        in_specs=[tok(GLA_WIDTH, 0), tok(GLA_WIDTH, 1), tok(GLA_WIDTH, 2), tok(LANES, DST["gl"][0] // LANES),
                  const((LANES, GLA_KEY_WIDTH)), const((1, GLA_KEY_WIDTH)), const((1, GLA_DV))],
        out_specs=[tok(GLA_WIDTH, 0),
                   pl.BlockSpec((n_seq, GLA_HEADS, GLA_DV, GLA_DK), lambda b, c: (b, 0, 0, 0))],
        out_shape=[jax.ShapeDtypeStruct((rows.bp, rows.t, GLA_WIDTH), BF16),
                   jax.ShapeDtypeStruct((rows.bp, GLA_HEADS, GLA_DV, GLA_DK), F32)],
        scratch_shapes=[pltpu.VMEM((n_seq, GLA_HEADS, GLA_DV, GLA_DK), F32)],
        compiler_params=_params("parallel", "arbitrary"),
    )(proj3, proj3, proj3, proj3, gate_w2, gate_b, norm_w)
    return out.reshape(rows.rows, GLA_WIDTH), final_state


def _columns(a, block):
    rows, heads, n = a.shape
    return a.reshape(rows // block, block, heads, n).transpose(0, 3, 1, 2).reshape(rows // block, n, block * heads)


def _gla_step_kernel(s_ref, qt_ref, kt_ref, gt_ref, v_ref, so_ref, o_ref):
    n_row, n_head = s_ref.shape[0], s_ref.shape[1]
    qt, kt, decay = qt_ref[0], kt_ref[0], jnp.exp(gt_ref[0])
    for b in range(n_row):
        for h in range(n_head):
            j = b * n_head + h
            s_new = decay[:, j:j + 1] * s_ref[b, h] + kt[:, j:j + 1] * v_ref[b, h:h + 1, :]
            so_ref[b, h] = s_new
            o_ref[b, h:h + 1, :] = jnp.sum(qt[:, j:j + 1] * s_new, axis=0, keepdims=True)


def gla_step(state, q, k, logd, v):
    rows, heads, dk, dv = state.shape
    blk = SAMPLE_BLOCK
    assert rows % blk == 0
    col = pl.BlockSpec((1, dk, blk * heads), lambda i: (i, 0, 0))
    s_spec = pl.BlockSpec((blk, heads, dk, dv), lambda i: (i, 0, 0, 0))
    v_spec = pl.BlockSpec((blk, heads, dv), lambda i: (i, 0, 0))
    return pl.pallas_call(
        _gla_step_kernel,
        grid=(rows // blk,),
        in_specs=[s_spec, col, col, col, v_spec],
        out_specs=[s_spec, v_spec],
        out_shape=[jax.ShapeDtypeStruct(state.shape, F32), jax.ShapeDtypeStruct((rows, heads, dv), F32)],
        compiler_params=_params("parallel"),
    )(state, _columns(q, blk), _columns(k, blk), _columns(logd, blk), v)


def _rwkv_step_kernel(s_ref, w_ref, kk_ref, ka_ref, k_ref, r_ref, ve_ref, vo_ref, so_ref, ye_ref, yo_ref):
    n_row, n_head, n_pair, width = s_ref.shape
    first = lax.broadcasted_iota(jnp.int32, (n_pair, width), 1) < RWKV_HEAD
    halves = lambda even, odd: jnp.where(first, even, odd)

    def half_sums(a):
        return (jnp.sum(jnp.where(first, a, 0.0), axis=1, keepdims=True),
                jnp.sum(jnp.where(first, 0.0, a), axis=1, keepdims=True))

    for b in range(n_row):
        for h in range(n_head):
            j = b * n_head + h
            row = lambda ref: ref[b, h:h + 1, :]
            s = s_ref[b, h]
            sa_e, sa_o = half_sums(s * -row(kk_ref))
            s_new = s * row(w_ref) + halves(sa_e, sa_o) * row(ka_ref) \
                + halves(ve_ref[0, :, j:j + 1], vo_ref[0, :, j:j + 1]) * row(k_ref)
            so_ref[b, h] = s_new
            y_e, y_o = half_sums(s_new * row(r_ref))
            ye_ref[0, :, j:j + 1] = y_e
            yo_ref[0, :, j:j + 1] = y_o


def rwkv_step(state, r, decay, k, v, kk, a):
    rows, heads, n, _ = state.shape
    blk = SAMPLE_BLOCK
    assert rows % blk == 0 and 2 * n == LANES
    twice = lambda z: jnp.concatenate([z, z], axis=-1)
    v_pairs = v.reshape(rows, heads, n // 2, 2)
    s_spec = pl.BlockSpec((blk, heads, n // 2, LANES), lambda i: (i, 0, 0, 0))
    row_spec = pl.BlockSpec((blk, heads, LANES), lambda i: (i, 0, 0))
    col_spec = pl.BlockSpec((1, n // 2, blk * heads), lambda i: (i, 0, 0))
    col_shape = jax.ShapeDtypeStruct((rows // blk, n // 2, blk * heads), F32)
    new_state, y_even, y_odd = pl.pallas_call(
        _rwkv_step_kernel,
        grid=(rows // blk,),
        in_specs=[s_spec] + [row_spec] * 5 + [col_spec] * 2,
        out_specs=[s_spec, col_spec, col_spec],
        out_shape=[jax.ShapeDtypeStruct((rows, heads, n // 2, LANES), F32), col_shape, col_shape],
        compiler_params=_params("parallel"),
    )(state.reshape(rows, heads, n // 2, LANES), twice(decay), twice(kk), twice(kk * a), twice(k), twice(r),
      _columns(v_pairs[..., 0], blk), _columns(v_pairs[..., 1], blk))
    uncol = lambda c: c.reshape(rows // blk, n // 2, blk, heads).transpose(0, 2, 3, 1).reshape(rows, heads, n // 2)
    y = jnp.stack([uncol(y_even), uncol(y_odd)], axis=-1).reshape(rows, heads, n)
    return new_state.reshape(state.shape), y


def _outproj_kernel(og_ref, orw_ref, x_ref, gt_ref, sc_ref, sh_ref, nw_ref, wo_ref, wr_ref,
                    x1s_ref, h2s_ref, lgs_ref, x1_ref, h2_ref, lg_ref, *, n_prompt_tiles):
    i = pl.program_id(0)

    @pl.when(i < n_prompt_tiles)
    def _():
        half = og_ref.shape[1]
        mix = _dot(og_ref[...], wo_ref[:half, :]) + _dot(orw_ref[...], wo_ref[half:, :])
        x1 = x_ref[...] + gt_ref[0] * mix
        h2 = _rmsnorm(x1, nw_ref[...], NORM_EPS) * (1.0 + sc_ref[0]) + sh_ref[0]
        x1_ref[...] = x1
        h2_ref[...] = h2
        lg_ref[...] = _dotp(h2, wr_ref[...], NN, 3)

    @pl.when(i >= n_prompt_tiles)
    def _():
        x1_ref[...] = x1s_ref[...]
        h2_ref[...] = h2s_ref[...]
        lg_ref[...] = lgs_ref[...]


def out_projection(t, o_gla, o_rwkv, xp, mod_p, norm_w, w_out, w_router, x1_s, h2_s, logits_s):
    n_p, d = xp.shape
    pad_s = x1_s.shape[0]
    tm = ROW_TILE
    npt, per = n_p // tm, t // tm
    prompt = lambda width: pl.BlockSpec((tm, width), lambda i: (jnp.minimum(i, npt - 1), 0))
    pmod = lambda col: pl.BlockSpec((1, 1, d), lambda i: (jnp.minimum(i // per, mod_p.shape[0] - 1), 0, col))
    sample = lambda width: pl.BlockSpec((tm, width), lambda i: (jnp.maximum(i - npt, 0), 0))
    rows = lambda width: pl.BlockSpec((tm, width), lambda i: (i, 0))
    const = lambda a: pl.BlockSpec(a.shape, lambda i: (0, 0), pipeline_mode=pl.Buffered(1))
    n_rows = n_p + pad_s
    return pl.pallas_call(
        functools.partial(_outproj_kernel, n_prompt_tiles=npt),
        grid=(n_rows // tm,),
        in_specs=[prompt(o_gla.shape[1]), prompt(o_rwkv.shape[1]), prompt(d), pmod(2), pmod(4), pmod(3),
                  const(norm_w), const(w_out), const(w_router), sample(d), sample(d), sample(LANES)],
        out_specs=[rows(d), rows(d), rows(LANES)],
        out_shape=[jax.ShapeDtypeStruct((n_rows, d), F32), jax.ShapeDtypeStruct((n_rows, d), F32),
                   jax.ShapeDtypeStruct((n_rows, LANES), F32)],
        compiler_params=_params("arbitrary"),
    )(o_gla, o_rwkv, xp, mod_p, mod_p, mod_p, norm_w, w_out, w_router, x1_s, h2_s, logits_s)


def _moe_kernel(tile_expert_ref, n_valid_ref, n_real_ref, src_ref, dst_ref, h2_hbm, wg_ref, wu_ref, wd_ref,
                y_hbm, xbuf, obuf, gather_sem, scatter_sem):
    del tile_expert_ref
    i = pl.program_id(0)
    n_tiles = pl.num_programs(0)
    n_valid = n_valid_ref[0]
    slot = i % 2

    def for_rows(n, fn):
        def group(g, carry):
            for u in range(DMA_UNROLL):
                fn(g * DMA_UNROLL + u)
            return carry

        def single(r, carry):
            fn(r)
            return carry
        full = n // DMA_UNROLL
        lax.fori_loop(0, full, group, 0)
        lax.fori_loop(full * DMA_UNROLL, n, single, 0)

    def gather_row(tile, s, r):
        tok = src_ref[tile * MOE_TILE + r]
        return pltpu.make_async_copy(h2_hbm.at[pl.ds(tok, 1), :], xbuf.at[s, pl.ds(r, 1), :], gather_sem.at[s])

    def scatter_row(tile, s, r):
        row = dst_ref[tile * MOE_TILE + r]
        return pltpu.make_async_copy(obuf.at[s, pl.ds(r, 1), :], y_hbm.at[pl.ds(row, 1), :], scatter_sem.at[s])

    def start_rows(row_copy, tile, s):
        for_rows(n_real_ref[tile], lambda r: row_copy(tile, s, r).start())

    def wait_rows(row_copy, whole_tile_copy, tile, s):
        n_real = n_real_ref[tile]

        @pl.when(n_real == MOE_TILE)
        def _():
            whole_tile_copy(s).wait()

        @pl.when(n_real < MOE_TILE)
        def _():
            for_rows(n_real, lambda r: row_copy(tile, s, r).wait())

    whole_gather = lambda s: pltpu.make_async_copy(h2_hbm.at[pl.ds(0, MOE_TILE), :], xbuf.at[s], gather_sem.at[s])
    whole_scatter = lambda s: pltpu.make_async_copy(obuf.at[s], y_hbm.at[pl.ds(0, MOE_TILE), :], scatter_sem.at[s])

    @pl.when(i == 0)
    def _():
        xbuf[...] = jnp.zeros_like(xbuf)
        start_rows(gather_row, 0, 0)

    @pl.when(i + 1 < n_valid)
    def _():
        start_rows(gather_row, i + 1, 1 - slot)

    @pl.when(i < n_valid)
    def _():
        wait_rows(gather_row, whole_gather, i, slot)

        @pl.when(i >= 2)
        def _():
            wait_rows(scatter_row, whole_scatter, i - 2, slot)

        x = xbuf[slot].astype(BF16)
        gate = _dot(x, wg_ref[0].astype(BF16))
        up = _dot(x, wu_ref[0].astype(BF16))
        hid = gate * jax.nn.sigmoid(gate) * up
        obuf[slot] = _dot(hid.astype(BF16), wd_ref[0].astype(BF16))
        start_rows(scatter_row, i, slot)

    @pl.when(i == n_tiles - 1)
    def _():
        @pl.when(n_valid >= 2)
        def _():
            wait_rows(scatter_row, whole_scatter, n_valid - 2, n_valid % 2)

        wait_rows(scatter_row, whole_scatter, n_valid - 1, (n_valid - 1) % 2)


def moe_experts(h2, plan, w_gate, w_up, w_down):
    n_rows, d = h2.shape
    _, _, ff = w_gate.shape
    n_tiles = plan["tile_expert"].shape[0]
    grid_spec = pltpu.PrefetchScalarGridSpec(
        num_scalar_prefetch=5,
        grid=(n_tiles,),
        in_specs=[pl.BlockSpec(memory_space=pl.ANY),
                  pl.BlockSpec((1, d, ff), lambda i, te, *_: (te[i], 0, 0)),
                  pl.BlockSpec((1, d, ff), lambda i, te, *_: (te[i], 0, 0)),
                  pl.BlockSpec((1, ff, d), lambda i, te, *_: (te[i], 0, 0))],
        out_specs=pl.BlockSpec(memory_space=pl.ANY),
        scratch_shapes=[pltpu.VMEM((2, MOE_TILE, d), F32), pltpu.VMEM((2, MOE_TILE, d), F32),
                        pltpu.SemaphoreType.DMA((2,)), pltpu.SemaphoreType.DMA((2,))],
    )
    return pl.pallas_call(
        _moe_kernel,
        grid_spec=grid_spec,
        out_shape=jax.ShapeDtypeStruct((2 * n_rows, d), F32),
        compiler_params=_params("arbitrary"),
    )(plan["tile_expert"], plan["n_valid"], plan["n_real"], plan["src_row"], plan["dst_row"], h2,
      w_gate, w_up, w_down)


def _route_kernel(lg_ref, idx_ref, wt_ref, cnt_ref, carry):
    i = pl.program_id(0)

    @pl.when(i == 0)
    def _():
        carry[...] = jnp.zeros_like(carry)

    lg = lg_ref[...]
    n, width = lg.shape
    lane = lax.broadcasted_iota(jnp.int32, (n, width), 1)
    row_max = lambda a: jnp.max(a, axis=1, keepdims=True)
    first_lane = lambda hit: jnp.min(jnp.where(hit, lane, width), axis=1, keepdims=True)

    is_group = lane < N_GROUPS
    g_exp = jnp.where(is_group, jnp.exp(lg - row_max(jnp.where(is_group, lg, -jnp.inf))), 0.0)
    p_group = g_exp / jnp.sum(g_exp, axis=1, keepdims=True)
    p_sel = row_max(p_group)
    g_sel = first_lane(is_group & (p_group == p_sel))
    lo = N_GROUPS + g_sel * EXPERTS_PER_GROUP
    in_group = (lane >= lo) & (lane < lo + EXPERTS_PER_GROUP)
    cand = jnp.where(in_group, lg, -jnp.inf)
    v1 = row_max(cand)
    l1 = first_lane(in_group & (cand == v1))
    cand2 = jnp.where(lane == l1, -jnp.inf, cand)
    v2 = row_max(cand2)
    l2 = first_lane(in_group & (lane != l1) & (cand2 == v2))
    t = jnp.exp(v2 - v1)
    w1 = p_sel / (1.0 + t)
    w2 = p_sel * t / (1.0 + t)
    e1, e2 = l1 - N_GROUPS, l2 - N_GROUPS

    hit1, hit2 = lane == e1, lane == e2
    both = (hit1 | hit2).astype(BF16)
    r_i = lax.broadcasted_iota(jnp.int32, (n, n), 0)
    c_i = lax.broadcasted_iota(jnp.int32, (n, n), 1)
    before = carry[0:1, :] + _dot((r_i > c_i).astype(BF16), both)
    rank1 = jnp.sum(jnp.where(hit1, before, 0.0), axis=1, keepdims=True).astype(jnp.int32)
    rank2 = jnp.sum(jnp.where(hit2, before, 0.0), axis=1, keepdims=True).astype(jnp.int32)
    carry[0:1, :] = carry[0:1, :] + jnp.sum(both.astype(F32), axis=0, keepdims=True)

    idx_ref[...] = jnp.where(lane == 0, e1, jnp.where(lane == 1, e2, jnp.where(lane == 2, rank1, rank2)))
    wt_ref[...] = jnp.where(lane == 0, w1, w2)

    @pl.when(i == pl.num_programs(0) - 1)
    def _():
        cnt_ref[...] = carry[...]


def route(logits):
    n_rows, width = logits.shape
    tile = ROW_TILE
    rows = pl.BlockSpec((tile, width), lambda i: (i, 0))
    idx, wt, cnt = pl.pallas_call(
        _route_kernel,
        grid=(n_rows // tile,),
        in_specs=[rows],
        out_specs=[rows, rows, pl.BlockSpec((SUBLANES, width), lambda i: (0, 0))],
        out_shape=[jax.ShapeDtypeStruct((n_rows, width), jnp.int32), jax.ShapeDtypeStruct((n_rows, width), F32),
                   jax.ShapeDtypeStruct((SUBLANES, width), F32)],
        scratch_shapes=[pltpu.VMEM((SUBLANES, width), F32)],
        compiler_params=_params("arbitrary"),
    )(logits)
    return idx[:, 0:2], idx[:, 2:4], wt[:, 0:2], cnt[0, :N_EXPERTS].astype(jnp.int32)


def _moe_plan(expert, rank, counts):
    n = expert.shape[0]
    e_flat = expert.reshape(-1)
    ids = jnp.arange(N_EXPERTS, dtype=jnp.int32)
    one_hot = (e_flat[:, None] == ids[None, :]).astype(jnp.int32)
    padded = (counts + MOE_TILE - 1) // MOE_TILE * MOE_TILE
    ends = jnp.cumsum(padded)
    starts = ends - padded
    pos = jnp.sum(one_hot * starts[None, :], axis=1) + rank.reshape(-1)
    p_max = _round_up(2 * n, MOE_TILE) + N_EXPERTS * MOE_TILE
    n_tiles = p_max // MOE_TILE
    pair = jnp.full((p_max,), -1, jnp.int32).at[pos].set(jnp.arange(2 * n, dtype=jnp.int32))
    pair0 = jnp.maximum(pair, 0)
    src_row = pair0 // 2
    dst_row = (pair0 % 2) * n + src_row
    tile_start = jnp.arange(n_tiles, dtype=jnp.int32) * MOE_TILE
    n_valid = ends[-1] // MOE_TILE
    tile_expert = jnp.sum((tile_start[:, None] >= ends[None, :]).astype(jnp.int32), axis=1)
    tile_expert = jnp.where(tile_start < ends[-1], tile_expert, tile_expert[n_valid - 1])
    n_real = jnp.clip((starts + counts)[tile_expert] - tile_start, 0, MOE_TILE)
    n_real = jnp.where(tile_start < ends[-1], n_real, 0).astype(jnp.int32)
    return dict(src_row=src_row, dst_row=dst_row, tile_expert=tile_expert,
                n_valid=n_valid.astype(jnp.int32)[None], n_real=n_real)


def _final_kernel(x1_ref, y0_ref, y1_ref, wt_ref, gtp_ref, gts_ref, scp_ref, scs_ref, shp_ref, shs_ref, nw_ref,
                  yp_ref, ys_ref, *, n_prompt_tiles):
    i = pl.program_id(0)

    def finish(gt, sc, sh):
        moe = wt_ref[:, 0:1] * y0_ref[...] + wt_ref[:, 1:2] * y1_ref[...]
        x2 = x1_ref[...] + gt * moe
        return _rmsnorm(x2, nw_ref[...], NORM_EPS) * (1.0 + sc) + sh

    @pl.when(i < n_prompt_tiles)
    def _():
        yp_ref[...] = finish(gtp_ref[0], scp_ref[0], shp_ref[0])

    @pl.when(i >= n_prompt_tiles)
    def _():
        ys_ref[...] = finish(gts_ref[...], scs_ref[...], shs_ref[...])


def final_norm(n_p, t, x1, y_pairs, w_top, mod_p, mod_s, modf_p, modf_s, norm_w):
    n_rows, d = x1.shape
    tile = ROW_TILE
    n_tiles, npt = n_rows // tile, n_p // tile
    per = t // tile
    rows = lambda off: pl.BlockSpec((tile, d), lambda i: (i + off, 0))
    prompt_out = pl.BlockSpec((tile, d), lambda i: (jnp.minimum(i, npt - 1), 0))
    sample = lambda col: pl.BlockSpec((tile, d), lambda i: (jnp.maximum(i - npt, 0), col))
    pmod = lambda col: pl.BlockSpec((1, 1, d), lambda i: (jnp.minimum(i // per, mod_p.shape[0] - 1), 0, col))
    return pl.pallas_call(
        functools.partial(_final_kernel, n_prompt_tiles=npt),
        grid=(n_tiles,),
        in_specs=[rows(0), rows(0), rows(n_tiles), pl.BlockSpec((tile, w_top.shape[1]), lambda i: (i, 0)),
                  pmod(5), sample(5), pmod(1), sample(1), pmod(0), sample(0),
                  pl.BlockSpec((1, d), lambda i: (0, 0))],
        out_specs=[prompt_out, sample(0)],
        out_shape=[jax.ShapeDtypeStruct((n_p, d), F32), jax.ShapeDtypeStruct((n_rows - n_p, d), F32)],
        compiler_params=_params("arbitrary"),
    )(x1, y_pairs, y_pairs, w_top, mod_p, mod_s, modf_p, modf_s, modf_p, modf_s, norm_w)


def _align_rows(wt):
    parts = []
    for name in _ALIGNED:
        off, w = SRC[name]
        part = wt[off:off + w]
        wa = DST[name][1]
        if wa != w:
            part = jnp.pad(part, ((0, wa - w), (0, 0)))
        parts.append(part)
    return jnp.concatenate(parts, axis=0)


def _rwkv_seg(a, name):
    off, w = SRC[name]
    return a[..., off - RWKV_SRC_BASE:off - RWKV_SRC_BASE + w]


def _rwkv_seg_padded(a, name):
    seg = _rwkv_seg(a, name)
    wa = DST[name][1]
    return jnp.pad(seg, [(0, 0)] * (a.ndim - 1) + [(0, wa - seg.shape[-1])])


def _orig_seg(p, name):
    off, w = SRC[name]
    return p[..., off:off + w]


def _pad_rows(w, rows):
    return jnp.pad(w, ((0, rows - w.shape[0]), (0, 0)))


def kernel(x_prompt, x_sample, state_gla, state_rwkv, state_shift, c_prompt, c_sample, w_ada, b_ada, norm_mix, norm_ffn, w_in, gla_gate_w2, gla_gate_b, gla_norm, rwkv_mu, rwkv_w0, rwkv_w2, rwkv_a0, rwkv_a2, rwkv_g2, rwkv_k_k, rwkv_k_a, rwkv_r_k, rwkv_gn_w, rwkv_gn_b, w_out, w_router_group, w_router_expert, w_exp_gate, w_exp_up, w_exp_down, norm_final, w_ada_final, b_ada_final):
    assert w_ada.shape[0] == 1, "single-layer step"
    bp, t, d = x_prompt.shape
    bs = x_sample.shape[0]
    assert x_sample.shape[1] == 1 and t % ROW_TILE == 0
    n_p = bp * t
    pad_s = _round_up(bs, ROW_TILE)
    n_rows = n_p + pad_s
    prompt = _PromptShape(bp, t, n_p)
    hi = lax.Precision.HIGHEST
    pad_sample = lambda a: jnp.pad(a, ((0, pad_s - a.shape[0]), (0, 0)))
    hdot = lambda a, w: jnp.dot(a, w, precision=hi)

    c_act = jax.nn.silu(jnp.concatenate([c_prompt, c_sample], axis=0))
    mod = matmul3(c_act, w_ada[0]) + b_ada[0]
    mod_f = matmul3(c_act, w_ada_final) + b_ada_final
    mod_p, mod_s = mod[:bp, None, :], mod[bp:]
    modf_p, modf_s = mod_f[:bp, None, :], mod_f[bp:]
    sh1_s, sc1_s, gt1_s, sh2_s, sc2_s, _ = jnp.split(mod_s, 6, axis=-1)
    xp = x_prompt.reshape(n_p, d)
    xs = x_sample[:, 0, :]

    w_in_t = jnp.swapaxes(w_in[0], 0, 1)
    proj = in_projection(xp, t, mod_p, norm_mix, _align_rows(w_in_t.astype(BF16)))
    o_gla, gla_t_p = gla_prompt(prompt, proj, _pad_rows(gla_gate_w2[0], LANES), gla_gate_b, gla_norm)
    new_gla_p = jnp.swapaxes(gla_t_p, -1, -2)
    mu = rwkv_mu[0]
    mu_big = jnp.concatenate([_rwkv_seg(mu, n) for n in ("r", "k7", "v7")])[None, :]
    mu_small = jnp.concatenate([_rwkv_seg_padded(mu, n) for n in ("wl", "al", "gl7")])[None, :]
    vecs = jnp.concatenate([rwkv_w0, rwkv_a0, rwkv_k_k, rwkv_k_a, rwkv_r_k[0].reshape(1, RWKV_WIDTH),
                            rwkv_gn_w, rwkv_gn_b, jnp.zeros((1, RWKV_WIDTH), F32)], axis=0)
    o_rwkv, rwkv_t_p = rwkv7_prompt(prompt, proj, mu_big, mu_small, vecs, _pad_rows(rwkv_w2[0], LANES),
                                    _pad_rows(rwkv_a2[0], LANES), rwkv_g2[0])
    new_rwkv_p = rwkv_t_p.reshape(bp, RWKV_HEAD, RWKV_HEADS, RWKV_HEAD).transpose(0, 2, 1, 3)
    last = jnp.stack([proj[(b + 1) * t - 1] for b in range(bp)])
    new_shift_p = jnp.concatenate([last[:, DST[n][0]:DST[n][0] + SRC[n][1]] for n in _RWKV_ORIG], axis=-1)

    h1_s = _rmsnorm(xs, norm_mix[0], NORM_EPS) * (1.0 + sc1_s) + sh1_s
    proj_s = matmul3(h1_s, w_in_t, w_transposed=True)
    heads = lambda a, n: a.reshape(a.shape[0], n, -1)
    logd_s = jax.nn.log_sigmoid(hdot(_orig_seg(proj_s, "gl"), gla_gate_w2[0]) + gla_gate_b[0]) \
        / GLA_GATE_NORMALIZER
    q_s = heads(_orig_seg(proj_s, "q") * (GLA_DK ** -0.5), GLA_HEADS)
    k_s, v_s, g_s = heads(_orig_seg(proj_s, "k"), GLA_HEADS), heads(_orig_seg(proj_s, "v"), GLA_HEADS), \
        heads(logd_s, GLA_HEADS)
    new_gla_s, o_s = gla_step(state_gla[0], q_s, k_s, g_s, v_s)
    o_s = _rmsnorm(o_s, gla_norm[0], GLA_NORM_EPS) * heads(jax.nn.silu(_orig_seg(proj_s, "og")), GLA_HEADS)

    rp_s = proj_s[:, RWKV_SRC_BASE:]
    new_shift_s = rp_s
    xs7 = rp_s + (state_shift[0] - rp_s) * mu
    sx = lambda name: _rwkv_seg(xs7, name)
    r_s, k7_s, v7_s = sx("r"), sx("k7"), sx("v7")
    w_pre = rwkv_w0[0] + hdot(jnp.tanh(sx("wl")), rwkv_w2[0])
    decay_s = jnp.exp(-jnp.exp(-jax.nn.softplus(-w_pre) - 0.5))
    a_s = jax.nn.sigmoid(rwkv_a0[0] + hdot(sx("al"), rwkv_a2[0]))
    g_s7 = hdot(jax.nn.sigmoid(sx("gl7")), rwkv_g2[0])
    hs = lambda z: z.reshape(bs, RWKV_HEADS, RWKV_HEAD)
    kk_s = hs(k7_s * rwkv_k_k[0])
    kk_s = kk_s / jnp.maximum(jnp.sqrt(jnp.sum(kk_s * kk_s, axis=-1, keepdims=True)), 1e-12)
    k7_s = k7_s * (1.0 + (a_s - 1.0) * rwkv_k_a[0])
    r_h, w_h, k_h, v_h, a_h = hs(r_s), hs(decay_s), hs(k7_s), hs(v7_s), hs(a_s)
    new_rwkv_s, y_s = rwkv_step(state_rwkv[0], r_h, w_h, k_h, v_h, kk_s, a_h)
    y_mu = jnp.mean(y_s, axis=-1, keepdims=True)
    y_var = jnp.mean(jnp.square(y_s - y_mu), axis=-1, keepdims=True)
    y_s = (y_s - y_mu) * lax.rsqrt(y_var + RWKV_GN_EPS) * rwkv_gn_w[0].reshape(RWKV_HEADS, RWKV_HEAD) \
        + rwkv_gn_b[0].reshape(RWKV_HEADS, RWKV_HEAD)
    bonus = jnp.sum(r_h * k_h * rwkv_r_k[0], axis=-1, keepdims=True) * v_h
    o_rs = (y_s + bonus).reshape(bs, RWKV_WIDTH) * g_s7
    mix_s = matmul3(jnp.concatenate([o_s.reshape(bs, GLA_WIDTH), o_rs], axis=-1), w_out[0])
    x1_s = xs + gt1_s * mix_s
    h2_s = _rmsnorm(x1_s, norm_ffn[0], NORM_EPS) * (1.0 + sc2_s) + sh2_s
    w_router = jnp.pad(jnp.concatenate([w_router_group[0], w_router_expert[0]], axis=-1),
                       ((0, 0), (0, LANES - N_GROUPS - N_EXPERTS)))
    logits_s = hdot(h2_s, w_router)

    x1, h2, logits = out_projection(t, o_gla, o_rwkv, xp, mod_p, norm_ffn, w_out[0].astype(BF16), w_router,
                                    pad_sample(x1_s), pad_sample(h2_s), pad_sample(logits_s))

    expert, rank, w_top, counts = route(logits)
    plan = _moe_plan(expert, rank, counts)
    y_pairs = moe_experts(h2, plan, w_exp_gate[0].reshape(N_EXPERTS, d, EXPERT_FF),
                          w_exp_up[0].reshape(N_EXPERTS, d, EXPERT_FF),
                          w_exp_down[0].reshape(N_EXPERTS, EXPERT_FF, d))

    y_p, y_s_pad = final_norm(n_p, t, x1, y_pairs, w_top, mod_p, pad_sample(mod_s), modf_p,
                              pad_sample(modf_s), norm_final[None, :])
    return (y_p.reshape(bp, t, d), y_s_pad[:bs, None, :], new_gla_p[None], new_rwkv_p[None], new_shift_p[None],
            new_gla_s[None], new_rwkv_s[None], new_shift_s[None])
```

```python
import collections
import functools

import jax
import jax.numpy as jnp
from jax import lax
from jax.experimental import pallas as pl
from jax.experimental.pallas import tpu as pltpu

F32 = jnp.float32
BF16 = jnp.bfloat16

D_MODEL = 2048
GLA_HEADS = 4
GLA_DK = 128
GLA_DV = 256
GLA_KEY_WIDTH = GLA_HEADS * GLA_DK
GLA_WIDTH = GLA_HEADS * GLA_DV
GLA_GATE_NORMALIZER = 16.0
RWKV_HEAD = 64
RWKV_HEADS = 16
RWKV_WIDTH = RWKV_HEAD * RWKV_HEADS
N_GROUPS = 4
EXPERTS_PER_GROUP = 8
N_EXPERTS = N_GROUPS * EXPERTS_PER_GROUP
EXPERT_FF = 512
NORM_EPS = 1e-6
GLA_NORM_EPS = 1e-5
RWKV_GN_EPS = 64e-5

LANES = 128
SUBLANES = 8
VMEM_LIMIT_BYTES = 56 * 1024 * 1024

CHUNK = 64
SUB = 16
RWKV_GROUP = 2
GROUP_W = RWKV_GROUP * RWKV_HEAD
GLA_SEQS = 2
LOG2E = 1.4426950408889634
RWKV_SEQS = 4
MOE_TILE = 256
DMA_UNROLL = 8
SAMPLE_BLOCK = 8
ROW_TILE = 256
NORM_SLAB = 128
K_SPLIT = 256

_ORIG = (("q", 512), ("k", 512), ("v", 1024), ("gl", 16), ("og", 1024),
         ("r", 1024), ("wl", 64), ("k7", 1024), ("v7", 1024), ("al", 64), ("gl7", 128))
_ALIGNED = ("q", "k", "v", "og", "r", "k7", "v7", "gl", "wl", "al", "gl7")
_RWKV_ORIG = ("r", "wl", "k7", "v7", "al", "gl7")


def _round_up(n, m):
    return (n + m - 1) // m * m


def _layouts():
    src, off = {}, 0
    for name, w in _ORIG:
        src[name] = (off, w)
        off += w
    dst, pos = {}, 0
    for name in _ALIGNED:
        wa = _round_up(src[name][1], LANES)
        dst[name] = (pos, wa)
        pos += wa
    return src, off, dst, pos


SRC, IN_COLS, DST, IN_COLS_ALIGNED = _layouts()
RWKV_SRC_BASE = SRC["r"][0]
BIG_W = 3 * RWKV_WIDTH
SMALL_W = 3 * LANES
assert DST["r"][0] == BIG_W and DST["k7"][0] == BIG_W + RWKV_WIDTH and DST["v7"][0] == BIG_W + 2 * RWKV_WIDTH
assert DST["v"][0] == RWKV_WIDTH and DST["og"][0] == 2 * RWKV_WIDTH and DST["k"][0] == GLA_KEY_WIDTH

NN = ((1,), (0,))
NT = ((1,), (1,))

_PromptShape = collections.namedtuple("_PromptShape", "bp t rows")


def _dot(a, b, dims=NN):
    return lax.dot_general(a, b, (dims, ((), ())), preferred_element_type=F32)


def _split2(x):
    hi = x.astype(BF16)
    lo = (x - hi.astype(F32)).astype(BF16)
    return hi, lo


def _dotp(a, b, dims=NN, passes=1):
    if passes == 1:
        return _dot(a.astype(BF16), b.astype(BF16), dims)
    ah, al = _split2(a)
    bh, bl = _split2(b)
    return _dot(ah, bh, dims) + (_dot(ah, bl, dims) + _dot(al, bh, dims))


def _cumsum_rows(x):
    n = x.shape[0]
    row = lax.broadcasted_iota(jnp.int32, (n, n), 0)
    col = lax.broadcasted_iota(jnp.int32, (n, n), 1)
    tri = (row >= col).astype(BF16)
    x1 = x.astype(BF16)
    r1 = x - x1.astype(F32)
    x2 = r1.astype(BF16)
    x3 = (r1 - x2.astype(F32)).astype(BF16)
    return _dot(tri, x1) + (_dot(tri, x2) + _dot(tri, x3))


def _softplus(z):
    return jnp.maximum(z, 0.0) + jnp.log1p(jnp.exp(-jnp.abs(z)))


def _rmsnorm(x, g, eps):
    return x * lax.rsqrt(jnp.mean(x * x, axis=-1, keepdims=True) + eps) * g


def _largest_tile(n, cap, mult=SUBLANES):
    if n <= cap:
        return n
    best = None
    for t in range(mult, cap + 1, mult):
        if n % t == 0:
            best = t
    assert best is not None, (n, cap)
    return best


def _params(*sem):
    return pltpu.CompilerParams(dimension_semantics=sem, vmem_limit_bytes=VMEM_LIMIT_BYTES)


def _mm3_kernel(a_ref, w_ref, o_ref, *, dims):
    @pl.when(pl.program_id(0) == 0)
    def _():
        o_ref[...] = jnp.zeros_like(o_ref)

    o_ref[...] += _dotp(a_ref[...], w_ref[...], dims, 3)


def matmul3(a, w, w_transposed=False):
    m, k = a.shape
    n = w.shape[0] if w_transposed else w.shape[1]
    tk = _largest_tile(k, K_SPLIT, LANES)
    w_spec = pl.BlockSpec((n, tk), lambda s: (0, s)) if w_transposed else pl.BlockSpec((tk, n), lambda s: (s, 0))
    return pl.pallas_call(
        functools.partial(_mm3_kernel, dims=NT if w_transposed else NN),
        grid=(k // tk,),
        in_specs=[pl.BlockSpec((m, tk), lambda s: (0, s)), w_spec],
        out_specs=pl.BlockSpec((m, n), lambda s: (0, 0)),
        out_shape=jax.ShapeDtypeStruct((m, n), F32),
        compiler_params=_params("arbitrary"),
    )(a, w)


def _prompt_mod(tiles_per_batch, width, col):
    return pl.BlockSpec((1, 1, width), lambda i, *_: (i // tiles_per_batch, 0, col))


def _inproj_kernel(x_ref, sc_ref, sh_ref, nw_ref, w_ref, o_ref, h_scr):
    @pl.when(pl.program_id(1) == 0)
    def _():
        for r in range(0, h_scr.shape[0], NORM_SLAB):
            sl = slice(r, r + NORM_SLAB)
            h = _rmsnorm(x_ref[sl, :], nw_ref[...], NORM_EPS) * (1.0 + sc_ref[0]) + sh_ref[0]
            h_scr[sl, :] = h.astype(BF16)

    o_ref[...] = _dot(h_scr[...], w_ref[...], NT)


def in_projection(xp, t, mod_p, norm_w, w_aligned):
    n_p, d = xp.shape
    n = w_aligned.shape[0]
    tm = _largest_tile(t, 1024, NORM_SLAB)
    tn = _largest_tile(n, 512, LANES)
    per = t // tm
    return pl.pallas_call(
        _inproj_kernel,
        grid=(n_p // tm, n // tn),
        in_specs=[pl.BlockSpec((tm, d), lambda i, j: (i, 0)), _prompt_mod(per, d, 1), _prompt_mod(per, d, 0),
                  pl.BlockSpec((1, d), lambda i, j: (0, 0)), pl.BlockSpec((tn, d), lambda i, j: (j, 0))],
        out_specs=pl.BlockSpec((tm, tn), lambda i, j: (i, j)),
        out_shape=jax.ShapeDtypeStruct((n_p, n), F32),
        scratch_shapes=[pltpu.VMEM((tm, d), BF16)],
        compiler_params=_params("parallel", "arbitrary"),
    )(xp, mod_p, mod_p, norm_w, w_aligned)


def _rwkv_chunk_kernel(big_ref, wl_ref, al_ref, gl_ref, mu_big_ref, mu_small_ref, vec_ref, w2_ref, a2_ref,
                       g2_ref, o_ref, s_ref, state, carry_big, carry_small, *, passes):
    c_idx = pl.program_id(1)
    n_seq, n_tok = big_ref.shape[0], big_ref.shape[1]
    n_grp = RWKV_WIDTH // GROUP_W
    grp = [slice(g * GROUP_W, (g + 1) * GROUP_W) for g in range(n_grp)]

    @pl.when(c_idx == 0)
    def _():
        state[...] = jnp.zeros_like(state)
        carry_big[...] = jnp.zeros_like(carry_big)
        carry_small[...] = jnp.zeros_like(carry_small)

    first_row = lax.broadcasted_iota(jnp.int32, (n_tok, 1), 0) == 0
    w0, a0, k_k, k_a, r_k, gn_w, gn_b = (vec_ref[i:i + 1, :] for i in range(7))

    gi = lax.broadcasted_iota(jnp.int32, (GROUP_W, GROUP_W), 0) // RWKV_HEAD
    gj = lax.broadcasted_iota(jnp.int32, (GROUP_W, GROUP_W), 1) // RWKV_HEAD
    head_ones = (gi == gj).astype(BF16)

    def head_sums(parts):
        hi, lo = _split2(jnp.concatenate(parts, axis=0))
        out = _dot(hi, head_ones) + _dot(lo, head_ones)
        return [out[i * n_tok:(i + 1) * n_tok] for i in range(len(parts))]

    def prepare(n):
        def token_shift(cur, carry, mu):
            prev = jnp.where(first_row, carry[n, 0:1, :], pltpu.roll(cur, 1, 0))
            carry[n, 0:1, :] = cur[n_tok - 1:n_tok, :]
            return cur + (prev - cur) * mu

        xs_big = token_shift(big_ref[n], carry_big, mu_big_ref[...])
        small = jnp.concatenate([wl_ref[n], al_ref[n], gl_ref[n]], axis=1)
        xs_small = token_shift(small, carry_small, mu_small_ref[...])
        r = xs_big[:, :RWKV_WIDTH]
        k7 = xs_big[:, RWKV_WIDTH:2 * RWKV_WIDTH]
        v = xs_big[:, 2 * RWKV_WIDTH:]
        w_pre = w0 + _dotp(jnp.tanh(xs_small[:, :LANES]), w2_ref[...])
        lw = -jnp.exp(-_softplus(-w_pre) - 0.5)
        a = jax.nn.sigmoid(a0 + _dotp(xs_small[:, LANES:2 * LANES], a2_ref[...]))
        gate = _dotp(jax.nn.sigmoid(xs_small[:, 2 * LANES:]), g2_ref[...])
        kk_raw = k7 * k_k
        k = k7 * (1.0 + (a - 1.0) * k_a)
        sums = [head_sums([kk_raw[:, sl] * kk_raw[:, sl], r[:, sl] * k[:, sl] * r_k[:, sl]]) for sl in grp]
        kk = jnp.concatenate([kk_raw[:, sl] / jnp.maximum(jnp.sqrt(sums[g][0]), 1e-12)
                              for g, sl in enumerate(grp)], axis=1)
        bonus = jnp.concatenate([sums[g][1] for g in range(n_grp)], axis=1) * v
        cum = _cumsum_rows(lw)
        cum_end = cum[n_tok - 1:n_tok, :]
        beta = kk * a
        g_inv = jnp.exp(-cum)
        g_end = jnp.exp(cum_end - cum)
        return dict(v=v, gate=gate, bonus=bonus, g_tot=jnp.exp(cum_end),
                    a_hat=-kk * jnp.exp(cum - lw), r_hat=r * jnp.exp(cum), b_hat=beta * g_inv,
                    k_hat=k * g_inv, b_end=beta * g_end, k_end=k * g_end)

    seqs = [prepare(n) for n in range(n_seq)]
    units = [(n, sl) for n in range(n_seq) for sl in grp]
    part = lambda name: [seqs[n][name][:, sl] for n, sl in units]

    lane = lax.broadcasted_iota(jnp.int32, (n_tok, GROUP_W), 1)
    tok = lax.broadcasted_iota(jnp.int32, (n_tok, GROUP_W), 0)
    lane_head = lane // RWKV_HEAD
    src_tok = lane % RWKV_HEAD
    strict = tok > src_tok
    incl = tok >= src_tok

    def bd(y):
        return jnp.concatenate([jnp.where(lane_head == h, y, 0.0) for h in range(RWKV_GROUP)], axis=0)

    gs = range(len(units))
    s0 = [state[n, :, sl] for n, sl in units]
    v_u, b_hat, k_hat, b_end, k_end = part("v"), part("b_hat"), part("k_hat"), part("b_end"), part("k_end")
    lhs2 = [jnp.concatenate([a, r], axis=0) for a, r in zip(part("a_hat"), part("r_hat"))]
    abrb = [_dotp(lhs2[g], bd(b_hat[g]), NT, passes) for g in gs]
    akrk = [_dotp(lhs2[g], bd(k_hat[g]), NT, passes) for g in gs]
    asrs = [_dotp(lhs2[g], bd(s0[g]), NT, passes) for g in gs]
    p = [jnp.where(strict, abrb[g][:n_tok], 0.0) for g in gs]
    ak = [jnp.where(strict, akrk[g][:n_tok], 0.0) for g in gs]
    rb = [jnp.where(incl, abrb[g][n_tok:], 0.0) for g in gs]
    rk = [jnp.where(incl, akrk[g][n_tok:], 0.0) for g in gs]
    bd_v = [bd(v_u[g]) for g in gs]
    x = [asrs[g][:n_tok] + _dotp(ak[g], bd_v[g], NN, passes) for g in gs]
    n_sq = n_tok.bit_length() - 1
    for it in range(n_sq):
        if it < n_sq - 1:
            both = [_dotp(p[g], jnp.concatenate([bd(p[g]), bd(x[g])], axis=1), NN, passes) for g in gs]
            x = [x[g] + both[g][:, GROUP_W:] for g in gs]
            p = [both[g][:, :GROUP_W] for g in gs]
        else:
            x = [x[g] + _dotp(p[g], bd(x[g]), NN, passes) for g in gs]
    y = [asrs[g][n_tok:] + _dotp(jnp.concatenate([rb[g], rk[g]], axis=1),
                                 jnp.concatenate([bd(x[g]), bd_v[g]], axis=0), NN, passes) for g in gs]
    full = [_dotp(jnp.concatenate([x[g], v_u[g]], axis=0).T,
                  jnp.concatenate([b_end[g], k_end[g]], axis=0), NN, passes) for g in gs]
    g_tot = part("g_tot")
    for g, (n, sl) in enumerate(units):
        upd = s0[g] * g_tot[g]
        for h in range(RWKV_GROUP):
            upd = upd + jnp.where(lane_head == h, full[g][h * RWKV_HEAD:(h + 1) * RWKV_HEAD, :], 0.0)
        state[n, :, sl] = upd

    inv_n = 1.0 / RWKV_HEAD
    mean = [head_sums([y[g]])[0] * inv_n for g in gs]
    dev = [y[g] - mean[g] for g in gs]
    var = [head_sums([dev[g] * dev[g]])[0] * inv_n for g in gs]
    bonus, gate = part("bonus"), part("gate")
    for g, (n, sl) in enumerate(units):
        yn = dev[g] * lax.rsqrt(var[g] + RWKV_GN_EPS) * gn_w[:, sl] + gn_b[:, sl]
        o_ref[n, :, sl] = ((yn + bonus[g]) * gate[g]).astype(o_ref.dtype)

    @pl.when(c_idx == pl.num_programs(1) - 1)
    def _():
        s_ref[...] = state[...]


def rwkv7_prompt(rows, proj, mu_big, mu_small, vecs, w2, a2, g2, *, passes=1):
    assert CHUNK == RWKV_HEAD and rows.t % CHUNK == 0
    nc = rows.t // CHUNK
    n_seq = _largest_tile(rows.bp, RWKV_SEQS, 1)
    proj3 = proj.reshape(rows.bp, rows.t, proj.shape[1])
    small_col = lambda name: DST[name][0] // LANES
    tok = lambda width, col: pl.BlockSpec((n_seq, CHUNK, width), lambda b, c: (b, c, col))
    const = lambda shape: pl.BlockSpec(shape, lambda b, c: (0, 0))
    out, final_state = pl.pallas_call(
        functools.partial(_rwkv_chunk_kernel, passes=passes),
        grid=(rows.bp // n_seq, nc),
        in_specs=[tok(BIG_W, 1), tok(LANES, small_col("wl")), tok(LANES, small_col("al")),
                  tok(LANES, small_col("gl7")),
                  const((1, BIG_W)), const((1, SMALL_W)), const((SUBLANES, RWKV_WIDTH)),
                  const((LANES, RWKV_WIDTH)), const((LANES, RWKV_WIDTH)), const((LANES, RWKV_WIDTH))],
        out_specs=[tok(RWKV_WIDTH, 0),
                   pl.BlockSpec((n_seq, RWKV_HEAD, RWKV_WIDTH), lambda b, c: (b, 0, 0))],
        out_shape=[jax.ShapeDtypeStruct((rows.bp, rows.t, RWKV_WIDTH), BF16),
                   jax.ShapeDtypeStruct((rows.bp, RWKV_HEAD, RWKV_WIDTH), F32)],
        scratch_shapes=[pltpu.VMEM((n_seq, RWKV_HEAD, RWKV_WIDTH), F32),
                        pltpu.VMEM((n_seq, SUBLANES, BIG_W), F32), pltpu.VMEM((n_seq, SUBLANES, SMALL_W), F32)],
        compiler_params=_params("parallel", "arbitrary"),
    )(proj3, proj3, proj3, proj3, mu_big, mu_small, vecs, w2, a2, g2)
    return out.reshape(rows.rows, RWKV_WIDTH), final_state


def _gla_chunk_kernel(qk_ref, v_ref, og_ref, gl_ref, w2_ref, gb_ref, nw_ref, o_ref, s_ref, state):
    c_idx = pl.program_id(1)
    n_seq, n_tok = qk_ref.shape[0], qk_ref.shape[1]
    n_sub = n_tok // SUB

    @pl.when(c_idx == 0)
    def _():
        state[...] = jnp.zeros_like(state)

    row_k = lax.broadcasted_iota(jnp.int32, (n_tok, GLA_DK), 0)
    att_row = lax.broadcasted_iota(jnp.int32, (SUB, n_tok), 0)
    att_col = lax.broadcasted_iota(jnp.int32, (SUB, n_tok), 1)
    own_col = [jnp.where((att_col >= i * SUB) & (att_col - i * SUB <= att_row), att_col - i * SUB, -1)
               for i in range(n_sub)]

    def cum_log2_decay(n):
        logd = -_softplus(-(_dotp(gl_ref[n], w2_ref[...]) + gb_ref[...])) * (LOG2E / GLA_GATE_NORMALIZER)
        return _cumsum_rows(logd)

    units = [(n, h) for n in range(n_seq) for h in range(GLA_HEADS)]
    us = range(len(units))
    ks = lambda h: slice(h * GLA_DK, (h + 1) * GLA_DK)
    vs = lambda h: slice(h * GLA_DV, (h + 1) * GLA_DV)
    b_seq = [cum_log2_decay(n) for n in range(n_seq)]
    q = [qk_ref[n, :, ks(h)] * (GLA_DK ** -0.5) for n, h in units]
    k = [qk_ref[n, :, GLA_KEY_WIDTH + h * GLA_DK:GLA_KEY_WIDTH + (h + 1) * GLA_DK] for n, h in units]
    b = [b_seq[n][:, ks(h)] for n, h in units]
    v = [v_ref[n, :, vs(h)] for n, h in units]
    st = [state[n, h] for n, h in units]
    o_inter = [_dotp(q[u] * jnp.exp2(b[u]), st[u], NT) for u in us]
    blocks = [[] for _ in us]
    for i in range(n_sub):
        lo = i * SUB
        rows = slice(lo, lo + SUB)
        if i > 0:
            att = [_dotp(q[u][rows] * jnp.exp2(b[u][rows] - b[u][lo - 1:lo]),
                         jnp.where(row_k < lo, k[u] * jnp.exp2(b[u][lo - 1:lo] - b[u]), 0.0), NT) for u in us]
        else:
            att = [jnp.zeros((SUB, n_tok), F32) for _ in us]
        for j in range(SUB):
            tok = slice(lo + j, lo + j + 1)
            col = [jnp.sum(q[u][rows] * (k[u][tok] * jnp.exp2(b[u][rows] - b[u][tok])), axis=1, keepdims=True)
                   for u in us]
            att = [jnp.where(own_col[i] == j, col[u], att[u]) for u in us]
        for u in us:
            blocks[u].append(o_inter[u][rows] + _dotp(att[u], v[u], NN))
    for u, (n, h) in enumerate(units):
        o = jnp.concatenate(blocks[u], axis=0)
        og = og_ref[n, :, vs(h)]
        o_ref[n, :, vs(h)] = (_rmsnorm(o, nw_ref[...], GLA_NORM_EPS)
                              * (og * jax.nn.sigmoid(og))).astype(o_ref.dtype)
        b_last = b[u][n_tok - 1:n_tok, :]
        state[n, h] = st[u] * jnp.exp2(b_last) + _dotp(v[u].T, k[u] * jnp.exp2(b_last - b[u]), NN)

    @pl.when(c_idx == pl.num_programs(1) - 1)
    def _():
        s_ref[...] = state[...]


def gla_prompt(rows, proj, gate_w2, gate_b, norm_w):
    nc = rows.t // CHUNK
    n_seq = _largest_tile(rows.bp, GLA_SEQS, 1)
    proj3 = proj.reshape(rows.bp, rows.t, proj.shape[1])
    tok = lambda width, col: pl.BlockSpec((n_seq, CHUNK, width), lambda b, c: (b, c, col))
    const = lambda shape: pl.BlockSpec(shape, lambda b, c: (0, 0))
    out, final_state = pl.pallas_call(
        _gla_chunk_kernel,
        grid=(rows.bp // n_seq, nc),
        in_specs=[tok(GLA_WIDTH, 0), tok(GLA_WIDTH, 1), tok(GLA_WIDTH, 2), tok(LANES, DST["gl"][0] // LANES),
                  const((LANES, GLA_KEY_WIDTH)), const((1, GLA_KEY_WIDTH)), const((1, GLA_DV))],
        out_specs=[tok(GLA_WIDTH, 0),
                   pl.BlockSpec((n_seq, GLA_HEADS, GLA_DV, GLA_DK), lambda b, c: (b, 0, 0, 0))],
        out_shape=[jax.ShapeDtypeStruct((rows.bp, rows.t, GLA_WIDTH), BF16),
                   jax.ShapeDtypeStruct((rows.bp, GLA_HEADS, GLA_DV, GLA_DK), F32)],
        scratch_shapes=[pltpu.VMEM((n_seq, GLA_HEADS, GLA_DV, GLA_DK), F32)],
        compiler_params=_params("parallel", "arbitrary"),
    )(proj3, proj3, proj3, proj3, gate_w2, gate_b, norm_w)
    return out.reshape(rows.rows, GLA_WIDTH), final_state


def _columns(a, block):
    rows, heads, n = a.shape
    return a.reshape(rows // block, block, heads, n).transpose(0, 3, 1, 2).reshape(rows // block, n, block * heads)


def _gla_step_kernel(s_ref, qt_ref, kt_ref, gt_ref, v_ref, so_ref, o_ref):
    n_row, n_head = s_ref.shape[0], s_ref.shape[1]
    qt, kt, decay = qt_ref[0], kt_ref[0], jnp.exp(gt_ref[0])
    for b in range(n_row):
        for h in range(n_head):
            j = b * n_head + h
            s_new = decay[:, j:j + 1] * s_ref[b, h] + kt[:, j:j + 1] * v_ref[b, h:h + 1, :]
            so_ref[b, h] = s_new
            o_ref[b, h:h + 1, :] = jnp.sum(qt[:, j:j + 1] * s_new, axis=0, keepdims=True)


def gla_step(state, q, k, logd, v):
    rows, heads, dk, dv = state.shape
    blk = SAMPLE_BLOCK
    assert rows % blk == 0
    col = pl.BlockSpec((1, dk, blk * heads), lambda i: (i, 0, 0))
    s_spec = pl.BlockSpec((blk, heads, dk, dv), lambda i: (i, 0, 0, 0))
    v_spec = pl.BlockSpec((blk, heads, dv), lambda i: (i, 0, 0))
    return pl.pallas_call(
        _gla_step_kernel,
        grid=(rows // blk,),
        in_specs=[s_spec, col, col, col, v_spec],
        out_specs=[s_spec, v_spec],
        out_shape=[jax.ShapeDtypeStruct(state.shape, F32), jax.ShapeDtypeStruct((rows, heads, dv), F32)],
        compiler_params=_params("parallel"),
    )(state, _columns(q, blk), _columns(k, blk), _columns(logd, blk), v)


def _outproj_kernel(og_ref, orw_ref, x_ref, gt_ref, sc_ref, sh_ref, nw_ref, wo_ref, wr_ref,
                    x1s_ref, h2s_ref, lgs_ref, x1_ref, h2_ref, lg_ref, *, n_prompt_tiles):
    i = pl.program_id(0)

    @pl.when(i < n_prompt_tiles)
    def _():
        half = og_ref.shape[1]
        mix = _dot(og_ref[...], wo_ref[:half, :]) + _dot(orw_ref[...], wo_ref[half:, :])
        x1 = x_ref[...] + gt_ref[0] * mix
        h2 = _rmsnorm(x1, nw_ref[...], NORM_EPS) * (1.0 + sc_ref[0]) + sh_ref[0]
        x1_ref[...] = x1
        h2_ref[...] = h2
        lg_ref[...] = _dotp(h2, wr_ref[...], NN, 3)

    @pl.when(i >= n_prompt_tiles)
    def _():
        x1_ref[...] = x1s_ref[...]
        h2_ref[...] = h2s_ref[...]
        lg_ref[...] = lgs_ref[...]


def out_projection(t, o_gla, o_rwkv, xp, mod_p, norm_w, w_out, w_router, x1_s, h2_s, logits_s):
    n_p, d = xp.shape
    pad_s = x1_s.shape[0]
    tm = ROW_TILE
    npt, per = n_p // tm, t // tm
    prompt = lambda width: pl.BlockSpec((tm, width), lambda i: (jnp.minimum(i, npt - 1), 0))
    pmod = lambda col: pl.BlockSpec((1, 1, d), lambda i: (jnp.minimum(i // per, mod_p.shape[0] - 1), 0, col))
    sample = lambda width: pl.BlockSpec((tm, width), lambda i: (jnp.maximum(i - npt, 0), 0))
    rows = lambda width: pl.BlockSpec((tm, width), lambda i: (i, 0))
    const = lambda a: pl.BlockSpec(a.shape, lambda i: (0, 0), pipeline_mode=pl.Buffered(1))
    n_rows = n_p + pad_s
    return pl.pallas_call(
        functools.partial(_outproj_kernel, n_prompt_tiles=npt),
        grid=(n_rows // tm,),
        in_specs=[prompt(o_gla.shape[1]), prompt(o_rwkv.shape[1]), prompt(d), pmod(2), pmod(4), pmod(3),
                  const(norm_w), const(w_out), const(w_router), sample(d), sample(d), sample(LANES)],
        out_specs=[rows(d), rows(d), rows(LANES)],
        out_shape=[jax.ShapeDtypeStruct((n_rows, d), F32), jax.ShapeDtypeStruct((n_rows, d), F32),
                   jax.ShapeDtypeStruct((n_rows, LANES), F32)],
        compiler_params=_params("arbitrary"),
    )(o_gla, o_rwkv, xp, mod_p, mod_p, mod_p, norm_w, w_out, w_router, x1_s, h2_s, logits_s)


def _moe_kernel(tile_expert_ref, n_valid_ref, n_real_ref, src_ref, dst_ref, h2_hbm, wg_ref, wu_ref, wd_ref,
                y_hbm, xbuf, obuf, gather_sem, scatter_sem):
    del tile_expert_ref
    i = pl.program_id(0)
    n_tiles = pl.num_programs(0)
    n_valid = n_valid_ref[0]
    slot = i % 2

    def for_rows(n, fn):
        def group(g, carry):
            for u in range(DMA_UNROLL):
                fn(g * DMA_UNROLL + u)
            return carry

        def single(r, carry):
            fn(r)
            return carry
        full = n // DMA_UNROLL
        lax.fori_loop(0, full, group, 0)
        lax.fori_loop(full * DMA_UNROLL, n, single, 0)

    def gather_row(tile, s, r):
        tok = src_ref[tile * MOE_TILE + r]
        return pltpu.make_async_copy(h2_hbm.at[pl.ds(tok, 1), :], xbuf.at[s, pl.ds(r, 1), :], gather_sem.at[s])

    def scatter_row(tile, s, r):
        row = dst_ref[tile * MOE_TILE + r]
        return pltpu.make_async_copy(obuf.at[s, pl.ds(r, 1), :], y_hbm.at[pl.ds(row, 1), :], scatter_sem.at[s])

    def start_rows(row_copy, tile, s):
        for_rows(n_real_ref[tile], lambda r: row_copy(tile, s, r).start())

    def wait_rows(row_copy, whole_tile_copy, tile, s):
        n_real = n_real_ref[tile]

        @pl.when(n_real == MOE_TILE)
        def _():
            whole_tile_copy(s).wait()

        @pl.when(n_real < MOE_TILE)
        def _():
            for_rows(n_real, lambda r: row_copy(tile, s, r).wait())

    whole_gather = lambda s: pltpu.make_async_copy(h2_hbm.at[pl.ds(0, MOE_TILE), :], xbuf.at[s], gather_sem.at[s])
    whole_scatter = lambda s: pltpu.make_async_copy(obuf.at[s], y_hbm.at[pl.ds(0, MOE_TILE), :], scatter_sem.at[s])

    @pl.when(i == 0)
    def _():
        xbuf[...] = jnp.zeros_like(xbuf)
        start_rows(gather_row, 0, 0)

    @pl.when(i + 1 < n_valid)
    def _():
        start_rows(gather_row, i + 1, 1 - slot)

    @pl.when(i < n_valid)
    def _():
        wait_rows(gather_row, whole_gather, i, slot)

        @pl.when(i >= 2)
        def _():
            wait_rows(scatter_row, whole_scatter, i - 2, slot)

        x = xbuf[slot].astype(BF16)
        gate = _dot(x, wg_ref[0].astype(BF16))
        up = _dot(x, wu_ref[0].astype(BF16))
        hid = gate * jax.nn.sigmoid(gate) * up
        obuf[slot] = _dot(hid.astype(BF16), wd_ref[0].astype(BF16))
        start_rows(scatter_row, i, slot)

    @pl.when(i == n_tiles - 1)
    def _():
        @pl.when(n_valid >= 2)
        def _():
            wait_rows(scatter_row, whole_scatter, n_valid - 2, n_valid % 2)

        wait_rows(scatter_row, whole_scatter, n_valid - 1, (n_valid - 1) % 2)


def moe_experts(h2, plan, w_gate, w_up, w_down):
    n_rows, d = h2.shape
    _, _, ff = w_gate.shape
    n_tiles = plan["tile_expert"].shape[0]
    grid_spec = pltpu.PrefetchScalarGridSpec(
        num_scalar_prefetch=5,
        grid=(n_tiles,),
        in_specs=[pl.BlockSpec(memory_space=pl.ANY),
                  pl.BlockSpec((1, d, ff), lambda i, te, *_: (te[i], 0, 0)),
                  pl.BlockSpec((1, d, ff), lambda i, te, *_: (te[i], 0, 0)),
                  pl.BlockSpec((1, ff, d), lambda i, te, *_: (te[i], 0, 0))],
        out_specs=pl.BlockSpec(memory_space=pl.ANY),
        scratch_shapes=[pltpu.VMEM((2, MOE_TILE, d), F32), pltpu.VMEM((2, MOE_TILE, d), F32),
                        pltpu.SemaphoreType.DMA((2,)), pltpu.SemaphoreType.DMA((2,))],
    )
    return pl.pallas_call(
        _moe_kernel,
        grid_spec=grid_spec,
        out_shape=jax.ShapeDtypeStruct((2 * n_rows, d), F32),
        compiler_params=_params("arbitrary"),
    )(plan["tile_expert"], plan["n_valid"], plan["n_real"], plan["src_row"], plan["dst_row"], h2,
      w_gate, w_up, w_down)


def _route_kernel(lg_ref, idx_ref, wt_ref, cnt_ref, carry):
    i = pl.program_id(0)

    @pl.when(i == 0)
    def _():
        carry[...] = jnp.zeros_like(carry)

    lg = lg_ref[...]
    n, width = lg.shape
    lane = lax.broadcasted_iota(jnp.int32, (n, width), 1)
    row_max = lambda a: jnp.max(a, axis=1, keepdims=True)
    first_lane = lambda hit: jnp.min(jnp.where(hit, lane, width), axis=1, keepdims=True)

    is_group = lane < N_GROUPS
    g_exp = jnp.where(is_group, jnp.exp(lg - row_max(jnp.where(is_group, lg, -jnp.inf))), 0.0)
    p_group = g_exp / jnp.sum(g_exp, axis=1, keepdims=True)
    p_sel = row_max(p_group)
    g_sel = first_lane(is_group & (p_group == p_sel))
    lo = N_GROUPS + g_sel * EXPERTS_PER_GROUP
    in_group = (lane >= lo) & (lane < lo + EXPERTS_PER_GROUP)
    cand = jnp.where(in_group, lg, -jnp.inf)
    v1 = row_max(cand)
    l1 = first_lane(in_group & (cand == v1))
    cand2 = jnp.where(lane == l1, -jnp.inf, cand)
    v2 = row_max(cand2)
    l2 = first_lane(in_group & (lane != l1) & (cand2 == v2))
    t = jnp.exp(v2 - v1)
    w1 = p_sel / (1.0 + t)
    w2 = p_sel * t / (1.0 + t)
    e1, e2 = l1 - N_GROUPS, l2 - N_GROUPS

    hit1, hit2 = lane == e1, lane == e2
    both = (hit1 | hit2).astype(BF16)
    r_i = lax.broadcasted_iota(jnp.int32, (n, n), 0)
    c_i = lax.broadcasted_iota(jnp.int32, (n, n), 1)
    before = carry[0:1, :] + _dot((r_i > c_i).astype(BF16), both)
    rank1 = jnp.sum(jnp.where(hit1, before, 0.0), axis=1, keepdims=True).astype(jnp.int32)
    rank2 = jnp.sum(jnp.where(hit2, before, 0.0), axis=1, keepdims=True).astype(jnp.int32)
    carry[0:1, :] = carry[0:1, :] + jnp.sum(both.astype(F32), axis=0, keepdims=True)

    idx_ref[...] = jnp.where(lane == 0, e1, jnp.where(lane == 1, e2, jnp.where(lane == 2, rank1, rank2)))
    wt_ref[...] = jnp.where(lane == 0, w1, w2)

    @pl.when(i == pl.num_programs(0) - 1)
    def _():
        cnt_ref[...] = carry[...]


def route(logits):
    n_rows, width = logits.shape
    tile = ROW_TILE
    rows = pl.BlockSpec((tile, width), lambda i: (i, 0))
    idx, wt, cnt = pl.pallas_call(
        _route_kernel,
        grid=(n_rows // tile,),
        in_specs=[rows],
        out_specs=[rows, rows, pl.BlockSpec((SUBLANES, width), lambda i: (0, 0))],
        out_shape=[jax.ShapeDtypeStruct((n_rows, width), jnp.int32), jax.ShapeDtypeStruct((n_rows, width), F32),
                   jax.ShapeDtypeStruct((SUBLANES, width), F32)],
        scratch_shapes=[pltpu.VMEM((SUBLANES, width), F32)],
        compiler_params=_params("arbitrary"),
    )(logits)
    return idx[:, 0:2], idx[:, 2:4], wt[:, 0:2], cnt[0, :N_EXPERTS].astype(jnp.int32)


def _moe_plan(expert, rank, counts):
    n = expert.shape[0]
    e_flat = expert.reshape(-1)
    ids = jnp.arange(N_EXPERTS, dtype=jnp.int32)
    one_hot = (e_flat[:, None] == ids[None, :]).astype(jnp.int32)
    padded = (counts + MOE_TILE - 1) // MOE_TILE * MOE_TILE
    ends = jnp.cumsum(padded)
    starts = ends - padded
    pos = jnp.sum(one_hot * starts[None, :], axis=1) + rank.reshape(-1)
    p_max = _round_up(2 * n, MOE_TILE) + N_EXPERTS * MOE_TILE
    n_tiles = p_max // MOE_TILE
    pair = jnp.full((p_max,), -1, jnp.int32).at[pos].set(jnp.arange(2 * n, dtype=jnp.int32))
    pair0 = jnp.maximum(pair, 0)
    src_row = pair0 // 2
    dst_row = (pair0 % 2) * n + src_row
    tile_start = jnp.arange(n_tiles, dtype=jnp.int32) * MOE_TILE
    n_valid = ends[-1] // MOE_TILE
    tile_expert = jnp.sum((tile_start[:, None] >= ends[None, :]).astype(jnp.int32), axis=1)
    tile_expert = jnp.where(tile_start < ends[-1], tile_expert, tile_expert[n_valid - 1])
    n_real = jnp.clip((starts + counts)[tile_expert] - tile_start, 0, MOE_TILE)
    n_real = jnp.where(tile_start < ends[-1], n_real, 0).astype(jnp.int32)
    return dict(src_row=src_row, dst_row=dst_row, tile_expert=tile_expert,
                n_valid=n_valid.astype(jnp.int32)[None], n_real=n_real)


def _final_kernel(x1_ref, y0_ref, y1_ref, wt_ref, gtp_ref, gts_ref, scp_ref, scs_ref, shp_ref, shs_ref, nw_ref,
                  yp_ref, ys_ref, *, n_prompt_tiles):
    i = pl.program_id(0)

    def finish(gt, sc, sh):
        moe = wt_ref[:, 0:1] * y0_ref[...] + wt_ref[:, 1:2] * y1_ref[...]
        x2 = x1_ref[...] + gt * moe
        return _rmsnorm(x2, nw_ref[...], NORM_EPS) * (1.0 + sc) + sh

    @pl.when(i < n_prompt_tiles)
    def _():
        yp_ref[...] = finish(gtp_ref[0], scp_ref[0], shp_ref[0])

    @pl.when(i >= n_prompt_tiles)
    def _():
        ys_ref[...] = finish(gts_ref[...], scs_ref[...], shs_ref[...])


def final_norm(n_p, t, x1, y_pairs, w_top, mod_p, mod_s, modf_p, modf_s, norm_w):
    n_rows, d = x1.shape
    tile = ROW_TILE
    n_tiles, npt = n_rows // tile, n_p // tile
    per = t // tile
    rows = lambda off: pl.BlockSpec((tile, d), lambda i: (i + off, 0))
    prompt_out = pl.BlockSpec((tile, d), lambda i: (jnp.minimum(i, npt - 1), 0))
    sample = lambda col: pl.BlockSpec((tile, d), lambda i: (jnp.maximum(i - npt, 0), col))
    pmod = lambda col: pl.BlockSpec((1, 1, d), lambda i: (jnp.minimum(i // per, mod_p.shape[0] - 1), 0, col))
    return pl.pallas_call(
        functools.partial(_final_kernel, n_prompt_tiles=npt),
        grid=(n_tiles,),
        in_specs=[rows(0), rows(0), rows(n_tiles), pl.BlockSpec((tile, w_top.shape[1]), lambda i: (i, 0)),
                  pmod(5), sample(5), pmod(1), sample(1), pmod(0), sample(0),
                  pl.BlockSpec((1, d), lambda i: (0, 0))],
        out_specs=[prompt_out, sample(0)],
        out_shape=[jax.ShapeDtypeStruct((n_p, d), F32), jax.ShapeDtypeStruct((n_rows - n_p, d), F32)],
        compiler_params=_params("arbitrary"),
    )(x1, y_pairs, y_pairs, w_top, mod_p, mod_s, modf_p, modf_s, modf_p, modf_s, norm_w)


def _align_rows(wt):
    parts = []
    for name in _ALIGNED:
        off, w = SRC[name]
        part = wt[off:off + w]
        wa = DST[name][1]
        if wa != w:
            part = jnp.pad(part, ((0, wa - w), (0, 0)))
        parts.append(part)
    return jnp.concatenate(parts, axis=0)


def _rwkv_seg(a, name):
    off, w = SRC[name]
    return a[..., off - RWKV_SRC_BASE:off - RWKV_SRC_BASE + w]


def _rwkv_seg_padded(a, name):
    seg = _rwkv_seg(a, name)
    wa = DST[name][1]
    return jnp.pad(seg, [(0, 0)] * (a.ndim - 1) + [(0, wa - seg.shape[-1])])


def _orig_seg(p, name):
    off, w = SRC[name]
    return p[..., off:off + w]


def _pad_rows(w, rows):
    return jnp.pad(w, ((0, rows - w.shape[0]), (0, 0)))


def kernel(x_prompt, x_sample, state_gla, state_rwkv, state_shift, c_prompt, c_sample, w_ada, b_ada, norm_mix, norm_ffn, w_in, gla_gate_w2, gla_gate_b, gla_norm, rwkv_mu, rwkv_w0, rwkv_w2, rwkv_a0, rwkv_a2, rwkv_g2, rwkv_k_k, rwkv_k_a, rwkv_r_k, rwkv_gn_w, rwkv_gn_b, w_out, w_router_group, w_router_expert, w_exp_gate, w_exp_up, w_exp_down, norm_final, w_ada_final, b_ada_final):
    assert w_ada.shape[0] == 1, "single-layer step"
    bp, t, d = x_prompt.shape
    bs = x_sample.shape[0]
    assert x_sample.shape[1] == 1 and t % ROW_TILE == 0
    n_p = bp * t
    pad_s = _round_up(bs, ROW_TILE)
    n_rows = n_p + pad_s
    prompt = _PromptShape(bp, t, n_p)
    hi = lax.Precision.HIGHEST
    pad_sample = lambda a: jnp.pad(a, ((0, pad_s - a.shape[0]), (0, 0)))
    hdot = lambda a, w: jnp.dot(a, w, precision=hi)

    c_act = jax.nn.silu(jnp.concatenate([c_prompt, c_sample], axis=0))
    mod = matmul3(c_act, w_ada[0]) + b_ada[0]
    mod_f = matmul3(c_act, w_ada_final) + b_ada_final
    mod_p, mod_s = mod[:bp, None, :], mod[bp:]
    modf_p, modf_s = mod_f[:bp, None, :], mod_f[bp:]
    sh1_s, sc1_s, gt1_s, sh2_s, sc2_s, _ = jnp.split(mod_s, 6, axis=-1)
    xp = x_prompt.reshape(n_p, d)
    xs = x_sample[:, 0, :]

    w_in_t = jnp.swapaxes(w_in[0], 0, 1)
    proj = in_projection(xp, t, mod_p, norm_mix, _align_rows(w_in_t.astype(BF16)))
    o_gla, gla_t_p = gla_prompt(prompt, proj, _pad_rows(gla_gate_w2[0], LANES), gla_gate_b, gla_norm)
    new_gla_p = jnp.swapaxes(gla_t_p, -1, -2)
    mu = rwkv_mu[0]
    mu_big = jnp.concatenate([_rwkv_seg(mu, n) for n in ("r", "k7", "v7")])[None, :]
    mu_small = jnp.concatenate([_rwkv_seg_padded(mu, n) for n in ("wl", "al", "gl7")])[None, :]
    vecs = jnp.concatenate([rwkv_w0, rwkv_a0, rwkv_k_k, rwkv_k_a, rwkv_r_k[0].reshape(1, RWKV_WIDTH),
                            rwkv_gn_w, rwkv_gn_b, jnp.zeros((1, RWKV_WIDTH), F32)], axis=0)
    o_rwkv, rwkv_t_p = rwkv7_prompt(prompt, proj, mu_big, mu_small, vecs, _pad_rows(rwkv_w2[0], LANES),
                                    _pad_rows(rwkv_a2[0], LANES), rwkv_g2[0])
    new_rwkv_p = rwkv_t_p.reshape(bp, RWKV_HEAD, RWKV_HEADS, RWKV_HEAD).transpose(0, 2, 1, 3)
    last = jnp.stack([proj[(b + 1) * t - 1] for b in range(bp)])
    new_shift_p = jnp.concatenate([last[:, DST[n][0]:DST[n][0] + SRC[n][1]] for n in _RWKV_ORIG], axis=-1)

    h1_s = _rmsnorm(xs, norm_mix[0], NORM_EPS) * (1.0 + sc1_s) + sh1_s
    proj_s = matmul3(h1_s, w_in_t, w_transposed=True)
    heads = lambda a, n: a.reshape(a.shape[0], n, -1)
    logd_s = jax.nn.log_sigmoid(hdot(_orig_seg(proj_s, "gl"), gla_gate_w2[0]) + gla_gate_b[0]) \
        / GLA_GATE_NORMALIZER
    q_s = heads(_orig_seg(proj_s, "q") * (GLA_DK ** -0.5), GLA_HEADS)
    k_s, v_s, g_s = heads(_orig_seg(proj_s, "k"), GLA_HEADS), heads(_orig_seg(proj_s, "v"), GLA_HEADS), \
        heads(logd_s, GLA_HEADS)
    new_gla_s, o_s = gla_step(state_gla[0], q_s, k_s, g_s, v_s)
    o_s = _rmsnorm(o_s, gla_norm[0], GLA_NORM_EPS) * heads(jax.nn.silu(_orig_seg(proj_s, "og")), GLA_HEADS)

    rp_s = proj_s[:, RWKV_SRC_BASE:]
    new_shift_s = rp_s
    xs7 = rp_s + (state_shift[0] - rp_s) * mu
    sx = lambda name: _rwkv_seg(xs7, name)
    r_s, k7_s, v7_s = sx("r"), sx("k7"), sx("v7")
    w_pre = rwkv_w0[0] + hdot(jnp.tanh(sx("wl")), rwkv_w2[0])
    decay_s = jnp.exp(-jnp.exp(-jax.nn.softplus(-w_pre) - 0.5))
    a_s = jax.nn.sigmoid(rwkv_a0[0] + hdot(sx("al"), rwkv_a2[0]))
    g_s7 = hdot(jax.nn.sigmoid(sx("gl7")), rwkv_g2[0])
    hs = lambda z: z.reshape(bs, RWKV_HEADS, RWKV_HEAD)
    kk_s = hs(k7_s * rwkv_k_k[0])
    kk_s = kk_s / jnp.maximum(jnp.sqrt(jnp.sum(kk_s * kk_s, axis=-1, keepdims=True)), 1e-12)
    k7_s = k7_s * (1.0 + (a_s - 1.0) * rwkv_k_a[0])
    r_h, w_h, k_h, v_h, a_h = hs(r_s), hs(decay_s), hs(k7_s), hs(v7_s), hs(a_s)
    s_prev = state_rwkv[0]
    sa = jnp.einsum("bhij,bhj->bhi", s_prev, -kk_s, precision=hi)
    new_rwkv_s = s_prev * w_h[:, :, None, :] + sa[..., None] * (kk_s * a_h)[:, :, None, :] \
        + v_h[..., None] * k_h[:, :, None, :]
    y_s = jnp.einsum("bhij,bhj->bhi", new_rwkv_s, r_h, precision=hi)
    y_mu = jnp.mean(y_s, axis=-1, keepdims=True)
    y_var = jnp.mean(jnp.square(y_s - y_mu), axis=-1, keepdims=True)
    y_s = (y_s - y_mu) * lax.rsqrt(y_var + RWKV_GN_EPS) * rwkv_gn_w[0].reshape(RWKV_HEADS, RWKV_HEAD) \
        + rwkv_gn_b[0].reshape(RWKV_HEADS, RWKV_HEAD)
    bonus = jnp.sum(r_h * k_h * rwkv_r_k[0], axis=-1, keepdims=True) * v_h
    o_rs = (y_s + bonus).reshape(bs, RWKV_WIDTH) * g_s7
    mix_s = matmul3(jnp.concatenate([o_s.reshape(bs, GLA_WIDTH), o_rs], axis=-1), w_out[0])
    x1_s = xs + gt1_s * mix_s
    h2_s = _rmsnorm(x1_s, norm_ffn[0], NORM_EPS) * (1.0 + sc2_s) + sh2_s
    w_router = jnp.pad(jnp.concatenate([w_router_group[0], w_router_expert[0]], axis=-1),
                       ((0, 0), (0, LANES - N_GROUPS - N_EXPERTS)))
    logits_s = hdot(h2_s, w_router)

    x1, h2, logits = out_projection(t, o_gla, o_rwkv, xp, mod_p, norm_ffn, w_out[0].astype(BF16), w_router,
                                    pad_sample(x1_s), pad_sample(h2_s), pad_sample(logits_s))

    expert, rank, w_top, counts = route(logits)
    plan = _moe_plan(expert, rank, counts)
    y_pairs = moe_experts(h2, plan, w_exp_gate[0].reshape(N_EXPERTS, d, EXPERT_FF),
                          w_exp_up[0].reshape(N_EXPERTS, d, EXPERT_FF),
                          w_exp_down[0].reshape(N_EXPERTS, EXPERT_FF, d))

    y_p, y_s_pad = final_norm(n_p, t, x1, y_pairs, w_top, mod_p, pad_sample(mod_s), modf_p,
                              pad_sample(modf_s), norm_final[None, :])
    return (y_p.reshape(bp, t, d), y_s_pad[:bs, None, :], new_gla_p[None], new_rwkv_p[None], new_shift_p[None],
            new_gla_s[None], new_rwkv_s[None], new_shift_s[None])
```

```python
import collections
import functools

import jax
import jax.numpy as jnp
from jax import lax
from jax.experimental import pallas as pl
from jax.experimental.pallas import tpu as pltpu

F32 = jnp.float32
BF16 = jnp.bfloat16

D_MODEL = 2048
GLA_HEADS = 4
GLA_DK = 128
GLA_DV = 256
GLA_KEY_WIDTH = GLA_HEADS * GLA_DK
GLA_WIDTH = GLA_HEADS * GLA_DV
GLA_GATE_NORMALIZER = 16.0
RWKV_HEAD = 64
RWKV_HEADS = 16
RWKV_WIDTH = RWKV_HEAD * RWKV_HEADS
N_GROUPS = 4
EXPERTS_PER_GROUP = 8
N_EXPERTS = N_GROUPS * EXPERTS_PER_GROUP
EXPERT_FF = 512
NORM_EPS = 1e-6
GLA_NORM_EPS = 1e-5
RWKV_GN_EPS = 64e-5

LANES = 128
SUBLANES = 8
VMEM_LIMIT_BYTES = 56 * 1024 * 1024

CHUNK = 64
SUB = 16
RWKV_GROUP = 2
GROUP_W = RWKV_GROUP * RWKV_HEAD
GLA_SEQS = 4
LOG2E = 1.4426950408889634
RWKV_SEQS = 4
MOE_TILE = 256
DMA_UNROLL = 8
SAMPLE_BLOCK = 8
ROW_TILE = 256
NORM_SLAB = 128
K_SPLIT = 256

_ORIG = (("q", 512), ("k", 512), ("v", 1024), ("gl", 16), ("og", 1024),
         ("r", 1024), ("wl", 64), ("k7", 1024), ("v7", 1024), ("al", 64), ("gl7", 128))
_ALIGNED = ("q", "k", "v", "og", "r", "k7", "v7", "gl", "wl", "al", "gl7")
_RWKV_ORIG = ("r", "wl", "k7", "v7", "al", "gl7")


def _round_up(n, m):
    return (n + m - 1) // m * m


def _layouts():
    src, off = {}, 0
    for name, w in _ORIG:
        src[name] = (off, w)
        off += w
    dst, pos = {}, 0
    for name in _ALIGNED:
        wa = _round_up(src[name][1], LANES)
        dst[name] = (pos, wa)
        pos += wa
    return src, off, dst, pos


SRC, IN_COLS, DST, IN_COLS_ALIGNED = _layouts()
RWKV_SRC_BASE = SRC["r"][0]
BIG_W = 3 * RWKV_WIDTH
SMALL_W = 3 * LANES
assert DST["r"][0] == BIG_W and DST["k7"][0] == BIG_W + RWKV_WIDTH and DST["v7"][0] == BIG_W + 2 * RWKV_WIDTH
assert DST["v"][0] == RWKV_WIDTH and DST["og"][0] == 2 * RWKV_WIDTH and DST["k"][0] == GLA_KEY_WIDTH

NN = ((1,), (0,))
NT = ((1,), (1,))

_PromptShape = collections.namedtuple("_PromptShape", "bp t rows")


def _dot(a, b, dims=NN):
    return lax.dot_general(a, b, (dims, ((), ())), preferred_element_type=F32)


def _split2(x):
    hi = x.astype(BF16)
    lo = (x - hi.astype(F32)).astype(BF16)
    return hi, lo


def _dotp(a, b, dims=NN, passes=1):
    if passes == 1:
        return _dot(a.astype(BF16), b.astype(BF16), dims)
    ah, al = _split2(a)
    bh, bl = _split2(b)
    return _dot(ah, bh, dims) + (_dot(ah, bl, dims) + _dot(al, bh, dims))


def _cumsum_rows(x):
    n = x.shape[0]
    row = lax.broadcasted_iota(jnp.int32, (n, n), 0)
    col = lax.broadcasted_iota(jnp.int32, (n, n), 1)
    tri = (row >= col).astype(BF16)
    x1 = x.astype(BF16)
    r1 = x - x1.astype(F32)
    x2 = r1.astype(BF16)
    x3 = (r1 - x2.astype(F32)).astype(BF16)
    return _dot(tri, x1) + (_dot(tri, x2) + _dot(tri, x3))


def _softplus(z):
    return jnp.maximum(z, 0.0) + jnp.log1p(jnp.exp(-jnp.abs(z)))


def _rmsnorm(x, g, eps):
    return x * lax.rsqrt(jnp.mean(x * x, axis=-1, keepdims=True) + eps) * g


def _largest_tile(n, cap, mult=SUBLANES):
    if n <= cap:
        return n
    best = None
    for t in range(mult, cap + 1, mult):
        if n % t == 0:
            best = t
    assert best is not None, (n, cap)
    return best


def _params(*sem):
    return pltpu.CompilerParams(dimension_semantics=sem, vmem_limit_bytes=VMEM_LIMIT_BYTES)


def _mm3_kernel(a_ref, w_ref, o_ref, *, dims):
    @pl.when(pl.program_id(0) == 0)
    def _():
        o_ref[...] = jnp.zeros_like(o_ref)

    o_ref[...] += _dotp(a_ref[...], w_ref[...], dims, 3)


def matmul3(a, w, w_transposed=False):
    m, k = a.shape
    n = w.shape[0] if w_transposed else w.shape[1]
    tk = _largest_tile(k, K_SPLIT, LANES)
    w_spec = pl.BlockSpec((n, tk), lambda s: (0, s)) if w_transposed else pl.BlockSpec((tk, n), lambda s: (s, 0))
    return pl.pallas_call(
        functools.partial(_mm3_kernel, dims=NT if w_transposed else NN),
        grid=(k // tk,),
        in_specs=[pl.BlockSpec((m, tk), lambda s: (0, s)), w_spec],
        out_specs=pl.BlockSpec((m, n), lambda s: (0, 0)),
        out_shape=jax.ShapeDtypeStruct((m, n), F32),
        compiler_params=_params("arbitrary"),
    )(a, w)


def _prompt_mod(tiles_per_batch, width, col):
    return pl.BlockSpec((1, 1, width), lambda i, *_: (i // tiles_per_batch, 0, col))


def _inproj_kernel(x_ref, sc_ref, sh_ref, nw_ref, w_ref, o_ref, h_scr):
    @pl.when(pl.program_id(1) == 0)
    def _():
        for r in range(0, h_scr.shape[0], NORM_SLAB):
            sl = slice(r, r + NORM_SLAB)
            h = _rmsnorm(x_ref[sl, :], nw_ref[...], NORM_EPS) * (1.0 + sc_ref[0]) + sh_ref[0]
            h_scr[sl, :] = h.astype(BF16)

    o_ref[...] = _dot(h_scr[...], w_ref[...], NT)


def in_projection(xp, t, mod_p, norm_w, w_aligned):
    n_p, d = xp.shape
    n = w_aligned.shape[0]
    tm = _largest_tile(t, 1024, NORM_SLAB)
    tn = _largest_tile(n, 1664, LANES)
    per = t // tm
    return pl.pallas_call(
        _inproj_kernel,
        grid=(n_p // tm, n // tn),
        in_specs=[pl.BlockSpec((tm, d), lambda i, j: (i, 0)), _prompt_mod(per, d, 1), _prompt_mod(per, d, 0),
                  pl.BlockSpec((1, d), lambda i, j: (0, 0)), pl.BlockSpec((tn, d), lambda i, j: (j, 0))],
        out_specs=pl.BlockSpec((tm, tn), lambda i, j: (i, j)),
        out_shape=jax.ShapeDtypeStruct((n_p, n), F32),
        scratch_shapes=[pltpu.VMEM((tm, d), BF16)],
        compiler_params=_params("parallel", "arbitrary"),
    )(xp, mod_p, mod_p, norm_w, w_aligned)


def _rwkv_chunk_kernel(big_ref, wl_ref, al_ref, gl_ref, mu_big_ref, mu_small_ref, vec_ref, w2_ref, a2_ref,
                       g2_ref, o_ref, s_ref, state, carry_big, carry_small, *, passes):
    c_idx = pl.program_id(1)
    n_seq, n_tok = big_ref.shape[0], big_ref.shape[1]
    n_grp = RWKV_WIDTH // GROUP_W
    grp = [slice(g * GROUP_W, (g + 1) * GROUP_W) for g in range(n_grp)]

    @pl.when(c_idx == 0)
    def _():
        state[...] = jnp.zeros_like(state)
        carry_big[...] = jnp.zeros_like(carry_big)
        carry_small[...] = jnp.zeros_like(carry_small)

    first_row = lax.broadcasted_iota(jnp.int32, (n_tok, 1), 0) == 0
    w0, a0, k_k, k_a, r_k, gn_w, gn_b = (vec_ref[i:i + 1, :] for i in range(7))

    gi = lax.broadcasted_iota(jnp.int32, (GROUP_W, GROUP_W), 0) // RWKV_HEAD
    gj = lax.broadcasted_iota(jnp.int32, (GROUP_W, GROUP_W), 1) // RWKV_HEAD
    head_ones = (gi == gj).astype(BF16)

    def head_sums(parts):
        hi, lo = _split2(jnp.concatenate(parts, axis=0))
        out = _dot(hi, head_ones) + _dot(lo, head_ones)
        return [out[i * n_tok:(i + 1) * n_tok] for i in range(len(parts))]

    def prepare(n):
        def token_shift(cur, carry, mu):
            prev = jnp.where(first_row, carry[n, 0:1, :], pltpu.roll(cur, 1, 0))
            carry[n, 0:1, :] = cur[n_tok - 1:n_tok, :]
            return cur + (prev - cur) * mu

        xs_big = token_shift(big_ref[n], carry_big, mu_big_ref[...])
        small = jnp.concatenate([wl_ref[n], al_ref[n], gl_ref[n]], axis=1)
        xs_small = token_shift(small, carry_small, mu_small_ref[...])
        r = xs_big[:, :RWKV_WIDTH]
        k7 = xs_big[:, RWKV_WIDTH:2 * RWKV_WIDTH]
        v = xs_big[:, 2 * RWKV_WIDTH:]
        w_pre = w0 + _dotp(jnp.tanh(xs_small[:, :LANES]), w2_ref[...])
        lw = -jnp.exp(-_softplus(-w_pre) - 0.5)
        a = jax.nn.sigmoid(a0 + _dotp(xs_small[:, LANES:2 * LANES], a2_ref[...]))
        gate = _dotp(jax.nn.sigmoid(xs_small[:, 2 * LANES:]), g2_ref[...])
        kk_raw = k7 * k_k
        k = k7 * (1.0 + (a - 1.0) * k_a)
        sums = [head_sums([kk_raw[:, sl] * kk_raw[:, sl], r[:, sl] * k[:, sl] * r_k[:, sl]]) for sl in grp]
        kk = jnp.concatenate([kk_raw[:, sl] / jnp.maximum(jnp.sqrt(sums[g][0]), 1e-12)
                              for g, sl in enumerate(grp)], axis=1)
        bonus = jnp.concatenate([sums[g][1] for g in range(n_grp)], axis=1) * v
        cum = _cumsum_rows(lw)
        cum_end = cum[n_tok - 1:n_tok, :]
        beta = kk * a
        g_inv = jnp.exp(-cum)
        g_end = jnp.exp(cum_end - cum)
        return dict(v=v, gate=gate, bonus=bonus, g_tot=jnp.exp(cum_end),
                    a_hat=-kk * jnp.exp(cum - lw), r_hat=r * jnp.exp(cum), b_hat=beta * g_inv,
                    k_hat=k * g_inv, b_end=beta * g_end, k_end=k * g_end)

    seqs = [prepare(n) for n in range(n_seq)]
    units = [(n, sl) for n in range(n_seq) for sl in grp]
    part = lambda name: [seqs[n][name][:, sl] for n, sl in units]

    lane = lax.broadcasted_iota(jnp.int32, (n_tok, GROUP_W), 1)
    tok = lax.broadcasted_iota(jnp.int32, (n_tok, GROUP_W), 0)
    lane_head = lane // RWKV_HEAD
    src_tok = lane % RWKV_HEAD
    strict = tok > src_tok
    incl = tok >= src_tok

    def bd(y):
        return jnp.concatenate([jnp.where(lane_head == h, y, 0.0) for h in range(RWKV_GROUP)], axis=0)

    gs = range(len(units))
    s0 = [state[n, :, sl] for n, sl in units]
    v_u, b_hat, k_hat, b_end, k_end = part("v"), part("b_hat"), part("k_hat"), part("b_end"), part("k_end")
    lhs2 = [jnp.concatenate([a, r], axis=0) for a, r in zip(part("a_hat"), part("r_hat"))]
    abrb = [_dotp(lhs2[g], bd(b_hat[g]), NT, passes) for g in gs]
    akrk = [_dotp(lhs2[g], bd(k_hat[g]), NT, passes) for g in gs]
    asrs = [_dotp(lhs2[g], bd(s0[g]), NT, passes) for g in gs]
    p = [jnp.where(strict, abrb[g][:n_tok], 0.0) for g in gs]
    ak = [jnp.where(strict, akrk[g][:n_tok], 0.0) for g in gs]
    rb = [jnp.where(incl, abrb[g][n_tok:], 0.0) for g in gs]
    rk = [jnp.where(incl, akrk[g][n_tok:], 0.0) for g in gs]
    bd_v = [bd(v_u[g]) for g in gs]
    x = [asrs[g][:n_tok] + _dotp(ak[g], bd_v[g], NN, passes) for g in gs]
    n_sq = n_tok.bit_length() - 1
    for it in range(n_sq):
        if it < n_sq - 1:
            both = [_dotp(p[g], jnp.concatenate([bd(p[g]), bd(x[g])], axis=1), NN, passes) for g in gs]
            x = [x[g] + both[g][:, GROUP_W:] for g in gs]
            p = [both[g][:, :GROUP_W] for g in gs]
        else:
            x = [x[g] + _dotp(p[g], bd(x[g]), NN, passes) for g in gs]
    y = [asrs[g][n_tok:] + _dotp(jnp.concatenate([rb[g], rk[g]], axis=1),
                                 jnp.concatenate([bd(x[g]), bd_v[g]], axis=0), NN, passes) for g in gs]
    full = [_dotp(jnp.concatenate([x[g], v_u[g]], axis=0).T,
                  jnp.concatenate([b_end[g], k_end[g]], axis=0), NN, passes) for g in gs]
    g_tot = part("g_tot")
    for g, (n, sl) in enumerate(units):
        upd = s0[g] * g_tot[g]
        for h in range(RWKV_GROUP):
            upd = upd + jnp.where(lane_head == h, full[g][h * RWKV_HEAD:(h + 1) * RWKV_HEAD, :], 0.0)
        state[n, :, sl] = upd

    inv_n = 1.0 / RWKV_HEAD
    mean = [head_sums([y[g]])[0] * inv_n for g in gs]
    dev = [y[g] - mean[g] for g in gs]
    var = [head_sums([dev[g] * dev[g]])[0] * inv_n for g in gs]
    bonus, gate = part("bonus"), part("gate")
    for g, (n, sl) in enumerate(units):
        yn = dev[g] * lax.rsqrt(var[g] + RWKV_GN_EPS) * gn_w[:, sl] + gn_b[:, sl]
        o_ref[n, :, sl] = ((yn + bonus[g]) * gate[g]).astype(o_ref.dtype)

    @pl.when(c_idx == pl.num_programs(1) - 1)
    def _():
        s_ref[...] = state[...]


def rwkv7_prompt(rows, proj, mu_big, mu_small, vecs, w2, a2, g2, *, passes=1):
    assert CHUNK == RWKV_HEAD and rows.t % CHUNK == 0
    nc = rows.t // CHUNK
    n_seq = _largest_tile(rows.bp, RWKV_SEQS, 1)
    proj3 = proj.reshape(rows.bp, rows.t, proj.shape[1])
    small_col = lambda name: DST[name][0] // LANES
    tok = lambda width, col: pl.BlockSpec((n_seq, CHUNK, width), lambda b, c: (b, c, col))
    const = lambda shape: pl.BlockSpec(shape, lambda b, c: (0, 0))
    out, final_state = pl.pallas_call(
        functools.partial(_rwkv_chunk_kernel, passes=passes),
        grid=(rows.bp // n_seq, nc),
        in_specs=[tok(BIG_W, 1), tok(LANES, small_col("wl")), tok(LANES, small_col("al")),
                  tok(LANES, small_col("gl7")),
                  const((1, BIG_W)), const((1, SMALL_W)), const((SUBLANES, RWKV_WIDTH)),
                  const((LANES, RWKV_WIDTH)), const((LANES, RWKV_WIDTH)), const((LANES, RWKV_WIDTH))],
        out_specs=[tok(RWKV_WIDTH, 0),
                   pl.BlockSpec((n_seq, RWKV_HEAD, RWKV_WIDTH), lambda b, c: (b, 0, 0))],
        out_shape=[jax.ShapeDtypeStruct((rows.bp, rows.t, RWKV_WIDTH), BF16),
                   jax.ShapeDtypeStruct((rows.bp, RWKV_HEAD, RWKV_WIDTH), F32)],
        scratch_shapes=[pltpu.VMEM((n_seq, RWKV_HEAD, RWKV_WIDTH), F32),
                        pltpu.VMEM((n_seq, SUBLANES, BIG_W), F32), pltpu.VMEM((n_seq, SUBLANES, SMALL_W), F32)],
        compiler_params=_params("parallel", "arbitrary"),
    )(proj3, proj3, proj3, proj3, mu_big, mu_small, vecs, w2, a2, g2)
    return out.reshape(rows.rows, RWKV_WIDTH), final_state


def _gla_chunk_kernel(qk_ref, v_ref, og_ref, gl_ref, w2_ref, gb_ref, nw_ref, o_ref, s_ref, state):
    c_idx = pl.program_id(1)
    n_seq, n_tok = qk_ref.shape[0], qk_ref.shape[1]
    n_sub = n_tok // SUB

    @pl.when(c_idx == 0)
    def _():
        state[...] = jnp.zeros_like(state)

    row_k = lax.broadcasted_iota(jnp.int32, (n_tok, GLA_DK), 0)
    att_row = lax.broadcasted_iota(jnp.int32, (SUB, n_tok), 0)
    att_col = lax.broadcasted_iota(jnp.int32, (SUB, n_tok), 1)
    own_col = [jnp.where((att_col >= i * SUB) & (att_col - i * SUB <= att_row), att_col - i * SUB, -1)
               for i in range(n_sub)]

    def cum_log2_decay(n):
        logd = -_softplus(-(_dotp(gl_ref[n], w2_ref[...]) + gb_ref[...])) * (LOG2E / GLA_GATE_NORMALIZER)
        return _cumsum_rows(logd)

    units = [(n, h) for n in range(n_seq) for h in range(GLA_HEADS)]
    us = range(len(units))
    ks = lambda h: slice(h * GLA_DK, (h + 1) * GLA_DK)
    vs = lambda h: slice(h * GLA_DV, (h + 1) * GLA_DV)
    b_seq = [cum_log2_decay(n) for n in range(n_seq)]
    q = [qk_ref[n, :, ks(h)] * (GLA_DK ** -0.5) for n, h in units]
    k = [qk_ref[n, :, GLA_KEY_WIDTH + h * GLA_DK:GLA_KEY_WIDTH + (h + 1) * GLA_DK] for n, h in units]
    b = [b_seq[n][:, ks(h)] for n, h in units]
    v = [v_ref[n, :, vs(h)] for n, h in units]
    st = [state[n, h] for n, h in units]
    o_inter = [_dotp(q[u] * jnp.exp2(b[u]), st[u], NT) for u in us]
    blocks = [[] for _ in us]
    for i in range(n_sub):
        lo = i * SUB
        rows = slice(lo, lo + SUB)
        if i > 0:
            att = [_dotp(q[u][rows] * jnp.exp2(b[u][rows] - b[u][lo - 1:lo]),
                         jnp.where(row_k < lo, k[u] * jnp.exp2(b[u][lo - 1:lo] - b[u]), 0.0), NT) for u in us]
        else:
            att = [jnp.zeros((SUB, n_tok), F32) for _ in us]
        for j in range(SUB):
            tok = slice(lo + j, lo + j + 1)
            col = [jnp.sum(q[u][rows] * (k[u][tok] * jnp.exp2(b[u][rows] - b[u][tok])), axis=1, keepdims=True)
                   for u in us]
            att = [jnp.where(own_col[i] == j, col[u], att[u]) for u in us]
        for u in us:
            blocks[u].append(o_inter[u][rows] + _dotp(att[u], v[u], NN))
    for u, (n, h) in enumerate(units):
        o = jnp.concatenate(blocks[u], axis=0)
        og = og_ref[n, :, vs(h)]
        o_ref[n, :, vs(h)] = (_rmsnorm(o, nw_ref[...], GLA_NORM_EPS)
                              * (og * jax.nn.sigmoid(og))).astype(o_ref.dtype)
        b_last = b[u][n_tok - 1:n_tok, :]
        state[n, h] = st[u] * jnp.exp2(b_last) + _dotp(v[u].T, k[u] * jnp.exp2(b_last - b[u]), NN)

    @pl.when(c_idx == pl.num_programs(1) - 1)
    def _():
        s_ref[...] = state[...]


def gla_prompt(rows, proj, gate_w2, gate_b, norm_w):
    nc = rows.t // CHUNK
    n_seq = _largest_tile(rows.bp, GLA_SEQS, 1)
    proj3 = proj.reshape(rows.bp, rows.t, proj.shape[1])
    tok = lambda width, col: pl.BlockSpec((n_seq, CHUNK, width), lambda b, c: (b, c, col))
    const = lambda shape: pl.BlockSpec(shape, lambda b, c: (0, 0))
    out, final_state = pl.pallas_call(
        _gla_chunk_kernel,
        grid=(rows.bp // n_seq, nc),
        in_specs=[tok(GLA_WIDTH, 0), tok(GLA_WIDTH, 1), tok(GLA_WIDTH, 2), tok(LANES, DST["gl"][0] // LANES),
                  const((LANES, GLA_KEY_WIDTH)), const((1, GLA_KEY_WIDTH)), const((1, GLA_DV))],
        out_specs=[tok(GLA_WIDTH, 0),
                   pl.BlockSpec((n_seq, GLA_HEADS, GLA_DV, GLA_DK), lambda b, c: (b, 0, 0, 0))],
        out_shape=[jax.ShapeDtypeStruct((rows.bp, rows.t, GLA_WIDTH), BF16),
                   jax.ShapeDtypeStruct((rows.bp, GLA_HEADS, GLA_DV, GLA_DK), F32)],
        scratch_shapes=[pltpu.VMEM((n_seq, GLA_HEADS, GLA_DV, GLA_DK), F32)],
        compiler_params=_params("parallel", "arbitrary"),
    )(proj3, proj3, proj3, proj3, gate_w2, gate_b, norm_w)
    return out.reshape(rows.rows, GLA_WIDTH), final_state


def _columns(a, block):
    rows, heads, n = a.shape
    return a.reshape(rows // block, block, heads, n).transpose(0, 3, 1, 2).reshape(rows // block, n, block * heads)


def _gla_step_kernel(s_ref, qt_ref, kt_ref, gt_ref, v_ref, so_ref, o_ref):
    n_row, n_head = s_ref.shape[0], s_ref.shape[1]
    qt, kt, decay = qt_ref[0], kt_ref[0], jnp.exp(gt_ref[0])
    for b in range(n_row):
        for h in range(n_head):
            j = b * n_head + h
            s_new = decay[:, j:j + 1] * s_ref[b, h] + kt[:, j:j + 1] * v_ref[b, h:h + 1, :]
            so_ref[b, h] = s_new
            o_ref[b, h:h + 1, :] = jnp.sum(qt[:, j:j + 1] * s_new, axis=0, keepdims=True)


def gla_step(state, q, k, logd, v):
    rows, heads, dk, dv = state.shape
    blk = SAMPLE_BLOCK
    assert rows % blk == 0
    col = pl.BlockSpec((1, dk, blk * heads), lambda i: (i, 0, 0))
    s_spec = pl.BlockSpec((blk, heads, dk, dv), lambda i: (i, 0, 0, 0))
    v_spec = pl.BlockSpec((blk, heads, dv), lambda i: (i, 0, 0))
    return pl.pallas_call(
        _gla_step_kernel,
        grid=(rows // blk,),
        in_specs=[s_spec, col, col, col, v_spec],
        out_specs=[s_spec, v_spec],
        out_shape=[jax.ShapeDtypeStruct(state.shape, F32), jax.ShapeDtypeStruct((rows, heads, dv), F32)],
        compiler_params=_params("parallel"),
    )(state, _columns(q, blk), _columns(k, blk), _columns(logd, blk), v)


def _outproj_kernel(og_ref, orw_ref, x_ref, gt_ref, sc_ref, sh_ref, nw_ref, wo_ref, wr_ref,
                    x1s_ref, h2s_ref, lgs_ref, x1_ref, h2_ref, lg_ref, *, n_prompt_tiles):
    i = pl.program_id(0)

    @pl.when(i < n_prompt_tiles)
    def _():
        half = og_ref.shape[1]
        mix = _dot(og_ref[...], wo_ref[:half, :]) + _dot(orw_ref[...], wo_ref[half:, :])
        x1 = x_ref[...] + gt_ref[0] * mix
        h2 = _rmsnorm(x1, nw_ref[...], NORM_EPS) * (1.0 + sc_ref[0]) + sh_ref[0]
        x1_ref[...] = x1
        h2_ref[...] = h2
        lg_ref[...] = _dotp(h2, wr_ref[...], NN, 3)

    @pl.when(i >= n_prompt_tiles)
    def _():
        x1_ref[...] = x1s_ref[...]
        h2_ref[...] = h2s_ref[...]
        lg_ref[...] = lgs_ref[...]


def out_projection(t, o_gla, o_rwkv, xp, mod_p, norm_w, w_out, w_router, x1_s, h2_s, logits_s):
    n_p, d = xp.shape
    pad_s = x1_s.shape[0]
    tm = ROW_TILE
    npt, per = n_p // tm, t // tm
    prompt = lambda width: pl.BlockSpec((tm, width), lambda i: (jnp.minimum(i, npt - 1), 0))
    pmod = lambda col: pl.BlockSpec((1, 1, d), lambda i: (jnp.minimum(i // per, mod_p.shape[0] - 1), 0, col))
    sample = lambda width: pl.BlockSpec((tm, width), lambda i: (jnp.maximum(i - npt, 0), 0))
    rows = lambda width: pl.BlockSpec((tm, width), lambda i: (i, 0))
    const = lambda a: pl.BlockSpec(a.shape, lambda i: (0, 0), pipeline_mode=pl.Buffered(1))
    n_rows = n_p + pad_s
    return pl.pallas_call(
        functools.partial(_outproj_kernel, n_prompt_tiles=npt),
        grid=(n_rows // tm,),
        in_specs=[prompt(o_gla.shape[1]), prompt(o_rwkv.shape[1]), prompt(d), pmod(2), pmod(4), pmod(3),
                  const(norm_w), const(w_out), const(w_router), sample(d), sample(d), sample(LANES)],
        out_specs=[rows(d), rows(d), rows(LANES)],
        out_shape=[jax.ShapeDtypeStruct((n_rows, d), F32), jax.ShapeDtypeStruct((n_rows, d), F32),
                   jax.ShapeDtypeStruct((n_rows, LANES), F32)],
        compiler_params=_params("arbitrary"),
    )(o_gla, o_rwkv, xp, mod_p, mod_p, mod_p, norm_w, w_out, w_router, x1_s, h2_s, logits_s)


def _moe_kernel(tile_expert_ref, n_valid_ref, n_real_ref, src_ref, dst_ref, h2_hbm, wg_ref, wu_ref, wd_ref,
                y_hbm, xbuf, obuf, gather_sem, scatter_sem):
    del tile_expert_ref
    i = pl.program_id(0)
    n_tiles = pl.num_programs(0)
    n_valid = n_valid_ref[0]
    slot = i % 2

    def for_rows(n, fn):
        def group(g, carry):
            for u in range(DMA_UNROLL):
                fn(g * DMA_UNROLL + u)
            return carry

        def single(r, carry):
            fn(r)
            return carry
        full = n // DMA_UNROLL
        lax.fori_loop(0, full, group, 0)
        lax.fori_loop(full * DMA_UNROLL, n, single, 0)

    def gather_row(tile, s, r):
        tok = src_ref[tile * MOE_TILE + r]
        return pltpu.make_async_copy(h2_hbm.at[pl.ds(tok, 1), :], xbuf.at[s, pl.ds(r, 1), :], gather_sem.at[s])

    def scatter_row(tile, s, r):
        row = dst_ref[tile * MOE_TILE + r]
        return pltpu.make_async_copy(obuf.at[s, pl.ds(r, 1), :], y_hbm.at[pl.ds(row, 1), :], scatter_sem.at[s])

    def start_rows(row_copy, tile, s):
        for_rows(n_real_ref[tile], lambda r: row_copy(tile, s, r).start())

    def wait_rows(row_copy, whole_tile_copy, tile, s):
        n_real = n_real_ref[tile]

        @pl.when(n_real == MOE_TILE)
        def _():
            whole_tile_copy(s).wait()

        @pl.when(n_real < MOE_TILE)
        def _():
            for_rows(n_real, lambda r: row_copy(tile, s, r).wait())

    whole_gather = lambda s: pltpu.make_async_copy(h2_hbm.at[pl.ds(0, MOE_TILE), :], xbuf.at[s], gather_sem.at[s])
    whole_scatter = lambda s: pltpu.make_async_copy(obuf.at[s], y_hbm.at[pl.ds(0, MOE_TILE), :], scatter_sem.at[s])

    @pl.when(i == 0)
    def _():
        xbuf[...] = jnp.zeros_like(xbuf)
        start_rows(gather_row, 0, 0)

    @pl.when(i + 1 < n_valid)
    def _():
        start_rows(gather_row, i + 1, 1 - slot)

    @pl.when(i < n_valid)
    def _():
        wait_rows(gather_row, whole_gather, i, slot)

        @pl.when(i >= 2)
        def _():
            wait_rows(scatter_row, whole_scatter, i - 2, slot)

        x = xbuf[slot].astype(BF16)
        gate = _dot(x, wg_ref[0].astype(BF16))
        up = _dot(x, wu_ref[0].astype(BF16))
        hid = gate * jax.nn.sigmoid(gate) * up
        obuf[slot] = _dot(hid.astype(BF16), wd_ref[0].astype(BF16))
        start_rows(scatter_row, i, slot)

    @pl.when(i == n_tiles - 1)
    def _():
        @pl.when(n_valid >= 2)
        def _():
            wait_rows(scatter_row, whole_scatter, n_valid - 2, n_valid % 2)

        wait_rows(scatter_row, whole_scatter, n_valid - 1, (n_valid - 1) % 2)


def moe_experts(h2, plan, w_gate, w_up, w_down):
    n_rows, d = h2.shape
    _, _, ff = w_gate.shape
    n_tiles = plan["tile_expert"].shape[0]
    grid_spec = pltpu.PrefetchScalarGridSpec(
        num_scalar_prefetch=5,
        grid=(n_tiles,),
        in_specs=[pl.BlockSpec(memory_space=pl.ANY),
                  pl.BlockSpec((1, d, ff), lambda i, te, *_: (te[i], 0, 0)),
                  pl.BlockSpec((1, d, ff), lambda i, te, *_: (te[i], 0, 0)),
                  pl.BlockSpec((1, ff, d), lambda i, te, *_: (te[i], 0, 0))],
        out_specs=pl.BlockSpec(memory_space=pl.ANY),
        scratch_shapes=[pltpu.VMEM((2, MOE_TILE, d), F32), pltpu.VMEM((2, MOE_TILE, d), F32),
                        pltpu.SemaphoreType.DMA((2,)), pltpu.SemaphoreType.DMA((2,))],
    )
    return pl.pallas_call(
        _moe_kernel,
        grid_spec=grid_spec,
        out_shape=jax.ShapeDtypeStruct((2 * n_rows, d), F32),
        compiler_params=_params("arbitrary"),
    )(plan["tile_expert"], plan["n_valid"], plan["n_real"], plan["src_row"], plan["dst_row"], h2,
      w_gate, w_up, w_down)


def _route_kernel(lg_ref, idx_ref, wt_ref, cnt_ref, carry):
    i = pl.program_id(0)

    @pl.when(i == 0)
    def _():
        carry[...] = jnp.zeros_like(carry)

    lg = lg_ref[...]
    n, width = lg.shape
    lane = lax.broadcasted_iota(jnp.int32, (n, width), 1)
    row_max = lambda a: jnp.max(a, axis=1, keepdims=True)
    first_lane = lambda hit: jnp.min(jnp.where(hit, lane, width), axis=1, keepdims=True)

    is_group = lane < N_GROUPS
    g_exp = jnp.where(is_group, jnp.exp(lg - row_max(jnp.where(is_group, lg, -jnp.inf))), 0.0)
    p_group = g_exp / jnp.sum(g_exp, axis=1, keepdims=True)
    p_sel = row_max(p_group)
    g_sel = first_lane(is_group & (p_group == p_sel))
    lo = N_GROUPS + g_sel * EXPERTS_PER_GROUP
    in_group = (lane >= lo) & (lane < lo + EXPERTS_PER_GROUP)
    cand = jnp.where(in_group, lg, -jnp.inf)
    v1 = row_max(cand)
    l1 = first_lane(in_group & (cand == v1))
    cand2 = jnp.where(lane == l1, -jnp.inf, cand)
    v2 = row_max(cand2)
    l2 = first_lane(in_group & (lane != l1) & (cand2 == v2))
    t = jnp.exp(v2 - v1)
    w1 = p_sel / (1.0 + t)
    w2 = p_sel * t / (1.0 + t)
    e1, e2 = l1 - N_GROUPS, l2 - N_GROUPS

    hit1, hit2 = lane == e1, lane == e2
    both = (hit1 | hit2).astype(BF16)
    r_i = lax.broadcasted_iota(jnp.int32, (n, n), 0)
    c_i = lax.broadcasted_iota(jnp.int32, (n, n), 1)
    before = carry[0:1, :] + _dot((r_i > c_i).astype(BF16), both)
    rank1 = jnp.sum(jnp.where(hit1, before, 0.0), axis=1, keepdims=True).astype(jnp.int32)
    rank2 = jnp.sum(jnp.where(hit2, before, 0.0), axis=1, keepdims=True).astype(jnp.int32)
    carry[0:1, :] = carry[0:1, :] + jnp.sum(both.astype(F32), axis=0, keepdims=True)

    idx_ref[...] = jnp.where(lane == 0, e1, jnp.where(lane == 1, e2, jnp.where(lane == 2, rank1, rank2)))
    wt_ref[...] = jnp.where(lane == 0, w1, w2)

    @pl.when(i == pl.num_programs(0) - 1)
    def _():
        cnt_ref[...] = carry[...]


def route(logits):
    n_rows, width = logits.shape
    tile = ROW_TILE
    rows = pl.BlockSpec((tile, width), lambda i: (i, 0))
    idx, wt, cnt = pl.pallas_call(
        _route_kernel,
        grid=(n_rows // tile,),
        in_specs=[rows],
        out_specs=[rows, rows, pl.BlockSpec((SUBLANES, width), lambda i: (0, 0))],
        out_shape=[jax.ShapeDtypeStruct((n_rows, width), jnp.int32), jax.ShapeDtypeStruct((n_rows, width), F32),
                   jax.ShapeDtypeStruct((SUBLANES, width), F32)],
        scratch_shapes=[pltpu.VMEM((SUBLANES, width), F32)],
        compiler_params=_params("arbitrary"),
    )(logits)
    return idx[:, 0:2], idx[:, 2:4], wt[:, 0:2], cnt[0, :N_EXPERTS].astype(jnp.int32)


def _moe_plan(expert, rank, counts):
    n = expert.shape[0]
    e_flat = expert.reshape(-1)
    ids = jnp.arange(N_EXPERTS, dtype=jnp.int32)
    one_hot = (e_flat[:, None] == ids[None, :]).astype(jnp.int32)
    padded = (counts + MOE_TILE - 1) // MOE_TILE * MOE_TILE
    ends = jnp.cumsum(padded)
    starts = ends - padded
    pos = jnp.sum(one_hot * starts[None, :], axis=1) + rank.reshape(-1)
    p_max = _round_up(2 * n, MOE_TILE) + N_EXPERTS * MOE_TILE
    n_tiles = p_max // MOE_TILE
    pair = jnp.full((p_max,), -1, jnp.int32).at[pos].set(jnp.arange(2 * n, dtype=jnp.int32))
    pair0 = jnp.maximum(pair, 0)
    src_row = pair0 // 2
    dst_row = (pair0 % 2) * n + src_row
    tile_start = jnp.arange(n_tiles, dtype=jnp.int32) * MOE_TILE
    n_valid = ends[-1] // MOE_TILE
    tile_expert = jnp.sum((tile_start[:, None] >= ends[None, :]).astype(jnp.int32), axis=1)
    tile_expert = jnp.where(tile_start < ends[-1], tile_expert, tile_expert[n_valid - 1])
    n_real = jnp.clip((starts + counts)[tile_expert] - tile_start, 0, MOE_TILE)
    n_real = jnp.where(tile_start < ends[-1], n_real, 0).astype(jnp.int32)
    return dict(src_row=src_row, dst_row=dst_row, tile_expert=tile_expert,
                n_valid=n_valid.astype(jnp.int32)[None], n_real=n_real)


def _final_kernel(x1_ref, y0_ref, y1_ref, wt_ref, gtp_ref, gts_ref, scp_ref, scs_ref, shp_ref, shs_ref, nw_ref,
                  yp_ref, ys_ref, *, n_prompt_tiles):
    i = pl.program_id(0)

    def finish(gt, sc, sh):
        moe = wt_ref[:, 0:1] * y0_ref[...] + wt_ref[:, 1:2] * y1_ref[...]
        x2 = x1_ref[...] + gt * moe
        return _rmsnorm(x2, nw_ref[...], NORM_EPS) * (1.0 + sc) + sh

    @pl.when(i < n_prompt_tiles)
    def _():
        yp_ref[...] = finish(gtp_ref[0], scp_ref[0], shp_ref[0])

    @pl.when(i >= n_prompt_tiles)
    def _():
        ys_ref[...] = finish(gts_ref[...], scs_ref[...], shs_ref[...])


def final_norm(n_p, t, x1, y_pairs, w_top, mod_p, mod_s, modf_p, modf_s, norm_w):
    n_rows, d = x1.shape
    tile = ROW_TILE
    n_tiles, npt = n_rows // tile, n_p // tile
    per = t // tile
    rows = lambda off: pl.BlockSpec((tile, d), lambda i: (i + off, 0))
    prompt_out = pl.BlockSpec((tile, d), lambda i: (jnp.minimum(i, npt - 1), 0))
    sample = lambda col: pl.BlockSpec((tile, d), lambda i: (jnp.maximum(i - npt, 0), col))
    pmod = lambda col: pl.BlockSpec((1, 1, d), lambda i: (jnp.minimum(i // per, mod_p.shape[0] - 1), 0, col))
    return pl.pallas_call(
        functools.partial(_final_kernel, n_prompt_tiles=npt),
        grid=(n_tiles,),
        in_specs=[rows(0), rows(0), rows(n_tiles), pl.BlockSpec((tile, w_top.shape[1]), lambda i: (i, 0)),
                  pmod(5), sample(5), pmod(1), sample(1), pmod(0), sample(0),
                  pl.BlockSpec((1, d), lambda i: (0, 0))],
        out_specs=[prompt_out, sample(0)],
        out_shape=[jax.ShapeDtypeStruct((n_p, d), F32), jax.ShapeDtypeStruct((n_rows - n_p, d), F32)],
        compiler_params=_params("arbitrary"),
    )(x1, y_pairs, y_pairs, w_top, mod_p, mod_s, modf_p, modf_s, modf_p, modf_s, norm_w)


def _align_rows(wt):
    parts = []
    for name in _ALIGNED:
        off, w = SRC[name]
        part = wt[off:off + w]
        wa = DST[name][1]
        if wa != w:
            part = jnp.pad(part, ((0, wa - w), (0, 0)))
        parts.append(part)
    return jnp.concatenate(parts, axis=0)


def _rwkv_seg(a, name):
    off, w = SRC[name]
    return a[..., off - RWKV_SRC_BASE:off - RWKV_SRC_BASE + w]


def _rwkv_seg_padded(a, name):
    seg = _rwkv_seg(a, name)
    wa = DST[name][1]
    return jnp.pad(seg, [(0, 0)] * (a.ndim - 1) + [(0, wa - seg.shape[-1])])


def _orig_seg(p, name):
    off, w = SRC[name]
    return p[..., off:off + w]


def _pad_rows(w, rows):
    return jnp.pad(w, ((0, rows - w.shape[0]), (0, 0)))


def kernel(x_prompt, x_sample, state_gla, state_rwkv, state_shift, c_prompt, c_sample, w_ada, b_ada, norm_mix, norm_ffn, w_in, gla_gate_w2, gla_gate_b, gla_norm, rwkv_mu, rwkv_w0, rwkv_w2, rwkv_a0, rwkv_a2, rwkv_g2, rwkv_k_k, rwkv_k_a, rwkv_r_k, rwkv_gn_w, rwkv_gn_b, w_out, w_router_group, w_router_expert, w_exp_gate, w_exp_up, w_exp_down, norm_final, w_ada_final, b_ada_final):
    assert w_ada.shape[0] == 1, "single-layer step"
    bp, t, d = x_prompt.shape
    bs = x_sample.shape[0]
    assert x_sample.shape[1] == 1 and t % ROW_TILE == 0
    n_p = bp * t
    pad_s = _round_up(bs, ROW_TILE)
    n_rows = n_p + pad_s
    prompt = _PromptShape(bp, t, n_p)
    hi = lax.Precision.HIGHEST
    pad_sample = lambda a: jnp.pad(a, ((0, pad_s - a.shape[0]), (0, 0)))
    hdot = lambda a, w: jnp.dot(a, w, precision=hi)

    c_act = jax.nn.silu(jnp.concatenate([c_prompt, c_sample], axis=0))
    mod = matmul3(c_act, w_ada[0]) + b_ada[0]
    mod_f = matmul3(c_act, w_ada_final) + b_ada_final
    mod_p, mod_s = mod[:bp, None, :], mod[bp:]
    modf_p, modf_s = mod_f[:bp, None, :], mod_f[bp:]
    sh1_s, sc1_s, gt1_s, sh2_s, sc2_s, _ = jnp.split(mod_s, 6, axis=-1)
    xp = x_prompt.reshape(n_p, d)
    xs = x_sample[:, 0, :]

    w_in_t = jnp.swapaxes(w_in[0], 0, 1)
    proj = in_projection(xp, t, mod_p, norm_mix, _align_rows(w_in_t.astype(BF16)))
    o_gla, gla_t_p = gla_prompt(prompt, proj, _pad_rows(gla_gate_w2[0], LANES), gla_gate_b, gla_norm)
    new_gla_p = jnp.swapaxes(gla_t_p, -1, -2)
    mu = rwkv_mu[0]
    mu_big = jnp.concatenate([_rwkv_seg(mu, n) for n in ("r", "k7", "v7")])[None, :]
    mu_small = jnp.concatenate([_rwkv_seg_padded(mu, n) for n in ("wl", "al", "gl7")])[None, :]
    vecs = jnp.concatenate([rwkv_w0, rwkv_a0, rwkv_k_k, rwkv_k_a, rwkv_r_k[0].reshape(1, RWKV_WIDTH),
                            rwkv_gn_w, rwkv_gn_b, jnp.zeros((1, RWKV_WIDTH), F32)], axis=0)
    o_rwkv, rwkv_t_p = rwkv7_prompt(prompt, proj, mu_big, mu_small, vecs, _pad_rows(rwkv_w2[0], LANES),
                                    _pad_rows(rwkv_a2[0], LANES), rwkv_g2[0])
    new_rwkv_p = rwkv_t_p.reshape(bp, RWKV_HEAD, RWKV_HEADS, RWKV_HEAD).transpose(0, 2, 1, 3)
    last = jnp.stack([proj[(b + 1) * t - 1] for b in range(bp)])
    new_shift_p = jnp.concatenate([last[:, DST[n][0]:DST[n][0] + SRC[n][1]] for n in _RWKV_ORIG], axis=-1)

    h1_s = _rmsnorm(xs, norm_mix[0], NORM_EPS) * (1.0 + sc1_s) + sh1_s
    proj_s = matmul3(h1_s, w_in_t, w_transposed=True)
    heads = lambda a, n: a.reshape(a.shape[0], n, -1)
    logd_s = jax.nn.log_sigmoid(hdot(_orig_seg(proj_s, "gl"), gla_gate_w2[0]) + gla_gate_b[0]) \
        / GLA_GATE_NORMALIZER
    q_s = heads(_orig_seg(proj_s, "q") * (GLA_DK ** -0.5), GLA_HEADS)
    k_s, v_s, g_s = heads(_orig_seg(proj_s, "k"), GLA_HEADS), heads(_orig_seg(proj_s, "v"), GLA_HEADS), \
        heads(logd_s, GLA_HEADS)
    new_gla_s, o_s = gla_step(state_gla[0], q_s, k_s, g_s, v_s)
    o_s = _rmsnorm(o_s, gla_norm[0], GLA_NORM_EPS) * heads(jax.nn.silu(_orig_seg(proj_s, "og")), GLA_HEADS)

    rp_s = proj_s[:, RWKV_SRC_BASE:]
    new_shift_s = rp_s
    xs7 = rp_s + (state_shift[0] - rp_s) * mu
    sx = lambda name: _rwkv_seg(xs7, name)
    r_s, k7_s, v7_s = sx("r"), sx("k7"), sx("v7")
    w_pre = rwkv_w0[0] + hdot(jnp.tanh(sx("wl")), rwkv_w2[0])
    decay_s = jnp.exp(-jnp.exp(-jax.nn.softplus(-w_pre) - 0.5))
    a_s = jax.nn.sigmoid(rwkv_a0[0] + hdot(sx("al"), rwkv_a2[0]))
    g_s7 = hdot(jax.nn.sigmoid(sx("gl7")), rwkv_g2[0])
    hs = lambda z: z.reshape(bs, RWKV_HEADS, RWKV_HEAD)
    kk_s = hs(k7_s * rwkv_k_k[0])
    kk_s = kk_s / jnp.maximum(jnp.sqrt(jnp.sum(kk_s * kk_s, axis=-1, keepdims=True)), 1e-12)
    k7_s = k7_s * (1.0 + (a_s - 1.0) * rwkv_k_a[0])
    r_h, w_h, k_h, v_h, a_h = hs(r_s), hs(decay_s), hs(k7_s), hs(v7_s), hs(a_s)
    s_prev = state_rwkv[0]
    sa = jnp.einsum("bhij,bhj->bhi", s_prev, -kk_s, precision=hi)
    new_rwkv_s = s_prev * w_h[:, :, None, :] + sa[..., None] * (kk_s * a_h)[:, :, None, :] \
        + v_h[..., None] * k_h[:, :, None, :]
    y_s = jnp.einsum("bhij,bhj->bhi", new_rwkv_s, r_h, precision=hi)
    y_mu = jnp.mean(y_s, axis=-1, keepdims=True)
    y_var = jnp.mean(jnp.square(y_s - y_mu), axis=-1, keepdims=True)
    y_s = (y_s - y_mu) * lax.rsqrt(y_var + RWKV_GN_EPS) * rwkv_gn_w[0].reshape(RWKV_HEADS, RWKV_HEAD) \
        + rwkv_gn_b[0].reshape(RWKV_HEADS, RWKV_HEAD)
    bonus = jnp.sum(r_h * k_h * rwkv_r_k[0], axis=-1, keepdims=True) * v_h
    o_rs = (y_s + bonus).reshape(bs, RWKV_WIDTH) * g_s7
    mix_s = matmul3(jnp.concatenate([o_s.reshape(bs, GLA_WIDTH), o_rs], axis=-1), w_out[0])
    x1_s = xs + gt1_s * mix_s
    h2_s = _rmsnorm(x1_s, norm_ffn[0], NORM_EPS) * (1.0 + sc2_s) + sh2_s
    w_router = jnp.pad(jnp.concatenate([w_router_group[0], w_router_expert[0]], axis=-1),
                       ((0, 0), (0, LANES - N_GROUPS - N_EXPERTS)))
    logits_s = hdot(h2_s, w_router)

    x1, h2, logits = out_projection(t, o_gla, o_rwkv, xp, mod_p, norm_ffn, w_out[0].astype(BF16), w_router,
                                    pad_sample(x1_s), pad_sample(h2_s), pad_sample(logits_s))

    expert, rank, w_top, counts = route(logits)
    plan = _moe_plan(expert, rank, counts)
    y_pairs = moe_experts(h2, plan, w_exp_gate[0].reshape(N_EXPERTS, d, EXPERT_FF),
                          w_exp_up[0].reshape(N_EXPERTS, d, EXPERT_FF),
                          w_exp_down[0].reshape(N_EXPERTS, EXPERT_FF, d))

    y_p, y_s_pad = final_norm(n_p, t, x1, y_pairs, w_top, mod_p, pad_sample(mod_s), modf_p,
                              pad_sample(modf_s), norm_final[None, :])
    return (y_p.reshape(bp, t, d), y_s_pad[:bs, None, :], new_gla_p[None], new_rwkv_p[None], new_shift_p[None],
            new_gla_s[None], new_rwkv_s[None], new_shift_s[None])
```

```python
import collections
import functools

import jax
import jax.numpy as jnp
from jax import lax
from jax.experimental import pallas as pl
from jax.experimental.pallas import tpu as pltpu

F32 = jnp.float32
BF16 = jnp.bfloat16

D_MODEL = 2048
GLA_HEADS = 4
GLA_DK = 128
GLA_DV = 256
GLA_KEY_WIDTH = GLA_HEADS * GLA_DK
GLA_WIDTH = GLA_HEADS * GLA_DV
GLA_GATE_NORMALIZER = 16.0
RWKV_HEAD = 64
RWKV_HEADS = 16
RWKV_WIDTH = RWKV_HEAD * RWKV_HEADS
N_GROUPS = 4
EXPERTS_PER_GROUP = 8
N_EXPERTS = N_GROUPS * EXPERTS_PER_GROUP
EXPERT_FF = 512
NORM_EPS = 1e-6
GLA_NORM_EPS = 1e-5
RWKV_GN_EPS = 64e-5

LANES = 128
SUBLANES = 8
VMEM_LIMIT_BYTES = 56 * 1024 * 1024

CHUNK = 64
SUB = 16
RWKV_GROUP = 2
GROUP_W = RWKV_GROUP * RWKV_HEAD
GLA_SEQS = 4
LOG2E = 1.4426950408889634
RWKV_SEQS = 4
MOE_TILE = 256
SAMPLE_BLOCK = 8
ROW_TILE = 256
NORM_SLAB = 128
K_SPLIT = 256

_ORIG = (("q", 512), ("k", 512), ("v", 1024), ("gl", 16), ("og", 1024),
         ("r", 1024), ("wl", 64), ("k7", 1024), ("v7", 1024), ("al", 64), ("gl7", 128))
_ALIGNED = ("q", "k", "v", "og", "r", "k7", "v7", "gl", "wl", "al", "gl7")
_RWKV_ORIG = ("r", "wl", "k7", "v7", "al", "gl7")


def _round_up(n, m):
    return (n + m - 1) // m * m


def _layouts():
    src, off = {}, 0
    for name, w in _ORIG:
        src[name] = (off, w)
        off += w
    dst, pos = {}, 0
    for name in _ALIGNED:
        wa = _round_up(src[name][1], LANES)
        dst[name] = (pos, wa)
        pos += wa
    return src, off, dst, pos


SRC, IN_COLS, DST, IN_COLS_ALIGNED = _layouts()
RWKV_SRC_BASE = SRC["r"][0]
BIG_W = 3 * RWKV_WIDTH
SMALL_W = 3 * LANES
assert DST["r"][0] == BIG_W and DST["k7"][0] == BIG_W + RWKV_WIDTH and DST["v7"][0] == BIG_W + 2 * RWKV_WIDTH
assert DST["v"][0] == RWKV_WIDTH and DST["og"][0] == 2 * RWKV_WIDTH and DST["k"][0] == GLA_KEY_WIDTH

NN = ((1,), (0,))
NT = ((1,), (1,))

_PromptShape = collections.namedtuple("_PromptShape", "bp t rows")


def _dot(a, b, dims=NN):
    return lax.dot_general(a, b, (dims, ((), ())), preferred_element_type=F32)


def _split2(x):
    hi = x.astype(BF16)
    lo = (x - hi.astype(F32)).astype(BF16)
    return hi, lo


def _dotp(a, b, dims=NN, passes=1):
    if passes == 1:
        return _dot(a.astype(BF16), b.astype(BF16), dims)
    ah, al = _split2(a)
    bh, bl = _split2(b)
    return _dot(ah, bh, dims) + (_dot(ah, bl, dims) + _dot(al, bh, dims))


def _cumsum_rows(x):
    n = x.shape[0]
    row = lax.broadcasted_iota(jnp.int32, (n, n), 0)
    col = lax.broadcasted_iota(jnp.int32, (n, n), 1)
    tri = (row >= col).astype(BF16)
    x1 = x.astype(BF16)
    r1 = x - x1.astype(F32)
    x2 = r1.astype(BF16)
    x3 = (r1 - x2.astype(F32)).astype(BF16)
    return _dot(tri, x1) + (_dot(tri, x2) + _dot(tri, x3))


def _softplus(z):
    return jnp.maximum(z, 0.0) + jnp.log1p(jnp.exp(-jnp.abs(z)))


def _rmsnorm(x, g, eps):
    return x * lax.rsqrt(jnp.mean(x * x, axis=-1, keepdims=True) + eps) * g


def _largest_tile(n, cap, mult=SUBLANES):
    if n <= cap:
        return n
    best = None
    for t in range(mult, cap + 1, mult):
        if n % t == 0:
            best = t
    assert best is not None, (n, cap)
    return best


def _params(*sem):
    return pltpu.CompilerParams(dimension_semantics=sem, vmem_limit_bytes=VMEM_LIMIT_BYTES)


def _mm3_kernel(a_ref, w_ref, o_ref, *, dims):
    @pl.when(pl.program_id(0) == 0)
    def _():
        o_ref[...] = jnp.zeros_like(o_ref)

    o_ref[...] += _dotp(a_ref[...], w_ref[...], dims, 3)


def matmul3(a, w, w_transposed=False):
    m, k = a.shape
    n = w.shape[0] if w_transposed else w.shape[1]
    tk = _largest_tile(k, K_SPLIT, LANES)
    w_spec = pl.BlockSpec((n, tk), lambda s: (0, s)) if w_transposed else pl.BlockSpec((tk, n), lambda s: (s, 0))
    return pl.pallas_call(
        functools.partial(_mm3_kernel, dims=NT if w_transposed else NN),
        grid=(k // tk,),
        in_specs=[pl.BlockSpec((m, tk), lambda s: (0, s)), w_spec],
        out_specs=pl.BlockSpec((m, n), lambda s: (0, 0)),
        out_shape=jax.ShapeDtypeStruct((m, n), F32),
        compiler_params=_params("arbitrary"),
    )(a, w)


def _prompt_mod(tiles_per_batch, width, col):
    return pl.BlockSpec((1, 1, width), lambda i, *_: (i // tiles_per_batch, 0, col))


def _inproj_kernel(x_ref, sc_ref, sh_ref, nw_ref, w_ref, o_ref, h_scr):
    @pl.when(pl.program_id(1) == 0)
    def _():
        for r in range(0, h_scr.shape[0], NORM_SLAB):
            sl = slice(r, r + NORM_SLAB)
            h = _rmsnorm(x_ref[sl, :], nw_ref[...], NORM_EPS) * (1.0 + sc_ref[0]) + sh_ref[0]
            h_scr[sl, :] = h.astype(BF16)

    o_ref[...] = _dot(h_scr[...], w_ref[...], NT)


def in_projection(xp, t, mod_p, norm_w, w_aligned):
    n_p, d = xp.shape
    n = w_aligned.shape[0]
    tm = _largest_tile(t, 1024, NORM_SLAB)
    tn = _largest_tile(n, 1664, LANES)
    per = t // tm
    return pl.pallas_call(
        _inproj_kernel,
        grid=(n_p // tm, n // tn),
        in_specs=[pl.BlockSpec((tm, d), lambda i, j: (i, 0)), _prompt_mod(per, d, 1), _prompt_mod(per, d, 0),
                  pl.BlockSpec((1, d), lambda i, j: (0, 0)), pl.BlockSpec((tn, d), lambda i, j: (j, 0))],
        out_specs=pl.BlockSpec((tm, tn), lambda i, j: (i, j)),
        out_shape=jax.ShapeDtypeStruct((n_p, n), F32),
        scratch_shapes=[pltpu.VMEM((tm, d), BF16)],
        compiler_params=_params("parallel", "arbitrary"),
    )(xp, mod_p, mod_p, norm_w, w_aligned)


def _rwkv_chunk_kernel(big_ref, wl_ref, al_ref, gl_ref, mu_big_ref, mu_small_ref, vec_ref, w2_ref, a2_ref,
                       g2_ref, o_ref, s_ref, state, carry_big, carry_small, *, passes):
    c_idx = pl.program_id(1)
    n_seq, n_tok = big_ref.shape[0], big_ref.shape[1]
    n_grp = RWKV_WIDTH // GROUP_W
    grp = [slice(g * GROUP_W, (g + 1) * GROUP_W) for g in range(n_grp)]

    @pl.when(c_idx == 0)
    def _():
        state[...] = jnp.zeros_like(state)
        carry_big[...] = jnp.zeros_like(carry_big)
        carry_small[...] = jnp.zeros_like(carry_small)

    first_row = lax.broadcasted_iota(jnp.int32, (n_tok, 1), 0) == 0
    w0, a0, k_k, k_a, r_k, gn_w, gn_b = (vec_ref[i:i + 1, :] for i in range(7))

    gi = lax.broadcasted_iota(jnp.int32, (GROUP_W, GROUP_W), 0) // RWKV_HEAD
    gj = lax.broadcasted_iota(jnp.int32, (GROUP_W, GROUP_W), 1) // RWKV_HEAD
    head_ones = (gi == gj).astype(BF16)

    def head_sums(parts):
        hi, lo = _split2(jnp.concatenate(parts, axis=0))
        out = _dot(hi, head_ones) + _dot(lo, head_ones)
        return [out[i * n_tok:(i + 1) * n_tok] for i in range(len(parts))]

    def prepare(n):
        def token_shift(cur, carry, mu):
            prev = jnp.where(first_row, carry[n, 0:1, :], pltpu.roll(cur, 1, 0))
            carry[n, 0:1, :] = cur[n_tok - 1:n_tok, :]
            return cur + (prev - cur) * mu

        xs_big = token_shift(big_ref[n], carry_big, mu_big_ref[...])
        small = jnp.concatenate([wl_ref[n], al_ref[n], gl_ref[n]], axis=1)
        xs_small = token_shift(small, carry_small, mu_small_ref[...])
        r = xs_big[:, :RWKV_WIDTH]
        k7 = xs_big[:, RWKV_WIDTH:2 * RWKV_WIDTH]
        v = xs_big[:, 2 * RWKV_WIDTH:]
        w_pre = w0 + _dotp(jnp.tanh(xs_small[:, :LANES]), w2_ref[...])
        lw = -jnp.exp(-_softplus(-w_pre) - 0.5)
        a = jax.nn.sigmoid(a0 + _dotp(xs_small[:, LANES:2 * LANES], a2_ref[...]))
        gate = _dotp(jax.nn.sigmoid(xs_small[:, 2 * LANES:]), g2_ref[...])
        kk_raw = k7 * k_k
        k = k7 * (1.0 + (a - 1.0) * k_a)
        sums = [head_sums([kk_raw[:, sl] * kk_raw[:, sl], r[:, sl] * k[:, sl] * r_k[:, sl]]) for sl in grp]
        kk = jnp.concatenate([kk_raw[:, sl] / jnp.maximum(jnp.sqrt(sums[g][0]), 1e-12)
                              for g, sl in enumerate(grp)], axis=1)
        bonus = jnp.concatenate([sums[g][1] for g in range(n_grp)], axis=1) * v
        cum = _cumsum_rows(lw)
        cum_end = cum[n_tok - 1:n_tok, :]
        beta = kk * a
        g_inv = jnp.exp(-cum)
        g_end = jnp.exp(cum_end - cum)
        return dict(v=v, gate=gate, bonus=bonus, g_tot=jnp.exp(cum_end),
                    a_hat=-kk * jnp.exp(cum - lw), r_hat=r * jnp.exp(cum), b_hat=beta * g_inv,
                    k_hat=k * g_inv, b_end=beta * g_end, k_end=k * g_end)

    seqs = [prepare(n) for n in range(n_seq)]
    units = [(n, sl) for n in range(n_seq) for sl in grp]
    part = lambda name: [seqs[n][name][:, sl] for n, sl in units]

    lane = lax.broadcasted_iota(jnp.int32, (n_tok, GROUP_W), 1)
    tok = lax.broadcasted_iota(jnp.int32, (n_tok, GROUP_W), 0)
    lane_head = lane // RWKV_HEAD
    src_tok = lane % RWKV_HEAD
    strict = tok > src_tok
    incl = tok >= src_tok

    def bd(y):
        return jnp.concatenate([jnp.where(lane_head == h, y, 0.0) for h in range(RWKV_GROUP)], axis=0)

    gs = range(len(units))
    s0 = [state[n, :, sl] for n, sl in units]
    v_u, b_hat, k_hat, b_end, k_end = part("v"), part("b_hat"), part("k_hat"), part("b_end"), part("k_end")
    lhs2 = [jnp.concatenate([a, r], axis=0) for a, r in zip(part("a_hat"), part("r_hat"))]
    abrb = [_dotp(lhs2[g], bd(b_hat[g]), NT, passes) for g in gs]
    akrk = [_dotp(lhs2[g], bd(k_hat[g]), NT, passes) for g in gs]
    asrs = [_dotp(lhs2[g], bd(s0[g]), NT, passes) for g in gs]
    p = [jnp.where(strict, abrb[g][:n_tok], 0.0) for g in gs]
    ak = [jnp.where(strict, akrk[g][:n_tok], 0.0) for g in gs]
    rb = [jnp.where(incl, abrb[g][n_tok:], 0.0) for g in gs]
    rk = [jnp.where(incl, akrk[g][n_tok:], 0.0) for g in gs]
    bd_v = [bd(v_u[g]) for g in gs]
    x = [asrs[g][:n_tok] + _dotp(ak[g], bd_v[g], NN, passes) for g in gs]
    n_sq = n_tok.bit_length() - 1
    for it in range(n_sq):
        if it < n_sq - 1:
            both = [_dotp(p[g], jnp.concatenate([bd(p[g]), bd(x[g])], axis=1), NN, passes) for g in gs]
            x = [x[g] + both[g][:, GROUP_W:] for g in gs]
            p = [both[g][:, :GROUP_W] for g in gs]
        else:
            x = [x[g] + _dotp(p[g], bd(x[g]), NN, passes) for g in gs]
    y = [asrs[g][n_tok:] + _dotp(jnp.concatenate([rb[g], rk[g]], axis=1),
                                 jnp.concatenate([bd(x[g]), bd_v[g]], axis=0), NN, passes) for g in gs]
    full = [_dotp(jnp.concatenate([x[g], v_u[g]], axis=0).T,
                  jnp.concatenate([b_end[g], k_end[g]], axis=0), NN, passes) for g in gs]
    g_tot = part("g_tot")
    for g, (n, sl) in enumerate(units):
        upd = s0[g] * g_tot[g]
        for h in range(RWKV_GROUP):
            upd = upd + jnp.where(lane_head == h, full[g][h * RWKV_HEAD:(h + 1) * RWKV_HEAD, :], 0.0)
        state[n, :, sl] = upd

    inv_n = 1.0 / RWKV_HEAD
    mean = [head_sums([y[g]])[0] * inv_n for g in gs]
    dev = [y[g] - mean[g] for g in gs]
    var = [head_sums([dev[g] * dev[g]])[0] * inv_n for g in gs]
    bonus, gate = part("bonus"), part("gate")
    for g, (n, sl) in enumerate(units):
        yn = dev[g] * lax.rsqrt(var[g] + RWKV_GN_EPS) * gn_w[:, sl] + gn_b[:, sl]
        o_ref[n, :, sl] = ((yn + bonus[g]) * gate[g]).astype(o_ref.dtype)

    @pl.when(c_idx == pl.num_programs(1) - 1)
    def _():
        s_ref[...] = state[...]


def rwkv7_prompt(rows, proj, mu_big, mu_small, vecs, w2, a2, g2, *, passes=1):
    assert CHUNK == RWKV_HEAD and rows.t % CHUNK == 0
    nc = rows.t // CHUNK
    n_seq = _largest_tile(rows.bp, RWKV_SEQS, 1)
    proj3 = proj.reshape(rows.bp, rows.t, proj.shape[1])
    small_col = lambda name: DST[name][0] // LANES
    tok = lambda width, col: pl.BlockSpec((n_seq, CHUNK, width), lambda b, c: (b, c, col))
    const = lambda shape: pl.BlockSpec(shape, lambda b, c: (0, 0))
    out, final_state = pl.pallas_call(
        functools.partial(_rwkv_chunk_kernel, passes=passes),
        grid=(rows.bp // n_seq, nc),
        in_specs=[tok(BIG_W, 1), tok(LANES, small_col("wl")), tok(LANES, small_col("al")),
                  tok(LANES, small_col("gl7")),
                  const((1, BIG_W)), const((1, SMALL_W)), const((SUBLANES, RWKV_WIDTH)),
                  const((LANES, RWKV_WIDTH)), const((LANES, RWKV_WIDTH)), const((LANES, RWKV_WIDTH))],
        out_specs=[tok(RWKV_WIDTH, 0),
                   pl.BlockSpec((n_seq, RWKV_HEAD, RWKV_WIDTH), lambda b, c: (b, 0, 0))],
        out_shape=[jax.ShapeDtypeStruct((rows.bp, rows.t, RWKV_WIDTH), BF16),
                   jax.ShapeDtypeStruct((rows.bp, RWKV_HEAD, RWKV_WIDTH), F32)],
        scratch_shapes=[pltpu.VMEM((n_seq, RWKV_HEAD, RWKV_WIDTH), F32),
                        pltpu.VMEM((n_seq, SUBLANES, BIG_W), F32), pltpu.VMEM((n_seq, SUBLANES, SMALL_W), F32)],
        compiler_params=_params("parallel", "arbitrary"),
    )(proj3, proj3, proj3, proj3, mu_big, mu_small, vecs, w2, a2, g2)
    return out.reshape(rows.rows, RWKV_WIDTH), final_state


def _gla_chunk_kernel(qk_ref, v_ref, og_ref, gl_ref, w2_ref, gb_ref, nw_ref, o_ref, s_ref, state):
    c_idx = pl.program_id(1)
    n_seq, n_tok = qk_ref.shape[0], qk_ref.shape[1]
    n_sub = n_tok // SUB

    @pl.when(c_idx == 0)
    def _():
        state[...] = jnp.zeros_like(state)

    row_k = lax.broadcasted_iota(jnp.int32, (n_tok, GLA_DK), 0)
    att_row = lax.broadcasted_iota(jnp.int32, (SUB, n_tok), 0)
    att_col = lax.broadcasted_iota(jnp.int32, (SUB, n_tok), 1)
    own_col = [jnp.where((att_col >= i * SUB) & (att_col - i * SUB <= att_row), att_col - i * SUB, -1)
               for i in range(n_sub)]

    def cum_log2_decay(n):
        logd = -_softplus(-(_dotp(gl_ref[n], w2_ref[...]) + gb_ref[...])) * (LOG2E / GLA_GATE_NORMALIZER)
        return _cumsum_rows(logd)

    units = [(n, h) for n in range(n_seq) for h in range(GLA_HEADS)]
    us = range(len(units))
    ks = lambda h: slice(h * GLA_DK, (h + 1) * GLA_DK)
    vs = lambda h: slice(h * GLA_DV, (h + 1) * GLA_DV)
    b_seq = [cum_log2_decay(n) for n in range(n_seq)]
    q = [qk_ref[n, :, ks(h)] * (GLA_DK ** -0.5) for n, h in units]
    k = [qk_ref[n, :, GLA_KEY_WIDTH + h * GLA_DK:GLA_KEY_WIDTH + (h + 1) * GLA_DK] for n, h in units]
    b = [b_seq[n][:, ks(h)] for n, h in units]
    v = [v_ref[n, :, vs(h)] for n, h in units]
    st = [state[n, h] for n, h in units]
    o_inter = [_dotp(q[u] * jnp.exp2(b[u]), st[u], NT) for u in us]
    blocks = [[] for _ in us]
    for i in range(n_sub):
        lo = i * SUB
        rows = slice(lo, lo + SUB)
        if i > 0:
            att = [_dotp(q[u][rows] * jnp.exp2(b[u][rows] - b[u][lo - 1:lo]),
                         jnp.where(row_k < lo, k[u] * jnp.exp2(b[u][lo - 1:lo] - b[u]), 0.0), NT) for u in us]
        else:
            att = [jnp.zeros((SUB, n_tok), F32) for _ in us]
        for j in range(SUB):
            tok = slice(lo + j, lo + j + 1)
            col = [jnp.sum(q[u][rows] * (k[u][tok] * jnp.exp2(b[u][rows] - b[u][tok])), axis=1, keepdims=True)
                   for u in us]
            att = [jnp.where(own_col[i] == j, col[u], att[u]) for u in us]
        for u in us:
            blocks[u].append(o_inter[u][rows] + _dotp(att[u], v[u], NN))
    for u, (n, h) in enumerate(units):
        o = jnp.concatenate(blocks[u], axis=0)
        og = og_ref[n, :, vs(h)]
        o_ref[n, :, vs(h)] = (_rmsnorm(o, nw_ref[...], GLA_NORM_EPS)
                              * (og * jax.nn.sigmoid(og))).astype(o_ref.dtype)
        b_last = b[u][n_tok - 1:n_tok, :]
        state[n, h] = st[u] * jnp.exp2(b_last) + _dotp(v[u].T, k[u] * jnp.exp2(b_last - b[u]), NN)

    @pl.when(c_idx == pl.num_programs(1) - 1)
    def _():
        s_ref[...] = state[...]


def gla_prompt(rows, proj, gate_w2, gate_b, norm_w):
    nc = rows.t // CHUNK
    n_seq = _largest_tile(rows.bp, GLA_SEQS, 1)
    proj3 = proj.reshape(rows.bp, rows.t, proj.shape[1])
    tok = lambda width, col: pl.BlockSpec((n_seq, CHUNK, width), lambda b, c: (b, c, col))
    const = lambda shape: pl.BlockSpec(shape, lambda b, c: (0, 0))
    out, final_state = pl.pallas_call(
        _gla_chunk_kernel,
        grid=(rows.bp // n_seq, nc),
        in_specs=[tok(GLA_WIDTH, 0), tok(GLA_WIDTH, 1), tok(GLA_WIDTH, 2), tok(LANES, DST["gl"][0] // LANES),
                  const((LANES, GLA_KEY_WIDTH)), const((1, GLA_KEY_WIDTH)), const((1, GLA_DV))],
        out_specs=[tok(GLA_WIDTH, 0),
                   pl.BlockSpec((n_seq, GLA_HEADS, GLA_DV, GLA_DK), lambda b, c: (b, 0, 0, 0))],
        out_shape=[jax.ShapeDtypeStruct((rows.bp, rows.t, GLA_WIDTH), BF16),
                   jax.ShapeDtypeStruct((rows.bp, GLA_HEADS, GLA_DV, GLA_DK), F32)],
        scratch_shapes=[pltpu.VMEM((n_seq, GLA_HEADS, GLA_DV, GLA_DK), F32)],
        compiler_params=_params("parallel", "arbitrary"),
    )(proj3, proj3, proj3, proj3, gate_w2, gate_b, norm_w)
    return out.reshape(rows.rows, GLA_WIDTH), final_state


def _columns(a, block):
    rows, heads, n = a.shape
    return a.reshape(rows // block, block, heads, n).transpose(0, 3, 1, 2).reshape(rows // block, n, block * heads)


def _gla_step_kernel(s_ref, qt_ref, kt_ref, gt_ref, v_ref, so_ref, o_ref):
    n_row, n_head = s_ref.shape[0], s_ref.shape[1]
    qt, kt, decay = qt_ref[0], kt_ref[0], jnp.exp(gt_ref[0])
    for b in range(n_row):
        for h in range(n_head):
            j = b * n_head + h
            s_new = decay[:, j:j + 1] * s_ref[b, h] + kt[:, j:j + 1] * v_ref[b, h:h + 1, :]
            so_ref[b, h] = s_new
            o_ref[b, h:h + 1, :] = jnp.sum(qt[:, j:j + 1] * s_new, axis=0, keepdims=True)


def gla_step(state, q, k, logd, v):
    rows, heads, dk, dv = state.shape
    blk = SAMPLE_BLOCK
    assert rows % blk == 0
    col = pl.BlockSpec((1, dk, blk * heads), lambda i: (i, 0, 0))
    s_spec = pl.BlockSpec((blk, heads, dk, dv), lambda i: (i, 0, 0, 0))
    v_spec = pl.BlockSpec((blk, heads, dv), lambda i: (i, 0, 0))
    return pl.pallas_call(
        _gla_step_kernel,
        grid=(rows // blk,),
        in_specs=[s_spec, col, col, col, v_spec],
        out_specs=[s_spec, v_spec],
        out_shape=[jax.ShapeDtypeStruct(state.shape, F32), jax.ShapeDtypeStruct((rows, heads, dv), F32)],
        compiler_params=_params("parallel"),
    )(state, _columns(q, blk), _columns(k, blk), _columns(logd, blk), v)


def _outproj_kernel(og_ref, orw_ref, x_ref, gt_ref, sc_ref, sh_ref, nw_ref, wo_ref, wr_ref,
                    x1s_ref, h2s_ref, lgs_ref, x1_ref, h2_ref, lg_ref, *, n_prompt_tiles):
    i = pl.program_id(0)

    @pl.when(i < n_prompt_tiles)
    def _():
        half = og_ref.shape[1]
        mix = _dot(og_ref[...], wo_ref[:half, :]) + _dot(orw_ref[...], wo_ref[half:, :])
        x1 = x_ref[...] + gt_ref[0] * mix
        h2 = _rmsnorm(x1, nw_ref[...], NORM_EPS) * (1.0 + sc_ref[0]) + sh_ref[0]
        x1_ref[...] = x1
        h2_ref[...] = h2
        lg_ref[...] = _dotp(h2, wr_ref[...], NN, 3)

    @pl.when(i >= n_prompt_tiles)
    def _():
        x1_ref[...] = x1s_ref[...]
        h2_ref[...] = h2s_ref[...]
        lg_ref[...] = lgs_ref[...]


def out_projection(t, o_gla, o_rwkv, xp, mod_p, norm_w, w_out, w_router, x1_s, h2_s, logits_s):
    n_p, d = xp.shape
    pad_s = x1_s.shape[0]
    tm = ROW_TILE
    npt, per = n_p // tm, t // tm
    prompt = lambda width: pl.BlockSpec((tm, width), lambda i: (jnp.minimum(i, npt - 1), 0))
    pmod = lambda col: pl.BlockSpec((1, 1, d), lambda i: (jnp.minimum(i // per, mod_p.shape[0] - 1), 0, col))
    sample = lambda width: pl.BlockSpec((tm, width), lambda i: (jnp.maximum(i - npt, 0), 0))
    rows = lambda width: pl.BlockSpec((tm, width), lambda i: (i, 0))
    const = lambda a: pl.BlockSpec(a.shape, lambda i: (0, 0), pipeline_mode=pl.Buffered(1))
    n_rows = n_p + pad_s
    return pl.pallas_call(
        functools.partial(_outproj_kernel, n_prompt_tiles=npt),
        grid=(n_rows // tm,),
        in_specs=[prompt(o_gla.shape[1]), prompt(o_rwkv.shape[1]), prompt(d), pmod(2), pmod(4), pmod(3),
                  const(norm_w), const(w_out), const(w_router), sample(d), sample(d), sample(LANES)],
        out_specs=[rows(d), rows(d), rows(LANES)],
        out_shape=[jax.ShapeDtypeStruct((n_rows, d), F32), jax.ShapeDtypeStruct((n_rows, d), F32),
                   jax.ShapeDtypeStruct((n_rows, LANES), F32)],
        compiler_params=_params("arbitrary"),
    )(o_gla, o_rwkv, xp, mod_p, mod_p, mod_p, norm_w, w_out, w_router, x1_s, h2_s, logits_s)


def _moe_kernel(tile_expert_ref, n_valid_ref, n_real_ref, src_ref, dst_ref, h2_hbm, wg_ref, wu_ref, wd_ref,
                y_hbm, xbuf, obuf, gather_sem, scatter_sem):
    del tile_expert_ref
    i = pl.program_id(0)
    n_tiles = pl.num_programs(0)
    n_valid = n_valid_ref[0]
    slot = i % 2

    n_grp = MOE_TILE // SUBLANES
    buf_groups = lambda s: pl.ds(s * n_grp, n_grp)

    def for_rows(n, fn):
        def group(g, carry):
            for u in range(SUBLANES):
                fn(g, u)
            return carry

        def single(r, carry):
            fn(lax.shift_right_logical(r, 3), lax.bitwise_and(r, SUBLANES - 1))
            return carry
        full = lax.shift_right_logical(n, 3)
        lax.fori_loop(0, full, group, 0)
        lax.fori_loop(full * SUBLANES, n, single, 0)

    def gather_row(tile, s, g, u):
        tok = src_ref[tile * MOE_TILE + g * SUBLANES + u]
        return pltpu.make_async_copy(h2_hbm.at[pl.ds(tok, 1), :], xbuf.at[s * n_grp + g, pl.ds(u, 1), :],
                                     gather_sem.at[s])

    def scatter_row(tile, s, g, u):
        row = dst_ref[tile * MOE_TILE + g * SUBLANES + u]
        return pltpu.make_async_copy(obuf.at[s * n_grp + g, pl.ds(u, 1), :], y_hbm.at[pl.ds(row, 1), :],
                                     scatter_sem.at[s])

    def start_rows(row_copy, tile, s):
        for_rows(n_real_ref[tile], lambda g, u: row_copy(tile, s, g, u).start())

    def wait_rows(row_copy, whole_tile_copy, tile, s):
        n_real = n_real_ref[tile]

        @pl.when(n_real == MOE_TILE)
        def _():
            whole_tile_copy(s).wait()

        @pl.when(n_real < MOE_TILE)
        def _():
            for_rows(n_real, lambda g, u: row_copy(tile, s, g, u).wait())

    whole_gather = lambda s: pltpu.make_async_copy(xbuf.at[buf_groups(1 - s)], xbuf.at[buf_groups(s)],
                                                   gather_sem.at[s])
    whole_scatter = lambda s: pltpu.make_async_copy(obuf.at[buf_groups(s)], obuf.at[buf_groups(1 - s)],
                                                    scatter_sem.at[s])

    @pl.when(i == 0)
    def _():
        xbuf[...] = jnp.zeros_like(xbuf)
        start_rows(gather_row, 0, 0)

    @pl.when(i + 1 < n_valid)
    def _():
        start_rows(gather_row, i + 1, 1 - slot)

    @pl.when(i < n_valid)
    def _():
        wait_rows(gather_row, whole_gather, i, slot)

        @pl.when(i >= 2)
        def _():
            wait_rows(scatter_row, whole_scatter, i - 2, slot)

        x = xbuf[buf_groups(slot)].reshape(MOE_TILE, -1).astype(BF16)
        gate = _dot(x, wg_ref[0].astype(BF16))
        up = _dot(x, wu_ref[0].astype(BF16))
        hid = gate * jax.nn.sigmoid(gate) * up
        obuf[buf_groups(slot)] = _dot(hid.astype(BF16), wd_ref[0].astype(BF16)).reshape(n_grp, SUBLANES, -1)
        start_rows(scatter_row, i, slot)

    @pl.when(i == n_tiles - 1)
    def _():
        @pl.when(n_valid >= 2)
        def _():
            wait_rows(scatter_row, whole_scatter, n_valid - 2, n_valid % 2)

        wait_rows(scatter_row, whole_scatter, n_valid - 1, (n_valid - 1) % 2)


def moe_experts(h2, plan, w_gate, w_up, w_down):
    n_rows, d = h2.shape
    _, _, ff = w_gate.shape
    n_tiles = plan["tile_expert"].shape[0]
    grid_spec = pltpu.PrefetchScalarGridSpec(
        num_scalar_prefetch=5,
        grid=(n_tiles,),
        in_specs=[pl.BlockSpec(memory_space=pl.ANY),
                  pl.BlockSpec((1, d, ff), lambda i, te, *_: (te[i], 0, 0)),
                  pl.BlockSpec((1, d, ff), lambda i, te, *_: (te[i], 0, 0)),
                  pl.BlockSpec((1, ff, d), lambda i, te, *_: (te[i], 0, 0))],
        out_specs=pl.BlockSpec(memory_space=pl.ANY),
        scratch_shapes=[pltpu.VMEM((2 * MOE_TILE // SUBLANES, SUBLANES, d), F32)] * 2
        + [pltpu.SemaphoreType.DMA((2,)), pltpu.SemaphoreType.DMA((2,))],
    )
    return pl.pallas_call(
        _moe_kernel,
        grid_spec=grid_spec,
        out_shape=jax.ShapeDtypeStruct((2 * n_rows, d), F32),
        compiler_params=_params("arbitrary"),
    )(plan["tile_expert"], plan["n_valid"], plan["n_real"], plan["src_row"], plan["dst_row"], h2,
      w_gate, w_up, w_down)


def _route_kernel(lg_ref, idx_ref, wt_ref, cnt_ref, carry):
    i = pl.program_id(0)

    @pl.when(i == 0)
    def _():
        carry[...] = jnp.zeros_like(carry)

    lg = lg_ref[...]
    n, width = lg.shape
    lane = lax.broadcasted_iota(jnp.int32, (n, width), 1)
    row_max = lambda a: jnp.max(a, axis=1, keepdims=True)
    first_lane = lambda hit: jnp.min(jnp.where(hit, lane, width), axis=1, keepdims=True)

    is_group = lane < N_GROUPS
    g_exp = jnp.where(is_group, jnp.exp(lg - row_max(jnp.where(is_group, lg, -jnp.inf))), 0.0)
    p_group = g_exp / jnp.sum(g_exp, axis=1, keepdims=True)
    p_sel = row_max(p_group)
    g_sel = first_lane(is_group & (p_group == p_sel))
    lo = N_GROUPS + g_sel * EXPERTS_PER_GROUP
    in_group = (lane >= lo) & (lane < lo + EXPERTS_PER_GROUP)
    cand = jnp.where(in_group, lg, -jnp.inf)
    v1 = row_max(cand)
    l1 = first_lane(in_group & (cand == v1))
    cand2 = jnp.where(lane == l1, -jnp.inf, cand)
    v2 = row_max(cand2)
    l2 = first_lane(in_group & (lane != l1) & (cand2 == v2))
    t = jnp.exp(v2 - v1)
    w1 = p_sel / (1.0 + t)
    w2 = p_sel * t / (1.0 + t)
    e1, e2 = l1 - N_GROUPS, l2 - N_GROUPS

    hit1, hit2 = lane == e1, lane == e2
    both = (hit1 | hit2).astype(BF16)
    r_i = lax.broadcasted_iota(jnp.int32, (n, n), 0)
    c_i = lax.broadcasted_iota(jnp.int32, (n, n), 1)
    before = carry[0:1, :] + _dot((r_i > c_i).astype(BF16), both)
    rank1 = jnp.sum(jnp.where(hit1, before, 0.0), axis=1, keepdims=True).astype(jnp.int32)
    rank2 = jnp.sum(jnp.where(hit2, before, 0.0), axis=1, keepdims=True).astype(jnp.int32)
    carry[0:1, :] = carry[0:1, :] + jnp.sum(both.astype(F32), axis=0, keepdims=True)

    idx_ref[...] = jnp.where(lane == 0, e1, jnp.where(lane == 1, e2, jnp.where(lane == 2, rank1, rank2)))
    wt_ref[...] = jnp.where(lane == 0, w1, w2)

    @pl.when(i == pl.num_programs(0) - 1)
    def _():
        cnt_ref[...] = carry[...]


def route(logits):
    n_rows, width = logits.shape
    tile = ROW_TILE
    rows = pl.BlockSpec((tile, width), lambda i: (i, 0))
    idx, wt, cnt = pl.pallas_call(
        _route_kernel,
        grid=(n_rows // tile,),
        in_specs=[rows],
        out_specs=[rows, rows, pl.BlockSpec((SUBLANES, width), lambda i: (0, 0))],
        out_shape=[jax.ShapeDtypeStruct((n_rows, width), jnp.int32), jax.ShapeDtypeStruct((n_rows, width), F32),
                   jax.ShapeDtypeStruct((SUBLANES, width), F32)],
        scratch_shapes=[pltpu.VMEM((SUBLANES, width), F32)],
        compiler_params=_params("arbitrary"),
    )(logits)
    return idx[:, 0:2], idx[:, 2:4], wt[:, 0:2], cnt[0, :N_EXPERTS].astype(jnp.int32)


def _moe_plan(expert, rank, counts):
    n = expert.shape[0]
    e_flat = expert.reshape(-1)
    ids = jnp.arange(N_EXPERTS, dtype=jnp.int32)
    one_hot = (e_flat[:, None] == ids[None, :]).astype(jnp.int32)
    padded = (counts + MOE_TILE - 1) // MOE_TILE * MOE_TILE
    ends = jnp.cumsum(padded)
    starts = ends - padded
    pos = jnp.sum(one_hot * starts[None, :], axis=1) + rank.reshape(-1)
    p_max = _round_up(2 * n, MOE_TILE) + N_EXPERTS * MOE_TILE
    n_tiles = p_max // MOE_TILE
    pair = jnp.full((p_max,), -1, jnp.int32).at[pos].set(jnp.arange(2 * n, dtype=jnp.int32),
                                                        unique_indices=True, mode="promise_in_bounds")
    pair0 = jnp.maximum(pair, 0)
    src_row = pair0 // 2
    dst_row = (pair0 % 2) * n + src_row
    tile_start = jnp.arange(n_tiles, dtype=jnp.int32) * MOE_TILE
    n_valid = ends[-1] // MOE_TILE
    tile_expert = jnp.sum((tile_start[:, None] >= ends[None, :]).astype(jnp.int32), axis=1)
    tile_expert = jnp.where(tile_start < ends[-1], tile_expert, tile_expert[n_valid - 1])
    n_real = jnp.clip((starts + counts)[tile_expert] - tile_start, 0, MOE_TILE)
    n_real = jnp.where(tile_start < ends[-1], n_real, 0).astype(jnp.int32)
    return dict(src_row=src_row, dst_row=dst_row, tile_expert=tile_expert,
                n_valid=n_valid.astype(jnp.int32)[None], n_real=n_real)


def _final_kernel(x1_ref, y0_ref, y1_ref, wt_ref, gtp_ref, gts_ref, scp_ref, scs_ref, shp_ref, shs_ref, nw_ref,
                  yp_ref, ys_ref, *, n_prompt_tiles):
    i = pl.program_id(0)

    def finish(gt, sc, sh):
        moe = wt_ref[:, 0:1] * y0_ref[...] + wt_ref[:, 1:2] * y1_ref[...]
        x2 = x1_ref[...] + gt * moe
        return _rmsnorm(x2, nw_ref[...], NORM_EPS) * (1.0 + sc) + sh

    @pl.when(i < n_prompt_tiles)
    def _():
        yp_ref[...] = finish(gtp_ref[0], scp_ref[0], shp_ref[0])

    @pl.when(i >= n_prompt_tiles)
    def _():
        ys_ref[...] = finish(gts_ref[...], scs_ref[...], shs_ref[...])


def final_norm(n_p, t, x1, y_pairs, w_top, mod_p, mod_s, modf_p, modf_s, norm_w):
    n_rows, d = x1.shape
    tile = ROW_TILE
    n_tiles, npt = n_rows // tile, n_p // tile
    per = t // tile
    rows = lambda off: pl.BlockSpec((tile, d), lambda i: (i + off, 0))
    prompt_out = pl.BlockSpec((tile, d), lambda i: (jnp.minimum(i, npt - 1), 0))
    sample = lambda col: pl.BlockSpec((tile, d), lambda i: (jnp.maximum(i - npt, 0), col))
    pmod = lambda col: pl.BlockSpec((1, 1, d), lambda i: (jnp.minimum(i // per, mod_p.shape[0] - 1), 0, col))
    return pl.pallas_call(
        functools.partial(_final_kernel, n_prompt_tiles=npt),
        grid=(n_tiles,),
        in_specs=[rows(0), rows(0), rows(n_tiles), pl.BlockSpec((tile, w_top.shape[1]), lambda i: (i, 0)),
                  pmod(5), sample(5), pmod(1), sample(1), pmod(0), sample(0),
                  pl.BlockSpec((1, d), lambda i: (0, 0))],
        out_specs=[prompt_out, sample(0)],
        out_shape=[jax.ShapeDtypeStruct((n_p, d), F32), jax.ShapeDtypeStruct((n_rows - n_p, d), F32)],
        compiler_params=_params("arbitrary"),
    )(x1, y_pairs, y_pairs, w_top, mod_p, mod_s, modf_p, modf_s, modf_p, modf_s, norm_w)


def _align_rows(wt):
    parts = []
    for name in _ALIGNED:
        off, w = SRC[name]
        part = wt[off:off + w]
        wa = DST[name][1]
        if wa != w:
            part = jnp.pad(part, ((0, wa - w), (0, 0)))
        parts.append(part)
    return jnp.concatenate(parts, axis=0)


def _rwkv_seg(a, name):
    off, w = SRC[name]
    return a[..., off - RWKV_SRC_BASE:off - RWKV_SRC_BASE + w]


def _rwkv_seg_padded(a, name):
    seg = _rwkv_seg(a, name)
    wa = DST[name][1]
    return jnp.pad(seg, [(0, 0)] * (a.ndim - 1) + [(0, wa - seg.shape[-1])])


def _orig_seg(p, name):
    off, w = SRC[name]
    return p[..., off:off + w]


def _pad_rows(w, rows):
    return jnp.pad(w, ((0, rows - w.shape[0]), (0, 0)))


def kernel(x_prompt, x_sample, state_gla, state_rwkv, state_shift, c_prompt, c_sample, w_ada, b_ada, norm_mix, norm_ffn, w_in, gla_gate_w2, gla_gate_b, gla_norm, rwkv_mu, rwkv_w0, rwkv_w2, rwkv_a0, rwkv_a2, rwkv_g2, rwkv_k_k, rwkv_k_a, rwkv_r_k, rwkv_gn_w, rwkv_gn_b, w_out, w_router_group, w_router_expert, w_exp_gate, w_exp_up, w_exp_down, norm_final, w_ada_final, b_ada_final):
    assert w_ada.shape[0] == 1, "single-layer step"
    bp, t, d = x_prompt.shape
    bs = x_sample.shape[0]
    assert x_sample.shape[1] == 1 and t % ROW_TILE == 0
    n_p = bp * t
    pad_s = _round_up(bs, ROW_TILE)
    n_rows = n_p + pad_s
    prompt = _PromptShape(bp, t, n_p)
    hi = lax.Precision.HIGHEST
    pad_sample = lambda a: jnp.pad(a, ((0, pad_s - a.shape[0]), (0, 0)))
    hdot = lambda a, w: jnp.dot(a, w, precision=hi)

    c_act = jax.nn.silu(jnp.concatenate([c_prompt, c_sample], axis=0))
    mod = matmul3(c_act, w_ada[0]) + b_ada[0]
    mod_f = matmul3(c_act, w_ada_final) + b_ada_final
    mod_p, mod_s = mod[:bp, None, :], mod[bp:]
    modf_p, modf_s = mod_f[:bp, None, :], mod_f[bp:]
    sh1_s, sc1_s, gt1_s, sh2_s, sc2_s, _ = jnp.split(mod_s, 6, axis=-1)
    xp = x_prompt.reshape(n_p, d)
    xs = x_sample[:, 0, :]

    w_in_t = jnp.swapaxes(w_in[0], 0, 1)
    proj = in_projection(xp, t, mod_p, norm_mix, _align_rows(w_in_t.astype(BF16)))
    o_gla, gla_t_p = gla_prompt(prompt, proj, _pad_rows(gla_gate_w2[0], LANES), gla_gate_b, gla_norm)
    new_gla_p = jnp.swapaxes(gla_t_p, -1, -2)
    mu = rwkv_mu[0]
    mu_big = jnp.concatenate([_rwkv_seg(mu, n) for n in ("r", "k7", "v7")])[None, :]
    mu_small = jnp.concatenate([_rwkv_seg_padded(mu, n) for n in ("wl", "al", "gl7")])[None, :]
    vecs = jnp.concatenate([rwkv_w0, rwkv_a0, rwkv_k_k, rwkv_k_a, rwkv_r_k[0].reshape(1, RWKV_WIDTH),
                            rwkv_gn_w, rwkv_gn_b, jnp.zeros((1, RWKV_WIDTH), F32)], axis=0)
    o_rwkv, rwkv_t_p = rwkv7_prompt(prompt, proj, mu_big, mu_small, vecs, _pad_rows(rwkv_w2[0], LANES),
                                    _pad_rows(rwkv_a2[0], LANES), rwkv_g2[0])
    new_rwkv_p = rwkv_t_p.reshape(bp, RWKV_HEAD, RWKV_HEADS, RWKV_HEAD).transpose(0, 2, 1, 3)
    last = jnp.stack([proj[(b + 1) * t - 1] for b in range(bp)])
    new_shift_p = jnp.concatenate([last[:, DST[n][0]:DST[n][0] + SRC[n][1]] for n in _RWKV_ORIG], axis=-1)

    h1_s = _rmsnorm(xs, norm_mix[0], NORM_EPS) * (1.0 + sc1_s) + sh1_s
    proj_s = matmul3(h1_s, w_in_t, w_transposed=True)
    heads = lambda a, n: a.reshape(a.shape[0], n, -1)
    logd_s = jax.nn.log_sigmoid(hdot(_orig_seg(proj_s, "gl"), gla_gate_w2[0]) + gla_gate_b[0]) \
        / GLA_GATE_NORMALIZER
    q_s = heads(_orig_seg(proj_s, "q") * (GLA_DK ** -0.5), GLA_HEADS)
    k_s, v_s, g_s = heads(_orig_seg(proj_s, "k"), GLA_HEADS), heads(_orig_seg(proj_s, "v"), GLA_HEADS), \
        heads(logd_s, GLA_HEADS)
    new_gla_s, o_s = gla_step(state_gla[0], q_s, k_s, g_s, v_s)
    o_s = _rmsnorm(o_s, gla_norm[0], GLA_NORM_EPS) * heads(jax.nn.silu(_orig_seg(proj_s, "og")), GLA_HEADS)

    rp_s = proj_s[:, RWKV_SRC_BASE:]
    new_shift_s = rp_s
    xs7 = rp_s + (state_shift[0] - rp_s) * mu
    sx = lambda name: _rwkv_seg(xs7, name)
    r_s, k7_s, v7_s = sx("r"), sx("k7"), sx("v7")
    w_pre = rwkv_w0[0] + hdot(jnp.tanh(sx("wl")), rwkv_w2[0])
    decay_s = jnp.exp(-jnp.exp(-jax.nn.softplus(-w_pre) - 0.5))
    a_s = jax.nn.sigmoid(rwkv_a0[0] + hdot(sx("al"), rwkv_a2[0]))
    g_s7 = hdot(jax.nn.sigmoid(sx("gl7")), rwkv_g2[0])
    hs = lambda z: z.reshape(bs, RWKV_HEADS, RWKV_HEAD)
    kk_s = hs(k7_s * rwkv_k_k[0])
    kk_s = kk_s / jnp.maximum(jnp.sqrt(jnp.sum(kk_s * kk_s, axis=-1, keepdims=True)), 1e-12)
    k7_s = k7_s * (1.0 + (a_s - 1.0) * rwkv_k_a[0])
    r_h, w_h, k_h, v_h, a_h = hs(r_s), hs(decay_s), hs(k7_s), hs(v7_s), hs(a_s)
    s_prev = state_rwkv[0]
    sa = jnp.einsum("bhij,bhj->bhi", s_prev, -kk_s, precision=hi)
    new_rwkv_s = s_prev * w_h[:, :, None, :] + sa[..., None] * (kk_s * a_h)[:, :, None, :] \
        + v_h[..., None] * k_h[:, :, None, :]
    y_s = jnp.einsum("bhij,bhj->bhi", new_rwkv_s, r_h, precision=hi)
    y_mu = jnp.mean(y_s, axis=-1, keepdims=True)
    y_var = jnp.mean(jnp.square(y_s - y_mu), axis=-1, keepdims=True)
    y_s = (y_s - y_mu) * lax.rsqrt(y_var + RWKV_GN_EPS) * rwkv_gn_w[0].reshape(RWKV_HEADS, RWKV_HEAD) \
        + rwkv_gn_b[0].reshape(RWKV_HEADS, RWKV_HEAD)
    bonus = jnp.sum(r_h * k_h * rwkv_r_k[0], axis=-1, keepdims=True) * v_h
    o_rs = (y_s + bonus).reshape(bs, RWKV_WIDTH) * g_s7
    mix_s = matmul3(jnp.concatenate([o_s.reshape(bs, GLA_WIDTH), o_rs], axis=-1), w_out[0])
    x1_s = xs + gt1_s * mix_s
    h2_s = _rmsnorm(x1_s, norm_ffn[0], NORM_EPS) * (1.0 + sc2_s) + sh2_s
    w_router = jnp.pad(jnp.concatenate([w_router_group[0], w_router_expert[0]], axis=-1),
                       ((0, 0), (0, LANES - N_GROUPS - N_EXPERTS)))
    logits_s = hdot(h2_s, w_router)

    x1, h2, logits = out_projection(t, o_gla, o_rwkv, xp, mod_p, norm_ffn, w_out[0].astype(BF16), w_router,
                                    pad_sample(x1_s), pad_sample(h2_s), pad_sample(logits_s))

    expert, rank, w_top, counts = route(logits)
    plan = _moe_plan(expert, rank, counts)
    y_pairs = moe_experts(h2, plan, w_exp_gate[0].reshape(N_EXPERTS, d, EXPERT_FF),
                          w_exp_up[0].reshape(N_EXPERTS, d, EXPERT_FF),
                          w_exp_down[0].reshape(N_EXPERTS, EXPERT_FF, d))

    y_p, y_s_pad = final_norm(n_p, t, x1, y_pairs, w_top, mod_p, pad_sample(mod_s), modf_p,
                              pad_sample(modf_s), norm_final[None, :])
    return (y_p.reshape(bp, t, d), y_s_pad[:bs, None, :], new_gla_p[None], new_rwkv_p[None], new_shift_p[None],
            new_gla_s[None], new_rwkv_s[None], new_shift_s[None])
```

```python
import collections
import functools

import jax
import jax.numpy as jnp
from jax import lax
from jax.experimental import pallas as pl
from jax.experimental.pallas import tpu as pltpu

F32 = jnp.float32
BF16 = jnp.bfloat16

D_MODEL = 2048
GLA_HEADS = 4
GLA_DK = 128
GLA_DV = 256
GLA_KEY_WIDTH = GLA_HEADS * GLA_DK
GLA_WIDTH = GLA_HEADS * GLA_DV
GLA_GATE_NORMALIZER = 16.0
RWKV_HEAD = 64
RWKV_HEADS = 16
RWKV_WIDTH = RWKV_HEAD * RWKV_HEADS
N_GROUPS = 4
EXPERTS_PER_GROUP = 8
N_EXPERTS = N_GROUPS * EXPERTS_PER_GROUP
EXPERT_FF = 512
NORM_EPS = 1e-6
GLA_NORM_EPS = 1e-5
RWKV_GN_EPS = 64e-5

LANES = 128
SUBLANES = 8
VMEM_LIMIT_BYTES = 56 * 1024 * 1024

CHUNK = 64
SUB = 16
RWKV_GROUP = 2
GROUP_W = RWKV_GROUP * RWKV_HEAD
GLA_SEQS = 4
LOG2E = 1.4426950408889634
RWKV_SEQS = 4
MOE_TILE = 256
SAMPLE_BLOCK = 8
ROW_TILE = 256
NORM_SLAB = 128
K_SPLIT = 256

_ORIG = (("q", 512), ("k", 512), ("v", 1024), ("gl", 16), ("og", 1024),
         ("r", 1024), ("wl", 64), ("k7", 1024), ("v7", 1024), ("al", 64), ("gl7", 128))
_ALIGNED = ("q", "k", "v", "og", "r", "k7", "v7", "gl", "wl", "al", "gl7")
_RWKV_ORIG = ("r", "wl", "k7", "v7", "al", "gl7")


def _round_up(n, m):
    return (n + m - 1) // m * m


def _layouts():
    src, off = {}, 0
    for name, w in _ORIG:
        src[name] = (off, w)
        off += w
    dst, pos = {}, 0
    for name in _ALIGNED:
        wa = _round_up(src[name][1], LANES)
        dst[name] = (pos, wa)
        pos += wa
    return src, off, dst, pos


SRC, IN_COLS, DST, IN_COLS_ALIGNED = _layouts()
RWKV_SRC_BASE = SRC["r"][0]
BIG_W = 3 * RWKV_WIDTH
SMALL_W = 3 * LANES
assert DST["r"][0] == BIG_W and DST["k7"][0] == BIG_W + RWKV_WIDTH and DST["v7"][0] == BIG_W + 2 * RWKV_WIDTH
assert DST["v"][0] == RWKV_WIDTH and DST["og"][0] == 2 * RWKV_WIDTH and DST["k"][0] == GLA_KEY_WIDTH

NN = ((1,), (0,))
NT = ((1,), (1,))

_PromptShape = collections.namedtuple("_PromptShape", "bp t rows")


def _dot(a, b, dims=NN):
    return lax.dot_general(a, b, (dims, ((), ())), preferred_element_type=F32)


def _split2(x):
    hi = x.astype(BF16)
    lo = (x - hi.astype(F32)).astype(BF16)
    return hi, lo


def _dotp(a, b, dims=NN, passes=1):
    if passes == 1:
        return _dot(a.astype(BF16), b.astype(BF16), dims)
    ah, al = _split2(a)
    bh, bl = _split2(b)
    return _dot(ah, bh, dims) + (_dot(ah, bl, dims) + _dot(al, bh, dims))


def _cumsum_rows(x):
    n = x.shape[0]
    row = lax.broadcasted_iota(jnp.int32, (n, n), 0)
    col = lax.broadcasted_iota(jnp.int32, (n, n), 1)
    tri = (row >= col).astype(BF16)
    x1 = x.astype(BF16)
    r1 = x - x1.astype(F32)
    x2 = r1.astype(BF16)
    x3 = (r1 - x2.astype(F32)).astype(BF16)
    return _dot(tri, x1) + (_dot(tri, x2) + _dot(tri, x3))


def _softplus(z):
    return jnp.maximum(z, 0.0) + jnp.log1p(jnp.exp(-jnp.abs(z)))


def _rmsnorm(x, g, eps):
    return x * lax.rsqrt(jnp.mean(x * x, axis=-1, keepdims=True) + eps) * g


def _largest_tile(n, cap, mult=SUBLANES):
    if n <= cap:
        return n
    best = None
    for t in range(mult, cap + 1, mult):
        if n % t == 0:
            best = t
    assert best is not None, (n, cap)
    return best


def _params(*sem):
    return pltpu.CompilerParams(dimension_semantics=sem, vmem_limit_bytes=VMEM_LIMIT_BYTES)


def _mm3_kernel(a_ref, w_ref, o_ref, *, dims):
    @pl.when(pl.program_id(0) == 0)
    def _():
        o_ref[...] = jnp.zeros_like(o_ref)

    o_ref[...] += _dotp(a_ref[...], w_ref[...], dims, 3)


def matmul3(a, w, w_transposed=False):
    m, k = a.shape
    n = w.shape[0] if w_transposed else w.shape[1]
    tk = _largest_tile(k, K_SPLIT, LANES)
    w_spec = pl.BlockSpec((n, tk), lambda s: (0, s)) if w_transposed else pl.BlockSpec((tk, n), lambda s: (s, 0))
    return pl.pallas_call(
        functools.partial(_mm3_kernel, dims=NT if w_transposed else NN),
        grid=(k // tk,),
        in_specs=[pl.BlockSpec((m, tk), lambda s: (0, s)), w_spec],
        out_specs=pl.BlockSpec((m, n), lambda s: (0, 0)),
        out_shape=jax.ShapeDtypeStruct((m, n), F32),
        compiler_params=_params("arbitrary"),
    )(a, w)


def _prompt_mod(tiles_per_batch, width, col):
    return pl.BlockSpec((1, 1, width), lambda i, *_: (i // tiles_per_batch, 0, col))


def _inproj_kernel(x_ref, sc_ref, sh_ref, nw_ref, w_ref, o_ref, h_scr):
    @pl.when(pl.program_id(1) == 0)
    def _():
        for r in range(0, h_scr.shape[0], NORM_SLAB):
            sl = slice(r, r + NORM_SLAB)
            h = _rmsnorm(x_ref[sl, :], nw_ref[...], NORM_EPS) * (1.0 + sc_ref[0]) + sh_ref[0]
            h_scr[sl, :] = h.astype(BF16)

    o_ref[...] = _dot(h_scr[...], w_ref[...], NT)


def in_projection(xp, t, mod_p, norm_w, w_aligned):
    n_p, d = xp.shape
    n = w_aligned.shape[0]
    tm = _largest_tile(t, 1024, NORM_SLAB)
    tn = _largest_tile(n, 1664, LANES)
    per = t // tm
    return pl.pallas_call(
        _inproj_kernel,
        grid=(n_p // tm, n // tn),
        in_specs=[pl.BlockSpec((tm, d), lambda i, j: (i, 0)), _prompt_mod(per, d, 1), _prompt_mod(per, d, 0),
                  pl.BlockSpec((1, d), lambda i, j: (0, 0)), pl.BlockSpec((tn, d), lambda i, j: (j, 0))],
        out_specs=pl.BlockSpec((tm, tn), lambda i, j: (i, j)),
        out_shape=jax.ShapeDtypeStruct((n_p, n), F32),
        scratch_shapes=[pltpu.VMEM((tm, d), BF16)],
        compiler_params=_params("parallel", "arbitrary"),
    )(xp, mod_p, mod_p, norm_w, w_aligned)


def _rwkv_chunk_kernel(big_ref, wl_ref, al_ref, gl_ref, mu_big_ref, mu_small_ref, vec_ref, w2_ref, a2_ref,
                       g2_ref, o_ref, s_ref, state, carry_big, carry_small, *, passes):
    c_idx = pl.program_id(1)
    n_seq, n_tok = big_ref.shape[0], big_ref.shape[1]
    n_grp = RWKV_WIDTH // GROUP_W
    grp = [slice(g * GROUP_W, (g + 1) * GROUP_W) for g in range(n_grp)]

    @pl.when(c_idx == 0)
    def _():
        state[...] = jnp.zeros_like(state)
        carry_big[...] = jnp.zeros_like(carry_big)
        carry_small[...] = jnp.zeros_like(carry_small)

    first_row = lax.broadcasted_iota(jnp.int32, (n_tok, 1), 0) == 0
    w0, a0, k_k, k_a, r_k, gn_w, gn_b = (vec_ref[i:i + 1, :] for i in range(7))

    gi = lax.broadcasted_iota(jnp.int32, (GROUP_W, GROUP_W), 0) // RWKV_HEAD
    gj = lax.broadcasted_iota(jnp.int32, (GROUP_W, GROUP_W), 1) // RWKV_HEAD
    head_ones = (gi == gj).astype(BF16)

    def head_sums(parts):
        hi, lo = _split2(jnp.concatenate(parts, axis=0))
        out = _dot(hi, head_ones) + _dot(lo, head_ones)
        return [out[i * n_tok:(i + 1) * n_tok] for i in range(len(parts))]

    def prepare(n):
        def token_shift(cur, carry, mu):
            prev = jnp.where(first_row, carry[n, 0:1, :], pltpu.roll(cur, 1, 0))
            carry[n, 0:1, :] = cur[n_tok - 1:n_tok, :]
            return cur + (prev - cur) * mu

        xs_big = token_shift(big_ref[n], carry_big, mu_big_ref[...])
        small = jnp.concatenate([wl_ref[n], al_ref[n], gl_ref[n]], axis=1)
        xs_small = token_shift(small, carry_small, mu_small_ref[...])
        r = xs_big[:, :RWKV_WIDTH]
        k7 = xs_big[:, RWKV_WIDTH:2 * RWKV_WIDTH]
        v = xs_big[:, 2 * RWKV_WIDTH:]
        w_pre = w0 + _dotp(jnp.tanh(xs_small[:, :LANES]), w2_ref[...])
        lw = -jnp.exp(-_softplus(-w_pre) - 0.5)
        a = jax.nn.sigmoid(a0 + _dotp(xs_small[:, LANES:2 * LANES], a2_ref[...]))
        gate = _dotp(jax.nn.sigmoid(xs_small[:, 2 * LANES:]), g2_ref[...])
        kk_raw = k7 * k_k
        k = k7 * (1.0 + (a - 1.0) * k_a)
        sums = [head_sums([kk_raw[:, sl] * kk_raw[:, sl], r[:, sl] * k[:, sl] * r_k[:, sl]]) for sl in grp]
        kk = jnp.concatenate([kk_raw[:, sl] / jnp.maximum(jnp.sqrt(sums[g][0]), 1e-12)
                              for g, sl in enumerate(grp)], axis=1)
        bonus = jnp.concatenate([sums[g][1] for g in range(n_grp)], axis=1) * v
        cum = _cumsum_rows(lw)
        cum_end = cum[n_tok - 1:n_tok, :]
        beta = kk * a
        g_inv = jnp.exp(-cum)
        g_end = jnp.exp(cum_end - cum)
        return dict(v=v, gate=gate, bonus=bonus, g_tot=jnp.exp(cum_end),
                    a_hat=-kk * jnp.exp(cum - lw), r_hat=r * jnp.exp(cum), b_hat=beta * g_inv,
                    k_hat=k * g_inv, b_end=beta * g_end, k_end=k * g_end)

    seqs = [prepare(n) for n in range(n_seq)]
    units = [(n, sl) for n in range(n_seq) for sl in grp]
    part = lambda name: [seqs[n][name][:, sl] for n, sl in units]

    lane = lax.broadcasted_iota(jnp.int32, (n_tok, GROUP_W), 1)
    tok = lax.broadcasted_iota(jnp.int32, (n_tok, GROUP_W), 0)
    lane_head = lane // RWKV_HEAD
    src_tok = lane % RWKV_HEAD
    strict = tok > src_tok
    incl = tok >= src_tok

    def bd(y):
        return jnp.concatenate([jnp.where(lane_head == h, y, 0.0) for h in range(RWKV_GROUP)], axis=0)

    gs = range(len(units))
    s0 = [state[n, :, sl] for n, sl in units]
    v_u, b_hat, k_hat, b_end, k_end = part("v"), part("b_hat"), part("k_hat"), part("b_end"), part("k_end")
    lhs2 = [jnp.concatenate([a, r], axis=0) for a, r in zip(part("a_hat"), part("r_hat"))]
    abrb = [_dotp(lhs2[g], bd(b_hat[g]), NT, passes) for g in gs]
    akrk = [_dotp(lhs2[g], bd(k_hat[g]), NT, passes) for g in gs]
    asrs = [_dotp(lhs2[g], bd(s0[g]), NT, passes) for g in gs]
    p = [jnp.where(strict, abrb[g][:n_tok], 0.0) for g in gs]
    ak = [jnp.where(strict, akrk[g][:n_tok], 0.0) for g in gs]
    rb = [jnp.where(incl, abrb[g][n_tok:], 0.0) for g in gs]
    rk = [jnp.where(incl, akrk[g][n_tok:], 0.0) for g in gs]
    bd_v = [bd(v_u[g]) for g in gs]
    x = [asrs[g][:n_tok] + _dotp(ak[g], bd_v[g], NN, passes) for g in gs]
    n_sq = n_tok.bit_length() - 1
    for it in range(n_sq):
        if it < n_sq - 1:
            both = [_dotp(p[g], jnp.concatenate([bd(p[g]), bd(x[g])], axis=1), NN, passes) for g in gs]
            x = [x[g] + both[g][:, GROUP_W:] for g in gs]
            p = [both[g][:, :GROUP_W] for g in gs]
        else:
            x = [x[g] + _dotp(p[g], bd(x[g]), NN, passes) for g in gs]
    y = [asrs[g][n_tok:] + _dotp(jnp.concatenate([rb[g], rk[g]], axis=1),
                                 jnp.concatenate([bd(x[g]), bd_v[g]], axis=0), NN, passes) for g in gs]
    full = [_dotp(jnp.concatenate([x[g], v_u[g]], axis=0).T,
                  jnp.concatenate([b_end[g], k_end[g]], axis=0), NN, passes) for g in gs]
    g_tot = part("g_tot")
    for g, (n, sl) in enumerate(units):
        upd = s0[g] * g_tot[g]
        for h in range(RWKV_GROUP):
            upd = upd + jnp.where(lane_head == h, full[g][h * RWKV_HEAD:(h + 1) * RWKV_HEAD, :], 0.0)
        state[n, :, sl] = upd

    inv_n = 1.0 / RWKV_HEAD
    mean = [head_sums([y[g]])[0] * inv_n for g in gs]
    dev = [y[g] - mean[g] for g in gs]
    var = [head_sums([dev[g] * dev[g]])[0] * inv_n for g in gs]
    bonus, gate = part("bonus"), part("gate")
    for g, (n, sl) in enumerate(units):
        yn = dev[g] * lax.rsqrt(var[g] + RWKV_GN_EPS) * gn_w[:, sl] + gn_b[:, sl]
        o_ref[n, :, sl] = ((yn + bonus[g]) * gate[g]).astype(o_ref.dtype)

    @pl.when(c_idx == pl.num_programs(1) - 1)
    def _():
        s_ref[...] = state[...]


def rwkv7_prompt(rows, proj, mu_big, mu_small, vecs, w2, a2, g2, *, passes=1):
    assert CHUNK == RWKV_HEAD and rows.t % CHUNK == 0
    nc = rows.t // CHUNK
    n_seq = _largest_tile(rows.bp, RWKV_SEQS, 1)
    proj3 = proj.reshape(rows.bp, rows.t, proj.shape[1])
    small_col = lambda name: DST[name][0] // LANES
    tok = lambda width, col: pl.BlockSpec((n_seq, CHUNK, width), lambda b, c: (b, c, col))
    const = lambda shape: pl.BlockSpec(shape, lambda b, c: (0, 0))
    out, final_state = pl.pallas_call(
        functools.partial(_rwkv_chunk_kernel, passes=passes),
        grid=(rows.bp // n_seq, nc),
        in_specs=[tok(BIG_W, 1), tok(LANES, small_col("wl")), tok(LANES, small_col("al")),
                  tok(LANES, small_col("gl7")),
                  const((1, BIG_W)), const((1, SMALL_W)), const((SUBLANES, RWKV_WIDTH)),
                  const((LANES, RWKV_WIDTH)), const((LANES, RWKV_WIDTH)), const((LANES, RWKV_WIDTH))],
        out_specs=[tok(RWKV_WIDTH, 0),
                   pl.BlockSpec((n_seq, RWKV_HEAD, RWKV_WIDTH), lambda b, c: (b, 0, 0))],
        out_shape=[jax.ShapeDtypeStruct((rows.bp, rows.t, RWKV_WIDTH), BF16),
                   jax.ShapeDtypeStruct((rows.bp, RWKV_HEAD, RWKV_WIDTH), F32)],
        scratch_shapes=[pltpu.VMEM((n_seq, RWKV_HEAD, RWKV_WIDTH), F32),
                        pltpu.VMEM((n_seq, SUBLANES, BIG_W), F32), pltpu.VMEM((n_seq, SUBLANES, SMALL_W), F32)],
        compiler_params=_params("parallel", "arbitrary"),
    )(proj3, proj3, proj3, proj3, mu_big, mu_small, vecs, w2, a2, g2)
    return out.reshape(rows.rows, RWKV_WIDTH), final_state


def _gla_chunk_kernel(qk_ref, v_ref, og_ref, gl_ref, w2_ref, gb_ref, nw_ref, o_ref, s_ref, state):
    c_idx = pl.program_id(1)
    n_seq, n_tok = qk_ref.shape[0], qk_ref.shape[1]
    n_sub = n_tok // SUB

    @pl.when(c_idx == 0)
    def _():
        state[...] = jnp.zeros_like(state)

    row_k = lax.broadcasted_iota(jnp.int32, (n_tok, GLA_DK), 0)
    att_row = lax.broadcasted_iota(jnp.int32, (SUB, n_tok), 0)
    att_col = lax.broadcasted_iota(jnp.int32, (SUB, n_tok), 1)
    own_col = [jnp.where((att_col >= i * SUB) & (att_col - i * SUB <= att_row), att_col - i * SUB, -1)
               for i in range(n_sub)]

    def cum_log2_decay(n):
        logd = -_softplus(-(_dotp(gl_ref[n], w2_ref[...]) + gb_ref[...])) * (LOG2E / GLA_GATE_NORMALIZER)
        return _cumsum_rows(logd)

    units = [(n, h) for n in range(n_seq) for h in range(GLA_HEADS)]
    us = range(len(units))
    ks = lambda h: slice(h * GLA_DK, (h + 1) * GLA_DK)
    vs = lambda h: slice(h * GLA_DV, (h + 1) * GLA_DV)
    b_seq = [cum_log2_decay(n) for n in range(n_seq)]
    q = [qk_ref[n, :, ks(h)] * (GLA_DK ** -0.5) for n, h in units]
    k = [qk_ref[n, :, GLA_KEY_WIDTH + h * GLA_DK:GLA_KEY_WIDTH + (h + 1) * GLA_DK] for n, h in units]
    b = [b_seq[n][:, ks(h)] for n, h in units]
    v = [v_ref[n, :, vs(h)] for n, h in units]
    st = [state[n, h] for n, h in units]
    o_inter = [_dotp(q[u] * jnp.exp2(b[u]), st[u], NT) for u in us]
    blocks = [[] for _ in us]
    for i in range(n_sub):
        lo = i * SUB
        rows = slice(lo, lo + SUB)
        if i > 0:
            att = [_dotp(q[u][rows] * jnp.exp2(b[u][rows] - b[u][lo - 1:lo]),
                         jnp.where(row_k < lo, k[u] * jnp.exp2(b[u][lo - 1:lo] - b[u]), 0.0), NT) for u in us]
        else:
            att = [jnp.zeros((SUB, n_tok), F32) for _ in us]
        for j in range(SUB):
            tok = slice(lo + j, lo + j + 1)
            col = [jnp.sum(q[u][rows] * (k[u][tok] * jnp.exp2(b[u][rows] - b[u][tok])), axis=1, keepdims=True)
                   for u in us]
            att = [jnp.where(own_col[i] == j, col[u], att[u]) for u in us]
        for u in us:
            blocks[u].append(o_inter[u][rows] + _dotp(att[u], v[u], NN))
    for u, (n, h) in enumerate(units):
        o = jnp.concatenate(blocks[u], axis=0)
        og = og_ref[n, :, vs(h)]
        o_ref[n, :, vs(h)] = (_rmsnorm(o, nw_ref[...], GLA_NORM_EPS)
                              * (og * jax.nn.sigmoid(og))).astype(o_ref.dtype)
        b_last = b[u][n_tok - 1:n_tok, :]
        state[n, h] = st[u] * jnp.exp2(b_last) + _dotp(v[u].T, k[u] * jnp.exp2(b_last - b[u]), NN)

    @pl.when(c_idx == pl.num_programs(1) - 1)
    def _():
        s_ref[...] = state[...]


def gla_prompt(rows, proj, gate_w2, gate_b, norm_w):
    nc = rows.t // CHUNK
    n_seq = _largest_tile(rows.bp, GLA_SEQS, 1)
    proj3 = proj.reshape(rows.bp, rows.t, proj.shape[1])
    tok = lambda width, col: pl.BlockSpec((n_seq, CHUNK, width), lambda b, c: (b, c, col))
    const = lambda shape: pl.BlockSpec(shape, lambda b, c: (0, 0))
    out, final_state = pl.pallas_call(
        _gla_chunk_kernel,
        grid=(rows.bp // n_seq, nc),
        in_specs=[tok(GLA_WIDTH, 0), tok(GLA_WIDTH, 1), tok(GLA_WIDTH, 2), tok(LANES, DST["gl"][0] // LANES),
                  const((LANES, GLA_KEY_WIDTH)), const((1, GLA_KEY_WIDTH)), const((1, GLA_DV))],
        out_specs=[tok(GLA_WIDTH, 0),
                   pl.BlockSpec((n_seq, GLA_HEADS, GLA_DV, GLA_DK), lambda b, c: (b, 0, 0, 0))],
        out_shape=[jax.ShapeDtypeStruct((rows.bp, rows.t, GLA_WIDTH), BF16),
                   jax.ShapeDtypeStruct((rows.bp, GLA_HEADS, GLA_DV, GLA_DK), F32)],
        scratch_shapes=[pltpu.VMEM((n_seq, GLA_HEADS, GLA_DV, GLA_DK), F32)],
        compiler_params=_params("parallel", "arbitrary"),
    )(proj3, proj3, proj3, proj3, gate_w2, gate_b, norm_w)
    return out.reshape(rows.rows, GLA_WIDTH), final_state


def _columns(a, block):
    rows, heads, n = a.shape
    return a.reshape(rows // block, block, heads, n).transpose(0, 3, 1, 2).reshape(rows // block, n, block * heads)


def _gla_step_kernel(s_ref, qt_ref, kt_ref, gt_ref, v_ref, so_ref, o_ref):
    n_row, n_head = s_ref.shape[0], s_ref.shape[1]
    qt, kt, decay = qt_ref[0], kt_ref[0], jnp.exp(gt_ref[0])
    for b in range(n_row):
        for h in range(n_head):
            j = b * n_head + h
            s_new = decay[:, j:j + 1] * s_ref[b, h] + kt[:, j:j + 1] * v_ref[b, h:h + 1, :]
            so_ref[b, h] = s_new
            o_ref[b, h:h + 1, :] = jnp.sum(qt[:, j:j + 1] * s_new, axis=0, keepdims=True)


def gla_step(state, q, k, logd, v):
    rows, heads, dk, dv = state.shape
    blk = SAMPLE_BLOCK
    assert rows % blk == 0
    col = pl.BlockSpec((1, dk, blk * heads), lambda i: (i, 0, 0))
    s_spec = pl.BlockSpec((blk, heads, dk, dv), lambda i: (i, 0, 0, 0))
    v_spec = pl.BlockSpec((blk, heads, dv), lambda i: (i, 0, 0))
    return pl.pallas_call(
        _gla_step_kernel,
        grid=(rows // blk,),
        in_specs=[s_spec, col, col, col, v_spec],
        out_specs=[s_spec, v_spec],
        out_shape=[jax.ShapeDtypeStruct(state.shape, F32), jax.ShapeDtypeStruct((rows, heads, dv), F32)],
        compiler_params=_params("parallel"),
    )(state, _columns(q, blk), _columns(k, blk), _columns(logd, blk), v)


def _outproj_kernel(og_ref, orw_ref, x_ref, gt_ref, sc_ref, sh_ref, nw_ref, wo_ref, wr_ref,
                    x1s_ref, h2s_ref, lgs_ref, x1_ref, h2_ref, lg_ref, *, n_prompt_tiles):
    i = pl.program_id(0)

    @pl.when(i < n_prompt_tiles)
    def _():
        half = og_ref.shape[1]
        mix = _dot(og_ref[...], wo_ref[:half, :]) + _dot(orw_ref[...], wo_ref[half:, :])
        x1 = x_ref[...] + gt_ref[0] * mix
        h2 = _rmsnorm(x1, nw_ref[...], NORM_EPS) * (1.0 + sc_ref[0]) + sh_ref[0]
        x1_ref[...] = x1
        h2_ref[...] = h2
        lg_ref[...] = _dotp(h2, wr_ref[...], NN, 3)

    @pl.when(i >= n_prompt_tiles)
    def _():
        x1_ref[...] = x1s_ref[...]
        h2_ref[...] = h2s_ref[...]
        lg_ref[...] = lgs_ref[...]


def out_projection(t, o_gla, o_rwkv, xp, mod_p, norm_w, w_out, w_router, x1_s, h2_s, logits_s):
    n_p, d = xp.shape
    pad_s = x1_s.shape[0]
    tm = ROW_TILE
    npt, per = n_p // tm, t // tm
    prompt = lambda width: pl.BlockSpec((tm, width), lambda i: (jnp.minimum(i, npt - 1), 0))
    pmod = lambda col: pl.BlockSpec((1, 1, d), lambda i: (jnp.minimum(i // per, mod_p.shape[0] - 1), 0, col))
    sample = lambda width: pl.BlockSpec((tm, width), lambda i: (jnp.maximum(i - npt, 0), 0))
    rows = lambda width: pl.BlockSpec((tm, width), lambda i: (i, 0))
    const = lambda a: pl.BlockSpec(a.shape, lambda i: (0, 0), pipeline_mode=pl.Buffered(1))
    n_rows = n_p + pad_s
    return pl.pallas_call(
        functools.partial(_outproj_kernel, n_prompt_tiles=npt),
        grid=(n_rows // tm,),
        in_specs=[prompt(o_gla.shape[1]), prompt(o_rwkv.shape[1]), prompt(d), pmod(2), pmod(4), pmod(3),
                  const(norm_w), const(w_out), const(w_router), sample(d), sample(d), sample(LANES)],
        out_specs=[rows(d), rows(d), rows(LANES)],
        out_shape=[jax.ShapeDtypeStruct((n_rows, d), F32), jax.ShapeDtypeStruct((n_rows, d), F32),
                   jax.ShapeDtypeStruct((n_rows, LANES), F32)],
        compiler_params=_params("arbitrary"),
    )(o_gla, o_rwkv, xp, mod_p, mod_p, mod_p, norm_w, w_out, w_router, x1_s, h2_s, logits_s)


def _moe_kernel(tile_expert_ref, n_valid_ref, n_real_ref, src_ref, dst_ref, h2_hbm, wg_ref, wu_ref, wd_ref,
                y_hbm, xbuf, obuf, gather_sem, scatter_sem):
    del tile_expert_ref
    i = pl.program_id(0)
    n_tiles = pl.num_programs(0)
    n_valid = n_valid_ref[0]
    slot = i % 2

    n_grp = MOE_TILE // SUBLANES
    buf_groups = lambda s: pl.ds(s * n_grp, n_grp)

    def for_rows(n, fn):
        def group(g, carry):
            for u in range(SUBLANES):
                fn(g, u)
            return carry

        def single(r, carry):
            fn(lax.shift_right_logical(r, 3), lax.bitwise_and(r, SUBLANES - 1))
            return carry
        full = lax.shift_right_logical(n, 3)
        lax.fori_loop(0, full, group, 0)
        lax.fori_loop(full * SUBLANES, n, single, 0)

    def gather_row(tile, s, g, u):
        tok = src_ref[tile * MOE_TILE + g * SUBLANES + u]
        return pltpu.make_async_copy(h2_hbm.at[pl.ds(tok, 1), :], xbuf.at[s * n_grp + g, pl.ds(u, 1), :],
                                     gather_sem.at[s])

    def scatter_row(tile, s, g, u):
        row = dst_ref[tile * MOE_TILE + g * SUBLANES + u]
        return pltpu.make_async_copy(obuf.at[s * n_grp + g, pl.ds(u, 1), :], y_hbm.at[pl.ds(row, 1), :],
                                     scatter_sem.at[s])

    def start_rows(row_copy, tile, s):
        for_rows(n_real_ref[tile], lambda g, u: row_copy(tile, s, g, u).start())

    def wait_rows(row_copy, whole_tile_copy, tile, s):
        n_real = n_real_ref[tile]

        @pl.when(n_real == MOE_TILE)
        def _():
            whole_tile_copy(s).wait()

        @pl.when(n_real < MOE_TILE)
        def _():
            for_rows(n_real, lambda g, u: row_copy(tile, s, g, u).wait())

    whole_gather = lambda s: pltpu.make_async_copy(xbuf.at[buf_groups(1 - s)], xbuf.at[buf_groups(s)],
                                                   gather_sem.at[s])
    whole_scatter = lambda s: pltpu.make_async_copy(obuf.at[buf_groups(s)], obuf.at[buf_groups(1 - s)],
                                                    scatter_sem.at[s])

    @pl.when(i == 0)
    def _():
        xbuf[...] = jnp.zeros_like(xbuf)
        start_rows(gather_row, 0, 0)

    @pl.when(i + 1 < n_valid)
    def _():
        start_rows(gather_row, i + 1, 1 - slot)

    @pl.when(i < n_valid)
    def _():
        wait_rows(gather_row, whole_gather, i, slot)

        @pl.when(i >= 2)
        def _():
            wait_rows(scatter_row, whole_scatter, i - 2, slot)

        x = xbuf[buf_groups(slot)].reshape(MOE_TILE, -1).astype(BF16)
        gate = _dot(x, wg_ref[0].astype(BF16))
        up = _dot(x, wu_ref[0].astype(BF16))
        hid = gate * jax.nn.sigmoid(gate) * up
        obuf[buf_groups(slot)] = _dot(hid.astype(BF16), wd_ref[0].astype(BF16)).reshape(n_grp, SUBLANES, -1)
        start_rows(scatter_row, i, slot)

    @pl.when(i == n_tiles - 1)
    def _():
        @pl.when(n_valid >= 2)
        def _():
            wait_rows(scatter_row, whole_scatter, n_valid - 2, n_valid % 2)

        wait_rows(scatter_row, whole_scatter, n_valid - 1, (n_valid - 1) % 2)


def moe_experts(h2, plan, w_gate, w_up, w_down):
    n_rows, d = h2.shape
    _, _, ff = w_gate.shape
    n_tiles = plan["tile_expert"].shape[0]
    grid_spec = pltpu.PrefetchScalarGridSpec(
        num_scalar_prefetch=5,
        grid=(n_tiles,),
        in_specs=[pl.BlockSpec(memory_space=pl.ANY),
                  pl.BlockSpec((1, d, ff), lambda i, te, *_: (te[i], 0, 0)),
                  pl.BlockSpec((1, d, ff), lambda i, te, *_: (te[i], 0, 0)),
                  pl.BlockSpec((1, ff, d), lambda i, te, *_: (te[i], 0, 0))],
        out_specs=pl.BlockSpec(memory_space=pl.ANY),
        scratch_shapes=[pltpu.VMEM((2 * MOE_TILE // SUBLANES, SUBLANES, d), F32)] * 2
        + [pltpu.SemaphoreType.DMA((2,)), pltpu.SemaphoreType.DMA((2,))],
    )
    return pl.pallas_call(
        _moe_kernel,
        grid_spec=grid_spec,
        out_shape=jax.ShapeDtypeStruct((2 * n_rows, d), F32),
        compiler_params=_params("arbitrary"),
    )(plan["tile_expert"], plan["n_valid"], plan["n_real"], plan["src_row"], plan["dst_row"], h2,
      w_gate, w_up, w_down)


def _route_kernel(lg_ref, idx_ref, wt_ref, cnt_ref, carry):
    i = pl.program_id(0)

    @pl.when(i == 0)
    def _():
        carry[...] = jnp.zeros_like(carry)

    lg = lg_ref[...]
    n, width = lg.shape
    lane = lax.broadcasted_iota(jnp.int32, (n, width), 1)
    row_max = lambda a: jnp.max(a, axis=1, keepdims=True)
    first_lane = lambda hit: jnp.min(jnp.where(hit, lane, width), axis=1, keepdims=True)

    is_group = lane < N_GROUPS
    g_exp = jnp.where(is_group, jnp.exp(lg - row_max(jnp.where(is_group, lg, -jnp.inf))), 0.0)
    p_group = g_exp / jnp.sum(g_exp, axis=1, keepdims=True)
    p_sel = row_max(p_group)
    g_sel = first_lane(is_group & (p_group == p_sel))
    lo = N_GROUPS + g_sel * EXPERTS_PER_GROUP
    in_group = (lane >= lo) & (lane < lo + EXPERTS_PER_GROUP)
    cand = jnp.where(in_group, lg, -jnp.inf)
    v1 = row_max(cand)
    l1 = first_lane(in_group & (cand == v1))
    cand2 = jnp.where(lane == l1, -jnp.inf, cand)
    v2 = row_max(cand2)
    l2 = first_lane(in_group & (lane != l1) & (cand2 == v2))
    t = jnp.exp(v2 - v1)
    w1 = p_sel / (1.0 + t)
    w2 = p_sel * t / (1.0 + t)
    e1, e2 = l1 - N_GROUPS, l2 - N_GROUPS

    hit1, hit2 = lane == e1, lane == e2
    both = (hit1 | hit2).astype(BF16)
    r_i = lax.broadcasted_iota(jnp.int32, (n, n), 0)
    c_i = lax.broadcasted_iota(jnp.int32, (n, n), 1)
    before = carry[0:1, :] + _dot((r_i > c_i).astype(BF16), both)
    rank1 = jnp.sum(jnp.where(hit1, before, 0.0), axis=1, keepdims=True).astype(jnp.int32)
    rank2 = jnp.sum(jnp.where(hit2, before, 0.0), axis=1, keepdims=True).astype(jnp.int32)
    carry[0:1, :] = carry[0:1, :] + jnp.sum(both.astype(F32), axis=0, keepdims=True)

    idx_ref[...] = jnp.where(lane == 0, e1, jnp.where(lane == 1, e2, jnp.where(lane == 2, rank1, rank2)))
    wt_ref[...] = jnp.where(lane == 0, w1, w2)

    @pl.when(i == pl.num_programs(0) - 1)
    def _():
        cnt_ref[...] = carry[...]


def route(logits):
    n_rows, width = logits.shape
    tile = ROW_TILE
    rows = pl.BlockSpec((tile, width), lambda i: (i, 0))
    idx, wt, cnt = pl.pallas_call(
        _route_kernel,
        grid=(n_rows // tile,),
        in_specs=[rows],
        out_specs=[rows, rows, pl.BlockSpec((SUBLANES, width), lambda i: (0, 0))],
        out_shape=[jax.ShapeDtypeStruct((n_rows, width), jnp.int32), jax.ShapeDtypeStruct((n_rows, width), F32),
                   jax.ShapeDtypeStruct((SUBLANES, width), F32)],
        scratch_shapes=[pltpu.VMEM((SUBLANES, width), F32)],
        compiler_params=_params("arbitrary"),
    )(logits)
    return idx[:, 0:2], idx[:, 2:4], wt[:, 0:2], cnt[0, :N_EXPERTS].astype(jnp.int32)


def _moe_plan(expert, rank, counts):
    n = expert.shape[0]
    e_flat = expert.reshape(-1)
    ids = jnp.arange(N_EXPERTS, dtype=jnp.int32)
    one_hot = (e_flat[:, None] == ids[None, :]).astype(jnp.int32)
    padded = (counts + MOE_TILE - 1) // MOE_TILE * MOE_TILE
    ends = jnp.cumsum(padded)
    starts = ends - padded
    pos = jnp.sum(one_hot * starts[None, :], axis=1) + rank.reshape(-1)
    p_max = _round_up(2 * n, MOE_TILE) + N_EXPERTS * MOE_TILE
    n_tiles = p_max // MOE_TILE
    pair = jnp.full((p_max,), -1, jnp.int32).at[pos].set(jnp.arange(2 * n, dtype=jnp.int32),
                                                        unique_indices=True, mode="promise_in_bounds")
    pair0 = jnp.maximum(pair, 0)
    src_row = pair0 // 2
    dst_row = (pair0 % 2) * n + src_row
    tile_start = jnp.arange(n_tiles, dtype=jnp.int32) * MOE_TILE
    n_valid = ends[-1] // MOE_TILE
    tile_expert = jnp.sum((tile_start[:, None] >= ends[None, :]).astype(jnp.int32), axis=1)
    tile_expert = jnp.where(tile_start < ends[-1], tile_expert, tile_expert[n_valid - 1])
    n_real = jnp.clip((starts + counts)[tile_expert] - tile_start, 0, MOE_TILE)
    n_real = jnp.where(tile_start < ends[-1], n_real, 0).astype(jnp.int32)
    return dict(src_row=src_row, dst_row=dst_row, tile_expert=tile_expert,
                n_valid=n_valid.astype(jnp.int32)[None], n_real=n_real)


def _final_kernel(x1_ref, y0_ref, y1_ref, wt_ref, gtp_ref, gts_ref, scp_ref, scs_ref, shp_ref, shs_ref, nw_ref,
                  yp_ref, ys_ref, *, n_prompt_tiles):
    i = pl.program_id(0)

    def finish(gt, sc, sh):
        moe = wt_ref[:, 0:1] * y0_ref[...] + wt_ref[:, 1:2] * y1_ref[...]
        x2 = x1_ref[...] + gt * moe
        return _rmsnorm(x2, nw_ref[...], NORM_EPS) * (1.0 + sc) + sh

    @pl.when(i < n_prompt_tiles)
    def _():
        yp_ref[...] = finish(gtp_ref[0], scp_ref[0], shp_ref[0])

    @pl.when(i >= n_prompt_tiles)
    def _():
        ys_ref[...] = finish(gts_ref[...], scs_ref[...], shs_ref[...])


def final_norm(n_p, t, x1, y_pairs, w_top, mod_p, mod_s, modf_p, modf_s, norm_w):
    n_rows, d = x1.shape
    tile = ROW_TILE
    n_tiles, npt = n_rows // tile, n_p // tile
    per = t // tile
    rows = lambda off: pl.BlockSpec((tile, d), lambda i: (i + off, 0))
    prompt_out = pl.BlockSpec((tile, d), lambda i: (jnp.minimum(i, npt - 1), 0))
    sample = lambda col: pl.BlockSpec((tile, d), lambda i: (jnp.maximum(i - npt, 0), col))
    pmod = lambda col: pl.BlockSpec((1, 1, d), lambda i: (jnp.minimum(i // per, mod_p.shape[0] - 1), 0, col))
    return pl.pallas_call(
        functools.partial(_final_kernel, n_prompt_tiles=npt),
        grid=(n_tiles,),
        in_specs=[rows(0), rows(0), rows(n_tiles), pl.BlockSpec((tile, w_top.shape[1]), lambda i: (i, 0)),
                  pmod(5), sample(5), pmod(1), sample(1), pmod(0), sample(0),
                  pl.BlockSpec((1, d), lambda i: (0, 0))],
        out_specs=[prompt_out, sample(0)],
        out_shape=[jax.ShapeDtypeStruct((n_p, d), F32), jax.ShapeDtypeStruct((n_rows - n_p, d), F32)],
        compiler_params=_params("arbitrary"),
    )(x1, y_pairs, y_pairs, w_top, mod_p, mod_s, modf_p, modf_s, modf_p, modf_s, norm_w)


def _align_rows_kernel(w_ref, o_ref):
    for name in _ALIGNED:
        off, w = SRC[name]
        dst, wa = DST[name]
        o_ref[dst:dst + w, :] = w_ref[off:off + w, :].astype(o_ref.dtype)
        if wa != w:
            o_ref[dst + w:dst + wa, :] = jnp.zeros((wa - w, o_ref.shape[1]), o_ref.dtype)


def _align_rows(wt):
    n, d = wt.shape
    tc = _largest_tile(d, 2 * LANES, LANES)
    return pl.pallas_call(
        _align_rows_kernel,
        grid=(d // tc,),
        in_specs=[pl.BlockSpec((n, tc), lambda j: (0, j))],
        out_specs=pl.BlockSpec((IN_COLS_ALIGNED, tc), lambda j: (0, j)),
        out_shape=jax.ShapeDtypeStruct((IN_COLS_ALIGNED, d), BF16),
        compiler_params=_params("parallel"),
    )(wt)


def _rwkv_seg(a, name):
    off, w = SRC[name]
    return a[..., off - RWKV_SRC_BASE:off - RWKV_SRC_BASE + w]


def _rwkv_seg_padded(a, name):
    seg = _rwkv_seg(a, name)
    wa = DST[name][1]
    return jnp.pad(seg, [(0, 0)] * (a.ndim - 1) + [(0, wa - seg.shape[-1])])


def _orig_seg(p, name):
    off, w = SRC[name]
    return p[..., off:off + w]


def _pad_rows(w, rows):
    return jnp.pad(w, ((0, rows - w.shape[0]), (0, 0)))


def kernel(x_prompt, x_sample, state_gla, state_rwkv, state_shift, c_prompt, c_sample, w_ada, b_ada, norm_mix, norm_ffn, w_in, gla_gate_w2, gla_gate_b, gla_norm, rwkv_mu, rwkv_w0, rwkv_w2, rwkv_a0, rwkv_a2, rwkv_g2, rwkv_k_k, rwkv_k_a, rwkv_r_k, rwkv_gn_w, rwkv_gn_b, w_out, w_router_group, w_router_expert, w_exp_gate, w_exp_up, w_exp_down, norm_final, w_ada_final, b_ada_final):
    assert w_ada.shape[0] == 1, "single-layer step"
    bp, t, d = x_prompt.shape
    bs = x_sample.shape[0]
    assert x_sample.shape[1] == 1 and t % ROW_TILE == 0
    n_p = bp * t
    pad_s = _round_up(bs, ROW_TILE)
    n_rows = n_p + pad_s
    prompt = _PromptShape(bp, t, n_p)
    hi = lax.Precision.HIGHEST
    pad_sample = lambda a: jnp.pad(a, ((0, pad_s - a.shape[0]), (0, 0)))
    hdot = lambda a, w: jnp.dot(a, w, precision=hi)

    c_act = jax.nn.silu(jnp.concatenate([c_prompt, c_sample], axis=0))
    mod = matmul3(c_act, w_ada[0]) + b_ada[0]
    mod_f = matmul3(c_act, w_ada_final) + b_ada_final
    mod_p, mod_s = mod[:bp, None, :], mod[bp:]
    modf_p, modf_s = mod_f[:bp, None, :], mod_f[bp:]
    sh1_s, sc1_s, gt1_s, sh2_s, sc2_s, _ = jnp.split(mod_s, 6, axis=-1)
    xp = x_prompt.reshape(n_p, d)
    xs = x_sample[:, 0, :]

    w_in_t = jnp.swapaxes(w_in[0], 0, 1)
    proj = in_projection(xp, t, mod_p, norm_mix, _align_rows(w_in_t))
    o_gla, gla_t_p = gla_prompt(prompt, proj, _pad_rows(gla_gate_w2[0], LANES), gla_gate_b, gla_norm)
    new_gla_p = jnp.swapaxes(gla_t_p, -1, -2)
    mu = rwkv_mu[0]
    mu_big = jnp.concatenate([_rwkv_seg(mu, n) for n in ("r", "k7", "v7")])[None, :]
    mu_small = jnp.concatenate([_rwkv_seg_padded(mu, n) for n in ("wl", "al", "gl7")])[None, :]
    vecs = jnp.concatenate([rwkv_w0, rwkv_a0, rwkv_k_k, rwkv_k_a, rwkv_r_k[0].reshape(1, RWKV_WIDTH),
                            rwkv_gn_w, rwkv_gn_b, jnp.zeros((1, RWKV_WIDTH), F32)], axis=0)
    o_rwkv, rwkv_t_p = rwkv7_prompt(prompt, proj, mu_big, mu_small, vecs, _pad_rows(rwkv_w2[0], LANES),
                                    _pad_rows(rwkv_a2[0], LANES), rwkv_g2[0])
    new_rwkv_p = rwkv_t_p.reshape(bp, RWKV_HEAD, RWKV_HEADS, RWKV_HEAD).transpose(0, 2, 1, 3)
    last = jnp.stack([proj[(b + 1) * t - 1] for b in range(bp)])
    new_shift_p = jnp.concatenate([last[:, DST[n][0]:DST[n][0] + SRC[n][1]] for n in _RWKV_ORIG], axis=-1)

    h1_s = _rmsnorm(xs, norm_mix[0], NORM_EPS) * (1.0 + sc1_s) + sh1_s
    proj_s = matmul3(h1_s, w_in_t, w_transposed=True)
    heads = lambda a, n: a.reshape(a.shape[0], n, -1)
    logd_s = jax.nn.log_sigmoid(hdot(_orig_seg(proj_s, "gl"), gla_gate_w2[0]) + gla_gate_b[0]) \
        / GLA_GATE_NORMALIZER
    q_s = heads(_orig_seg(proj_s, "q") * (GLA_DK ** -0.5), GLA_HEADS)
    k_s, v_s, g_s = heads(_orig_seg(proj_s, "k"), GLA_HEADS), heads(_orig_seg(proj_s, "v"), GLA_HEADS), \
        heads(logd_s, GLA_HEADS)
    new_gla_s, o_s = gla_step(state_gla[0], q_s, k_s, g_s, v_s)
    o_s = _rmsnorm(o_s, gla_norm[0], GLA_NORM_EPS) * heads(jax.nn.silu(_orig_seg(proj_s, "og")), GLA_HEADS)

    rp_s = proj_s[:, RWKV_SRC_BASE:]
    new_shift_s = rp_s
    xs7 = rp_s + (state_shift[0] - rp_s) * mu
    sx = lambda name: _rwkv_seg(xs7, name)
    r_s, k7_s, v7_s = sx("r"), sx("k7"), sx("v7")
    w_pre = rwkv_w0[0] + hdot(jnp.tanh(sx("wl")), rwkv_w2[0])
    decay_s = jnp.exp(-jnp.exp(-jax.nn.softplus(-w_pre) - 0.5))
    a_s = jax.nn.sigmoid(rwkv_a0[0] + hdot(sx("al"), rwkv_a2[0]))
    g_s7 = hdot(jax.nn.sigmoid(sx("gl7")), rwkv_g2[0])
    hs = lambda z: z.reshape(bs, RWKV_HEADS, RWKV_HEAD)
    kk_s = hs(k7_s * rwkv_k_k[0])
    kk_s = kk_s / jnp.maximum(jnp.sqrt(jnp.sum(kk_s * kk_s, axis=-1, keepdims=True)), 1e-12)
    k7_s = k7_s * (1.0 + (a_s - 1.0) * rwkv_k_a[0])
    r_h, w_h, k_h, v_h, a_h = hs(r_s), hs(decay_s), hs(k7_s), hs(v7_s), hs(a_s)
    s_prev = state_rwkv[0]
    sa = jnp.einsum("bhij,bhj->bhi", s_prev, -kk_s, precision=hi)
    new_rwkv_s = s_prev * w_h[:, :, None, :] + sa[..., None] * (kk_s * a_h)[:, :, None, :] \
        + v_h[..., None] * k_h[:, :, None, :]
    y_s = jnp.einsum("bhij,bhj->bhi", new_rwkv_s, r_h, precision=hi)
    y_mu = jnp.mean(y_s, axis=-1, keepdims=True)
    y_var = jnp.mean(jnp.square(y_s - y_mu), axis=-1, keepdims=True)
    y_s = (y_s - y_mu) * lax.rsqrt(y_var + RWKV_GN_EPS) * rwkv_gn_w[0].reshape(RWKV_HEADS, RWKV_HEAD) \
        + rwkv_gn_b[0].reshape(RWKV_HEADS, RWKV_HEAD)
    bonus = jnp.sum(r_h * k_h * rwkv_r_k[0], axis=-1, keepdims=True) * v_h
    o_rs = (y_s + bonus).reshape(bs, RWKV_WIDTH) * g_s7
    mix_s = matmul3(jnp.concatenate([o_s.reshape(bs, GLA_WIDTH), o_rs], axis=-1), w_out[0])
    x1_s = xs + gt1_s * mix_s
    h2_s = _rmsnorm(x1_s, norm_ffn[0], NORM_EPS) * (1.0 + sc2_s) + sh2_s
    w_router = jnp.pad(jnp.concatenate([w_router_group[0], w_router_expert[0]], axis=-1),
                       ((0, 0), (0, LANES - N_GROUPS - N_EXPERTS)))
    logits_s = hdot(h2_s, w_router)

    x1, h2, logits = out_projection(t, o_gla, o_rwkv, xp, mod_p, norm_ffn, w_out[0].astype(BF16), w_router,
                                    pad_sample(x1_s), pad_sample(h2_s), pad_sample(logits_s))

    expert, rank, w_top, counts = route(logits)
    plan = _moe_plan(expert, rank, counts)
    y_pairs = moe_experts(h2, plan, w_exp_gate[0].reshape(N_EXPERTS, d, EXPERT_FF),
                          w_exp_up[0].reshape(N_EXPERTS, d, EXPERT_FF),
                          w_exp_down[0].reshape(N_EXPERTS, EXPERT_FF, d))

    y_p, y_s_pad = final_norm(n_p, t, x1, y_pairs, w_top, mod_p, pad_sample(mod_s), modf_p,
                              pad_sample(modf_s), norm_final[None, :])
    return (y_p.reshape(bp, t, d), y_s_pad[:bs, None, :], new_gla_p[None], new_rwkv_p[None], new_shift_p[None],
            new_gla_s[None], new_rwkv_s[None], new_shift_s[None])
```

```python
import collections
import functools

import jax
import jax.numpy as jnp
from jax import lax
from jax.experimental import pallas as pl
from jax.experimental.pallas import tpu as pltpu

F32 = jnp.float32
BF16 = jnp.bfloat16

D_MODEL = 2048
GLA_HEADS = 4
GLA_DK = 128
GLA_DV = 256
GLA_KEY_WIDTH = GLA_HEADS * GLA_DK
GLA_WIDTH = GLA_HEADS * GLA_DV
GLA_GATE_NORMALIZER = 16.0
RWKV_HEAD = 64
RWKV_HEADS = 16
RWKV_WIDTH = RWKV_HEAD * RWKV_HEADS
N_GROUPS = 4
EXPERTS_PER_GROUP = 8
N_EXPERTS = N_GROUPS * EXPERTS_PER_GROUP
EXPERT_FF = 512
NORM_EPS = 1e-6
GLA_NORM_EPS = 1e-5
RWKV_GN_EPS = 64e-5

LANES = 128
SUBLANES = 8
VMEM_LIMIT_BYTES = 56 * 1024 * 1024

CHUNK = 64
SUB = 16
RWKV_GROUP = 2
GROUP_W = RWKV_GROUP * RWKV_HEAD
GLA_SEQS = 4
LOG2E = 1.4426950408889634
RWKV_SEQS = 4
MOE_TILE = 256
SAMPLE_BLOCK = 8
ROW_TILE = 256
NORM_SLAB = 128
K_SPLIT = 256

_ORIG = (("q", 512), ("k", 512), ("v", 1024), ("gl", 16), ("og", 1024),
         ("r", 1024), ("wl", 64), ("k7", 1024), ("v7", 1024), ("al", 64), ("gl7", 128))
_ALIGNED = ("q", "k", "v", "og", "r", "k7", "v7", "gl", "wl", "al", "gl7")
_RWKV_ORIG = ("r", "wl", "k7", "v7", "al", "gl7")


def _round_up(n, m):
    return (n + m - 1) // m * m


def _layouts():
    src, off = {}, 0
    for name, w in _ORIG:
        src[name] = (off, w)
        off += w
    dst, pos = {}, 0
    for name in _ALIGNED:
        wa = _round_up(src[name][1], LANES)
        dst[name] = (pos, wa)
        pos += wa
    return src, off, dst, pos


SRC, IN_COLS, DST, IN_COLS_ALIGNED = _layouts()
RWKV_SRC_BASE = SRC["r"][0]
BIG_W = 3 * RWKV_WIDTH
SMALL_W = 3 * LANES
assert DST["r"][0] == BIG_W and DST["k7"][0] == BIG_W + RWKV_WIDTH and DST["v7"][0] == BIG_W + 2 * RWKV_WIDTH
assert DST["v"][0] == RWKV_WIDTH and DST["og"][0] == 2 * RWKV_WIDTH and DST["k"][0] == GLA_KEY_WIDTH

NN = ((1,), (0,))
NT = ((1,), (1,))

_PromptShape = collections.namedtuple("_PromptShape", "bp t rows")


def _dot(a, b, dims=NN):
    return lax.dot_general(a, b, (dims, ((), ())), preferred_element_type=F32)


def _split2(x):
    hi = x.astype(BF16)
    lo = (x - hi.astype(F32)).astype(BF16)
    return hi, lo


def _dotp(a, b, dims=NN, passes=1):
    if passes == 1:
        return _dot(a.astype(BF16), b.astype(BF16), dims)
    ah, al = _split2(a)
    bh, bl = _split2(b)
    return _dot(ah, bh, dims) + (_dot(ah, bl, dims) + _dot(al, bh, dims))


def _cumsum_rows(x):
    n = x.shape[0]
    row = lax.broadcasted_iota(jnp.int32, (n, n), 0)
    col = lax.broadcasted_iota(jnp.int32, (n, n), 1)
    tri = (row >= col).astype(BF16)
    x1 = x.astype(BF16)
    r1 = x - x1.astype(F32)
    x2 = r1.astype(BF16)
    x3 = (r1 - x2.astype(F32)).astype(BF16)
    return _dot(tri, x1) + (_dot(tri, x2) + _dot(tri, x3))


def _softplus(z):
    return jnp.maximum(z, 0.0) + jnp.log1p(jnp.exp(-jnp.abs(z)))


def _rmsnorm(x, g, eps):
    return x * lax.rsqrt(jnp.mean(x * x, axis=-1, keepdims=True) + eps) * g


def _largest_tile(n, cap, mult=SUBLANES):
    if n <= cap:
        return n
    best = None
    for t in range(mult, cap + 1, mult):
        if n % t == 0:
            best = t
    assert best is not None, (n, cap)
    return best


def _params(*sem):
    return pltpu.CompilerParams(dimension_semantics=sem, vmem_limit_bytes=VMEM_LIMIT_BYTES)


def _mm3_kernel(a_ref, w_ref, o_ref):
    @pl.when(pl.program_id(0) == 0)
    def _():
        o_ref[...] = jnp.zeros_like(o_ref)

    o_ref[...] += _dotp(a_ref[...], w_ref[...], NN, 3)


def matmul3(a, w):
    m, k = a.shape
    n = w.shape[1]
    tk = _largest_tile(k, K_SPLIT, LANES)
    return pl.pallas_call(
        _mm3_kernel,
        grid=(k // tk,),
        in_specs=[pl.BlockSpec((m, tk), lambda s: (0, s)), pl.BlockSpec((tk, n), lambda s: (s, 0))],
        out_specs=pl.BlockSpec((m, n), lambda s: (0, 0)),
        out_shape=jax.ShapeDtypeStruct((m, n), F32),
        compiler_params=_params("arbitrary"),
    )(a, w)


def _prompt_mod(tiles_per_batch, width, col):
    return pl.BlockSpec((1, 1, width), lambda i, *_: (i // tiles_per_batch, 0, col))


def _inproj_kernel(x_ref, sc_ref, sh_ref, nw_ref, w_ref, o_ref, h_scr):
    @pl.when(pl.program_id(1) == 0)
    def _():
        for r in range(0, h_scr.shape[0], NORM_SLAB):
            sl = slice(r, r + NORM_SLAB)
            h = _rmsnorm(x_ref[sl, :], nw_ref[...], NORM_EPS) * (1.0 + sc_ref[0]) + sh_ref[0]
            h_scr[sl, :] = h.astype(BF16)

    o_ref[...] = _dot(h_scr[...], w_ref[...], NT)


def in_projection(xp, t, mod_p, norm_w, w_aligned):
    n_p, d = xp.shape
    n = w_aligned.shape[0]
    tm = _largest_tile(t, 1024, NORM_SLAB)
    tn = _largest_tile(n, 1664, LANES)
    per = t // tm
    return pl.pallas_call(
        _inproj_kernel,
        grid=(n_p // tm, n // tn),
        in_specs=[pl.BlockSpec((tm, d), lambda i, j: (i, 0)), _prompt_mod(per, d, 1), _prompt_mod(per, d, 0),
                  pl.BlockSpec((1, d), lambda i, j: (0, 0)), pl.BlockSpec((tn, d), lambda i, j: (j, 0))],
        out_specs=pl.BlockSpec((tm, tn), lambda i, j: (i, j)),
        out_shape=jax.ShapeDtypeStruct((n_p, n), F32),
        scratch_shapes=[pltpu.VMEM((tm, d), BF16)],
        compiler_params=_params("parallel", "arbitrary"),
    )(xp, mod_p, mod_p, norm_w, w_aligned)


def _rwkv_chunk_kernel(big_ref, wl_ref, al_ref, gl_ref, mu_big_ref, mu_small_ref, vec_ref, w2_ref, a2_ref,
                       g2_ref, o_ref, s_ref, state, carry_big, carry_small, *, passes):
    c_idx = pl.program_id(1)
    n_seq, n_tok = big_ref.shape[0], big_ref.shape[1]
    n_grp = RWKV_WIDTH // GROUP_W
    grp = [slice(g * GROUP_W, (g + 1) * GROUP_W) for g in range(n_grp)]

    @pl.when(c_idx == 0)
    def _():
        state[...] = jnp.zeros_like(state)
        carry_big[...] = jnp.zeros_like(carry_big)
        carry_small[...] = jnp.zeros_like(carry_small)

    first_row = lax.broadcasted_iota(jnp.int32, (n_tok, 1), 0) == 0
    w0, a0, k_k, k_a, r_k, gn_w, gn_b = (vec_ref[i:i + 1, :] for i in range(7))

    gi = lax.broadcasted_iota(jnp.int32, (GROUP_W, GROUP_W), 0) // RWKV_HEAD
    gj = lax.broadcasted_iota(jnp.int32, (GROUP_W, GROUP_W), 1) // RWKV_HEAD
    head_ones = (gi == gj).astype(BF16)

    def head_sums(parts):
        hi, lo = _split2(jnp.concatenate(parts, axis=0))
        out = _dot(hi, head_ones) + _dot(lo, head_ones)
        return [out[i * n_tok:(i + 1) * n_tok] for i in range(len(parts))]

    def prepare(n):
        def token_shift(cur, carry, mu):
            prev = jnp.where(first_row, carry[n, 0:1, :], pltpu.roll(cur, 1, 0))
            carry[n, 0:1, :] = cur[n_tok - 1:n_tok, :]
            return cur + (prev - cur) * mu

        xs_big = token_shift(big_ref[n], carry_big, mu_big_ref[...])
        small = jnp.concatenate([wl_ref[n], al_ref[n], gl_ref[n]], axis=1)
        xs_small = token_shift(small, carry_small, mu_small_ref[...])
        r = xs_big[:, :RWKV_WIDTH]
        k7 = xs_big[:, RWKV_WIDTH:2 * RWKV_WIDTH]
        v = xs_big[:, 2 * RWKV_WIDTH:]
        w_pre = w0 + _dotp(jnp.tanh(xs_small[:, :LANES]), w2_ref[...])
        lw = -jnp.exp(-_softplus(-w_pre) - 0.5)
        a = jax.nn.sigmoid(a0 + _dotp(xs_small[:, LANES:2 * LANES], a2_ref[...]))
        gate = _dotp(jax.nn.sigmoid(xs_small[:, 2 * LANES:]), g2_ref[...])
        kk_raw = k7 * k_k
        k = k7 * (1.0 + (a - 1.0) * k_a)
        sums = [head_sums([kk_raw[:, sl] * kk_raw[:, sl], r[:, sl] * k[:, sl] * r_k[:, sl]]) for sl in grp]
        kk = jnp.concatenate([kk_raw[:, sl] / jnp.maximum(jnp.sqrt(sums[g][0]), 1e-12)
                              for g, sl in enumerate(grp)], axis=1)
        bonus = jnp.concatenate([sums[g][1] for g in range(n_grp)], axis=1) * v
        cum = _cumsum_rows(lw)
        cum_end = cum[n_tok - 1:n_tok, :]
        beta = kk * a
        g_inv = jnp.exp(-cum)
        g_end = jnp.exp(cum_end - cum)
        return dict(v=v, gate=gate, bonus=bonus, g_tot=jnp.exp(cum_end),
                    a_hat=-kk * jnp.exp(cum - lw), r_hat=r * jnp.exp(cum), b_hat=beta * g_inv,
                    k_hat=k * g_inv, b_end=beta * g_end, k_end=k * g_end)

    seqs = [prepare(n) for n in range(n_seq)]
    units = [(n, sl) for n in range(n_seq) for sl in grp]
    part = lambda name: [seqs[n][name][:, sl] for n, sl in units]

    lane = lax.broadcasted_iota(jnp.int32, (n_tok, GROUP_W), 1)
    tok = lax.broadcasted_iota(jnp.int32, (n_tok, GROUP_W), 0)
    lane_head = lane // RWKV_HEAD
    src_tok = lane % RWKV_HEAD
    strict = tok > src_tok
    incl = tok >= src_tok

    def bd(y):
        return jnp.concatenate([jnp.where(lane_head == h, y, 0.0) for h in range(RWKV_GROUP)], axis=0)

    gs = range(len(units))
    s0 = [state[n, :, sl] for n, sl in units]
    v_u, b_hat, k_hat, b_end, k_end = part("v"), part("b_hat"), part("k_hat"), part("b_end"), part("k_end")
    lhs2 = [jnp.concatenate([a, r], axis=0) for a, r in zip(part("a_hat"), part("r_hat"))]
    abrb = [_dotp(lhs2[g], bd(b_hat[g]), NT, passes) for g in gs]
    akrk = [_dotp(lhs2[g], bd(k_hat[g]), NT, passes) for g in gs]
    asrs = [_dotp(lhs2[g], bd(s0[g]), NT, passes) for g in gs]
    p = [jnp.where(strict, abrb[g][:n_tok], 0.0) for g in gs]
    ak = [jnp.where(strict, akrk[g][:n_tok], 0.0) for g in gs]
    rb = [jnp.where(incl, abrb[g][n_tok:], 0.0) for g in gs]
    rk = [jnp.where(incl, akrk[g][n_tok:], 0.0) for g in gs]
    bd_v = [bd(v_u[g]) for g in gs]
    x = [asrs[g][:n_tok] + _dotp(ak[g], bd_v[g], NN, passes) for g in gs]
    n_sq = n_tok.bit_length() - 1
    for it in range(n_sq):
        if it < n_sq - 1:
            both = [_dotp(p[g], jnp.concatenate([bd(p[g]), bd(x[g])], axis=1), NN, passes) for g in gs]
            x = [x[g] + both[g][:, GROUP_W:] for g in gs]
            p = [both[g][:, :GROUP_W] for g in gs]
        else:
            x = [x[g] + _dotp(p[g], bd(x[g]), NN, passes) for g in gs]
    y = [asrs[g][n_tok:] + _dotp(jnp.concatenate([rb[g], rk[g]], axis=1),
                                 jnp.concatenate([bd(x[g]), bd_v[g]], axis=0), NN, passes) for g in gs]
    full = [_dotp(jnp.concatenate([x[g], v_u[g]], axis=0).T,
                  jnp.concatenate([b_end[g], k_end[g]], axis=0), NN, passes) for g in gs]
    g_tot = part("g_tot")
    for g, (n, sl) in enumerate(units):
        upd = s0[g] * g_tot[g]
        for h in range(RWKV_GROUP):
            upd = upd + jnp.where(lane_head == h, full[g][h * RWKV_HEAD:(h + 1) * RWKV_HEAD, :], 0.0)
        state[n, :, sl] = upd

    inv_n = 1.0 / RWKV_HEAD
    mean = [head_sums([y[g]])[0] * inv_n for g in gs]
    dev = [y[g] - mean[g] for g in gs]
    var = [head_sums([dev[g] * dev[g]])[0] * inv_n for g in gs]
    bonus, gate = part("bonus"), part("gate")
    for g, (n, sl) in enumerate(units):
        yn = dev[g] * lax.rsqrt(var[g] + RWKV_GN_EPS) * gn_w[:, sl] + gn_b[:, sl]
        o_ref[n, :, sl] = ((yn + bonus[g]) * gate[g]).astype(o_ref.dtype)

    @pl.when(c_idx == pl.num_programs(1) - 1)
    def _():
        s_ref[...] = state[...]


def rwkv7_prompt(rows, proj, mu_big, mu_small, vecs, w2, a2, g2, *, passes=1):
    assert CHUNK == RWKV_HEAD and rows.t % CHUNK == 0
    nc = rows.t // CHUNK
    n_seq = _largest_tile(rows.bp, RWKV_SEQS, 1)
    proj3 = proj.reshape(rows.bp, rows.t, proj.shape[1])
    small_col = lambda name: DST[name][0] // LANES
    tok = lambda width, col: pl.BlockSpec((n_seq, CHUNK, width), lambda b, c: (b, c, col))
    const = lambda shape: pl.BlockSpec(shape, lambda b, c: (0, 0))
    out, final_state = pl.pallas_call(
        functools.partial(_rwkv_chunk_kernel, passes=passes),
        grid=(rows.bp // n_seq, nc),
        in_specs=[tok(BIG_W, 1), tok(LANES, small_col("wl")), tok(LANES, small_col("al")),
                  tok(LANES, small_col("gl7")),
                  const((1, BIG_W)), const((1, SMALL_W)), const((SUBLANES, RWKV_WIDTH)),
                  const((LANES, RWKV_WIDTH)), const((LANES, RWKV_WIDTH)), const((LANES, RWKV_WIDTH))],
        out_specs=[tok(RWKV_WIDTH, 0),
                   pl.BlockSpec((n_seq, RWKV_HEAD, RWKV_WIDTH), lambda b, c: (b, 0, 0))],
        out_shape=[jax.ShapeDtypeStruct((rows.bp, rows.t, RWKV_WIDTH), BF16),
                   jax.ShapeDtypeStruct((rows.bp, RWKV_HEAD, RWKV_WIDTH), F32)],
        scratch_shapes=[pltpu.VMEM((n_seq, RWKV_HEAD, RWKV_WIDTH), F32),
                        pltpu.VMEM((n_seq, SUBLANES, BIG_W), F32), pltpu.VMEM((n_seq, SUBLANES, SMALL_W), F32)],
        compiler_params=_params("parallel", "arbitrary"),
    )(proj3, proj3, proj3, proj3, mu_big, mu_small, vecs, w2, a2, g2)
    return out.reshape(rows.rows, RWKV_WIDTH), final_state


def _gla_chunk_kernel(qk_ref, v_ref, og_ref, gl_ref, w2_ref, gb_ref, nw_ref, o_ref, s_ref, state):
    c_idx = pl.program_id(1)
    n_seq, n_tok = qk_ref.shape[0], qk_ref.shape[1]
    n_sub = n_tok // SUB

    @pl.when(c_idx == 0)
    def _():
        state[...] = jnp.zeros_like(state)

    row_k = lax.broadcasted_iota(jnp.int32, (n_tok, GLA_DK), 0)
    att_row = lax.broadcasted_iota(jnp.int32, (SUB, n_tok), 0)
    att_col = lax.broadcasted_iota(jnp.int32, (SUB, n_tok), 1)
    own_col = [jnp.where((att_col >= i * SUB) & (att_col - i * SUB <= att_row), att_col - i * SUB, -1)
               for i in range(n_sub)]

    def cum_log2_decay(n):
        logd = -_softplus(-(_dotp(gl_ref[n], w2_ref[...]) + gb_ref[...])) * (LOG2E / GLA_GATE_NORMALIZER)
        return _cumsum_rows(logd)

    units = [(n, h) for n in range(n_seq) for h in range(GLA_HEADS)]
    us = range(len(units))
    ks = lambda h: slice(h * GLA_DK, (h + 1) * GLA_DK)
    vs = lambda h: slice(h * GLA_DV, (h + 1) * GLA_DV)
    b_seq = [cum_log2_decay(n) for n in range(n_seq)]
    q = [qk_ref[n, :, ks(h)] * (GLA_DK ** -0.5) for n, h in units]
    k = [qk_ref[n, :, GLA_KEY_WIDTH + h * GLA_DK:GLA_KEY_WIDTH + (h + 1) * GLA_DK] for n, h in units]
    b = [b_seq[n][:, ks(h)] for n, h in units]
    v = [v_ref[n, :, vs(h)] for n, h in units]
    st = [state[n, h] for n, h in units]
    o_inter = [_dotp(q[u] * jnp.exp2(b[u]), st[u], NT) for u in us]
    blocks = [[] for _ in us]
    for i in range(n_sub):
        lo = i * SUB
        rows = slice(lo, lo + SUB)
        if i > 0:
            att = [_dotp(q[u][rows] * jnp.exp2(b[u][rows] - b[u][lo - 1:lo]),
                         jnp.where(row_k < lo, k[u] * jnp.exp2(b[u][lo - 1:lo] - b[u]), 0.0), NT) for u in us]
        else:
            att = [jnp.zeros((SUB, n_tok), F32) for _ in us]
        for j in range(SUB):
            tok = slice(lo + j, lo + j + 1)
            col = [jnp.sum(q[u][rows] * (k[u][tok] * jnp.exp2(b[u][rows] - b[u][tok])), axis=1, keepdims=True)
                   for u in us]
            att = [jnp.where(own_col[i] == j, col[u], att[u]) for u in us]
        for u in us:
            blocks[u].append(o_inter[u][rows] + _dotp(att[u], v[u], NN))
    for u, (n, h) in enumerate(units):
        o = jnp.concatenate(blocks[u], axis=0)
        og = og_ref[n, :, vs(h)]
        o_ref[n, :, vs(h)] = (_rmsnorm(o, nw_ref[...], GLA_NORM_EPS)
                              * (og * jax.nn.sigmoid(og))).astype(o_ref.dtype)
        b_last = b[u][n_tok - 1:n_tok, :]
        state[n, h] = st[u] * jnp.exp2(b_last) + _dotp(v[u].T, k[u] * jnp.exp2(b_last - b[u]), NN)

    @pl.when(c_idx == pl.num_programs(1) - 1)
    def _():
        s_ref[...] = state[...]


def gla_prompt(rows, proj, gate_w2, gate_b, norm_w):
    nc = rows.t // CHUNK
    n_seq = _largest_tile(rows.bp, GLA_SEQS, 1)
    proj3 = proj.reshape(rows.bp, rows.t, proj.shape[1])
    tok = lambda width, col: pl.BlockSpec((n_seq, CHUNK, width), lambda b, c: (b, c, col))
    const = lambda shape: pl.BlockSpec(shape, lambda b, c: (0, 0))
    out, final_state = pl.pallas_call(
        _gla_chunk_kernel,
        grid=(rows.bp // n_seq, nc),
        in_specs=[tok(GLA_WIDTH, 0), tok(GLA_WIDTH, 1), tok(GLA_WIDTH, 2), tok(LANES, DST["gl"][0] // LANES),
                  const((LANES, GLA_KEY_WIDTH)), const((1, GLA_KEY_WIDTH)), const((1, GLA_DV))],
        out_specs=[tok(GLA_WIDTH, 0),
                   pl.BlockSpec((n_seq, GLA_HEADS, GLA_DV, GLA_DK), lambda b, c: (b, 0, 0, 0))],
        out_shape=[jax.ShapeDtypeStruct((rows.bp, rows.t, GLA_WIDTH), BF16),
                   jax.ShapeDtypeStruct((rows.bp, GLA_HEADS, GLA_DV, GLA_DK), F32)],
        scratch_shapes=[pltpu.VMEM((n_seq, GLA_HEADS, GLA_DV, GLA_DK), F32)],
        compiler_params=_params("parallel", "arbitrary"),
    )(proj3, proj3, proj3, proj3, gate_w2, gate_b, norm_w)
    return out.reshape(rows.rows, GLA_WIDTH), final_state


def _columns(a, block):
    rows, heads, n = a.shape
    return a.reshape(rows // block, block, heads, n).transpose(0, 3, 1, 2).reshape(rows // block, n, block * heads)


def _gla_step_kernel(s_ref, qt_ref, kt_ref, gt_ref, v_ref, so_ref, o_ref):
    n_row, n_head = s_ref.shape[0], s_ref.shape[1]
    qt, kt, decay = qt_ref[0], kt_ref[0], jnp.exp(gt_ref[0])
    for b in range(n_row):
        for h in range(n_head):
            j = b * n_head + h
            s_new = decay[:, j:j + 1] * s_ref[b, h] + kt[:, j:j + 1] * v_ref[b, h:h + 1, :]
            so_ref[b, h] = s_new
            o_ref[b, h:h + 1, :] = jnp.sum(qt[:, j:j + 1] * s_new, axis=0, keepdims=True)


def gla_step(state, q, k, logd, v):
    rows, heads, dk, dv = state.shape
    blk = SAMPLE_BLOCK
    assert rows % blk == 0
    col = pl.BlockSpec((1, dk, blk * heads), lambda i: (i, 0, 0))
    s_spec = pl.BlockSpec((blk, heads, dk, dv), lambda i: (i, 0, 0, 0))
    v_spec = pl.BlockSpec((blk, heads, dv), lambda i: (i, 0, 0))
    return pl.pallas_call(
        _gla_step_kernel,
        grid=(rows // blk,),
        in_specs=[s_spec, col, col, col, v_spec],
        out_specs=[s_spec, v_spec],
        out_shape=[jax.ShapeDtypeStruct(state.shape, F32), jax.ShapeDtypeStruct((rows, heads, dv), F32)],
        compiler_params=_params("parallel"),
    )(state, _columns(q, blk), _columns(k, blk), _columns(logd, blk), v)


def _outproj_kernel(og_ref, orw_ref, x_ref, gt_ref, sc_ref, sh_ref, nw_ref, wo_ref, wr_ref,
                    x1s_ref, h2s_ref, lgs_ref, x1_ref, h2_ref, lg_ref, *, n_prompt_tiles):
    i = pl.program_id(0)

    @pl.when(i < n_prompt_tiles)
    def _():
        half = og_ref.shape[1]
        mix = _dot(og_ref[...], wo_ref[:half, :]) + _dot(orw_ref[...], wo_ref[half:, :])
        x1 = x_ref[...] + gt_ref[0] * mix
        h2 = _rmsnorm(x1, nw_ref[...], NORM_EPS) * (1.0 + sc_ref[0]) + sh_ref[0]
        x1_ref[...] = x1
        h2_ref[...] = h2
        lg_ref[...] = _dotp(h2, wr_ref[...], NN, 3)

    @pl.when(i >= n_prompt_tiles)
    def _():
        x1_ref[...] = x1s_ref[...]
        h2_ref[...] = h2s_ref[...]
        lg_ref[...] = lgs_ref[...]


def out_projection(t, o_gla, o_rwkv, xp, mod_p, norm_w, w_out, w_router, x1_s, h2_s, logits_s):
    n_p, d = xp.shape
    pad_s = x1_s.shape[0]
    tm = ROW_TILE
    npt, per = n_p // tm, t // tm
    prompt = lambda width: pl.BlockSpec((tm, width), lambda i: (jnp.minimum(i, npt - 1), 0))
    pmod = lambda col: pl.BlockSpec((1, 1, d), lambda i: (jnp.minimum(i // per, mod_p.shape[0] - 1), 0, col))
    sample = lambda width: pl.BlockSpec((tm, width), lambda i: (jnp.maximum(i - npt, 0), 0))
    rows = lambda width: pl.BlockSpec((tm, width), lambda i: (i, 0))
    const = lambda a: pl.BlockSpec(a.shape, lambda i: (0, 0), pipeline_mode=pl.Buffered(1))
    n_rows = n_p + pad_s
    return pl.pallas_call(
        functools.partial(_outproj_kernel, n_prompt_tiles=npt),
        grid=(n_rows // tm,),
        in_specs=[prompt(o_gla.shape[1]), prompt(o_rwkv.shape[1]), prompt(d), pmod(2), pmod(4), pmod(3),
                  const(norm_w), const(w_out), const(w_router), sample(d), sample(d), sample(LANES)],
        out_specs=[rows(d), rows(d), rows(LANES)],
        out_shape=[jax.ShapeDtypeStruct((n_rows, d), F32), jax.ShapeDtypeStruct((n_rows, d), F32),
                   jax.ShapeDtypeStruct((n_rows, LANES), F32)],
        compiler_params=_params("arbitrary"),
    )(o_gla, o_rwkv, xp, mod_p, mod_p, mod_p, norm_w, w_out, w_router, x1_s, h2_s, logits_s)


def _moe_kernel(tile_expert_ref, n_valid_ref, n_real_ref, src_ref, dst_ref, h2_hbm, wg_ref, wu_ref, wd_ref,
                y_hbm, xbuf, obuf, gather_sem, scatter_sem):
    del tile_expert_ref
    i = pl.program_id(0)
    n_tiles = pl.num_programs(0)
    n_valid = n_valid_ref[0]
    slot = i % 2

    n_grp = MOE_TILE // SUBLANES
    buf_groups = lambda s: pl.ds(s * n_grp, n_grp)

    def for_rows(n, fn):
        def group(g, carry):
            for u in range(SUBLANES):
                fn(g, u)
            return carry

        def single(r, carry):
            fn(lax.shift_right_logical(r, 3), lax.bitwise_and(r, SUBLANES - 1))
            return carry
        full = lax.shift_right_logical(n, 3)
        lax.fori_loop(0, full, group, 0)
        lax.fori_loop(full * SUBLANES, n, single, 0)

    def gather_row(tile, s, g, u):
        tok = src_ref[tile * MOE_TILE + g * SUBLANES + u]
        return pltpu.make_async_copy(h2_hbm.at[pl.ds(tok, 1), :], xbuf.at[s * n_grp + g, pl.ds(u, 1), :],
                                     gather_sem.at[s])

    def scatter_row(tile, s, g, u):
        row = dst_ref[tile * MOE_TILE + g * SUBLANES + u]
        return pltpu.make_async_copy(obuf.at[s * n_grp + g, pl.ds(u, 1), :], y_hbm.at[pl.ds(row, 1), :],
                                     scatter_sem.at[s])

    def start_rows(row_copy, tile, s):
        for_rows(n_real_ref[tile], lambda g, u: row_copy(tile, s, g, u).start())

    def wait_rows(row_copy, whole_tile_copy, tile, s):
        n_real = n_real_ref[tile]

        @pl.when(n_real == MOE_TILE)
        def _():
            whole_tile_copy(s).wait()

        @pl.when(n_real < MOE_TILE)
        def _():
            for_rows(n_real, lambda g, u: row_copy(tile, s, g, u).wait())

    whole_gather = lambda s: pltpu.make_async_copy(xbuf.at[buf_groups(1 - s)], xbuf.at[buf_groups(s)],
                                                   gather_sem.at[s])
    whole_scatter = lambda s: pltpu.make_async_copy(obuf.at[buf_groups(s)], obuf.at[buf_groups(1 - s)],
                                                    scatter_sem.at[s])

    @pl.when(i == 0)
    def _():
        xbuf[...] = jnp.zeros_like(xbuf)
        start_rows(gather_row, 0, 0)

    @pl.when(i + 1 < n_valid)
    def _():
        start_rows(gather_row, i + 1, 1 - slot)

    @pl.when(i < n_valid)
    def _():
        wait_rows(gather_row, whole_gather, i, slot)

        @pl.when(i >= 2)
        def _():
            wait_rows(scatter_row, whole_scatter, i - 2, slot)

        def expert_mlp(n_rows):
            groups = pl.ds(slot * n_grp, n_rows // SUBLANES)
            x = xbuf[groups].reshape(n_rows, -1).astype(BF16)
            gate = _dot(x, wg_ref[0].astype(BF16))
            up = _dot(x, wu_ref[0].astype(BF16))
            hid = gate * jax.nn.sigmoid(gate) * up
            obuf[groups] = _dot(hid.astype(BF16), wd_ref[0].astype(BF16)).reshape(n_rows // SUBLANES, SUBLANES, -1)

        half = MOE_TILE // 2
        pl.when(n_real_ref[i] > half)(functools.partial(expert_mlp, MOE_TILE))
        pl.when(n_real_ref[i] <= half)(functools.partial(expert_mlp, half))
        start_rows(scatter_row, i, slot)

    @pl.when(i == n_tiles - 1)
    def _():
        @pl.when(n_valid >= 2)
        def _():
            wait_rows(scatter_row, whole_scatter, n_valid - 2, n_valid % 2)

        wait_rows(scatter_row, whole_scatter, n_valid - 1, (n_valid - 1) % 2)


def moe_experts(h2, plan, w_gate, w_up, w_down):
    n_rows, d = h2.shape
    _, _, ff = w_gate.shape
    n_tiles = plan["tile_expert"].shape[0]
    grid_spec = pltpu.PrefetchScalarGridSpec(
        num_scalar_prefetch=5,
        grid=(n_tiles,),
        in_specs=[pl.BlockSpec(memory_space=pl.ANY),
                  pl.BlockSpec((1, d, ff), lambda i, te, *_: (te[i], 0, 0)),
                  pl.BlockSpec((1, d, ff), lambda i, te, *_: (te[i], 0, 0)),
                  pl.BlockSpec((1, ff, d), lambda i, te, *_: (te[i], 0, 0))],
        out_specs=pl.BlockSpec(memory_space=pl.ANY),
        scratch_shapes=[pltpu.VMEM((2 * MOE_TILE // SUBLANES, SUBLANES, d), F32)] * 2
        + [pltpu.SemaphoreType.DMA((2,)), pltpu.SemaphoreType.DMA((2,))],
    )
    return pl.pallas_call(
        _moe_kernel,
        grid_spec=grid_spec,
        out_shape=jax.ShapeDtypeStruct((2 * n_rows, d), F32),
        compiler_params=_params("arbitrary"),
    )(plan["tile_expert"], plan["n_valid"], plan["n_real"], plan["src_row"], plan["dst_row"], h2,
      w_gate, w_up, w_down)


def _route_kernel(lg_ref, idx_ref, wt_ref, cnt_ref, carry):
    i = pl.program_id(0)

    @pl.when(i == 0)
    def _():
        carry[...] = jnp.zeros_like(carry)

    lg = lg_ref[...]
    n, width = lg.shape
    lane = lax.broadcasted_iota(jnp.int32, (n, width), 1)
    row_max = lambda a: jnp.max(a, axis=1, keepdims=True)
    first_lane = lambda hit: jnp.min(jnp.where(hit, lane, width), axis=1, keepdims=True)

    is_group = lane < N_GROUPS
    g_exp = jnp.where(is_group, jnp.exp(lg - row_max(jnp.where(is_group, lg, -jnp.inf))), 0.0)
    p_group = g_exp / jnp.sum(g_exp, axis=1, keepdims=True)
    p_sel = row_max(p_group)
    g_sel = first_lane(is_group & (p_group == p_sel))
    lo = N_GROUPS + g_sel * EXPERTS_PER_GROUP
    in_group = (lane >= lo) & (lane < lo + EXPERTS_PER_GROUP)
    cand = jnp.where(in_group, lg, -jnp.inf)
    v1 = row_max(cand)
    l1 = first_lane(in_group & (cand == v1))
    cand2 = jnp.where(lane == l1, -jnp.inf, cand)
    v2 = row_max(cand2)
    l2 = first_lane(in_group & (lane != l1) & (cand2 == v2))
    t = jnp.exp(v2 - v1)
    w1 = p_sel / (1.0 + t)
    w2 = p_sel * t / (1.0 + t)
    e1, e2 = l1 - N_GROUPS, l2 - N_GROUPS

    hit1, hit2 = lane == e1, lane == e2
    both = (hit1 | hit2).astype(BF16)
    r_i = lax.broadcasted_iota(jnp.int32, (n, n), 0)
    c_i = lax.broadcasted_iota(jnp.int32, (n, n), 1)
    before = carry[0:1, :] + _dot((r_i > c_i).astype(BF16), both)
    rank1 = jnp.sum(jnp.where(hit1, before, 0.0), axis=1, keepdims=True).astype(jnp.int32)
    rank2 = jnp.sum(jnp.where(hit2, before, 0.0), axis=1, keepdims=True).astype(jnp.int32)
    carry[0:1, :] = carry[0:1, :] + jnp.sum(both.astype(F32), axis=0, keepdims=True)

    idx_ref[...] = jnp.where(lane == 0, e1, jnp.where(lane == 1, e2, jnp.where(lane == 2, rank1, rank2)))
    wt_ref[...] = jnp.where(lane == 0, w1, w2)

    @pl.when(i == pl.num_programs(0) - 1)
    def _():
        cnt_ref[...] = carry[...]


def route(logits):
    n_rows, width = logits.shape
    tile = ROW_TILE
    rows = pl.BlockSpec((tile, width), lambda i: (i, 0))
    idx, wt, cnt = pl.pallas_call(
        _route_kernel,
        grid=(n_rows // tile,),
        in_specs=[rows],
        out_specs=[rows, rows, pl.BlockSpec((SUBLANES, width), lambda i: (0, 0))],
        out_shape=[jax.ShapeDtypeStruct((n_rows, width), jnp.int32), jax.ShapeDtypeStruct((n_rows, width), F32),
                   jax.ShapeDtypeStruct((SUBLANES, width), F32)],
        scratch_shapes=[pltpu.VMEM((SUBLANES, width), F32)],
        compiler_params=_params("arbitrary"),
    )(logits)
    return idx[:, 0:2], idx[:, 2:4], wt[:, 0:2], cnt[0, :N_EXPERTS].astype(jnp.int32)


def _moe_plan(expert, rank, counts):
    n = expert.shape[0]
    e_flat = expert.reshape(-1)
    ids = jnp.arange(N_EXPERTS, dtype=jnp.int32)
    one_hot = (e_flat[:, None] == ids[None, :]).astype(jnp.int32)
    padded = (counts + MOE_TILE - 1) // MOE_TILE * MOE_TILE
    ends = jnp.cumsum(padded)
    starts = ends - padded
    pos = jnp.sum(one_hot * starts[None, :], axis=1) + rank.reshape(-1)
    p_max = _round_up(2 * n, MOE_TILE) + N_EXPERTS * MOE_TILE
    n_tiles = p_max // MOE_TILE
    pair = jnp.full((p_max,), -1, jnp.int32).at[pos].set(jnp.arange(2 * n, dtype=jnp.int32),
                                                        unique_indices=True, mode="promise_in_bounds")
    pair0 = jnp.maximum(pair, 0)
    src_row = pair0 // 2
    dst_row = (pair0 % 2) * n + src_row
    tile_start = jnp.arange(n_tiles, dtype=jnp.int32) * MOE_TILE
    n_valid = ends[-1] // MOE_TILE
    tile_expert = jnp.sum((tile_start[:, None] >= ends[None, :]).astype(jnp.int32), axis=1)
    tile_expert = jnp.where(tile_start < ends[-1], tile_expert, tile_expert[n_valid - 1])
    n_real = jnp.clip((starts + counts)[tile_expert] - tile_start, 0, MOE_TILE)
    n_real = jnp.where(tile_start < ends[-1], n_real, 0).astype(jnp.int32)
    return dict(src_row=src_row, dst_row=dst_row, tile_expert=tile_expert,
                n_valid=n_valid.astype(jnp.int32)[None], n_real=n_real)


def _final_kernel(x1_ref, y0_ref, y1_ref, wt_ref, gtp_ref, gts_ref, scp_ref, scs_ref, shp_ref, shs_ref, nw_ref,
                  yp_ref, ys_ref, *, n_prompt_tiles):
    i = pl.program_id(0)

    def finish(gt, sc, sh):
        moe = wt_ref[:, 0:1] * y0_ref[...] + wt_ref[:, 1:2] * y1_ref[...]
        x2 = x1_ref[...] + gt * moe
        return _rmsnorm(x2, nw_ref[...], NORM_EPS) * (1.0 + sc) + sh

    @pl.when(i < n_prompt_tiles)
    def _():
        yp_ref[...] = finish(gtp_ref[0], scp_ref[0], shp_ref[0])

    @pl.when(i >= n_prompt_tiles)
    def _():
        ys_ref[...] = finish(gts_ref[...], scs_ref[...], shs_ref[...])


def final_norm(n_p, t, x1, y_pairs, w_top, mod_p, mod_s, modf_p, modf_s, norm_w):
    n_rows, d = x1.shape
    tile = ROW_TILE
    n_tiles, npt = n_rows // tile, n_p // tile
    per = t // tile
    rows = lambda off: pl.BlockSpec((tile, d), lambda i: (i + off, 0))
    prompt_out = pl.BlockSpec((tile, d), lambda i: (jnp.minimum(i, npt - 1), 0))
    sample = lambda col: pl.BlockSpec((tile, d), lambda i: (jnp.maximum(i - npt, 0), col))
    pmod = lambda col: pl.BlockSpec((1, 1, d), lambda i: (jnp.minimum(i // per, mod_p.shape[0] - 1), 0, col))
    return pl.pallas_call(
        functools.partial(_final_kernel, n_prompt_tiles=npt),
        grid=(n_tiles,),
        in_specs=[rows(0), rows(0), rows(n_tiles), pl.BlockSpec((tile, w_top.shape[1]), lambda i: (i, 0)),
                  pmod(5), sample(5), pmod(1), sample(1), pmod(0), sample(0),
                  pl.BlockSpec((1, d), lambda i: (0, 0))],
        out_specs=[prompt_out, sample(0)],
        out_shape=[jax.ShapeDtypeStruct((n_p, d), F32), jax.ShapeDtypeStruct((n_rows - n_p, d), F32)],
        compiler_params=_params("arbitrary"),
    )(x1, y_pairs, y_pairs, w_top, mod_p, mod_s, modf_p, modf_s, modf_p, modf_s, norm_w)


def _align_rows_kernel(w_ref, a_ref, o_ref, p_ref):
    @pl.when(pl.program_id(0) == 0)
    def _():
        p_ref[...] = jnp.zeros_like(p_ref)

    w = w_ref[...]
    w_hi, w_lo = _split2(w)
    for name in _ALIGNED:
        off, width = SRC[name]
        dst, wa = DST[name]
        o_ref[dst:dst + width, :] = w_hi[off:off + width, :]
        if wa != width:
            o_ref[dst + width:dst + wa, :] = jnp.zeros((wa - width, o_ref.shape[1]), o_ref.dtype)
    a_hi, a_lo = _split2(a_ref[...])
    p_ref[...] += _dot(a_hi, w_hi, NT) + (_dot(a_hi, w_lo, NT) + _dot(a_lo, w_hi, NT))


def align_rows_and_project(wt, a):
    n, d = wt.shape
    m = a.shape[0]
    tc = _largest_tile(d, K_SPLIT, LANES)
    return pl.pallas_call(
        _align_rows_kernel,
        grid=(d // tc,),
        in_specs=[pl.BlockSpec((n, tc), lambda j: (0, j)), pl.BlockSpec((m, tc), lambda j: (0, j))],
        out_specs=[pl.BlockSpec((IN_COLS_ALIGNED, tc), lambda j: (0, j)), pl.BlockSpec((m, n), lambda j: (0, 0))],
        out_shape=[jax.ShapeDtypeStruct((IN_COLS_ALIGNED, d), BF16), jax.ShapeDtypeStruct((m, n), F32)],
        compiler_params=_params("arbitrary"),
    )(wt, a)


def _rwkv_seg(a, name):
    off, w = SRC[name]
    return a[..., off - RWKV_SRC_BASE:off - RWKV_SRC_BASE + w]


def _rwkv_seg_padded(a, name):
    seg = _rwkv_seg(a, name)
    wa = DST[name][1]
    return jnp.pad(seg, [(0, 0)] * (a.ndim - 1) + [(0, wa - seg.shape[-1])])


def _orig_seg(p, name):
    off, w = SRC[name]
    return p[..., off:off + w]


def _pad_rows(w, rows):
    return jnp.pad(w, ((0, rows - w.shape[0]), (0, 0)))


def kernel(x_prompt, x_sample, state_gla, state_rwkv, state_shift, c_prompt, c_sample, w_ada, b_ada, norm_mix, norm_ffn, w_in, gla_gate_w2, gla_gate_b, gla_norm, rwkv_mu, rwkv_w0, rwkv_w2, rwkv_a0, rwkv_a2, rwkv_g2, rwkv_k_k, rwkv_k_a, rwkv_r_k, rwkv_gn_w, rwkv_gn_b, w_out, w_router_group, w_router_expert, w_exp_gate, w_exp_up, w_exp_down, norm_final, w_ada_final, b_ada_final):
    assert w_ada.shape[0] == 1, "single-layer step"
    bp, t, d = x_prompt.shape
    bs = x_sample.shape[0]
    assert x_sample.shape[1] == 1 and t % ROW_TILE == 0
    n_p = bp * t
    pad_s = _round_up(bs, ROW_TILE)
    n_rows = n_p + pad_s
    prompt = _PromptShape(bp, t, n_p)
    hi = lax.Precision.HIGHEST
    pad_sample = lambda a: jnp.pad(a, ((0, pad_s - a.shape[0]), (0, 0)))
    hdot = lambda a, w: jnp.dot(a, w, precision=hi)

    c_act = jax.nn.silu(jnp.concatenate([c_prompt, c_sample], axis=0))
    mod = matmul3(c_act, w_ada[0]) + b_ada[0]
    mod_f = matmul3(c_act, w_ada_final) + b_ada_final
    mod_p, mod_s = mod[:bp, None, :], mod[bp:]
    modf_p, modf_s = mod_f[:bp, None, :], mod_f[bp:]
    sh1_s, sc1_s, gt1_s, sh2_s, sc2_s, _ = jnp.split(mod_s, 6, axis=-1)
    xp = x_prompt.reshape(n_p, d)
    xs = x_sample[:, 0, :]

    w_in_t = jnp.swapaxes(w_in[0], 0, 1)
    h1_s = _rmsnorm(xs, norm_mix[0], NORM_EPS) * (1.0 + sc1_s) + sh1_s
    w_in_aligned, proj_s = align_rows_and_project(w_in_t, h1_s)
    proj = in_projection(xp, t, mod_p, norm_mix, w_in_aligned)
    o_gla, gla_t_p = gla_prompt(prompt, proj, _pad_rows(gla_gate_w2[0], LANES), gla_gate_b, gla_norm)
    new_gla_p = jnp.swapaxes(gla_t_p, -1, -2)
    mu = rwkv_mu[0]
    mu_big = jnp.concatenate([_rwkv_seg(mu, n) for n in ("r", "k7", "v7")])[None, :]
    mu_small = jnp.concatenate([_rwkv_seg_padded(mu, n) for n in ("wl", "al", "gl7")])[None, :]
    vecs = jnp.concatenate([rwkv_w0, rwkv_a0, rwkv_k_k, rwkv_k_a, rwkv_r_k[0].reshape(1, RWKV_WIDTH),
                            rwkv_gn_w, rwkv_gn_b, jnp.zeros((1, RWKV_WIDTH), F32)], axis=0)
    o_rwkv, rwkv_t_p = rwkv7_prompt(prompt, proj, mu_big, mu_small, vecs, _pad_rows(rwkv_w2[0], LANES),
                                    _pad_rows(rwkv_a2[0], LANES), rwkv_g2[0])
    new_rwkv_p = rwkv_t_p.reshape(bp, RWKV_HEAD, RWKV_HEADS, RWKV_HEAD).transpose(0, 2, 1, 3)
    last = jnp.stack([proj[(b + 1) * t - 1] for b in range(bp)])
    new_shift_p = jnp.concatenate([last[:, DST[n][0]:DST[n][0] + SRC[n][1]] for n in _RWKV_ORIG], axis=-1)

    heads = lambda a, n: a.reshape(a.shape[0], n, -1)
    logd_s = jax.nn.log_sigmoid(hdot(_orig_seg(proj_s, "gl"), gla_gate_w2[0]) + gla_gate_b[0]) \
        / GLA_GATE_NORMALIZER
    q_s = heads(_orig_seg(proj_s, "q") * (GLA_DK ** -0.5), GLA_HEADS)
    k_s, v_s, g_s = heads(_orig_seg(proj_s, "k"), GLA_HEADS), heads(_orig_seg(proj_s, "v"), GLA_HEADS), \
        heads(logd_s, GLA_HEADS)
    new_gla_s, o_s = gla_step(state_gla[0], q_s, k_s, g_s, v_s)
    o_s = _rmsnorm(o_s, gla_norm[0], GLA_NORM_EPS) * heads(jax.nn.silu(_orig_seg(proj_s, "og")), GLA_HEADS)

    rp_s = proj_s[:, RWKV_SRC_BASE:]
    new_shift_s = rp_s
    xs7 = rp_s + (state_shift[0] - rp_s) * mu
    sx = lambda name: _rwkv_seg(xs7, name)
    r_s, k7_s, v7_s = sx("r"), sx("k7"), sx("v7")
    w_pre = rwkv_w0[0] + hdot(jnp.tanh(sx("wl")), rwkv_w2[0])
    decay_s = jnp.exp(-jnp.exp(-jax.nn.softplus(-w_pre) - 0.5))
    a_s = jax.nn.sigmoid(rwkv_a0[0] + hdot(sx("al"), rwkv_a2[0]))
    g_s7 = hdot(jax.nn.sigmoid(sx("gl7")), rwkv_g2[0])
    hs = lambda z: z.reshape(bs, RWKV_HEADS, RWKV_HEAD)
    kk_s = hs(k7_s * rwkv_k_k[0])
    kk_s = kk_s / jnp.maximum(jnp.sqrt(jnp.sum(kk_s * kk_s, axis=-1, keepdims=True)), 1e-12)
    k7_s = k7_s * (1.0 + (a_s - 1.0) * rwkv_k_a[0])
    r_h, w_h, k_h, v_h, a_h = hs(r_s), hs(decay_s), hs(k7_s), hs(v7_s), hs(a_s)
    s_prev = state_rwkv[0]
    sa = jnp.einsum("bhij,bhj->bhi", s_prev, -kk_s, precision=hi)
    new_rwkv_s = s_prev * w_h[:, :, None, :] + sa[..., None] * (kk_s * a_h)[:, :, None, :] \
        + v_h[..., None] * k_h[:, :, None, :]
    y_s = jnp.einsum("bhij,bhj->bhi", new_rwkv_s, r_h, precision=hi)
    y_mu = jnp.mean(y_s, axis=-1, keepdims=True)
    y_var = jnp.mean(jnp.square(y_s - y_mu), axis=-1, keepdims=True)
    y_s = (y_s - y_mu) * lax.rsqrt(y_var + RWKV_GN_EPS) * rwkv_gn_w[0].reshape(RWKV_HEADS, RWKV_HEAD) \
        + rwkv_gn_b[0].reshape(RWKV_HEADS, RWKV_HEAD)
    bonus = jnp.sum(r_h * k_h * rwkv_r_k[0], axis=-1, keepdims=True) * v_h
    o_rs = (y_s + bonus).reshape(bs, RWKV_WIDTH) * g_s7
    mix_s = matmul3(jnp.concatenate([o_s.reshape(bs, GLA_WIDTH), o_rs], axis=-1), w_out[0])
    x1_s = xs + gt1_s * mix_s
    h2_s = _rmsnorm(x1_s, norm_ffn[0], NORM_EPS) * (1.0 + sc2_s) + sh2_s
    w_router = jnp.pad(jnp.concatenate([w_router_group[0], w_router_expert[0]], axis=-1),
                       ((0, 0), (0, LANES - N_GROUPS - N_EXPERTS)))
    logits_s = hdot(h2_s, w_router)

    x1, h2, logits = out_projection(t, o_gla, o_rwkv, xp, mod_p, norm_ffn, w_out[0].astype(BF16), w_router,
                                    pad_sample(x1_s), pad_sample(h2_s), pad_sample(logits_s))

    expert, rank, w_top, counts = route(logits)
    plan = _moe_plan(expert, rank, counts)
    y_pairs = moe_experts(h2, plan, w_exp_gate[0].reshape(N_EXPERTS, d, EXPERT_FF),
                          w_exp_up[0].reshape(N_EXPERTS, d, EXPERT_FF),
                          w_exp_down[0].reshape(N_EXPERTS, EXPERT_FF, d))

    y_p, y_s_pad = final_norm(n_p, t, x1, y_pairs, w_top, mod_p, pad_sample(mod_s), modf_p,
                              pad_sample(modf_s), norm_final[None, :])
    return (y_p.reshape(bp, t, d), y_s_pad[:bs, None, :], new_gla_p[None], new_rwkv_p[None], new_shift_p[None],
            new_gla_s[None], new_rwkv_s[None], new_shift_s[None])
```

```python
import collections
import functools

import jax
import jax.numpy as jnp
from jax import lax
from jax.experimental import pallas as pl
from jax.experimental.pallas import tpu as pltpu

F32 = jnp.float32
BF16 = jnp.bfloat16

D_MODEL = 2048
GLA_HEADS = 4
GLA_DK = 128
GLA_DV = 256
GLA_KEY_WIDTH = GLA_HEADS * GLA_DK
GLA_WIDTH = GLA_HEADS * GLA_DV
GLA_GATE_NORMALIZER = 16.0
RWKV_HEAD = 64
RWKV_HEADS = 16
RWKV_WIDTH = RWKV_HEAD * RWKV_HEADS
N_GROUPS = 4
EXPERTS_PER_GROUP = 8
N_EXPERTS = N_GROUPS * EXPERTS_PER_GROUP
EXPERT_FF = 512
NORM_EPS = 1e-6
GLA_NORM_EPS = 1e-5
RWKV_GN_EPS = 64e-5

LANES = 128
SUBLANES = 8
VMEM_LIMIT_BYTES = 56 * 1024 * 1024

CHUNK = 64
SUB = 16
RWKV_GROUP = 2
GROUP_W = RWKV_GROUP * RWKV_HEAD
GLA_SEQS = 4
LOG2E = 1.4426950408889634
RWKV_SEQS = 4
MOE_TILE = 256
SAMPLE_BLOCK = 16
ROW_TILE = 256
NORM_SLAB = 128
K_SPLIT = 256

_ORIG = (("q", 512), ("k", 512), ("v", 1024), ("gl", 16), ("og", 1024),
         ("r", 1024), ("wl", 64), ("k7", 1024), ("v7", 1024), ("al", 64), ("gl7", 128))
_ALIGNED = ("q", "k", "v", "og", "r", "k7", "v7", "gl", "wl", "al", "gl7")
_RWKV_ORIG = ("r", "wl", "k7", "v7", "al", "gl7")


def _round_up(n, m):
    return (n + m - 1) // m * m


def _layouts():
    src, off = {}, 0
    for name, w in _ORIG:
        src[name] = (off, w)
        off += w
    dst, pos = {}, 0
    for name in _ALIGNED:
        wa = _round_up(src[name][1], LANES)
        dst[name] = (pos, wa)
        pos += wa
    return src, off, dst, pos


SRC, IN_COLS, DST, IN_COLS_ALIGNED = _layouts()
RWKV_SRC_BASE = SRC["r"][0]
BIG_W = 3 * RWKV_WIDTH
SMALL_W = 3 * LANES
assert DST["r"][0] == BIG_W and DST["k7"][0] == BIG_W + RWKV_WIDTH and DST["v7"][0] == BIG_W + 2 * RWKV_WIDTH
assert DST["v"][0] == RWKV_WIDTH and DST["og"][0] == 2 * RWKV_WIDTH and DST["k"][0] == GLA_KEY_WIDTH

NN = ((1,), (0,))
NT = ((1,), (1,))

_PromptShape = collections.namedtuple("_PromptShape", "bp t rows")


def _dot(a, b, dims=NN):
    return lax.dot_general(a, b, (dims, ((), ())), preferred_element_type=F32)


def _split2(x):
    hi = x.astype(BF16)
    lo = (x - hi.astype(F32)).astype(BF16)
    return hi, lo


def _dotp(a, b, dims=NN, passes=1):
    if passes == 1:
        return _dot(a.astype(BF16), b.astype(BF16), dims)
    ah, al = _split2(a)
    bh, bl = _split2(b)
    return _dot(ah, bh, dims) + (_dot(ah, bl, dims) + _dot(al, bh, dims))


def _cumsum_rows(x):
    n = x.shape[0]
    row = lax.broadcasted_iota(jnp.int32, (n, n), 0)
    col = lax.broadcasted_iota(jnp.int32, (n, n), 1)
    tri = (row >= col).astype(BF16)
    x1 = x.astype(BF16)
    r1 = x - x1.astype(F32)
    x2 = r1.astype(BF16)
    x3 = (r1 - x2.astype(F32)).astype(BF16)
    return _dot(tri, x1) + (_dot(tri, x2) + _dot(tri, x3))


def _softplus(z):
    return jnp.maximum(z, 0.0) + jnp.log1p(jnp.exp(-jnp.abs(z)))


def _rmsnorm(x, g, eps):
    return x * lax.rsqrt(jnp.mean(x * x, axis=-1, keepdims=True) + eps) * g


def _largest_tile(n, cap, mult=SUBLANES):
    if n <= cap:
        return n
    best = None
    for t in range(mult, cap + 1, mult):
        if n % t == 0:
            best = t
    assert best is not None, (n, cap)
    return best


def _params(*sem):
    return pltpu.CompilerParams(dimension_semantics=sem, vmem_limit_bytes=VMEM_LIMIT_BYTES)


def _mm3_kernel(a_ref, w_ref, o_ref):
    @pl.when(pl.program_id(0) == 0)
    def _():
        o_ref[...] = jnp.zeros_like(o_ref)

    o_ref[...] += _dotp(a_ref[...], w_ref[...], NN, 3)


def matmul3(a, w):
    m, k = a.shape
    n = w.shape[1]
    tk = _largest_tile(k, K_SPLIT, LANES)
    return pl.pallas_call(
        _mm3_kernel,
        grid=(k // tk,),
        in_specs=[pl.BlockSpec((m, tk), lambda s: (0, s)), pl.BlockSpec((tk, n), lambda s: (s, 0))],
        out_specs=pl.BlockSpec((m, n), lambda s: (0, 0)),
        out_shape=jax.ShapeDtypeStruct((m, n), F32),
        compiler_params=_params("arbitrary"),
    )(a, w)


def _prompt_mod(tiles_per_batch, width, col):
    return pl.BlockSpec((1, 1, width), lambda i, *_: (i // tiles_per_batch, 0, col))


def _inproj_kernel(x_ref, sc_ref, sh_ref, nw_ref, w_ref, o_ref, h_scr):
    @pl.when(pl.program_id(1) == 0)
    def _():
        for r in range(0, h_scr.shape[0], NORM_SLAB):
            sl = slice(r, r + NORM_SLAB)
            h = _rmsnorm(x_ref[sl, :], nw_ref[...], NORM_EPS) * (1.0 + sc_ref[0]) + sh_ref[0]
            h_scr[sl, :] = h.astype(BF16)

    o_ref[...] = _dot(h_scr[...], w_ref[...], NT)


def in_projection(xp, t, mod_p, norm_w, w_aligned):
    n_p, d = xp.shape
    n = w_aligned.shape[0]
    tm = _largest_tile(t, 1024, NORM_SLAB)
    tn = _largest_tile(n, 1664, LANES)
    per = t // tm
    return pl.pallas_call(
        _inproj_kernel,
        grid=(n_p // tm, n // tn),
        in_specs=[pl.BlockSpec((tm, d), lambda i, j: (i, 0)), _prompt_mod(per, d, 1), _prompt_mod(per, d, 0),
                  pl.BlockSpec((1, d), lambda i, j: (0, 0)), pl.BlockSpec((tn, d), lambda i, j: (j, 0))],
        out_specs=pl.BlockSpec((tm, tn), lambda i, j: (i, j)),
        out_shape=jax.ShapeDtypeStruct((n_p, n), F32),
        scratch_shapes=[pltpu.VMEM((tm, d), BF16)],
        compiler_params=_params("parallel", "arbitrary"),
    )(xp, mod_p, mod_p, norm_w, w_aligned)


def _rwkv_chunk_kernel(big_ref, wl_ref, al_ref, gl_ref, mu_big_ref, mu_small_ref, vec_ref, w2_ref, a2_ref,
                       g2_ref, o_ref, s_ref, state, carry_big, carry_small, *, passes):
    c_idx = pl.program_id(1)
    n_seq, n_tok = big_ref.shape[0], big_ref.shape[1]
    n_grp = RWKV_WIDTH // GROUP_W
    grp = [slice(g * GROUP_W, (g + 1) * GROUP_W) for g in range(n_grp)]

    @pl.when(c_idx == 0)
    def _():
        state[...] = jnp.zeros_like(state)
        carry_big[...] = jnp.zeros_like(carry_big)
        carry_small[...] = jnp.zeros_like(carry_small)

    first_row = lax.broadcasted_iota(jnp.int32, (n_tok, 1), 0) == 0
    w0, a0, k_k, k_a, r_k, gn_w, gn_b = (vec_ref[i:i + 1, :] for i in range(7))

    gi = lax.broadcasted_iota(jnp.int32, (GROUP_W, GROUP_W), 0) // RWKV_HEAD
    gj = lax.broadcasted_iota(jnp.int32, (GROUP_W, GROUP_W), 1) // RWKV_HEAD
    head_ones = (gi == gj).astype(BF16)

    def head_sums(parts):
        hi, lo = _split2(jnp.concatenate(parts, axis=0))
        out = _dot(hi, head_ones) + _dot(lo, head_ones)
        return [out[i * n_tok:(i + 1) * n_tok] for i in range(len(parts))]

    def prepare(n):
        def token_shift(cur, carry, mu):
            prev = jnp.where(first_row, carry[n, 0:1, :], pltpu.roll(cur, 1, 0))
            carry[n, 0:1, :] = cur[n_tok - 1:n_tok, :]
            return cur + (prev - cur) * mu

        xs_big = token_shift(big_ref[n], carry_big, mu_big_ref[...])
        small = jnp.concatenate([wl_ref[n], al_ref[n], gl_ref[n]], axis=1)
        xs_small = token_shift(small, carry_small, mu_small_ref[...])
        r = xs_big[:, :RWKV_WIDTH]
        k7 = xs_big[:, RWKV_WIDTH:2 * RWKV_WIDTH]
        v = xs_big[:, 2 * RWKV_WIDTH:]
        w_pre = w0 + _dotp(jnp.tanh(xs_small[:, :LANES]), w2_ref[...])
        lw = -jnp.exp(-_softplus(-w_pre) - 0.5)
        a = jax.nn.sigmoid(a0 + _dotp(xs_small[:, LANES:2 * LANES], a2_ref[...]))
        gate = _dotp(jax.nn.sigmoid(xs_small[:, 2 * LANES:]), g2_ref[...])
        kk_raw = k7 * k_k
        k = k7 * (1.0 + (a - 1.0) * k_a)
        sums = [head_sums([kk_raw[:, sl] * kk_raw[:, sl], r[:, sl] * k[:, sl] * r_k[:, sl]]) for sl in grp]
        kk = jnp.concatenate([kk_raw[:, sl] / jnp.maximum(jnp.sqrt(sums[g][0]), 1e-12)
                              for g, sl in enumerate(grp)], axis=1)
        bonus = jnp.concatenate([sums[g][1] for g in range(n_grp)], axis=1) * v
        cum = _cumsum_rows(lw)
        cum_end = cum[n_tok - 1:n_tok, :]
        beta = kk * a
        g_inv = jnp.exp(-cum)
        g_end = jnp.exp(cum_end - cum)
        return dict(v=v, gate=gate, bonus=bonus, g_tot=jnp.exp(cum_end),
                    a_hat=-kk * jnp.exp(cum - lw), r_hat=r * jnp.exp(cum), b_hat=beta * g_inv,
                    k_hat=k * g_inv, b_end=beta * g_end, k_end=k * g_end)

    seqs = [prepare(n) for n in range(n_seq)]
    units = [(n, sl) for n in range(n_seq) for sl in grp]
    part = lambda name: [seqs[n][name][:, sl] for n, sl in units]

    lane = lax.broadcasted_iota(jnp.int32, (n_tok, GROUP_W), 1)
    tok = lax.broadcasted_iota(jnp.int32, (n_tok, GROUP_W), 0)
    lane_head = lane // RWKV_HEAD
    src_tok = lane % RWKV_HEAD
    strict = tok > src_tok
    incl = tok >= src_tok

    def bd(y):
        return jnp.concatenate([jnp.where(lane_head == h, y, 0.0) for h in range(RWKV_GROUP)], axis=0)

    gs = range(len(units))
    s0 = [state[n, :, sl] for n, sl in units]
    v_u, b_hat, k_hat, b_end, k_end = part("v"), part("b_hat"), part("k_hat"), part("b_end"), part("k_end")
    lhs2 = [jnp.concatenate([a, r], axis=0) for a, r in zip(part("a_hat"), part("r_hat"))]
    abrb = [_dotp(lhs2[g], bd(b_hat[g]), NT, passes) for g in gs]
    akrk = [_dotp(lhs2[g], bd(k_hat[g]), NT, passes) for g in gs]
    asrs = [_dotp(lhs2[g], bd(s0[g]), NT, passes) for g in gs]
    p = [jnp.where(strict, abrb[g][:n_tok], 0.0) for g in gs]
    ak = [jnp.where(strict, akrk[g][:n_tok], 0.0) for g in gs]
    rb = [jnp.where(incl, abrb[g][n_tok:], 0.0) for g in gs]
    rk = [jnp.where(incl, akrk[g][n_tok:], 0.0) for g in gs]
    bd_v = [bd(v_u[g]) for g in gs]
    x = [asrs[g][:n_tok] + _dotp(ak[g], bd_v[g], NN, passes) for g in gs]
    n_sq = n_tok.bit_length() - 1
    for it in range(n_sq):
        if it < n_sq - 1:
            both = [_dotp(p[g], jnp.concatenate([bd(p[g]), bd(x[g])], axis=1), NN, passes) for g in gs]
            x = [x[g] + both[g][:, GROUP_W:] for g in gs]
            p = [both[g][:, :GROUP_W] for g in gs]
        else:
            x = [x[g] + _dotp(p[g], bd(x[g]), NN, passes) for g in gs]
    y = [asrs[g][n_tok:] + _dotp(jnp.concatenate([rb[g], rk[g]], axis=1),
                                 jnp.concatenate([bd(x[g]), bd_v[g]], axis=0), NN, passes) for g in gs]
    full = [_dotp(jnp.concatenate([x[g], v_u[g]], axis=0).T,
                  jnp.concatenate([b_end[g], k_end[g]], axis=0), NN, passes) for g in gs]
    g_tot = part("g_tot")
    for g, (n, sl) in enumerate(units):
        upd = s0[g] * g_tot[g]
        for h in range(RWKV_GROUP):
            upd = upd + jnp.where(lane_head == h, full[g][h * RWKV_HEAD:(h + 1) * RWKV_HEAD, :], 0.0)
        state[n, :, sl] = upd

    inv_n = 1.0 / RWKV_HEAD
    mean = [head_sums([y[g]])[0] * inv_n for g in gs]
    dev = [y[g] - mean[g] for g in gs]
    var = [head_sums([dev[g] * dev[g]])[0] * inv_n for g in gs]
    bonus, gate = part("bonus"), part("gate")
    for g, (n, sl) in enumerate(units):
        yn = dev[g] * lax.rsqrt(var[g] + RWKV_GN_EPS) * gn_w[:, sl] + gn_b[:, sl]
        o_ref[n, :, sl] = ((yn + bonus[g]) * gate[g]).astype(o_ref.dtype)

    @pl.when(c_idx == pl.num_programs(1) - 1)
    def _():
        s_ref[...] = state[...]


def rwkv7_prompt(rows, proj, mu_big, mu_small, vecs, w2, a2, g2, *, passes=1):
    assert CHUNK == RWKV_HEAD and rows.t % CHUNK == 0
    nc = rows.t // CHUNK
    n_seq = _largest_tile(rows.bp, RWKV_SEQS, 1)
    proj3 = proj.reshape(rows.bp, rows.t, proj.shape[1])
    small_col = lambda name: DST[name][0] // LANES
    tok = lambda width, col: pl.BlockSpec((n_seq, CHUNK, width), lambda b, c: (b, c, col))
    const = lambda shape: pl.BlockSpec(shape, lambda b, c: (0, 0))
    out, final_state = pl.pallas_call(
        functools.partial(_rwkv_chunk_kernel, passes=passes),
        grid=(rows.bp // n_seq, nc),
        in_specs=[tok(BIG_W, 1), tok(LANES, small_col("wl")), tok(LANES, small_col("al")),
                  tok(LANES, small_col("gl7")),
                  const((1, BIG_W)), const((1, SMALL_W)), const((SUBLANES, RWKV_WIDTH)),
                  const((LANES, RWKV_WIDTH)), const((LANES, RWKV_WIDTH)), const((LANES, RWKV_WIDTH))],
        out_specs=[tok(RWKV_WIDTH, 0),
                   pl.BlockSpec((n_seq, RWKV_HEAD, RWKV_WIDTH), lambda b, c: (b, 0, 0))],
        out_shape=[jax.ShapeDtypeStruct((rows.bp, rows.t, RWKV_WIDTH), BF16),
                   jax.ShapeDtypeStruct((rows.bp, RWKV_HEAD, RWKV_WIDTH), F32)],
        scratch_shapes=[pltpu.VMEM((n_seq, RWKV_HEAD, RWKV_WIDTH), F32),
                        pltpu.VMEM((n_seq, SUBLANES, BIG_W), F32), pltpu.VMEM((n_seq, SUBLANES, SMALL_W), F32)],
        compiler_params=_params("parallel", "arbitrary"),
    )(proj3, proj3, proj3, proj3, mu_big, mu_small, vecs, w2, a2, g2)
    return out.reshape(rows.rows, RWKV_WIDTH), final_state


def _gla_chunk_kernel(qk_ref, v_ref, og_ref, gl_ref, w2_ref, gb_ref, nw_ref, o_ref, s_ref, state):
    c_idx = pl.program_id(1)
    n_seq, n_tok = qk_ref.shape[0], qk_ref.shape[1]
    n_sub = n_tok // SUB

    @pl.when(c_idx == 0)
    def _():
        state[...] = jnp.zeros_like(state)

    row_k = lax.broadcasted_iota(jnp.int32, (n_tok, GLA_DK), 0)
    att_row = lax.broadcasted_iota(jnp.int32, (SUB, n_tok), 0)
    att_col = lax.broadcasted_iota(jnp.int32, (SUB, n_tok), 1)
    own_col = [jnp.where((att_col >= i * SUB) & (att_col - i * SUB <= att_row), att_col - i * SUB, -1)
               for i in range(n_sub)]

    def cum_log2_decay(n):
        logd = -_softplus(-(_dotp(gl_ref[n], w2_ref[...]) + gb_ref[...])) * (LOG2E / GLA_GATE_NORMALIZER)
        return _cumsum_rows(logd)

    units = [(n, h) for n in range(n_seq) for h in range(GLA_HEADS)]
    us = range(len(units))
    ks = lambda h: slice(h * GLA_DK, (h + 1) * GLA_DK)
    vs = lambda h: slice(h * GLA_DV, (h + 1) * GLA_DV)
    b_seq = [cum_log2_decay(n) for n in range(n_seq)]
    q = [qk_ref[n, :, ks(h)] * (GLA_DK ** -0.5) for n, h in units]
    k = [qk_ref[n, :, GLA_KEY_WIDTH + h * GLA_DK:GLA_KEY_WIDTH + (h + 1) * GLA_DK] for n, h in units]
    b = [b_seq[n][:, ks(h)] for n, h in units]
    v = [v_ref[n, :, vs(h)] for n, h in units]
    st = [state[n, h] for n, h in units]
    o_inter = [_dotp(q[u] * jnp.exp2(b[u]), st[u], NT) for u in us]
    blocks = [[] for _ in us]
    for i in range(n_sub):
        lo = i * SUB
        rows = slice(lo, lo + SUB)
        if i > 0:
            att = [_dotp(q[u][rows] * jnp.exp2(b[u][rows] - b[u][lo - 1:lo]),
                         jnp.where(row_k < lo, k[u] * jnp.exp2(b[u][lo - 1:lo] - b[u]), 0.0), NT) for u in us]
        else:
            att = [jnp.zeros((SUB, n_tok), F32) for _ in us]
        for j in range(SUB):
            tok = slice(lo + j, lo + j + 1)
            col = [jnp.sum(q[u][rows] * (k[u][tok] * jnp.exp2(b[u][rows] - b[u][tok])), axis=1, keepdims=True)
                   for u in us]
            att = [jnp.where(own_col[i] == j, col[u], att[u]) for u in us]
        for u in us:
            blocks[u].append(o_inter[u][rows] + _dotp(att[u], v[u], NN))
    for u, (n, h) in enumerate(units):
        o = jnp.concatenate(blocks[u], axis=0)
        og = og_ref[n, :, vs(h)]
        o_ref[n, :, vs(h)] = (_rmsnorm(o, nw_ref[...], GLA_NORM_EPS)
                              * (og * jax.nn.sigmoid(og))).astype(o_ref.dtype)
        b_last = b[u][n_tok - 1:n_tok, :]
        state[n, h] = st[u] * jnp.exp2(b_last) + _dotp(v[u].T, k[u] * jnp.exp2(b_last - b[u]), NN)

    @pl.when(c_idx == pl.num_programs(1) - 1)
    def _():
        s_ref[...] = state[...]


def gla_prompt(rows, proj, gate_w2, gate_b, norm_w):
    nc = rows.t // CHUNK
    n_seq = _largest_tile(rows.bp, GLA_SEQS, 1)
    proj3 = proj.reshape(rows.bp, rows.t, proj.shape[1])
    tok = lambda width, col: pl.BlockSpec((n_seq, CHUNK, width), lambda b, c: (b, c, col))
    const = lambda shape: pl.BlockSpec(shape, lambda b, c: (0, 0))
    out, final_state = pl.pallas_call(
        _gla_chunk_kernel,
        grid=(rows.bp // n_seq, nc),
        in_specs=[tok(GLA_WIDTH, 0), tok(GLA_WIDTH, 1), tok(GLA_WIDTH, 2), tok(LANES, DST["gl"][0] // LANES),
                  const((LANES, GLA_KEY_WIDTH)), const((1, GLA_KEY_WIDTH)), const((1, GLA_DV))],
        out_specs=[tok(GLA_WIDTH, 0),
                   pl.BlockSpec((n_seq, GLA_HEADS, GLA_DV, GLA_DK), lambda b, c: (b, 0, 0, 0))],
        out_shape=[jax.ShapeDtypeStruct((rows.bp, rows.t, GLA_WIDTH), BF16),
                   jax.ShapeDtypeStruct((rows.bp, GLA_HEADS, GLA_DV, GLA_DK), F32)],
        scratch_shapes=[pltpu.VMEM((n_seq, GLA_HEADS, GLA_DV, GLA_DK), F32)],
        compiler_params=_params("parallel", "arbitrary"),
    )(proj3, proj3, proj3, proj3, gate_w2, gate_b, norm_w)
    return out.reshape(rows.rows, GLA_WIDTH), final_state


def _columns(a, block):
    rows, heads, n = a.shape
    return a.reshape(rows // block, block, heads, n).transpose(0, 3, 1, 2).reshape(rows // block, n, block * heads)


def _gla_step_kernel(s_ref, qt_ref, kt_ref, gt_ref, v_ref, so_ref, o_ref):
    n_row, n_head = s_ref.shape[0], s_ref.shape[1]
    qt, kt, decay = qt_ref[0], kt_ref[0], jnp.exp(gt_ref[0])
    for b in range(n_row):
        for h in range(n_head):
            j = b * n_head + h
            s_new = decay[:, j:j + 1] * s_ref[b, h] + kt[:, j:j + 1] * v_ref[b, h:h + 1, :]
            so_ref[b, h] = s_new
            o_ref[b, h:h + 1, :] = jnp.sum(qt[:, j:j + 1] * s_new, axis=0, keepdims=True)


def gla_step(state, q, k, logd, v):
    rows, heads, dk, dv = state.shape
    blk = _largest_tile(rows, SAMPLE_BLOCK, 1)
    col = pl.BlockSpec((1, dk, blk * heads), lambda i: (i, 0, 0))
    s_spec = pl.BlockSpec((blk, heads, dk, dv), lambda i: (i, 0, 0, 0))
    v_spec = pl.BlockSpec((blk, heads, dv), lambda i: (i, 0, 0))
    return pl.pallas_call(
        _gla_step_kernel,
        grid=(rows // blk,),
        in_specs=[s_spec, col, col, col, v_spec],
        out_specs=[s_spec, v_spec],
        out_shape=[jax.ShapeDtypeStruct(state.shape, F32), jax.ShapeDtypeStruct((rows, heads, dv), F32)],
        compiler_params=_params("parallel"),
    )(state, _columns(q, blk), _columns(k, blk), _columns(logd, blk), v)


def _outproj_kernel(og_ref, orw_ref, x_ref, gt_ref, sc_ref, sh_ref, nw_ref, wo_ref, wr_ref,
                    x1s_ref, h2s_ref, lgs_ref, x1_ref, h2_ref, lg_ref, *, n_prompt_tiles):
    i = pl.program_id(0)

    @pl.when(i < n_prompt_tiles)
    def _():
        half = og_ref.shape[1]
        mix = _dot(og_ref[...], wo_ref[:half, :]) + _dot(orw_ref[...], wo_ref[half:, :])
        x1 = x_ref[...] + gt_ref[0] * mix
        h2 = _rmsnorm(x1, nw_ref[...], NORM_EPS) * (1.0 + sc_ref[0]) + sh_ref[0]
        x1_ref[...] = x1
        h2_ref[...] = h2
        lg_ref[...] = _dotp(h2, wr_ref[...], NN, 3)

    @pl.when(i >= n_prompt_tiles)
    def _():
        x1_ref[...] = x1s_ref[...]
        h2_ref[...] = h2s_ref[...]
        lg_ref[...] = lgs_ref[...]


def out_projection(t, o_gla, o_rwkv, xp, mod_p, norm_w, w_out, w_router, x1_s, h2_s, logits_s):
    n_p, d = xp.shape
    pad_s = x1_s.shape[0]
    tm = ROW_TILE
    npt, per = n_p // tm, t // tm
    prompt = lambda width: pl.BlockSpec((tm, width), lambda i: (jnp.minimum(i, npt - 1), 0))
    pmod = lambda col: pl.BlockSpec((1, 1, d), lambda i: (jnp.minimum(i // per, mod_p.shape[0] - 1), 0, col))
    sample = lambda width: pl.BlockSpec((tm, width), lambda i: (jnp.maximum(i - npt, 0), 0))
    rows = lambda width: pl.BlockSpec((tm, width), lambda i: (i, 0))
    const = lambda a: pl.BlockSpec(a.shape, lambda i: (0, 0), pipeline_mode=pl.Buffered(1))
    n_rows = n_p + pad_s
    return pl.pallas_call(
        functools.partial(_outproj_kernel, n_prompt_tiles=npt),
        grid=(n_rows // tm,),
        in_specs=[prompt(o_gla.shape[1]), prompt(o_rwkv.shape[1]), prompt(d), pmod(2), pmod(4), pmod(3),
                  const(norm_w), const(w_out), const(w_router), sample(d), sample(d), sample(LANES)],
        out_specs=[rows(d), rows(d), rows(LANES)],
        out_shape=[jax.ShapeDtypeStruct((n_rows, d), F32), jax.ShapeDtypeStruct((n_rows, d), F32),
                   jax.ShapeDtypeStruct((n_rows, LANES), F32)],
        compiler_params=_params("arbitrary"),
    )(o_gla, o_rwkv, xp, mod_p, mod_p, mod_p, norm_w, w_out, w_router, x1_s, h2_s, logits_s)


def _moe_kernel(tile_expert_ref, n_valid_ref, n_real_ref, src_ref, dst_ref, h2_hbm, wg_ref, wu_ref, wd_ref,
                y_hbm, xbuf, obuf, gather_sem, scatter_sem):
    del tile_expert_ref
    i = pl.program_id(0)
    n_tiles = pl.num_programs(0)
    n_valid = n_valid_ref[0]
    slot = i % 2

    n_grp = MOE_TILE // SUBLANES
    buf_groups = lambda s: pl.ds(s * n_grp, n_grp)

    def for_rows(n, fn):
        def group(g, carry):
            for u in range(SUBLANES):
                fn(g, u)
            return carry

        def single(r, carry):
            fn(lax.shift_right_logical(r, 3), lax.bitwise_and(r, SUBLANES - 1))
            return carry
        full = lax.shift_right_logical(n, 3)
        lax.fori_loop(0, full, group, 0)
        lax.fori_loop(full * SUBLANES, n, single, 0)

    def gather_row(tile, s, g, u):
        tok = src_ref[tile * MOE_TILE + g * SUBLANES + u]
        return pltpu.make_async_copy(h2_hbm.at[pl.ds(tok, 1), :], xbuf.at[s * n_grp + g, pl.ds(u, 1), :],
                                     gather_sem.at[s])

    def scatter_row(tile, s, g, u):
        row = dst_ref[tile * MOE_TILE + g * SUBLANES + u]
        return pltpu.make_async_copy(obuf.at[s * n_grp + g, pl.ds(u, 1), :], y_hbm.at[pl.ds(row, 1), :],
                                     scatter_sem.at[s])

    def start_rows(row_copy, tile, s):
        for_rows(n_real_ref[tile], lambda g, u: row_copy(tile, s, g, u).start())

    def wait_rows(row_copy, whole_tile_copy, tile, s):
        n_real = n_real_ref[tile]

        @pl.when(n_real == MOE_TILE)
        def _():
            whole_tile_copy(s).wait()

        @pl.when(n_real < MOE_TILE)
        def _():
            for_rows(n_real, lambda g, u: row_copy(tile, s, g, u).wait())

    whole_gather = lambda s: pltpu.make_async_copy(xbuf.at[buf_groups(1 - s)], xbuf.at[buf_groups(s)],
                                                   gather_sem.at[s])
    whole_scatter = lambda s: pltpu.make_async_copy(obuf.at[buf_groups(s)], obuf.at[buf_groups(1 - s)],
                                                    scatter_sem.at[s])

    @pl.when(i == 0)
    def _():
        xbuf[...] = jnp.zeros_like(xbuf)
        start_rows(gather_row, 0, 0)

    @pl.when(i + 1 < n_valid)
    def _():
        start_rows(gather_row, i + 1, 1 - slot)

    @pl.when(i < n_valid)
    def _():
        wait_rows(gather_row, whole_gather, i, slot)

        @pl.when(i >= 2)
        def _():
            wait_rows(scatter_row, whole_scatter, i - 2, slot)

        x = xbuf[buf_groups(slot)].reshape(MOE_TILE, -1).astype(BF16)
        gate = _dot(x, wg_ref[0].astype(BF16))
        up = _dot(x, wu_ref[0].astype(BF16))
        hid = gate * jax.nn.sigmoid(gate) * up
        obuf[buf_groups(slot)] = _dot(hid.astype(BF16), wd_ref[0].astype(BF16)).reshape(n_grp, SUBLANES, -1)
        start_rows(scatter_row, i, slot)

    @pl.when(i == n_tiles - 1)
    def _():
        @pl.when(n_valid >= 2)
        def _():
            wait_rows(scatter_row, whole_scatter, n_valid - 2, n_valid % 2)

        wait_rows(scatter_row, whole_scatter, n_valid - 1, (n_valid - 1) % 2)


def moe_experts(h2, plan, w_gate, w_up, w_down):
    n_rows, d = h2.shape
    _, _, ff = w_gate.shape
    n_tiles = plan["tile_expert"].shape[0]
    grid_spec = pltpu.PrefetchScalarGridSpec(
        num_scalar_prefetch=5,
        grid=(n_tiles,),
        in_specs=[pl.BlockSpec(memory_space=pl.ANY),
                  pl.BlockSpec((1, d, ff), lambda i, te, *_: (te[i], 0, 0)),
                  pl.BlockSpec((1, d, ff), lambda i, te, *_: (te[i], 0, 0)),
                  pl.BlockSpec((1, ff, d), lambda i, te, *_: (te[i], 0, 0))],
        out_specs=pl.BlockSpec(memory_space=pl.ANY),
        scratch_shapes=[pltpu.VMEM((2 * MOE_TILE // SUBLANES, SUBLANES, d), F32)] * 2
        + [pltpu.SemaphoreType.DMA((2,)), pltpu.SemaphoreType.DMA((2,))],
    )
    return pl.pallas_call(
        _moe_kernel,
        grid_spec=grid_spec,
        out_shape=jax.ShapeDtypeStruct((2 * n_rows, d), F32),
        compiler_params=_params("arbitrary"),
    )(plan["tile_expert"], plan["n_valid"], plan["n_real"], plan["src_row"], plan["dst_row"], h2,
      w_gate, w_up, w_down)


def _route_kernel(lg_ref, idx_ref, wt_ref, cnt_ref, carry):
    i = pl.program_id(0)

    @pl.when(i == 0)
    def _():
        carry[...] = jnp.zeros_like(carry)

    lg = lg_ref[...]
    n, width = lg.shape
    lane = lax.broadcasted_iota(jnp.int32, (n, width), 1)
    row_max = lambda a: jnp.max(a, axis=1, keepdims=True)
    first_lane = lambda hit: jnp.min(jnp.where(hit, lane, width), axis=1, keepdims=True)

    is_group = lane < N_GROUPS
    g_exp = jnp.where(is_group, jnp.exp(lg - row_max(jnp.where(is_group, lg, -jnp.inf))), 0.0)
    p_group = g_exp / jnp.sum(g_exp, axis=1, keepdims=True)
    p_sel = row_max(p_group)
    g_sel = first_lane(is_group & (p_group == p_sel))
    lo = N_GROUPS + g_sel * EXPERTS_PER_GROUP
    in_group = (lane >= lo) & (lane < lo + EXPERTS_PER_GROUP)
    cand = jnp.where(in_group, lg, -jnp.inf)
    v1 = row_max(cand)
    l1 = first_lane(in_group & (cand == v1))
    cand2 = jnp.where(lane == l1, -jnp.inf, cand)
    v2 = row_max(cand2)
    l2 = first_lane(in_group & (lane != l1) & (cand2 == v2))
    t = jnp.exp(v2 - v1)
    w1 = p_sel / (1.0 + t)
    w2 = p_sel * t / (1.0 + t)
    e1, e2 = l1 - N_GROUPS, l2 - N_GROUPS

    hit1, hit2 = lane == e1, lane == e2
    both = (hit1 | hit2).astype(BF16)
    r_i = lax.broadcasted_iota(jnp.int32, (n, n), 0)
    c_i = lax.broadcasted_iota(jnp.int32, (n, n), 1)
    before = carry[0:1, :] + _dot((r_i > c_i).astype(BF16), both)
    rank1 = jnp.sum(jnp.where(hit1, before, 0.0), axis=1, keepdims=True).astype(jnp.int32)
    rank2 = jnp.sum(jnp.where(hit2, before, 0.0), axis=1, keepdims=True).astype(jnp.int32)
    carry[0:1, :] = carry[0:1, :] + jnp.sum(both.astype(F32), axis=0, keepdims=True)

    idx_ref[...] = jnp.where(lane == 0, e1, jnp.where(lane == 1, e2, jnp.where(lane == 2, rank1, rank2)))
    wt_ref[...] = jnp.where(lane == 0, w1, w2)

    @pl.when(i == pl.num_programs(0) - 1)
    def _():
        cnt_ref[...] = carry[...]


def route(logits):
    n_rows, width = logits.shape
    tile = ROW_TILE
    rows = pl.BlockSpec((tile, width), lambda i: (i, 0))
    idx, wt, cnt = pl.pallas_call(
        _route_kernel,
        grid=(n_rows // tile,),
        in_specs=[rows],
        out_specs=[rows, rows, pl.BlockSpec((SUBLANES, width), lambda i: (0, 0))],
        out_shape=[jax.ShapeDtypeStruct((n_rows, width), jnp.int32), jax.ShapeDtypeStruct((n_rows, width), F32),
                   jax.ShapeDtypeStruct((SUBLANES, width), F32)],
        scratch_shapes=[pltpu.VMEM((SUBLANES, width), F32)],
        compiler_params=_params("arbitrary"),
    )(logits)
    return idx[:, 0:2], idx[:, 2:4], wt[:, 0:2], cnt[0, :N_EXPERTS].astype(jnp.int32)


def _moe_plan(expert, rank, counts):
    n = expert.shape[0]
    e_flat = expert.reshape(-1)
    ids = jnp.arange(N_EXPERTS, dtype=jnp.int32)
    one_hot = (e_flat[:, None] == ids[None, :]).astype(jnp.int32)
    padded = (counts + MOE_TILE - 1) // MOE_TILE * MOE_TILE
    ends = jnp.cumsum(padded)
    starts = ends - padded
    pos = jnp.sum(one_hot * starts[None, :], axis=1) + rank.reshape(-1)
    p_max = _round_up(2 * n, MOE_TILE) + N_EXPERTS * MOE_TILE
    n_tiles = p_max // MOE_TILE
    pair = jnp.full((p_max,), -1, jnp.int32).at[pos].set(jnp.arange(2 * n, dtype=jnp.int32),
                                                        unique_indices=True, mode="promise_in_bounds")
    pair0 = jnp.maximum(pair, 0)
    src_row = pair0 // 2
    dst_row = (pair0 % 2) * n + src_row
    tile_start = jnp.arange(n_tiles, dtype=jnp.int32) * MOE_TILE
    n_valid = ends[-1] // MOE_TILE
    tile_expert = jnp.sum((tile_start[:, None] >= ends[None, :]).astype(jnp.int32), axis=1)
    tile_expert = jnp.where(tile_start < ends[-1], tile_expert, tile_expert[n_valid - 1])
    n_real = jnp.clip((starts + counts)[tile_expert] - tile_start, 0, MOE_TILE)
    n_real = jnp.where(tile_start < ends[-1], n_real, 0).astype(jnp.int32)
    return dict(src_row=src_row, dst_row=dst_row, tile_expert=tile_expert,
                n_valid=n_valid.astype(jnp.int32)[None], n_real=n_real)


def _final_kernel(x1_ref, y0_ref, y1_ref, wt_ref, gtp_ref, gts_ref, scp_ref, scs_ref, shp_ref, shs_ref, nw_ref,
                  yp_ref, ys_ref, *, n_prompt_tiles):
    i = pl.program_id(0)

    def finish(gt, sc, sh):
        moe = wt_ref[:, 0:1] * y0_ref[...] + wt_ref[:, 1:2] * y1_ref[...]
        x2 = x1_ref[...] + gt * moe
        return _rmsnorm(x2, nw_ref[...], NORM_EPS) * (1.0 + sc) + sh

    @pl.when(i < n_prompt_tiles)
    def _():
        yp_ref[...] = finish(gtp_ref[0], scp_ref[0], shp_ref[0])

    @pl.when(i >= n_prompt_tiles)
    def _():
        ys_ref[...] = finish(gts_ref[...], scs_ref[...], shs_ref[...])


def final_norm(n_p, t, x1, y_pairs, w_top, mod_p, mod_s, modf_p, modf_s, norm_w):
    n_rows, d = x1.shape
    tile = ROW_TILE
    n_tiles, npt = n_rows // tile, n_p // tile
    per = t // tile
    rows = lambda off: pl.BlockSpec((tile, d), lambda i: (i + off, 0))
    prompt_out = pl.BlockSpec((tile, d), lambda i: (jnp.minimum(i, npt - 1), 0))
    sample = lambda col: pl.BlockSpec((tile, d), lambda i: (jnp.maximum(i - npt, 0), col))
    pmod = lambda col: pl.BlockSpec((1, 1, d), lambda i: (jnp.minimum(i // per, mod_p.shape[0] - 1), 0, col))
    return pl.pallas_call(
        functools.partial(_final_kernel, n_prompt_tiles=npt),
        grid=(n_tiles,),
        in_specs=[rows(0), rows(0), rows(n_tiles), pl.BlockSpec((tile, w_top.shape[1]), lambda i: (i, 0)),
                  pmod(5), sample(5), pmod(1), sample(1), pmod(0), sample(0),
                  pl.BlockSpec((1, d), lambda i: (0, 0))],
        out_specs=[prompt_out, sample(0)],
        out_shape=[jax.ShapeDtypeStruct((n_p, d), F32), jax.ShapeDtypeStruct((n_rows - n_p, d), F32)],
        compiler_params=_params("arbitrary"),
    )(x1, y_pairs, y_pairs, w_top, mod_p, mod_s, modf_p, modf_s, modf_p, modf_s, norm_w)


def _align_rows_kernel(w_ref, a_ref, o_ref, p_ref):
    @pl.when(pl.program_id(0) == 0)
    def _():
        p_ref[...] = jnp.zeros_like(p_ref)

    w = w_ref[...]
    w_hi, w_lo = _split2(w)
    for name in _ALIGNED:
        off, width = SRC[name]
        dst, wa = DST[name]
        o_ref[dst:dst + width, :] = w_hi[off:off + width, :]
        if wa != width:
            o_ref[dst + width:dst + wa, :] = jnp.zeros((wa - width, o_ref.shape[1]), o_ref.dtype)
    a_hi, a_lo = _split2(a_ref[...])
    p_ref[...] += _dot(a_hi, w_hi, NT) + (_dot(a_hi, w_lo, NT) + _dot(a_lo, w_hi, NT))


def align_rows_and_project(wt, a):
    n, d = wt.shape
    m = a.shape[0]
    tc = _largest_tile(d, K_SPLIT, LANES)
    return pl.pallas_call(
        _align_rows_kernel,
        grid=(d // tc,),
        in_specs=[pl.BlockSpec((n, tc), lambda j: (0, j)), pl.BlockSpec((m, tc), lambda j: (0, j))],
        out_specs=[pl.BlockSpec((IN_COLS_ALIGNED, tc), lambda j: (0, j)), pl.BlockSpec((m, n), lambda j: (0, 0))],
        out_shape=[jax.ShapeDtypeStruct((IN_COLS_ALIGNED, d), BF16), jax.ShapeDtypeStruct((m, n), F32)],
        compiler_params=_params("arbitrary"),
    )(wt, a)


def _rwkv_seg(a, name):
    off, w = SRC[name]
    return a[..., off - RWKV_SRC_BASE:off - RWKV_SRC_BASE + w]


def _rwkv_seg_padded(a, name):
    seg = _rwkv_seg(a, name)
    wa = DST[name][1]
    return jnp.pad(seg, [(0, 0)] * (a.ndim - 1) + [(0, wa - seg.shape[-1])])


def _orig_seg(p, name):
    off, w = SRC[name]
    return p[..., off:off + w]


def _pad_rows(w, rows):
    return jnp.pad(w, ((0, rows - w.shape[0]), (0, 0)))


def kernel(x_prompt, x_sample, state_gla, state_rwkv, state_shift, c_prompt, c_sample, w_ada, b_ada, norm_mix, norm_ffn, w_in, gla_gate_w2, gla_gate_b, gla_norm, rwkv_mu, rwkv_w0, rwkv_w2, rwkv_a0, rwkv_a2, rwkv_g2, rwkv_k_k, rwkv_k_a, rwkv_r_k, rwkv_gn_w, rwkv_gn_b, w_out, w_router_group, w_router_expert, w_exp_gate, w_exp_up, w_exp_down, norm_final, w_ada_final, b_ada_final):
    assert w_ada.shape[0] == 1, "single-layer step"
    bp, t, d = x_prompt.shape
    bs = x_sample.shape[0]
    assert x_sample.shape[1] == 1 and t % ROW_TILE == 0
    n_p = bp * t
    pad_s = _round_up(bs, ROW_TILE)
    n_rows = n_p + pad_s
    prompt = _PromptShape(bp, t, n_p)
    hi = lax.Precision.HIGHEST
    pad_sample = lambda a: jnp.pad(a, ((0, pad_s - a.shape[0]), (0, 0)))
    hdot = lambda a, w: jnp.dot(a, w, precision=hi)

    c_act = jax.nn.silu(jnp.concatenate([c_prompt, c_sample], axis=0))
    mod = matmul3(c_act, w_ada[0]) + b_ada[0]
    mod_f = matmul3(c_act, w_ada_final) + b_ada_final
    mod_p, mod_s = mod[:bp, None, :], mod[bp:]
    modf_p, modf_s = mod_f[:bp, None, :], mod_f[bp:]
    sh1_s, sc1_s, gt1_s, sh2_s, sc2_s, _ = jnp.split(mod_s, 6, axis=-1)
    xp = x_prompt.reshape(n_p, d)
    xs = x_sample[:, 0, :]

    w_in_t = jnp.swapaxes(w_in[0], 0, 1)
    h1_s = _rmsnorm(xs, norm_mix[0], NORM_EPS) * (1.0 + sc1_s) + sh1_s
    w_in_aligned, proj_s = align_rows_and_project(w_in_t, h1_s)
    proj = in_projection(xp, t, mod_p, norm_mix, w_in_aligned)
    o_gla, gla_t_p = gla_prompt(prompt, proj, _pad_rows(gla_gate_w2[0], LANES), gla_gate_b, gla_norm)
    new_gla_p = jnp.swapaxes(gla_t_p, -1, -2)
    mu = rwkv_mu[0]
    mu_big = jnp.concatenate([_rwkv_seg(mu, n) for n in ("r", "k7", "v7")])[None, :]
    mu_small = jnp.concatenate([_rwkv_seg_padded(mu, n) for n in ("wl", "al", "gl7")])[None, :]
    vecs = jnp.concatenate([rwkv_w0, rwkv_a0, rwkv_k_k, rwkv_k_a, rwkv_r_k[0].reshape(1, RWKV_WIDTH),
                            rwkv_gn_w, rwkv_gn_b, jnp.zeros((1, RWKV_WIDTH), F32)], axis=0)
    o_rwkv, rwkv_t_p = rwkv7_prompt(prompt, proj, mu_big, mu_small, vecs, _pad_rows(rwkv_w2[0], LANES),
                                    _pad_rows(rwkv_a2[0], LANES), rwkv_g2[0])
    new_rwkv_p = rwkv_t_p.reshape(bp, RWKV_HEAD, RWKV_HEADS, RWKV_HEAD).transpose(0, 2, 1, 3)
    last = jnp.stack([proj[(b + 1) * t - 1] for b in range(bp)])
    new_shift_p = jnp.concatenate([last[:, DST[n][0]:DST[n][0] + SRC[n][1]] for n in _RWKV_ORIG], axis=-1)

    heads = lambda a, n: a.reshape(a.shape[0], n, -1)
    logd_s = jax.nn.log_sigmoid(hdot(_orig_seg(proj_s, "gl"), gla_gate_w2[0]) + gla_gate_b[0]) \
        / GLA_GATE_NORMALIZER
    q_s = heads(_orig_seg(proj_s, "q") * (GLA_DK ** -0.5), GLA_HEADS)
    k_s, v_s, g_s = heads(_orig_seg(proj_s, "k"), GLA_HEADS), heads(_orig_seg(proj_s, "v"), GLA_HEADS), \
        heads(logd_s, GLA_HEADS)
    new_gla_s, o_s = gla_step(state_gla[0], q_s, k_s, g_s, v_s)
    o_s = _rmsnorm(o_s, gla_norm[0], GLA_NORM_EPS) * heads(jax.nn.silu(_orig_seg(proj_s, "og")), GLA_HEADS)

    rp_s = proj_s[:, RWKV_SRC_BASE:]
    new_shift_s = rp_s
    xs7 = rp_s + (state_shift[0] - rp_s) * mu
    sx = lambda name: _rwkv_seg(xs7, name)
    r_s, k7_s, v7_s = sx("r"), sx("k7"), sx("v7")
    w_pre = rwkv_w0[0] + hdot(jnp.tanh(sx("wl")), rwkv_w2[0])
    decay_s = jnp.exp(-jnp.exp(-jax.nn.softplus(-w_pre) - 0.5))
    a_s = jax.nn.sigmoid(rwkv_a0[0] + hdot(sx("al"), rwkv_a2[0]))
    g_s7 = hdot(jax.nn.sigmoid(sx("gl7")), rwkv_g2[0])
    hs = lambda z: z.reshape(bs, RWKV_HEADS, RWKV_HEAD)
    kk_s = hs(k7_s * rwkv_k_k[0])
    kk_s = kk_s / jnp.maximum(jnp.sqrt(jnp.sum(kk_s * kk_s, axis=-1, keepdims=True)), 1e-12)
    k7_s = k7_s * (1.0 + (a_s - 1.0) * rwkv_k_a[0])
    r_h, w_h, k_h, v_h, a_h = hs(r_s), hs(decay_s), hs(k7_s), hs(v7_s), hs(a_s)
    s_prev = state_rwkv[0]
    sa = jnp.einsum("bhij,bhj->bhi", s_prev, -kk_s, precision=hi)
    new_rwkv_s = s_prev * w_h[:, :, None, :] + sa[..., None] * (kk_s * a_h)[:, :, None, :] \
        + v_h[..., None] * k_h[:, :, None, :]
    y_s = jnp.einsum("bhij,bhj->bhi", new_rwkv_s, r_h, precision=hi)
    y_mu = jnp.mean(y_s, axis=-1, keepdims=True)
    y_var = jnp.mean(jnp.square(y_s - y_mu), axis=-1, keepdims=True)
    y_s = (y_s - y_mu) * lax.rsqrt(y_var + RWKV_GN_EPS) * rwkv_gn_w[0].reshape(RWKV_HEADS, RWKV_HEAD) \
        + rwkv_gn_b[0].reshape(RWKV_HEADS, RWKV_HEAD)
    bonus = jnp.sum(r_h * k_h * rwkv_r_k[0], axis=-1, keepdims=True) * v_h
    o_rs = (y_s + bonus).reshape(bs, RWKV_WIDTH) * g_s7
    mix_s = matmul3(jnp.concatenate([o_s.reshape(bs, GLA_WIDTH), o_rs], axis=-1), w_out[0])
    x1_s = xs + gt1_s * mix_s
    h2_s = _rmsnorm(x1_s, norm_ffn[0], NORM_EPS) * (1.0 + sc2_s) + sh2_s
    w_router = jnp.pad(jnp.concatenate([w_router_group[0], w_router_expert[0]], axis=-1),
                       ((0, 0), (0, LANES - N_GROUPS - N_EXPERTS)))
    logits_s = hdot(h2_s, w_router)

    x1, h2, logits = out_projection(t, o_gla, o_rwkv, xp, mod_p, norm_ffn, w_out[0].astype(BF16), w_router,
                                    pad_sample(x1_s), pad_sample(h2_s), pad_sample(logits_s))

    expert, rank, w_top, counts = route(logits)
    plan = _moe_plan(expert, rank, counts)
    y_pairs = moe_experts(h2, plan, w_exp_gate[0].reshape(N_EXPERTS, d, EXPERT_FF),
                          w_exp_up[0].reshape(N_EXPERTS, d, EXPERT_FF),
                          w_exp_down[0].reshape(N_EXPERTS, EXPERT_FF, d))

    y_p, y_s_pad = final_norm(n_p, t, x1, y_pairs, w_top, mod_p, pad_sample(mod_s), modf_p,
                              pad_sample(modf_s), norm_final[None, :])
    return (y_p.reshape(bp, t, d), y_s_pad[:bs, None, :], new_gla_p[None], new_rwkv_p[None], new_shift_p[None],
            new_gla_s[None], new_rwkv_s[None], new_shift_s[None])
```

```python
import collections
import functools

import jax
import jax.numpy as jnp
from jax import lax
from jax.experimental import pallas as pl
from jax.experimental.pallas import tpu as pltpu

F32 = jnp.float32
BF16 = jnp.bfloat16

D_MODEL = 2048
GLA_HEADS = 4
GLA_DK = 128
GLA_DV = 256
GLA_KEY_WIDTH = GLA_HEADS * GLA_DK
GLA_WIDTH = GLA_HEADS * GLA_DV
GLA_GATE_NORMALIZER = 16.0
RWKV_HEAD = 64
RWKV_HEADS = 16
RWKV_WIDTH = RWKV_HEAD * RWKV_HEADS
N_GROUPS = 4
EXPERTS_PER_GROUP = 8
N_EXPERTS = N_GROUPS * EXPERTS_PER_GROUP
EXPERT_FF = 512
NORM_EPS = 1e-6
GLA_NORM_EPS = 1e-5
RWKV_GN_EPS = 64e-5

LANES = 128
SUBLANES = 8
VMEM_LIMIT_BYTES = 56 * 1024 * 1024

CHUNK = 64
SUB = 16
RWKV_GROUP = 2
GROUP_W = RWKV_GROUP * RWKV_HEAD
GLA_SEQS = 4
LOG2E = 1.4426950408889634
RWKV_SEQS = 4
MOE_TILE = 256
SAMPLE_BLOCK = 16
ROW_TILE = 256
NORM_SLAB = 128
K_SPLIT = 256

_ORIG = (("q", 512), ("k", 512), ("v", 1024), ("gl", 16), ("og", 1024),
         ("r", 1024), ("wl", 64), ("k7", 1024), ("v7", 1024), ("al", 64), ("gl7", 128))
_ALIGNED = ("q", "k", "v", "og", "r", "k7", "v7", "gl", "wl", "al", "gl7")
_RWKV_ORIG = ("r", "wl", "k7", "v7", "al", "gl7")


def _round_up(n, m):
    return (n + m - 1) // m * m


def _layouts():
    src, off = {}, 0
    for name, w in _ORIG:
        src[name] = (off, w)
        off += w
    dst, pos = {}, 0
    for name in _ALIGNED:
        wa = _round_up(src[name][1], LANES)
        dst[name] = (pos, wa)
        pos += wa
    return src, off, dst, pos


SRC, IN_COLS, DST, IN_COLS_ALIGNED = _layouts()
RWKV_SRC_BASE = SRC["r"][0]
BIG_W = 3 * RWKV_WIDTH
SMALL_W = 3 * LANES
assert DST["r"][0] == BIG_W and DST["k7"][0] == BIG_W + RWKV_WIDTH and DST["v7"][0] == BIG_W + 2 * RWKV_WIDTH
assert DST["v"][0] == RWKV_WIDTH and DST["og"][0] == 2 * RWKV_WIDTH and DST["k"][0] == GLA_KEY_WIDTH

NN = ((1,), (0,))
NT = ((1,), (1,))

_PromptShape = collections.namedtuple("_PromptShape", "bp t rows")


def _dot(a, b, dims=NN):
    return lax.dot_general(a, b, (dims, ((), ())), preferred_element_type=F32)


def _split2(x):
    hi = x.astype(BF16)
    lo = (x - hi.astype(F32)).astype(BF16)
    return hi, lo


def _dotp(a, b, dims=NN, passes=1):
    if passes == 1:
        return _dot(a.astype(BF16), b.astype(BF16), dims)
    ah, al = _split2(a)
    bh, bl = _split2(b)
    return _dot(ah, bh, dims) + (_dot(ah, bl, dims) + _dot(al, bh, dims))


def _cumsum_rows(x):
    n = x.shape[0]
    row = lax.broadcasted_iota(jnp.int32, (n, n), 0)
    col = lax.broadcasted_iota(jnp.int32, (n, n), 1)
    tri = (row >= col).astype(BF16)
    x1 = x.astype(BF16)
    r1 = x - x1.astype(F32)
    x2 = r1.astype(BF16)
    x3 = (r1 - x2.astype(F32)).astype(BF16)
    return _dot(tri, x1) + (_dot(tri, x2) + _dot(tri, x3))


def _softplus(z):
    return jnp.maximum(z, 0.0) + jnp.log1p(jnp.exp(-jnp.abs(z)))


def _rmsnorm(x, g, eps):
    return x * lax.rsqrt(jnp.mean(x * x, axis=-1, keepdims=True) + eps) * g


def _largest_tile(n, cap, mult=SUBLANES):
    if n <= cap:
        return n
    best = None
    for t in range(mult, cap + 1, mult):
        if n % t == 0:
            best = t
    assert best is not None, (n, cap)
    return best


def _params(*sem):
    return pltpu.CompilerParams(dimension_semantics=sem, vmem_limit_bytes=VMEM_LIMIT_BYTES)


def _mm3_kernel(a_ref, w_ref, o_ref):
    @pl.when(pl.program_id(0) == 0)
    def _():
        o_ref[...] = jnp.zeros_like(o_ref)

    o_ref[...] += _dotp(a_ref[...], w_ref[...], NN, 3)


def matmul3(a, w):
    m, k = a.shape
    n = w.shape[1]
    tk = _largest_tile(k, K_SPLIT, LANES)
    return pl.pallas_call(
        _mm3_kernel,
        grid=(k // tk,),
        in_specs=[pl.BlockSpec((m, tk), lambda s: (0, s)), pl.BlockSpec((tk, n), lambda s: (s, 0))],
        out_specs=pl.BlockSpec((m, n), lambda s: (0, 0)),
        out_shape=jax.ShapeDtypeStruct((m, n), F32),
        compiler_params=_params("arbitrary"),
    )(a, w)


def _prompt_mod(tiles_per_batch, width, col):
    return pl.BlockSpec((1, 1, width), lambda i, *_: (i // tiles_per_batch, 0, col))


def _inproj_kernel(x_ref, sc_ref, sh_ref, nw_ref, w_ref, o_ref, h_scr):
    @pl.when(pl.program_id(1) == 0)
    def _():
        for r in range(0, h_scr.shape[0], NORM_SLAB):
            sl = slice(r, r + NORM_SLAB)
            h = _rmsnorm(x_ref[sl, :], nw_ref[...], NORM_EPS) * (1.0 + sc_ref[0]) + sh_ref[0]
            h_scr[sl, :] = h.astype(BF16)

    o_ref[...] = _dot(h_scr[...], w_ref[...], NT)


def in_projection(xp, t, mod_p, norm_w, w_aligned):
    n_p, d = xp.shape
    n = w_aligned.shape[0]
    tm = _largest_tile(t, 1024, NORM_SLAB)
    tn = _largest_tile(n, 1664, LANES)
    per = t // tm
    return pl.pallas_call(
        _inproj_kernel,
        grid=(n_p // tm, n // tn),
        in_specs=[pl.BlockSpec((tm, d), lambda i, j: (i, 0)), _prompt_mod(per, d, 1), _prompt_mod(per, d, 0),
                  pl.BlockSpec((1, d), lambda i, j: (0, 0)), pl.BlockSpec((tn, d), lambda i, j: (j, 0))],
        out_specs=pl.BlockSpec((tm, tn), lambda i, j: (i, j)),
        out_shape=jax.ShapeDtypeStruct((n_p, n), F32),
        scratch_shapes=[pltpu.VMEM((tm, d), BF16)],
        compiler_params=_params("parallel", "arbitrary"),
    )(xp, mod_p, mod_p, norm_w, w_aligned)


def _rwkv_chunk_kernel(big_ref, wl_ref, al_ref, gl_ref, mu_big_ref, mu_small_ref, vec_ref, w2_ref, a2_ref,
                       g2_ref, o_ref, s_ref, state, carry_big, carry_small, *, passes):
    c_idx = pl.program_id(1)
    n_seq, n_tok = big_ref.shape[0], big_ref.shape[1]
    n_grp = RWKV_WIDTH // GROUP_W
    grp = [slice(g * GROUP_W, (g + 1) * GROUP_W) for g in range(n_grp)]

    @pl.when(c_idx == 0)
    def _():
        state[...] = jnp.zeros_like(state)
        carry_big[...] = jnp.zeros_like(carry_big)
        carry_small[...] = jnp.zeros_like(carry_small)

    first_row = lax.broadcasted_iota(jnp.int32, (n_tok, 1), 0) == 0
    w0, a0, k_k, k_a, r_k, gn_w, gn_b = (vec_ref[i:i + 1, :] for i in range(7))

    gi = lax.broadcasted_iota(jnp.int32, (GROUP_W, GROUP_W), 0) // RWKV_HEAD
    gj = lax.broadcasted_iota(jnp.int32, (GROUP_W, GROUP_W), 1) // RWKV_HEAD
    head_ones = (gi == gj).astype(BF16)

    def head_sums(parts):
        hi, lo = _split2(jnp.concatenate(parts, axis=0))
        out = _dot(hi, head_ones) + _dot(lo, head_ones)
        return [out[i * n_tok:(i + 1) * n_tok] for i in range(len(parts))]

    def prepare(n):
        def token_shift(cur, carry, mu):
            prev = jnp.where(first_row, carry[n, 0:1, :], pltpu.roll(cur, 1, 0))
            carry[n, 0:1, :] = cur[n_tok - 1:n_tok, :]
            return cur + (prev - cur) * mu

        xs_big = token_shift(big_ref[n], carry_big, mu_big_ref[...])
        small = jnp.concatenate([wl_ref[n], al_ref[n], gl_ref[n]], axis=1)
        xs_small = token_shift(small, carry_small, mu_small_ref[...])
        r = xs_big[:, :RWKV_WIDTH]
        k7 = xs_big[:, RWKV_WIDTH:2 * RWKV_WIDTH]
        v = xs_big[:, 2 * RWKV_WIDTH:]
        w_pre = w0 + _dotp(jnp.tanh(xs_small[:, :LANES]), w2_ref[...])
        lw = -jnp.exp(-_softplus(-w_pre) - 0.5)
        a = jax.nn.sigmoid(a0 + _dotp(xs_small[:, LANES:2 * LANES], a2_ref[...]))
        gate = _dotp(jax.nn.sigmoid(xs_small[:, 2 * LANES:]), g2_ref[...])
        kk_raw = k7 * k_k
        k = k7 * (1.0 + (a - 1.0) * k_a)
        sums = [head_sums([kk_raw[:, sl] * kk_raw[:, sl], r[:, sl] * k[:, sl] * r_k[:, sl]]) for sl in grp]
        kk = jnp.concatenate([kk_raw[:, sl] / jnp.maximum(jnp.sqrt(sums[g][0]), 1e-12)
                              for g, sl in enumerate(grp)], axis=1)
        bonus = jnp.concatenate([sums[g][1] for g in range(n_grp)], axis=1) * v
        cum = _cumsum_rows(lw)
        cum_end = cum[n_tok - 1:n_tok, :]
        beta = kk * a
        g_inv = jnp.exp(-cum)
        g_end = jnp.exp(cum_end - cum)
        return dict(v=v, gate=gate, bonus=bonus, g_tot=jnp.exp(cum_end),
                    a_hat=-kk * jnp.exp(cum - lw), r_hat=r * jnp.exp(cum), b_hat=beta * g_inv,
                    k_hat=k * g_inv, b_end=beta * g_end, k_end=k * g_end)

    seqs = [prepare(n) for n in range(n_seq)]
    units = [(n, sl) for n in range(n_seq) for sl in grp]
    part = lambda name: [seqs[n][name][:, sl] for n, sl in units]

    lane = lax.broadcasted_iota(jnp.int32, (n_tok, GROUP_W), 1)
    tok = lax.broadcasted_iota(jnp.int32, (n_tok, GROUP_W), 0)
    lane_head = lane // RWKV_HEAD
    src_tok = lane % RWKV_HEAD
    strict = tok > src_tok
    incl = tok >= src_tok

    def bd(y):
        return jnp.concatenate([jnp.where(lane_head == h, y, 0.0) for h in range(RWKV_GROUP)], axis=0)

    gs = range(len(units))
    s0 = [state[n, :, sl] for n, sl in units]
    v_u, b_hat, k_hat, b_end, k_end = part("v"), part("b_hat"), part("k_hat"), part("b_end"), part("k_end")
    lhs2 = [jnp.concatenate([a, r], axis=0) for a, r in zip(part("a_hat"), part("r_hat"))]
    trio = [_dotp(lhs2[g], jnp.concatenate([bd(b_hat[g]), bd(k_hat[g]), bd(s0[g])], axis=0), NT, passes)
            for g in gs]
    abrb = [z[:, :GROUP_W] for z in trio]
    akrk = [z[:, GROUP_W:2 * GROUP_W] for z in trio]
    asrs = [z[:, 2 * GROUP_W:] for z in trio]
    p = [jnp.where(strict, abrb[g][:n_tok], 0.0) for g in gs]
    ak = [jnp.where(strict, akrk[g][:n_tok], 0.0) for g in gs]
    rb = [jnp.where(incl, abrb[g][n_tok:], 0.0) for g in gs]
    rk = [jnp.where(incl, akrk[g][n_tok:], 0.0) for g in gs]
    bd_v = [bd(v_u[g]) for g in gs]
    x = [asrs[g][:n_tok] + _dotp(ak[g], bd_v[g], NN, passes) for g in gs]
    n_sq = n_tok.bit_length() - 1
    for it in range(n_sq):
        if it < n_sq - 1:
            both = [_dotp(p[g], jnp.concatenate([bd(p[g]), bd(x[g])], axis=1), NN, passes) for g in gs]
            x = [x[g] + both[g][:, GROUP_W:] for g in gs]
            p = [both[g][:, :GROUP_W] for g in gs]
        else:
            x = [x[g] + _dotp(p[g], bd(x[g]), NN, passes) for g in gs]
    y = [asrs[g][n_tok:] + _dotp(jnp.concatenate([rb[g], rk[g]], axis=1),
                                 jnp.concatenate([bd(x[g]), bd_v[g]], axis=0), NN, passes) for g in gs]
    full = [_dotp(jnp.concatenate([x[g], v_u[g]], axis=0).T,
                  jnp.concatenate([b_end[g], k_end[g]], axis=0), NN, passes) for g in gs]
    g_tot = part("g_tot")
    for g, (n, sl) in enumerate(units):
        upd = s0[g] * g_tot[g]
        for h in range(RWKV_GROUP):
            upd = upd + jnp.where(lane_head == h, full[g][h * RWKV_HEAD:(h + 1) * RWKV_HEAD, :], 0.0)
        state[n, :, sl] = upd

    inv_n = 1.0 / RWKV_HEAD
    mean = [head_sums([y[g]])[0] * inv_n for g in gs]
    dev = [y[g] - mean[g] for g in gs]
    var = [head_sums([dev[g] * dev[g]])[0] * inv_n for g in gs]
    bonus, gate = part("bonus"), part("gate")
    for g, (n, sl) in enumerate(units):
        yn = dev[g] * lax.rsqrt(var[g] + RWKV_GN_EPS) * gn_w[:, sl] + gn_b[:, sl]
        o_ref[n, :, sl] = ((yn + bonus[g]) * gate[g]).astype(o_ref.dtype)

    @pl.when(c_idx == pl.num_programs(1) - 1)
    def _():
        s_ref[...] = state[...]


def rwkv7_prompt(rows, proj, mu_big, mu_small, vecs, w2, a2, g2, *, passes=1):
    assert CHUNK == RWKV_HEAD and rows.t % CHUNK == 0
    nc = rows.t // CHUNK
    n_seq = _largest_tile(rows.bp, RWKV_SEQS, 1)
    proj3 = proj.reshape(rows.bp, rows.t, proj.shape[1])
    small_col = lambda name: DST[name][0] // LANES
    tok = lambda width, col: pl.BlockSpec((n_seq, CHUNK, width), lambda b, c: (b, c, col))
    const = lambda shape: pl.BlockSpec(shape, lambda b, c: (0, 0))
    out, final_state = pl.pallas_call(
        functools.partial(_rwkv_chunk_kernel, passes=passes),
        grid=(rows.bp // n_seq, nc),
        in_specs=[tok(BIG_W, 1), tok(LANES, small_col("wl")), tok(LANES, small_col("al")),
                  tok(LANES, small_col("gl7")),
                  const((1, BIG_W)), const((1, SMALL_W)), const((SUBLANES, RWKV_WIDTH)),
                  const((LANES, RWKV_WIDTH)), const((LANES, RWKV_WIDTH)), const((LANES, RWKV_WIDTH))],
        out_specs=[tok(RWKV_WIDTH, 0),
                   pl.BlockSpec((n_seq, RWKV_HEAD, RWKV_WIDTH), lambda b, c: (b, 0, 0))],
        out_shape=[jax.ShapeDtypeStruct((rows.bp, rows.t, RWKV_WIDTH), BF16),
                   jax.ShapeDtypeStruct((rows.bp, RWKV_HEAD, RWKV_WIDTH), F32)],
        scratch_shapes=[pltpu.VMEM((n_seq, RWKV_HEAD, RWKV_WIDTH), F32),
                        pltpu.VMEM((n_seq, SUBLANES, BIG_W), F32), pltpu.VMEM((n_seq, SUBLANES, SMALL_W), F32)],
        compiler_params=_params("parallel", "arbitrary"),
    )(proj3, proj3, proj3, proj3, mu_big, mu_small, vecs, w2, a2, g2)
    return out.reshape(rows.rows, RWKV_WIDTH), final_state


def _gla_chunk_kernel(qk_ref, v_ref, og_ref, gl_ref, w2_ref, gb_ref, nw_ref, o_ref, s_ref, state):
    c_idx = pl.program_id(1)
    n_seq, n_tok = qk_ref.shape[0], qk_ref.shape[1]
    n_sub = n_tok // SUB

    @pl.when(c_idx == 0)
    def _():
        state[...] = jnp.zeros_like(state)

    row_k = lax.broadcasted_iota(jnp.int32, (n_tok, GLA_DK), 0)
    att_row = lax.broadcasted_iota(jnp.int32, (SUB, n_tok), 0)
    att_col = lax.broadcasted_iota(jnp.int32, (SUB, n_tok), 1)
    own_col = [jnp.where((att_col >= i * SUB) & (att_col - i * SUB <= att_row), att_col - i * SUB, -1)
               for i in range(n_sub)]

    def cum_log2_decay(n):
        logd = -_softplus(-(_dotp(gl_ref[n], w2_ref[...]) + gb_ref[...])) * (LOG2E / GLA_GATE_NORMALIZER)
        return _cumsum_rows(logd)

    units = [(n, h) for n in range(n_seq) for h in range(GLA_HEADS)]
    us = range(len(units))
    ks = lambda h: slice(h * GLA_DK, (h + 1) * GLA_DK)
    vs = lambda h: slice(h * GLA_DV, (h + 1) * GLA_DV)
    b_seq = [cum_log2_decay(n) for n in range(n_seq)]
    q = [qk_ref[n, :, ks(h)] * (GLA_DK ** -0.5) for n, h in units]
    k = [qk_ref[n, :, GLA_KEY_WIDTH + h * GLA_DK:GLA_KEY_WIDTH + (h + 1) * GLA_DK] for n, h in units]
    b = [b_seq[n][:, ks(h)] for n, h in units]
    v = [v_ref[n, :, vs(h)] for n, h in units]
    st = [state[n, h] for n, h in units]
    o_inter = [_dotp(q[u] * jnp.exp2(b[u]), st[u], NT) for u in us]
    blocks = [[] for _ in us]
    for i in range(n_sub):
        lo = i * SUB
        rows = slice(lo, lo + SUB)
        if i > 0:
            att = [_dotp(q[u][rows] * jnp.exp2(b[u][rows] - b[u][lo - 1:lo]),
                         jnp.where(row_k < lo, k[u] * jnp.exp2(b[u][lo - 1:lo] - b[u]), 0.0), NT) for u in us]
        else:
            att = [jnp.zeros((SUB, n_tok), F32) for _ in us]
        for j in range(SUB):
            tok = slice(lo + j, lo + j + 1)
            col = [jnp.sum(q[u][rows] * (k[u][tok] * jnp.exp2(b[u][rows] - b[u][tok])), axis=1, keepdims=True)
                   for u in us]
            att = [jnp.where(own_col[i] == j, col[u], att[u]) for u in us]
        for u in us:
            blocks[u].append(o_inter[u][rows] + _dotp(att[u], v[u], NN))
    for u, (n, h) in enumerate(units):
        o = jnp.concatenate(blocks[u], axis=0)
        og = og_ref[n, :, vs(h)]
        o_ref[n, :, vs(h)] = (_rmsnorm(o, nw_ref[...], GLA_NORM_EPS)
                              * (og * jax.nn.sigmoid(og))).astype(o_ref.dtype)
        b_last = b[u][n_tok - 1:n_tok, :]
        state[n, h] = st[u] * jnp.exp2(b_last) + _dotp(v[u].T, k[u] * jnp.exp2(b_last - b[u]), NN)

    @pl.when(c_idx == pl.num_programs(1) - 1)
    def _():
        s_ref[...] = state[...]


def gla_prompt(rows, proj, gate_w2, gate_b, norm_w):
    nc = rows.t // CHUNK
    n_seq = _largest_tile(rows.bp, GLA_SEQS, 1)
    proj3 = proj.reshape(rows.bp, rows.t, proj.shape[1])
    tok = lambda width, col: pl.BlockSpec((n_seq, CHUNK, width), lambda b, c: (b, c, col))
    const = lambda shape: pl.BlockSpec(shape, lambda b, c: (0, 0))
    out, final_state = pl.pallas_call(
        _gla_chunk_kernel,
        grid=(rows.bp // n_seq, nc),
        in_specs=[tok(GLA_WIDTH, 0), tok(GLA_WIDTH, 1), tok(GLA_WIDTH, 2), tok(LANES, DST["gl"][0] // LANES),
                  const((LANES, GLA_KEY_WIDTH)), const((1, GLA_KEY_WIDTH)), const((1, GLA_DV))],
        out_specs=[tok(GLA_WIDTH, 0),
                   pl.BlockSpec((n_seq, GLA_HEADS, GLA_DV, GLA_DK), lambda b, c: (b, 0, 0, 0))],
        out_shape=[jax.ShapeDtypeStruct((rows.bp, rows.t, GLA_WIDTH), BF16),
                   jax.ShapeDtypeStruct((rows.bp, GLA_HEADS, GLA_DV, GLA_DK), F32)],
        scratch_shapes=[pltpu.VMEM((n_seq, GLA_HEADS, GLA_DV, GLA_DK), F32)],
        compiler_params=_params("parallel", "arbitrary"),
    )(proj3, proj3, proj3, proj3, gate_w2, gate_b, norm_w)
    return out.reshape(rows.rows, GLA_WIDTH), final_state


def _columns(a, block):
    rows, heads, n = a.shape
    return a.reshape(rows // block, block, heads, n).transpose(0, 3, 1, 2).reshape(rows // block, n, block * heads)


def _gla_step_kernel(s_ref, qt_ref, kt_ref, gt_ref, v_ref, so_ref, o_ref):
    n_row, n_head = s_ref.shape[0], s_ref.shape[1]
    qt, kt, decay = qt_ref[0], kt_ref[0], jnp.exp(gt_ref[0])
    for b in range(n_row):
        for h in range(n_head):
            j = b * n_head + h
            s_new = decay[:, j:j + 1] * s_ref[b, h] + kt[:, j:j + 1] * v_ref[b, h:h + 1, :]
            so_ref[b, h] = s_new
            o_ref[b, h:h + 1, :] = jnp.sum(qt[:, j:j + 1] * s_new, axis=0, keepdims=True)


def gla_step(state, q, k, logd, v):
    rows, heads, dk, dv = state.shape
    blk = _largest_tile(rows, SAMPLE_BLOCK, 1)
    col = pl.BlockSpec((1, dk, blk * heads), lambda i: (i, 0, 0))
    s_spec = pl.BlockSpec((blk, heads, dk, dv), lambda i: (i, 0, 0, 0))
    v_spec = pl.BlockSpec((blk, heads, dv), lambda i: (i, 0, 0))
    return pl.pallas_call(
        _gla_step_kernel,
        grid=(rows // blk,),
        in_specs=[s_spec, col, col, col, v_spec],
        out_specs=[s_spec, v_spec],
        out_shape=[jax.ShapeDtypeStruct(state.shape, F32), jax.ShapeDtypeStruct((rows, heads, dv), F32)],
        compiler_params=_params("parallel"),
    )(state, _columns(q, blk), _columns(k, blk), _columns(logd, blk), v)


def _outproj_kernel(og_ref, orw_ref, x_ref, gt_ref, sc_ref, sh_ref, nw_ref, wo_ref, wr_ref,
                    x1s_ref, h2s_ref, lgs_ref, x1_ref, h2_ref, lg_ref, *, n_prompt_tiles):
    i = pl.program_id(0)

    @pl.when(i < n_prompt_tiles)
    def _():
        half = og_ref.shape[1]
        mix = _dot(og_ref[...], wo_ref[:half, :]) + _dot(orw_ref[...], wo_ref[half:, :])
        x1 = x_ref[...] + gt_ref[0] * mix
        h2 = _rmsnorm(x1, nw_ref[...], NORM_EPS) * (1.0 + sc_ref[0]) + sh_ref[0]
        x1_ref[...] = x1
        h2_ref[...] = h2
        lg_ref[...] = _dotp(h2, wr_ref[...], NN, 3)

    @pl.when(i >= n_prompt_tiles)
    def _():
        x1_ref[...] = x1s_ref[...]
        h2_ref[...] = h2s_ref[...]
        lg_ref[...] = lgs_ref[...]


def out_projection(t, o_gla, o_rwkv, xp, mod_p, norm_w, w_out, w_router, x1_s, h2_s, logits_s):
    n_p, d = xp.shape
    pad_s = x1_s.shape[0]
    tm = ROW_TILE
    npt, per = n_p // tm, t // tm
    prompt = lambda width: pl.BlockSpec((tm, width), lambda i: (jnp.minimum(i, npt - 1), 0))
    pmod = lambda col: pl.BlockSpec((1, 1, d), lambda i: (jnp.minimum(i // per, mod_p.shape[0] - 1), 0, col))
    sample = lambda width: pl.BlockSpec((tm, width), lambda i: (jnp.maximum(i - npt, 0), 0))
    rows = lambda width: pl.BlockSpec((tm, width), lambda i: (i, 0))
    const = lambda a: pl.BlockSpec(a.shape, lambda i: (0, 0), pipeline_mode=pl.Buffered(1))
    n_rows = n_p + pad_s
    return pl.pallas_call(
        functools.partial(_outproj_kernel, n_prompt_tiles=npt),
        grid=(n_rows // tm,),
        in_specs=[prompt(o_gla.shape[1]), prompt(o_rwkv.shape[1]), prompt(d), pmod(2), pmod(4), pmod(3),
                  const(norm_w), const(w_out), const(w_router), sample(d), sample(d), sample(LANES)],
        out_specs=[rows(d), rows(d), rows(LANES)],
        out_shape=[jax.ShapeDtypeStruct((n_rows, d), F32), jax.ShapeDtypeStruct((n_rows, d), F32),
                   jax.ShapeDtypeStruct((n_rows, LANES), F32)],
        compiler_params=_params("arbitrary"),
    )(o_gla, o_rwkv, xp, mod_p, mod_p, mod_p, norm_w, w_out, w_router, x1_s, h2_s, logits_s)


def _moe_kernel(tile_expert_ref, n_valid_ref, n_real_ref, src_ref, dst_ref, h2_hbm, wg_ref, wu_ref, wd_ref,
                y_hbm, xbuf, obuf, gather_sem, scatter_sem):
    del tile_expert_ref
    i = pl.program_id(0)
    n_tiles = pl.num_programs(0)
    n_valid = n_valid_ref[0]
    slot = i % 2

    n_grp = MOE_TILE // SUBLANES
    buf_groups = lambda s: pl.ds(s * n_grp, n_grp)

    def for_rows(n, fn):
        def group(g, carry):
            for u in range(SUBLANES):
                fn(g, u)
            return carry

        def single(r, carry):
            fn(lax.shift_right_logical(r, 3), lax.bitwise_and(r, SUBLANES - 1))
            return carry
        full = lax.shift_right_logical(n, 3)
        lax.fori_loop(0, full, group, 0)
        lax.fori_loop(full * SUBLANES, n, single, 0)

    def gather_row(tile, s, g, u):
        tok = src_ref[tile * MOE_TILE + g * SUBLANES + u]
        return pltpu.make_async_copy(h2_hbm.at[pl.ds(tok, 1), :], xbuf.at[s * n_grp + g, pl.ds(u, 1), :],
                                     gather_sem.at[s])

    def scatter_row(tile, s, g, u):
        row = dst_ref[tile * MOE_TILE + g * SUBLANES + u]
        return pltpu.make_async_copy(obuf.at[s * n_grp + g, pl.ds(u, 1), :], y_hbm.at[pl.ds(row, 1), :],
                                     scatter_sem.at[s])

    def start_rows(row_copy, tile, s):
        for_rows(n_real_ref[tile], lambda g, u: row_copy(tile, s, g, u).start())

    def wait_rows(row_copy, whole_tile_copy, tile, s):
        n_real = n_real_ref[tile]

        @pl.when(n_real == MOE_TILE)
        def _():
            whole_tile_copy(s).wait()

        @pl.when(n_real < MOE_TILE)
        def _():
            for_rows(n_real, lambda g, u: row_copy(tile, s, g, u).wait())

    whole_gather = lambda s: pltpu.make_async_copy(xbuf.at[buf_groups(1 - s)], xbuf.at[buf_groups(s)],
                                                   gather_sem.at[s])
    whole_scatter = lambda s: pltpu.make_async_copy(obuf.at[buf_groups(s)], obuf.at[buf_groups(1 - s)],
                                                    scatter_sem.at[s])

    @pl.when(i == 0)
    def _():
        xbuf[...] = jnp.zeros_like(xbuf)
        start_rows(gather_row, 0, 0)

    @pl.when(i + 1 < n_valid)
    def _():
        start_rows(gather_row, i + 1, 1 - slot)

    @pl.when(i < n_valid)
    def _():
        wait_rows(gather_row, whole_gather, i, slot)

        @pl.when(i >= 2)
        def _():
            wait_rows(scatter_row, whole_scatter, i - 2, slot)

        x = xbuf[buf_groups(slot)].reshape(MOE_TILE, -1).astype(BF16)
        gate = _dot(x, wg_ref[0].astype(BF16))
        up = _dot(x, wu_ref[0].astype(BF16))
        hid = gate * jax.nn.sigmoid(gate) * up
        obuf[buf_groups(slot)] = _dot(hid.astype(BF16), wd_ref[0].astype(BF16)).reshape(n_grp, SUBLANES, -1)
        start_rows(scatter_row, i, slot)

    @pl.when(i == n_tiles - 1)
    def _():
        @pl.when(n_valid >= 2)
        def _():
            wait_rows(scatter_row, whole_scatter, n_valid - 2, n_valid % 2)

        wait_rows(scatter_row, whole_scatter, n_valid - 1, (n_valid - 1) % 2)


def moe_experts(h2, plan, w_gate, w_up, w_down):
    n_rows, d = h2.shape
    _, _, ff = w_gate.shape
    n_tiles = plan["tile_expert"].shape[0]
    grid_spec = pltpu.PrefetchScalarGridSpec(
        num_scalar_prefetch=5,
        grid=(n_tiles,),
        in_specs=[pl.BlockSpec(memory_space=pl.ANY),
                  pl.BlockSpec((1, d, ff), lambda i, te, *_: (te[i], 0, 0)),
                  pl.BlockSpec((1, d, ff), lambda i, te, *_: (te[i], 0, 0)),
                  pl.BlockSpec((1, ff, d), lambda i, te, *_: (te[i], 0, 0))],
        out_specs=pl.BlockSpec(memory_space=pl.ANY),
        scratch_shapes=[pltpu.VMEM((2 * MOE_TILE // SUBLANES, SUBLANES, d), F32)] * 2
        + [pltpu.SemaphoreType.DMA((2,)), pltpu.SemaphoreType.DMA((2,))],
    )
    return pl.pallas_call(
        _moe_kernel,
        grid_spec=grid_spec,
        out_shape=jax.ShapeDtypeStruct((2 * n_rows, d), F32),
        compiler_params=_params("arbitrary"),
    )(plan["tile_expert"], plan["n_valid"], plan["n_real"], plan["src_row"], plan["dst_row"], h2,
      w_gate, w_up, w_down)


def _route_kernel(lg_ref, idx_ref, wt_ref, cnt_ref, carry):
    i = pl.program_id(0)

    @pl.when(i == 0)
    def _():
        carry[...] = jnp.zeros_like(carry)

    lg = lg_ref[...]
    n, width = lg.shape
    lane = lax.broadcasted_iota(jnp.int32, (n, width), 1)
    row_max = lambda a: jnp.max(a, axis=1, keepdims=True)
    first_lane = lambda hit: jnp.min(jnp.where(hit, lane, width), axis=1, keepdims=True)

    is_group = lane < N_GROUPS
    g_exp = jnp.where(is_group, jnp.exp(lg - row_max(jnp.where(is_group, lg, -jnp.inf))), 0.0)
    p_group = g_exp / jnp.sum(g_exp, axis=1, keepdims=True)
    p_sel = row_max(p_group)
    g_sel = first_lane(is_group & (p_group == p_sel))
    lo = N_GROUPS + g_sel * EXPERTS_PER_GROUP
    in_group = (lane >= lo) & (lane < lo + EXPERTS_PER_GROUP)
    cand = jnp.where(in_group, lg, -jnp.inf)
    v1 = row_max(cand)
    l1 = first_lane(in_group & (cand == v1))
    cand2 = jnp.where(lane == l1, -jnp.inf, cand)
    v2 = row_max(cand2)
    l2 = first_lane(in_group & (lane != l1) & (cand2 == v2))
    t = jnp.exp(v2 - v1)
    w1 = p_sel / (1.0 + t)
    w2 = p_sel * t / (1.0 + t)
    e1, e2 = l1 - N_GROUPS, l2 - N_GROUPS

    hit1, hit2 = lane == e1, lane == e2
    both = (hit1 | hit2).astype(BF16)
    r_i = lax.broadcasted_iota(jnp.int32, (n, n), 0)
    c_i = lax.broadcasted_iota(jnp.int32, (n, n), 1)
    before = carry[0:1, :] + _dot((r_i > c_i).astype(BF16), both)
    rank1 = jnp.sum(jnp.where(hit1, before, 0.0), axis=1, keepdims=True).astype(jnp.int32)
    rank2 = jnp.sum(jnp.where(hit2, before, 0.0), axis=1, keepdims=True).astype(jnp.int32)
    carry[0:1, :] = carry[0:1, :] + jnp.sum(both.astype(F32), axis=0, keepdims=True)

    idx_ref[...] = jnp.where(lane == 0, e1, jnp.where(lane == 1, e2, jnp.where(lane == 2, rank1, rank2)))
    wt_ref[...] = jnp.where(lane == 0, w1, w2)

    @pl.when(i == pl.num_programs(0) - 1)
    def _():
        cnt_ref[...] = carry[...]


def route(logits):
    n_rows, width = logits.shape
    tile = ROW_TILE
    rows = pl.BlockSpec((tile, width), lambda i: (i, 0))
    idx, wt, cnt = pl.pallas_call(
        _route_kernel,
        grid=(n_rows // tile,),
        in_specs=[rows],
        out_specs=[rows, rows, pl.BlockSpec((SUBLANES, width), lambda i: (0, 0))],
        out_shape=[jax.ShapeDtypeStruct((n_rows, width), jnp.int32), jax.ShapeDtypeStruct((n_rows, width), F32),
                   jax.ShapeDtypeStruct((SUBLANES, width), F32)],
        scratch_shapes=[pltpu.VMEM((SUBLANES, width), F32)],
        compiler_params=_params("arbitrary"),
    )(logits)
    return idx[:, 0:2], idx[:, 2:4], wt[:, 0:2], cnt[0, :N_EXPERTS].astype(jnp.int32)


def _moe_plan(expert, rank, counts):
    n = expert.shape[0]
    e_flat = expert.reshape(-1)
    ids = jnp.arange(N_EXPERTS, dtype=jnp.int32)
    one_hot = (e_flat[:, None] == ids[None, :]).astype(jnp.int32)
    padded = (counts + MOE_TILE - 1) // MOE_TILE * MOE_TILE
    ends = jnp.cumsum(padded)
    starts = ends - padded
    pos = jnp.sum(one_hot * starts[None, :], axis=1) + rank.reshape(-1)
    p_max = _round_up(2 * n, MOE_TILE) + N_EXPERTS * MOE_TILE
    n_tiles = p_max // MOE_TILE
    pair = jnp.full((p_max,), -1, jnp.int32).at[pos].set(jnp.arange(2 * n, dtype=jnp.int32),
                                                        unique_indices=True, mode="promise_in_bounds")
    pair0 = jnp.maximum(pair, 0)
    src_row = pair0 // 2
    dst_row = (pair0 % 2) * n + src_row
    tile_start = jnp.arange(n_tiles, dtype=jnp.int32) * MOE_TILE
    n_valid = ends[-1] // MOE_TILE
    tile_expert = jnp.sum((tile_start[:, None] >= ends[None, :]).astype(jnp.int32), axis=1)
    tile_expert = jnp.where(tile_start < ends[-1], tile_expert, tile_expert[n_valid - 1])
    n_real = jnp.clip((starts + counts)[tile_expert] - tile_start, 0, MOE_TILE)
    n_real = jnp.where(tile_start < ends[-1], n_real, 0).astype(jnp.int32)
    return dict(src_row=src_row, dst_row=dst_row, tile_expert=tile_expert,
                n_valid=n_valid.astype(jnp.int32)[None], n_real=n_real)


def _final_kernel(x1_ref, y0_ref, y1_ref, wt_ref, gtp_ref, gts_ref, scp_ref, scs_ref, shp_ref, shs_ref, nw_ref,
                  yp_ref, ys_ref, *, n_prompt_tiles):
    i = pl.program_id(0)

    def finish(gt, sc, sh):
        moe = wt_ref[:, 0:1] * y0_ref[...] + wt_ref[:, 1:2] * y1_ref[...]
        x2 = x1_ref[...] + gt * moe
        return _rmsnorm(x2, nw_ref[...], NORM_EPS) * (1.0 + sc) + sh

    @pl.when(i < n_prompt_tiles)
    def _():
        yp_ref[...] = finish(gtp_ref[0], scp_ref[0], shp_ref[0])

    @pl.when(i >= n_prompt_tiles)
    def _():
        ys_ref[...] = finish(gts_ref[...], scs_ref[...], shs_ref[...])


def final_norm(n_p, t, x1, y_pairs, w_top, mod_p, mod_s, modf_p, modf_s, norm_w):
    n_rows, d = x1.shape
    tile = ROW_TILE
    n_tiles, npt = n_rows // tile, n_p // tile
    per = t // tile
    rows = lambda off: pl.BlockSpec((tile, d), lambda i: (i + off, 0))
    prompt_out = pl.BlockSpec((tile, d), lambda i: (jnp.minimum(i, npt - 1), 0))
    sample = lambda col: pl.BlockSpec((tile, d), lambda i: (jnp.maximum(i - npt, 0), col))
    pmod = lambda col: pl.BlockSpec((1, 1, d), lambda i: (jnp.minimum(i // per, mod_p.shape[0] - 1), 0, col))
    return pl.pallas_call(
        functools.partial(_final_kernel, n_prompt_tiles=npt),
        grid=(n_tiles,),
        in_specs=[rows(0), rows(0), rows(n_tiles), pl.BlockSpec((tile, w_top.shape[1]), lambda i: (i, 0)),
                  pmod(5), sample(5), pmod(1), sample(1), pmod(0), sample(0),
                  pl.BlockSpec((1, d), lambda i: (0, 0))],
        out_specs=[prompt_out, sample(0)],
        out_shape=[jax.ShapeDtypeStruct((n_p, d), F32), jax.ShapeDtypeStruct((n_rows - n_p, d), F32)],
        compiler_params=_params("arbitrary"),
    )(x1, y_pairs, y_pairs, w_top, mod_p, mod_s, modf_p, modf_s, modf_p, modf_s, norm_w)


def _align_rows_kernel(w_ref, a_ref, o_ref, p_ref):
    @pl.when(pl.program_id(0) == 0)
    def _():
        p_ref[...] = jnp.zeros_like(p_ref)

    w = w_ref[...]
    w_hi, w_lo = _split2(w)
    for name in _ALIGNED:
        off, width = SRC[name]
        dst, wa = DST[name]
        o_ref[dst:dst + width, :] = w_hi[off:off + width, :]
        if wa != width:
            o_ref[dst + width:dst + wa, :] = jnp.zeros((wa - width, o_ref.shape[1]), o_ref.dtype)
    a_hi, a_lo = _split2(a_ref[...])
    p_ref[...] += _dot(a_hi, w_hi, NT) + (_dot(a_hi, w_lo, NT) + _dot(a_lo, w_hi, NT))


def align_rows_and_project(wt, a):
    n, d = wt.shape
    m = a.shape[0]
    tc = _largest_tile(d, K_SPLIT, LANES)
    return pl.pallas_call(
        _align_rows_kernel,
        grid=(d // tc,),
        in_specs=[pl.BlockSpec((n, tc), lambda j: (0, j)), pl.BlockSpec((m, tc), lambda j: (0, j))],
        out_specs=[pl.BlockSpec((IN_COLS_ALIGNED, tc), lambda j: (0, j)), pl.BlockSpec((m, n), lambda j: (0, 0))],
        out_shape=[jax.ShapeDtypeStruct((IN_COLS_ALIGNED, d), BF16), jax.ShapeDtypeStruct((m, n), F32)],
        compiler_params=_params("arbitrary"),
    )(wt, a)


def _rwkv_seg(a, name):
    off, w = SRC[name]
    return a[..., off - RWKV_SRC_BASE:off - RWKV_SRC_BASE + w]


def _rwkv_seg_padded(a, name):
    seg = _rwkv_seg(a, name)
    wa = DST[name][1]
    return jnp.pad(seg, [(0, 0)] * (a.ndim - 1) + [(0, wa - seg.shape[-1])])


def _orig_seg(p, name):
    off, w = SRC[name]
    return p[..., off:off + w]


def _pad_rows(w, rows):
    return jnp.pad(w, ((0, rows - w.shape[0]), (0, 0)))


def kernel(x_prompt, x_sample, state_gla, state_rwkv, state_shift, c_prompt, c_sample, w_ada, b_ada, norm_mix, norm_ffn, w_in, gla_gate_w2, gla_gate_b, gla_norm, rwkv_mu, rwkv_w0, rwkv_w2, rwkv_a0, rwkv_a2, rwkv_g2, rwkv_k_k, rwkv_k_a, rwkv_r_k, rwkv_gn_w, rwkv_gn_b, w_out, w_router_group, w_router_expert, w_exp_gate, w_exp_up, w_exp_down, norm_final, w_ada_final, b_ada_final):
    assert w_ada.shape[0] == 1, "single-layer step"
    bp, t, d = x_prompt.shape
    bs = x_sample.shape[0]
    assert x_sample.shape[1] == 1 and t % ROW_TILE == 0
    n_p = bp * t
    pad_s = _round_up(bs, ROW_TILE)
    n_rows = n_p + pad_s
    prompt = _PromptShape(bp, t, n_p)
    hi = lax.Precision.HIGHEST
    pad_sample = lambda a: jnp.pad(a, ((0, pad_s - a.shape[0]), (0, 0)))
    hdot = lambda a, w: jnp.dot(a, w, precision=hi)

    c_act = jax.nn.silu(jnp.concatenate([c_prompt, c_sample], axis=0))
    mod = matmul3(c_act, w_ada[0]) + b_ada[0]
    mod_f = matmul3(c_act, w_ada_final) + b_ada_final
    mod_p, mod_s = mod[:bp, None, :], mod[bp:]
    modf_p, modf_s = mod_f[:bp, None, :], mod_f[bp:]
    sh1_s, sc1_s, gt1_s, sh2_s, sc2_s, _ = jnp.split(mod_s, 6, axis=-1)
    xp = x_prompt.reshape(n_p, d)
    xs = x_sample[:, 0, :]

    w_in_t = jnp.swapaxes(w_in[0], 0, 1)
    h1_s = _rmsnorm(xs, norm_mix[0], NORM_EPS) * (1.0 + sc1_s) + sh1_s
    w_in_aligned, proj_s = align_rows_and_project(w_in_t, h1_s)
    proj = in_projection(xp, t, mod_p, norm_mix, w_in_aligned)
    o_gla, gla_t_p = gla_prompt(prompt, proj, _pad_rows(gla_gate_w2[0], LANES), gla_gate_b, gla_norm)
    new_gla_p = jnp.swapaxes(gla_t_p, -1, -2)
    mu = rwkv_mu[0]
    mu_big = jnp.concatenate([_rwkv_seg(mu, n) for n in ("r", "k7", "v7")])[None, :]
    mu_small = jnp.concatenate([_rwkv_seg_padded(mu, n) for n in ("wl", "al", "gl7")])[None, :]
    vecs = jnp.concatenate([rwkv_w0, rwkv_a0, rwkv_k_k, rwkv_k_a, rwkv_r_k[0].reshape(1, RWKV_WIDTH),
                            rwkv_gn_w, rwkv_gn_b, jnp.zeros((1, RWKV_WIDTH), F32)], axis=0)
    o_rwkv, rwkv_t_p = rwkv7_prompt(prompt, proj, mu_big, mu_small, vecs, _pad_rows(rwkv_w2[0], LANES),
                                    _pad_rows(rwkv_a2[0], LANES), rwkv_g2[0])
    new_rwkv_p = rwkv_t_p.reshape(bp, RWKV_HEAD, RWKV_HEADS, RWKV_HEAD).transpose(0, 2, 1, 3)
    last = jnp.stack([proj[(b + 1) * t - 1] for b in range(bp)])
    new_shift_p = jnp.concatenate([last[:, DST[n][0]:DST[n][0] + SRC[n][1]] for n in _RWKV_ORIG], axis=-1)

    heads = lambda a, n: a.reshape(a.shape[0], n, -1)
    logd_s = jax.nn.log_sigmoid(hdot(_orig_seg(proj_s, "gl"), gla_gate_w2[0]) + gla_gate_b[0]) \
        / GLA_GATE_NORMALIZER
    q_s = heads(_orig_seg(proj_s, "q") * (GLA_DK ** -0.5), GLA_HEADS)
    k_s, v_s, g_s = heads(_orig_seg(proj_s, "k"), GLA_HEADS), heads(_orig_seg(proj_s, "v"), GLA_HEADS), \
        heads(logd_s, GLA_HEADS)
    new_gla_s, o_s = gla_step(state_gla[0], q_s, k_s, g_s, v_s)
    o_s = _rmsnorm(o_s, gla_norm[0], GLA_NORM_EPS) * heads(jax.nn.silu(_orig_seg(proj_s, "og")), GLA_HEADS)

    rp_s = proj_s[:, RWKV_SRC_BASE:]
    new_shift_s = rp_s
    xs7 = rp_s + (state_shift[0] - rp_s) * mu
    sx = lambda name: _rwkv_seg(xs7, name)
    r_s, k7_s, v7_s = sx("r"), sx("k7"), sx("v7")
    w_pre = rwkv_w0[0] + hdot(jnp.tanh(sx("wl")), rwkv_w2[0])
    decay_s = jnp.exp(-jnp.exp(-jax.nn.softplus(-w_pre) - 0.5))
    a_s = jax.nn.sigmoid(rwkv_a0[0] + hdot(sx("al"), rwkv_a2[0]))
    g_s7 = hdot(jax.nn.sigmoid(sx("gl7")), rwkv_g2[0])
    hs = lambda z: z.reshape(bs, RWKV_HEADS, RWKV_HEAD)
    kk_s = hs(k7_s * rwkv_k_k[0])
    kk_s = kk_s / jnp.maximum(jnp.sqrt(jnp.sum(kk_s * kk_s, axis=-1, keepdims=True)), 1e-12)
    k7_s = k7_s * (1.0 + (a_s - 1.0) * rwkv_k_a[0])
    r_h, w_h, k_h, v_h, a_h = hs(r_s), hs(decay_s), hs(k7_s), hs(v7_s), hs(a_s)
    s_prev = state_rwkv[0]
    sa = jnp.einsum("bhij,bhj->bhi", s_prev, -kk_s, precision=hi)
    new_rwkv_s = s_prev * w_h[:, :, None, :] + sa[..., None] * (kk_s * a_h)[:, :, None, :] \
        + v_h[..., None] * k_h[:, :, None, :]
    y_s = jnp.einsum("bhij,bhj->bhi", new_rwkv_s, r_h, precision=hi)
    y_mu = jnp.mean(y_s, axis=-1, keepdims=True)
    y_var = jnp.mean(jnp.square(y_s - y_mu), axis=-1, keepdims=True)
    y_s = (y_s - y_mu) * lax.rsqrt(y_var + RWKV_GN_EPS) * rwkv_gn_w[0].reshape(RWKV_HEADS, RWKV_HEAD) \
        + rwkv_gn_b[0].reshape(RWKV_HEADS, RWKV_HEAD)
    bonus = jnp.sum(r_h * k_h * rwkv_r_k[0], axis=-1, keepdims=True) * v_h
    o_rs = (y_s + bonus).reshape(bs, RWKV_WIDTH) * g_s7
    mix_s = matmul3(jnp.concatenate([o_s.reshape(bs, GLA_WIDTH), o_rs], axis=-1), w_out[0])
    x1_s = xs + gt1_s * mix_s
    h2_s = _rmsnorm(x1_s, norm_ffn[0], NORM_EPS) * (1.0 + sc2_s) + sh2_s
    w_router = jnp.pad(jnp.concatenate([w_router_group[0], w_router_expert[0]], axis=-1),
                       ((0, 0), (0, LANES - N_GROUPS - N_EXPERTS)))
    logits_s = hdot(h2_s, w_router)

    x1, h2, logits = out_projection(t, o_gla, o_rwkv, xp, mod_p, norm_ffn, w_out[0].astype(BF16), w_router,
                                    pad_sample(x1_s), pad_sample(h2_s), pad_sample(logits_s))

    expert, rank, w_top, counts = route(logits)
    plan = _moe_plan(expert, rank, counts)
    y_pairs = moe_experts(h2, plan, w_exp_gate[0].reshape(N_EXPERTS, d, EXPERT_FF),
                          w_exp_up[0].reshape(N_EXPERTS, d, EXPERT_FF),
                          w_exp_down[0].reshape(N_EXPERTS, EXPERT_FF, d))

    y_p, y_s_pad = final_norm(n_p, t, x1, y_pairs, w_top, mod_p, pad_sample(mod_s), modf_p,
                              pad_sample(modf_s), norm_final[None, :])
    return (y_p.reshape(bp, t, d), y_s_pad[:bs, None, :], new_gla_p[None], new_rwkv_p[None], new_shift_p[None],
            new_gla_s[None], new_rwkv_s[None], new_shift_s[None])
```
